```python
import jax, jax.numpy as jnp
from jax import lax
import numpy as np

D_MODEL = 1024
BATCH = 8
SEQ = 2048
DEPTH = 4

SB_HEADS = 8
SB_HEAD_DIM = 64
SB_WIDTH = SB_HEADS * SB_HEAD_DIM
MLA_HEADS = 8
MLA_NOPE_DIM = 64
MLA_ROPE_DIM = 32
MLA_V_DIM = 64
MLA_QK_DIM = MLA_NOPE_DIM + MLA_ROPE_DIM
MLA_Q_RANK = 384
MLA_KV_RANK = 256
MLA_WIDTH = MLA_HEADS * MLA_V_DIM
D_FF = 4 * D_MODEL
BLOCK_Q = 128
ROPE_THETA = 10000.0
NORM_EPS = 1e-6
N_MOD = 6
IN_WIDTHS = (SB_WIDTH, SB_WIDTH, SB_WIDTH, MLA_Q_RANK, MLA_KV_RANK, MLA_ROPE_DIM, D_MODEL, D_MODEL)
IN_DIM = sum(IN_WIDTHS)

kernel_name = "hybrid_stickbreaking_mla_sqrelu_adaln"


def _rms_norm(x, g):
    xf = x.astype(jnp.float32)
    y = xf * lax.rsqrt(jnp.mean(xf * xf, axis=-1, keepdims=True) + NORM_EPS)
    return y.astype(x.dtype) * g


def _split_cols(p, widths):
    outs, start = [], 0
    for w in widths:
        outs.append(p[..., start:start + w])
        start += w
    return outs


def _rope_tables(positions):
    inv_freq = 1.0 / (ROPE_THETA ** (jnp.arange(0, MLA_ROPE_DIM, 2, dtype=jnp.float32) / MLA_ROPE_DIM))
    ang = positions.astype(jnp.float32)[..., None] * inv_freq
    return jnp.cos(ang), jnp.sin(ang)


def _apply_rope(t, cos, sin):
    half = t.shape[-1] // 2
    t1, t2 = t[..., :half], t[..., half:]
    cs = cos[:, :, None, :].astype(t.dtype)
    sn = sin[:, :, None, :].astype(t.dtype)
    return jnp.concatenate([t1 * cs - t2 * sn, t2 * cs + t1 * sn], axis=-1)


def _stick_breaking_weights(z, mask):
    log_fail = jnp.where(mask, jax.nn.log_sigmoid(-z), 0.0)
    later = lax.cumsum(log_fail, axis=3, reverse=True) - log_fail
    return jnp.where(mask, jnp.exp(jax.nn.log_sigmoid(z) + later), 0.0)


def _softmax_weights(z, mask):
    return jax.nn.softmax(jnp.where(mask, z, -jnp.inf), axis=-1)


def _causal_block_attention(q, k, v, weight_fn, strict):
    seq = q.shape[1]
    scale = q.shape[-1] ** -0.5
    outs = []
    for t0 in range(0, seq, BLOCK_Q):
        end = t0 + BLOCK_Q
        z = jnp.einsum('bqhd,bkhd->bhqk', q[:, t0:end].astype(jnp.float32),
                       k[:, :end].astype(jnp.float32)) * scale
        t_idx = t0 + jnp.arange(BLOCK_Q)[:, None]
        s_idx = jnp.arange(end)[None, :]
        mask = (s_idx < t_idx) if strict else (s_idx <= t_idx)
        w = weight_fn(z, mask)
        outs.append(jnp.einsum('bhqk,bkhd->bqhd', w.astype(v.dtype), v[:, :end]))
    return jnp.concatenate(outs, axis=1)


def _fwd_setup_inputs(seed: int = 0) -> dict:
    key = jax.random.key(seed)
    ks = jax.random.split(key, 20)

    def nrm(k, shape, fan_in):
        return jax.random.normal(k, shape, jnp.float32) * (fan_in ** -0.5)

    def gain(k, shape):
        return 1.0 + 0.05 * jax.random.normal(k, shape, jnp.float32)

    x = jax.random.normal(ks[0], (BATCH, SEQ, D_MODEL), jnp.float32)
    c = jax.random.normal(ks[1], (BATCH, D_MODEL), jnp.float32)
    offsets = jax.random.randint(ks[2], (BATCH, 1), 0, 1024, dtype=jnp.int32)
    positions = (offsets + jnp.arange(SEQ, dtype=jnp.int32)[None, :]).astype(jnp.int32)
    return {
        "x": x,
        "c": c,
        "positions": positions,
        "w_ada": nrm(ks[3], (DEPTH, D_MODEL, N_MOD * D_MODEL), D_MODEL),
        "b_ada": 0.02 * jax.random.normal(ks[4], (DEPTH, N_MOD * D_MODEL), jnp.float32),
        "g_mix_norm": gain(ks[5], (DEPTH, D_MODEL)),
        "w_in": nrm(ks[6], (DEPTH, D_MODEL, IN_DIM), D_MODEL),
        "g_q_lat": gain(ks[7], (DEPTH, MLA_Q_RANK)),
        "w_q_up": nrm(ks[8], (DEPTH, MLA_Q_RANK, MLA_HEADS * MLA_QK_DIM), MLA_Q_RANK),
        "g_kv_lat": gain(ks[9], (DEPTH, MLA_KV_RANK)),
        "w_kv_up": nrm(ks[10], (DEPTH, MLA_KV_RANK, MLA_HEADS * (MLA_NOPE_DIM + MLA_V_DIM)), MLA_KV_RANK),
        "w_sb_out": nrm(ks[11], (DEPTH, SB_WIDTH, D_MODEL), SB_WIDTH),
        "w_mla_out": nrm(ks[12], (DEPTH, MLA_WIDTH, D_MODEL), MLA_WIDTH),
        "w_mix_out": nrm(ks[13], (DEPTH, D_MODEL, D_MODEL), D_MODEL),
        "g_mlp_norm": gain(ks[14], (DEPTH, D_MODEL)),
        "w_up": nrm(ks[15], (DEPTH, D_MODEL, D_FF), D_MODEL),
        "w_down": nrm(ks[16], (DEPTH, D_FF, D_MODEL), D_FF),
        "g_final": gain(ks[17], (D_MODEL,)),
    }


def _fwd_reference(x, c, positions, w_ada, b_ada, g_mix_norm, w_in, g_q_lat, w_q_up,
              g_kv_lat, w_kv_up, w_sb_out, w_mla_out, w_mix_out, g_mlp_norm,
              w_up, w_down, g_final):
    B, S, _ = x.shape
    cos, sin = _rope_tables(positions)
    c_act = jax.nn.silu(c)
    for l in range(DEPTH):
        mod = (c_act @ w_ada[l] + b_ada[l])[:, None, :]
        shift1, scale1, gate1, shift2, scale2, gate2 = jnp.split(mod, N_MOD, axis=-1)

        h = _rms_norm(x, g_mix_norm[l]) * (1.0 + scale1) + shift1
        p = h @ w_in[l]
        q_sb, k_sb, v_sb, q_lat, kv_lat, k_rope, gate_sb, gate_mla = _split_cols(p, IN_WIDTHS)

        o_sb = _causal_block_attention(
            q_sb.reshape(B, S, SB_HEADS, SB_HEAD_DIM),
            k_sb.reshape(B, S, SB_HEADS, SB_HEAD_DIM),
            v_sb.reshape(B, S, SB_HEADS, SB_HEAD_DIM),
            _stick_breaking_weights, strict=True)
        o_sb = o_sb.reshape(B, S, SB_WIDTH) @ w_sb_out[l]

        q = (_rms_norm(q_lat, g_q_lat[l]) @ w_q_up[l]).reshape(B, S, MLA_HEADS, MLA_QK_DIM)
        kv = (_rms_norm(kv_lat, g_kv_lat[l]) @ w_kv_up[l]).reshape(
            B, S, MLA_HEADS, MLA_NOPE_DIM + MLA_V_DIM)
        k_nope, v_mla = kv[..., :MLA_NOPE_DIM], kv[..., MLA_NOPE_DIM:]
        q_full = jnp.concatenate(
            [q[..., :MLA_NOPE_DIM], _apply_rope(q[..., MLA_NOPE_DIM:], cos, sin)], axis=-1)
        k_pe = _apply_rope(k_rope[:, :, None, :], cos, sin)
        k_full = jnp.concatenate(
            [k_nope, jnp.broadcast_to(k_pe, (B, S, MLA_HEADS, MLA_ROPE_DIM))], axis=-1)
        o_mla = _causal_block_attention(q_full, k_full, v_mla, _softmax_weights, strict=False)
        o_mla = o_mla.reshape(B, S, MLA_WIDTH) @ w_mla_out[l]

        merged = jax.nn.sigmoid(gate_sb) * o_sb + jax.nn.sigmoid(gate_mla) * o_mla
        x = x + gate1 * (merged @ w_mix_out[l])

        h = _rms_norm(x, g_mlp_norm[l]) * (1.0 + scale2) + shift2
        x = x + gate2 * (jnp.square(jax.nn.relu(h @ w_up[l])) @ w_down[l])

    return _rms_norm(x, g_final)


import jax as _jax
import jax.numpy as _jnp

TWIN_FORMAT = 'train_step'
FWD_PARAMS = ['x', 'c', 'positions', 'w_ada', 'b_ada', 'g_mix_norm', 'w_in', 'g_q_lat', 'w_q_up', 'g_kv_lat', 'w_kv_up', 'w_sb_out', 'w_mla_out', 'w_mix_out', 'g_mlp_norm', 'w_up', 'w_down', 'g_final']
TWIN_WEIGHTS = ['w_ada', 'b_ada', 'g_mix_norm', 'w_in', 'g_q_lat', 'w_q_up', 'g_kv_lat', 'w_kv_up', 'w_sb_out', 'w_mla_out', 'w_mix_out', 'g_mlp_norm', 'w_up', 'w_down', 'g_final']
TWIN_DIFF_INPUT = 'x'
TWIN_INPUTS = ['x', 'c', 'positions', 'w_ada', 'b_ada', 'g_mix_norm', 'w_in', 'g_q_lat', 'w_q_up', 'g_kv_lat', 'w_kv_up', 'w_sb_out', 'w_mla_out', 'w_mix_out', 'g_mlp_norm', 'w_up', 'w_down', 'g_final', 'loss_target', 'm_w_ada', 'm_b_ada', 'm_g_mix_norm', 'm_w_in', 'm_g_q_lat', 'm_w_q_up', 'm_g_kv_lat', 'm_w_kv_up', 'm_w_sb_out', 'm_w_mla_out', 'm_w_mix_out', 'm_g_mlp_norm', 'm_w_up', 'm_w_down', 'm_g_final', 'v_w_ada', 'v_b_ada', 'v_g_mix_norm', 'v_w_in', 'v_g_q_lat', 'v_w_q_up', 'v_g_kv_lat', 'v_w_kv_up', 'v_w_sb_out', 'v_w_mla_out', 'v_w_mix_out', 'v_g_mlp_norm', 'v_w_up', 'v_w_down', 'v_g_final']
TWIN_OUTPUTS = ['loss', 'grad_x', 'grad_w_ada', 'grad_b_ada', 'grad_g_mix_norm', 'grad_w_in', 'grad_g_q_lat', 'grad_w_q_up', 'grad_g_kv_lat', 'grad_w_kv_up', 'grad_w_sb_out', 'grad_w_mla_out', 'grad_w_mix_out', 'grad_g_mlp_norm', 'grad_w_up', 'grad_w_down', 'grad_g_final', 'delta_w_ada', 'delta_b_ada', 'delta_g_mix_norm', 'delta_w_in', 'delta_g_q_lat', 'delta_w_q_up', 'delta_g_kv_lat', 'delta_w_kv_up', 'delta_w_sb_out', 'delta_w_mla_out', 'delta_w_mix_out', 'delta_g_mlp_norm', 'delta_w_up', 'delta_w_down', 'delta_g_final', 'new_m_w_ada', 'new_m_b_ada', 'new_m_g_mix_norm', 'new_m_w_in', 'new_m_g_q_lat', 'new_m_w_q_up', 'new_m_g_kv_lat', 'new_m_w_kv_up', 'new_m_w_sb_out', 'new_m_w_mla_out', 'new_m_w_mix_out', 'new_m_g_mlp_norm', 'new_m_w_up', 'new_m_w_down', 'new_m_g_final', 'new_v_w_ada', 'new_v_b_ada', 'new_v_g_mix_norm', 'new_v_w_in', 'new_v_g_q_lat', 'new_v_w_q_up', 'new_v_g_kv_lat', 'new_v_w_kv_up', 'new_v_w_sb_out', 'new_v_w_mla_out', 'new_v_w_mix_out', 'new_v_g_mlp_norm', 'new_v_w_up', 'new_v_w_down', 'new_v_g_final']
TWIN_LEAF_KINDS = {'loss': 'loss', 'grad_x': 'grad_x', 'grad_w_ada': 'grad_w', 'grad_b_ada': 'grad_w', 'grad_g_mix_norm': 'grad_w', 'grad_w_in': 'grad_w', 'grad_g_q_lat': 'grad_w', 'grad_w_q_up': 'grad_w', 'grad_g_kv_lat': 'grad_w', 'grad_w_kv_up': 'grad_w', 'grad_w_sb_out': 'grad_w', 'grad_w_mla_out': 'grad_w', 'grad_w_mix_out': 'grad_w', 'grad_g_mlp_norm': 'grad_w', 'grad_w_up': 'grad_w', 'grad_w_down': 'grad_w', 'grad_g_final': 'grad_w', 'delta_w_ada': 'delta_w', 'delta_b_ada': 'delta_w', 'delta_g_mix_norm': 'delta_w', 'delta_w_in': 'delta_w', 'delta_g_q_lat': 'delta_w', 'delta_w_q_up': 'delta_w', 'delta_g_kv_lat': 'delta_w', 'delta_w_kv_up': 'delta_w', 'delta_w_sb_out': 'delta_w', 'delta_w_mla_out': 'delta_w', 'delta_w_mix_out': 'delta_w', 'delta_g_mlp_norm': 'delta_w', 'delta_w_up': 'delta_w', 'delta_w_down': 'delta_w', 'delta_g_final': 'delta_w', 'new_m_w_ada': 'new_m', 'new_m_b_ada': 'new_m', 'new_m_g_mix_norm': 'new_m', 'new_m_w_in': 'new_m', 'new_m_g_q_lat': 'new_m', 'new_m_w_q_up': 'new_m', 'new_m_g_kv_lat': 'new_m', 'new_m_w_kv_up': 'new_m', 'new_m_w_sb_out': 'new_m', 'new_m_w_mla_out': 'new_m', 'new_m_w_mix_out': 'new_m', 'new_m_g_mlp_norm': 'new_m', 'new_m_w_up': 'new_m', 'new_m_w_down': 'new_m', 'new_m_g_final': 'new_m', 'new_v_w_ada': 'new_v', 'new_v_b_ada': 'new_v', 'new_v_g_mix_norm': 'new_v', 'new_v_w_in': 'new_v', 'new_v_g_q_lat': 'new_v', 'new_v_w_q_up': 'new_v', 'new_v_g_kv_lat': 'new_v', 'new_v_w_kv_up': 'new_v', 'new_v_w_sb_out': 'new_v', 'new_v_w_mla_out': 'new_v', 'new_v_w_mix_out': 'new_v', 'new_v_g_mlp_norm': 'new_v', 'new_v_w_up': 'new_v', 'new_v_w_down': 'new_v', 'new_v_g_final': 'new_v'}


def _forward(args):
    return _fwd_reference(*[args[k] for k in FWD_PARAMS])


def _output_shape():
    out = _jax.eval_shape(lambda: _forward(_fwd_setup_inputs(0)))
    return out.shape, out.dtype

N_MICROBATCH = 1
ADAM_LR = 0.001
ADAM_B1 = 0.9
ADAM_B2 = 0.999
ADAM_EPS = 1e-08
ADAM_WD = 0.01
ADAM_STEP = 10
PER_EXAMPLE_BATCH_AXIS = {'x': 0, 'c': 0, 'positions': 0, 'loss_target': 0}
SHARED_INPUTS = []
_WEIGHT_DTYPES = {'w_ada': _jnp.float32, 'b_ada': _jnp.float32, 'g_mix_norm': _jnp.float32, 'w_in': _jnp.float32, 'g_q_lat': _jnp.float32, 'w_q_up': _jnp.float32, 'g_kv_lat': _jnp.float32, 'w_kv_up': _jnp.float32, 'w_sb_out': _jnp.float32, 'w_mla_out': _jnp.float32, 'w_mix_out': _jnp.float32, 'g_mlp_norm': _jnp.float32, 'w_up': _jnp.float32, 'w_down': _jnp.float32, 'g_final': _jnp.float32}
MOMENT_SCALE = {'w_ada': 1.021412e-01, 'b_ada': 1.771272e-01, 'g_mix_norm': 4.480129e-02, 'w_in': 3.466876e-02, 'g_q_lat': 6.712774e-03, 'w_q_up': 4.787973e-03, 'g_kv_lat': 7.950449e-02, 'w_kv_up': 3.772932e-02, 'w_sb_out': 5.403605e-02, 'w_mla_out': 3.692909e-02, 'w_mix_out': 6.533236e-02, 'g_mlp_norm': 1.103835e-01, 'w_up': 7.282980e-02, 'w_down': 1.691500e-01, 'g_final': 1.734021e+01}


def _to_microbatches(a, axis):
    t = _jnp.moveaxis(a, axis, 0)
    t = t.reshape((N_MICROBATCH, t.shape[0] // N_MICROBATCH) + t.shape[1:])
    return _jnp.moveaxis(t, 1, axis + 1)


def setup_inputs(seed: int = 0) -> dict:
    inp = _fwd_setup_inputs(seed)
    key = _jax.random.fold_in(_jax.random.key(seed), 7919)
    shape, _ = _output_shape()
    out = dict(inp)
    out["loss_target"] = _jax.random.normal(_jax.random.fold_in(key, 0), shape, _jnp.float32)
    for i, name in enumerate(TWIN_WEIGHTS):
        w = inp[name].astype(_jnp.float32)
        if MOMENT_SCALE is None:
            s = _jnp.sqrt(_jnp.mean(_jnp.square(w)) + 1e-30)
        else:
            s = MOMENT_SCALE[name]
        km, kv = _jax.random.split(_jax.random.fold_in(key, i + 1))
        out[name] = w
        out["m_" + name] = s * _jax.random.normal(km, w.shape, _jnp.float32)
        out["v_" + name] = (s * s) * _jax.random.uniform(kv, w.shape, _jnp.float32, 0.5, 1.5)
    if N_MICROBATCH > 1:
        for name, axis in PER_EXAMPLE_BATCH_AXIS.items():
            out[name] = _to_microbatches(out[name], axis)
    return {'x': out['x'], 'c': out['c'], 'positions': out['positions'], 'w_ada': out['w_ada'], 'b_ada': out['b_ada'], 'g_mix_norm': out['g_mix_norm'], 'w_in': out['w_in'], 'g_q_lat': out['g_q_lat'], 'w_q_up': out['w_q_up'], 'g_kv_lat': out['g_kv_lat'], 'w_kv_up': out['w_kv_up'], 'w_sb_out': out['w_sb_out'], 'w_mla_out': out['w_mla_out'], 'w_mix_out': out['w_mix_out'], 'g_mlp_norm': out['g_mlp_norm'], 'w_up': out['w_up'], 'w_down': out['w_down'], 'g_final': out['g_final'], 'loss_target': out['loss_target'], 'm_w_ada': out['m_w_ada'], 'm_b_ada': out['m_b_ada'], 'm_g_mix_norm': out['m_g_mix_norm'], 'm_w_in': out['m_w_in'], 'm_g_q_lat': out['m_g_q_lat'], 'm_w_q_up': out['m_w_q_up'], 'm_g_kv_lat': out['m_g_kv_lat'], 'm_w_kv_up': out['m_w_kv_up'], 'm_w_sb_out': out['m_w_sb_out'], 'm_w_mla_out': out['m_w_mla_out'], 'm_w_mix_out': out['m_w_mix_out'], 'm_g_mlp_norm': out['m_g_mlp_norm'], 'm_w_up': out['m_w_up'], 'm_w_down': out['m_w_down'], 'm_g_final': out['m_g_final'], 'v_w_ada': out['v_w_ada'], 'v_b_ada': out['v_b_ada'], 'v_g_mix_norm': out['v_g_mix_norm'], 'v_w_in': out['v_w_in'], 'v_g_q_lat': out['v_g_q_lat'], 'v_w_q_up': out['v_w_q_up'], 'v_g_kv_lat': out['v_g_kv_lat'], 'v_w_kv_up': out['v_w_kv_up'], 'v_w_sb_out': out['v_w_sb_out'], 'v_w_mla_out': out['v_w_mla_out'], 'v_w_mix_out': out['v_w_mix_out'], 'v_g_mlp_norm': out['v_g_mlp_norm'], 'v_w_up': out['v_w_up'], 'v_w_down': out['v_w_down'], 'v_g_final': out['v_g_final']}


def _loss(weights, diff, rest, loss_target):
    with _jax.named_scope("forward"):
        args = {**rest, TWIN_DIFF_INPUT: diff, **{k: w.astype(_WEIGHT_DTYPES[k]) for k, w in weights.items()}}
        y = _forward(args)
    with _jax.named_scope("loss_head"):
        err = _jnp.square(y.astype(_jnp.float32) - loss_target)
        return 0.5 * _jnp.sum(_jnp.mean(err, axis=-1)) if err.ndim else 0.5 * err


def _adamw(w, g, m, v):
    m = ADAM_B1 * m + (1.0 - ADAM_B1) * g
    v = ADAM_B2 * v + (1.0 - ADAM_B2) * _jnp.square(g)
    m_hat = m / (1.0 - ADAM_B1 ** ADAM_STEP)
    v_hat = v / (1.0 - ADAM_B2 ** ADAM_STEP)
    delta = -ADAM_LR * (m_hat / (_jnp.sqrt(v_hat) + ADAM_EPS) + ADAM_WD * w)
    return delta, m, v


def reference(x, c, positions, w_ada, b_ada, g_mix_norm, w_in, g_q_lat, w_q_up, g_kv_lat, w_kv_up, w_sb_out, w_mla_out, w_mix_out, g_mlp_norm, w_up, w_down, g_final, loss_target, m_w_ada, m_b_ada, m_g_mix_norm, m_w_in, m_g_q_lat, m_w_q_up, m_g_kv_lat, m_w_kv_up, m_w_sb_out, m_w_mla_out, m_w_mix_out, m_g_mlp_norm, m_w_up, m_w_down, m_g_final, v_w_ada, v_b_ada, v_g_mix_norm, v_w_in, v_g_q_lat, v_w_q_up, v_g_kv_lat, v_w_kv_up, v_w_sb_out, v_w_mla_out, v_w_mix_out, v_g_mlp_norm, v_w_up, v_w_down, v_g_final):
    given = dict(x=x, c=c, positions=positions, w_ada=w_ada, b_ada=b_ada, g_mix_norm=g_mix_norm, w_in=w_in, g_q_lat=g_q_lat, w_q_up=w_q_up, g_kv_lat=g_kv_lat, w_kv_up=w_kv_up, w_sb_out=w_sb_out, w_mla_out=w_mla_out, w_mix_out=w_mix_out, g_mlp_norm=g_mlp_norm, w_up=w_up, w_down=w_down, g_final=g_final, loss_target=loss_target, m_w_ada=m_w_ada, m_b_ada=m_b_ada, m_g_mix_norm=m_g_mix_norm, m_w_in=m_w_in, m_g_q_lat=m_g_q_lat, m_w_q_up=m_w_q_up, m_g_kv_lat=m_g_kv_lat, m_w_kv_up=m_w_kv_up, m_w_sb_out=m_w_sb_out, m_w_mla_out=m_w_mla_out, m_w_mix_out=m_w_mix_out, m_g_mlp_norm=m_g_mlp_norm, m_w_up=m_w_up, m_w_down=m_w_down, m_g_final=m_g_final, v_w_ada=v_w_ada, v_b_ada=v_b_ada, v_g_mix_norm=v_g_mix_norm, v_w_in=v_w_in, v_g_q_lat=v_g_q_lat, v_w_q_up=v_w_q_up, v_g_kv_lat=v_g_kv_lat, v_w_kv_up=v_w_kv_up, v_w_sb_out=v_w_sb_out, v_w_mla_out=v_w_mla_out, v_w_mix_out=v_w_mix_out, v_g_mlp_norm=v_g_mlp_norm, v_w_up=v_w_up, v_w_down=v_w_down, v_g_final=v_g_final)
    weights = {n: given[n] for n in TWIN_WEIGHTS}
    shared = {n: given[n] for n in SHARED_INPUTS}
    per_example = {n: given[n] for n in ['x', 'c', 'positions']}
    grad_fn = _jax.value_and_grad(_loss, argnums=(0, 1))

    def one_microbatch(ex, loss_target):
        ex = dict(ex)
        diff = ex.pop(TWIN_DIFF_INPUT)
        return grad_fn(weights, diff, {**shared, **ex}, loss_target)

    if N_MICROBATCH == 1:
        loss, (grad_w, grad_x) = one_microbatch(per_example, given["loss_target"])
    else:
        def body(carry, xs):
            loss_sum, grad_sum = carry
            l_k, (gw_k, gx_k) = one_microbatch(xs[0], xs[1])
            with _jax.named_scope("update"):
                return (loss_sum + l_k, _jax.tree.map(_jnp.add, grad_sum, gw_k)), gx_k

        init = (_jnp.zeros((), _jnp.float32), _jax.tree.map(_jnp.zeros_like, weights))
        (loss, grad_w), grad_x = _jax.lax.scan(body, init, (per_example, given["loss_target"]))
    with _jax.named_scope("update"):
        delta_w, new_m, new_v = {}, {}, {}
        for n in TWIN_WEIGHTS:
            delta_w[n], new_m[n], new_v[n] = _adamw(weights[n], grad_w[n], given["m_" + n], given["v_" + n])
    return (loss, grad_x, *[grad_w[n] for n in TWIN_WEIGHTS], *[delta_w[n] for n in TWIN_WEIGHTS],
            *[new_m[n] for n in TWIN_WEIGHTS], *[new_v[n] for n in TWIN_WEIGHTS])
```

```python
import functools

import jax
import jax.numpy as jnp
from jax import lax
from jax.experimental import pallas as pl
from jax.experimental.pallas import tpu as pltpu

F32 = jnp.float32
BF16 = jnp.bfloat16
NDEV = 8
LANES = 128
HEAD_DIM = 64
ROPE_DIM = 32
MLA_QK_DIM = HEAD_DIM + ROPE_DIM
ROPE_THETA = 10000.0
NORM_EPS = 1e-6
ADAM_LR = 0.001
ADAM_B1 = 0.9
ADAM_B2 = 0.999
ADAM_EPS = 1e-08
ADAM_WD = 0.01
ADAM_STEP = 10
VMEM_LIMIT = 48 * 1024 * 1024
PACK_ALIGN = 16 * LANES


def _pcall(body, **kw):
    return pl.pallas_call(body, **kw)


def _tile(n, pref):
    for t in (512, 384, 256, 128, 64, 32, 16, 8):
        if t <= pref and n % t == 0:
            return t
    return n


def _roundup(n, m):
    return (n + m - 1) // m * m


_CP = pltpu.CompilerParams(vmem_limit_bytes=VMEM_LIMIT)


def _rowwise(name, fn, rows, vecs, outs, reds=(), tb=256):
    rows = [r if isinstance(r, tuple) else (r, r.shape[1], 0, 0) for r in rows]
    nrows = None
    for arr, width, col, roff in rows:
        if roff == 0 and nrows is None:
            nrows = arr.shape[0]
    first_off = [r for r in rows if r[3] != 0]
    if first_off:
        nrows = min(nrows, first_off[0][3])
    tb = _tile(nrows, tb)
    nblk = nrows // tb
    n_in = len(rows) + len(vecs)
    n_out = len(outs)

    def body(*refs):
        vals = [r[...] for r in refs[:n_in]]
        res = fn(*vals)
        if not isinstance(res, (tuple, list)):
            res = (res,)
        for ref, val in zip(refs[n_in:n_in + n_out], res[:n_out]):
            ref[...] = val.astype(ref.dtype)
        for ref, val in zip(refs[n_in + n_out:], res[n_out:]):
            @pl.when(pl.program_id(0) == 0)
            def _(ref=ref):
                ref[...] = jnp.zeros_like(ref)
            ref[...] += jnp.sum(val.astype(F32), axis=0, keepdims=True)

    in_specs = []
    for arr, width, col, roff in rows:
        in_specs.append(pl.BlockSpec((tb, width), functools.partial(
            lambda i, col, rb: (rb + i, col), col=col, rb=roff // tb)))
    for v in vecs:
        in_specs.append(pl.BlockSpec(v.shape, lambda i, nd=v.ndim: (0,) * nd))
    out_specs = [pl.BlockSpec((tb, w), lambda i: (i, 0)) for w, _ in outs]
    out_specs += [pl.BlockSpec((1, w), lambda i: (0, 0)) for w in reds]
    out_shape = [jax.ShapeDtypeStruct((nrows, w), dt) for w, dt in outs]
    out_shape += [jax.ShapeDtypeStruct((1, w), F32) for w in reds]
    res = _pcall(body, name=name, grid=(nblk,), in_specs=in_specs, out_specs=out_specs,
                 out_shape=out_shape, compiler_params=_CP)(*[r[0] for r in rows], *vecs)
    return res


_DIMS = {"nn": (((1,), (0,)), ((), ())), "nt": (((1,), (1,)), ((), ())), "tn": (((0,), (0,)), ((), ()))}


def _matmul(name, a, b, mode, out_dtype=F32):
    if mode == "nn":
        (m, k), n = a.shape, b.shape[1]
    elif mode == "nt":
        (m, k), n = a.shape, b.shape[0]
    else:
        (k, m), n = a.shape, b.shape[1]
    tm, tn = _tile(m, 512), _tile(n, 512)
    dims = _DIMS[mode]

    def body(a_ref, b_ref, o_ref):
        o_ref[...] = lax.dot_general(a_ref[...].astype(BF16), b_ref[...].astype(BF16), dims,
                                     preferred_element_type=F32).astype(o_ref.dtype)

    a_spec = pl.BlockSpec((k, tm), lambda j, i: (0, i)) if mode == "tn" else pl.BlockSpec((tm, k), lambda j, i: (i, 0))
    b_spec = pl.BlockSpec((tn, k), lambda j, i: (j, 0)) if mode == "nt" else pl.BlockSpec((k, tn), lambda j, i: (0, j))
    return _pcall(body, name=name, grid=(n // tn, m // tm), in_specs=[a_spec, b_spec],
                  out_specs=pl.BlockSpec((tm, tn), lambda j, i: (i, j)),
                  out_shape=jax.ShapeDtypeStruct((m, n), out_dtype), compiler_params=_CP)(a, b)


def _exchange(name, src, a2a):
    shp = src.shape[-2:]

    def body(src_ref, out_ref, send_sems, recv_sems, local_sem):
        x, y, c = lax.axis_index("x"), lax.axis_index("y"), lax.axis_index("c")
        me = 4 * x + 2 * y + c

        def src_for(dev):
            return src_ref.at[dev] if a2a else src_ref

        local = pltpu.make_async_copy(src_for(me), out_ref.at[me], local_sem)
        local.start()
        copies = []
        for k in range(1, NDEV):
            px = 1 - x if (k >> 2) & 1 else x
            py = 1 - y if (k >> 1) & 1 else y
            pc = 1 - c if k & 1 else c
            cp = pltpu.make_async_remote_copy(
                src_ref=src_for(4 * px + 2 * py + pc), dst_ref=out_ref.at[me],
                send_sem=send_sems.at[k - 1], recv_sem=recv_sems.at[k - 1],
                device_id=(px, py, pc), device_id_type=pl.DeviceIdType.MESH)
            cp.start()
            copies.append(cp)
        for cp in copies:
            cp.wait()
        local.wait()

    return _pcall(body, name=name, in_specs=[pl.BlockSpec(memory_space=pl.ANY)],
                  out_specs=pl.BlockSpec(memory_space=pl.ANY),
                  out_shape=jax.ShapeDtypeStruct((NDEV,) + shp, src.dtype),
                  scratch_shapes=[pltpu.SemaphoreType.DMA((NDEV - 1,)), pltpu.SemaphoreType.DMA((NDEV - 1,)),
                                  pltpu.SemaphoreType.DMA(())])(src)


def _sum8(name, land):
    n, r, c = land.shape
    flat = land.reshape(n * r, c)

    def fn(*blocks):
        acc = blocks[0].astype(F32)
        for b in blocks[1:]:
            acc = acc + b.astype(F32)
        return (acc,)

    return _rowwise(name, fn, [(flat, c, 0, i * r) for i in range(n)], [], [(c, F32)], tb=512)[0]


def _dot_nt(a, b):
    return lax.dot_general(a, b, _DIMS["nt"], preferred_element_type=F32)


def _dot_tn(a, b):
    return lax.dot_general(a, b, _DIMS["tn"], preferred_element_type=F32)


def _dot_nn(a, b):
    return jnp.dot(a, b, preferred_element_type=F32)


def _running_sum(v, tri):
    hi = v.astype(BF16)
    r1 = v - hi.astype(F32)
    mid = r1.astype(BF16)
    lo = (r1 - mid.astype(F32)).astype(BF16)
    return _dot_nn(hi, tri) + _dot_nn(mid, tri) + _dot_nn(lo, tri)


def _tri(tk, rel):
    j = lax.broadcasted_iota(jnp.int32, (tk, tk), 0)
    s = lax.broadcasted_iota(jnp.int32, (tk, tk), 1)
    return {"after": j > s, "upto": j <= s, "before": j < s}[rel].astype(BF16)


def _head_masks():
    lane = lax.broadcasted_iota(jnp.int32, (1, LANES), 1)
    return [(lane // HEAD_DIM) == h for h in range(2)]


def _sb_scores(qh, k, qi, kb, tq, tk, scale):
    z = _dot_nt(qh, k) * scale
    row = lax.broadcasted_iota(jnp.int32, (tq, tk), 0) + qi * tq
    col = lax.broadcasted_iota(jnp.int32, (tq, tk), 1) + kb * tk
    mask = col < row
    e = jnp.exp(-jnp.abs(z))
    log_sig = jnp.minimum(z, 0.0) - jnp.log(1.0 + e)
    log_fail = jnp.where(mask, log_sig - z, 0.0)
    return z, mask, e, log_sig, log_fail


def _sb_fwd(name, p, nhp, tq, tk):
    s = p.shape[0]
    scale = HEAD_DIM ** -0.5
    nq = s // tq

    def body(q_ref, k_ref, v_ref, o_ref, tot_ref):
        qi = pl.program_id(1)
        masks = _head_masks()
        after = _tri(tk, "after")
        nkb = ((qi + 1) * tq + tk - 1) // tk
        q = q_ref[...]
        outs, tots = [], []
        for hm in masks:
            qh = jnp.where(hm, q, 0.0).astype(BF16)

            def step(j, carry, qh=qh):
                later, acc = carry
                kb = nkb - 1 - j
                ks = pl.multiple_of(kb * tk, tk)
                k = k_ref[pl.ds(ks, tk), :].astype(BF16)
                v = v_ref[pl.ds(ks, tk), :].astype(BF16)
                _, mask, _, log_sig, log_fail = _sb_scores(qh, k, qi, kb, tq, tk, scale)
                w = jnp.where(mask, jnp.exp(log_sig + _running_sum(log_fail, after) + later), 0.0)
                acc = acc + _dot_nn(w.astype(BF16), v)
                return later + jnp.sum(log_fail, axis=1, keepdims=True), acc

            tot, acc = lax.fori_loop(0, nkb, step, (jnp.zeros((tq, 1), F32), jnp.zeros((tq, LANES), F32)))
            outs.append(acc)
            tots.append(jnp.broadcast_to(tot, (tq, LANES)))
        o_ref[...] = jnp.where(masks[0], outs[0], outs[1])
        tot_ref[...] = jnp.where(masks[0], tots[0], tots[1])

    blk = pl.BlockSpec((tq, LANES), lambda h, i: (i, h))
    shape = jax.ShapeDtypeStruct((s, nhp * LANES), F32)
    return _pcall(body, name=name, grid=(nhp, nq),
                  in_specs=[blk, pl.BlockSpec((s, LANES), lambda h, i: (0, nhp + h)),
                            pl.BlockSpec((s, LANES), lambda h, i: (0, 2 * nhp + h))],
                  out_specs=[blk, blk], out_shape=[shape, shape], compiler_params=_CP)(p, p, p)


def _sb_bwd(name, p, tot, do, nhp, tq, tk):
    s = p.shape[0]
    scale = HEAD_DIM ** -0.5
    nq = s // tq

    def body(q_ref, k_ref, v_ref, tot_ref, do_ref, dq_ref, dk_ref, dv_ref):
        qi = pl.program_id(1)

        @pl.when(qi == 0)
        def _():
            dk_ref[...] = jnp.zeros_like(dk_ref)
            dv_ref[...] = jnp.zeros_like(dv_ref)

        masks = _head_masks()
        upto, before = _tri(tk, "upto"), _tri(tk, "before")
        nkb = ((qi + 1) * tq + tk - 1) // tk
        q = q_ref[...]
        qb = q.astype(BF16)
        dout = do_ref[...]
        doutb = dout.astype(BF16)
        dqs = []
        for h, hm in enumerate(masks):
            qh = jnp.where(hm, q, 0.0).astype(BF16)
            doh = jnp.where(hm, dout, 0.0).astype(BF16)
            total = tot_ref[:, h * HEAD_DIM:h * HEAD_DIM + 1]

            def step(kb, carry, qh=qh, doh=doh, total=total, hm=hm):
                fail_before, g_before, dq = carry
                ks = pl.multiple_of(kb * tk, tk)
                k = k_ref[pl.ds(ks, tk), :].astype(BF16)
                v = v_ref[pl.ds(ks, tk), :].astype(BF16)
                z, mask, e, log_sig, log_fail = _sb_scores(qh, k, qi, kb, tq, tk, scale)
                later = total - fail_before - _running_sum(log_fail, upto)
                w = jnp.where(mask, jnp.exp(log_sig + later), 0.0)
                g = w * _dot_nt(doh, v)
                g_earlier = g_before + _running_sum(g, before)
                inv = 1.0 / (1.0 + e)
                sig = jnp.where(z >= 0.0, 1.0, e) * inv
                sig_neg = jnp.where(z >= 0.0, e, 1.0) * inv
                dz = jnp.where(mask, g * sig_neg - sig * g_earlier, 0.0) * scale
                dzb = dz.astype(BF16)
                dq = dq + _dot_nn(dzb, k)
                dk_ref[pl.ds(ks, tk), :] += jnp.where(hm, _dot_tn(dzb, qb), 0.0)
                dv_ref[pl.ds(ks, tk), :] += jnp.where(hm, _dot_tn(w.astype(BF16), doutb), 0.0)
                return (fail_before + jnp.sum(log_fail, axis=1, keepdims=True),
                        g_before + jnp.sum(g, axis=1, keepdims=True), dq)

            zero = jnp.zeros((tq, 1), F32)
            _, _, dq = lax.fori_loop(0, nkb, step, (zero, zero, jnp.zeros((tq, LANES), F32)))
            dqs.append(dq)
        dq_ref[...] = jnp.where(masks[0], dqs[0], dqs[1])

    blk = pl.BlockSpec((tq, LANES), lambda h, i: (i, h))
    full = pl.BlockSpec((s, LANES), lambda h, i: (0, h))
    shape = jax.ShapeDtypeStruct((s, nhp * LANES), F32)
    return _pcall(body, name=name, grid=(nhp, nq),
                  in_specs=[blk, pl.BlockSpec((s, LANES), lambda h, i: (0, nhp + h)),
                            pl.BlockSpec((s, LANES), lambda h, i: (0, 2 * nhp + h)), blk, blk],
                  out_specs=[blk, full, full], out_shape=[shape, shape, shape],
                  compiler_params=_CP)(p, p, p, tot, do)


def _mla_mask(qi, kb, tq, tk):
    row = lax.broadcasted_iota(jnp.int32, (tq, tk), 0) + qi * tq
    col = lax.broadcasted_iota(jnp.int32, (tq, tk), 1) + kb * tk
    return col <= row


def _mla_fwd(name, q, k, v, tq, tk):
    s = q.shape[0]
    nhp = v.shape[1] // LANES
    scale = MLA_QK_DIM ** -0.5
    nq = s // tq

    def body(q_ref, k_ref, v_ref, o_ref, lse_ref):
        qi = pl.program_id(1)
        masks = _head_masks()
        nkb = ((qi + 1) * tq + tk - 1) // tk
        outs, lses = [], []
        for h in range(2):
            qh = q_ref[:, h * LANES:(h + 1) * LANES]

            def step(kb, carry, qh=qh, h=h):
                m, l, acc = carry
                ks = pl.multiple_of(kb * tk, tk)
                kh = k_ref[pl.ds(ks, tk), h * LANES:(h + 1) * LANES]
                vv = v_ref[pl.ds(ks, tk), :]
                z = jnp.where(_mla_mask(qi, kb, tq, tk), _dot_nt(qh, kh) * scale, -1e30)
                m_new = jnp.maximum(m, jnp.max(z, axis=1, keepdims=True))
                a = jnp.exp(m - m_new)
                pr = jnp.exp(z - m_new)
                return m_new, a * l + jnp.sum(pr, axis=1, keepdims=True), a * acc + _dot_nn(pr.astype(BF16), vv)

            m, l, acc = lax.fori_loop(0, nkb, step, (jnp.full((tq, 1), -1e30, F32), jnp.zeros((tq, 1), F32),
                                                     jnp.zeros((tq, LANES), F32)))
            outs.append(acc / l)
            lses.append(jnp.broadcast_to(m + jnp.log(l), (tq, LANES)))
        o_ref[...] = jnp.where(masks[0], outs[0], outs[1])
        lse_ref[...] = jnp.where(masks[0], lses[0], lses[1])

    shape = jax.ShapeDtypeStruct((s, nhp * LANES), F32)
    blk = pl.BlockSpec((tq, LANES), lambda h, i: (i, h))
    return _pcall(body, name=name, grid=(nhp, nq),
                  in_specs=[pl.BlockSpec((tq, 2 * LANES), lambda h, i: (i, h)),
                            pl.BlockSpec((s, 2 * LANES), lambda h, i: (0, h)),
                            pl.BlockSpec((s, LANES), lambda h, i: (0, h))],
                  out_specs=[blk, blk], out_shape=[shape, shape], compiler_params=_CP)(q, k, v)


def _mla_bwd(name, q, k, v, o, lse, do, tq, tk):
    s = q.shape[0]
    nhp = v.shape[1] // LANES
    scale = MLA_QK_DIM ** -0.5
    nq = s // tq

    def body(q_ref, k_ref, v_ref, o_ref, lse_ref, do_ref, dq_ref, dk_ref, dv_ref):
        qi = pl.program_id(1)

        @pl.when(qi == 0)
        def _():
            dk_ref[...] = jnp.zeros_like(dk_ref)
            dv_ref[...] = jnp.zeros_like(dv_ref)

        masks = _head_masks()
        nkb = ((qi + 1) * tq + tk - 1) // tk
        dout = do_ref[...]
        doutb = dout.astype(BF16)
        prod = dout * o_ref[...]
        for h in range(2):
            hm = masks[h]
            qh = q_ref[:, h * LANES:(h + 1) * LANES]
            doh = jnp.where(hm, dout, 0.0).astype(BF16)
            total = jnp.sum(jnp.where(hm, prod, 0.0), axis=1, keepdims=True)
            lse_h = lse_ref[:, h * HEAD_DIM:h * HEAD_DIM + 1]

            def step(kb, dq, qh=qh, doh=doh, total=total, lse_h=lse_h, hm=hm, h=h):
                ks = pl.multiple_of(kb * tk, tk)
                kh = k_ref[pl.ds(ks, tk), h * LANES:(h + 1) * LANES]
                vv = v_ref[pl.ds(ks, tk), :]
                pr = jnp.where(_mla_mask(qi, kb, tq, tk), jnp.exp(_dot_nt(qh, kh) * scale - lse_h), 0.0)
                ds = (pr * (_dot_nt(doh, vv) - total) * scale).astype(BF16)
                dk_ref[pl.ds(ks, tk), h * LANES:(h + 1) * LANES] += _dot_tn(ds, qh)
                dv_ref[pl.ds(ks, tk), :] += jnp.where(hm, _dot_tn(pr.astype(BF16), doutb), 0.0)
                return dq + _dot_nn(ds, kh)

            dq_ref[:, h * LANES:(h + 1) * LANES] = lax.fori_loop(0, nkb, step, jnp.zeros((tq, LANES), F32))

    blk = pl.BlockSpec((tq, LANES), lambda h, i: (i, h))
    blk2 = pl.BlockSpec((tq, 2 * LANES), lambda h, i: (i, h))
    full = pl.BlockSpec((s, LANES), lambda h, i: (0, h))
    full2 = pl.BlockSpec((s, 2 * LANES), lambda h, i: (0, h))
    return _pcall(body, name=name, grid=(nhp, nq), in_specs=[blk2, full2, full, blk, blk, blk],
                  out_specs=[blk2, full2, full],
                  out_shape=[jax.ShapeDtypeStruct(q.shape, F32), jax.ShapeDtypeStruct(k.shape, F32),
                             jax.ShapeDtypeStruct(v.shape, F32)], compiler_params=_CP)(q, k, v, o, lse, do)


def _norm_parts(x):
    r = lax.rsqrt(jnp.mean(x * x, axis=-1, keepdims=True) + NORM_EPS)
    return r, x * r


def _rmsmod_fwd(x, g, sc, sh):
    _, xh = _norm_parts(x)
    return ((xh * g) * (1.0 + sc) + sh,)


def _rmsmod_bwd(dh, x, dres, g, sc):
    r, xh = _norm_parts(x)
    dy = dh * (1.0 + sc)
    dxh = dy * g
    dx = r * (dxh - xh * jnp.mean(dxh * xh, axis=-1, keepdims=True)) + dres
    return dx, dh, dh * (xh * g), dy * xh


def _rms_bwd_plain(dh, x, g):
    r, xh = _norm_parts(x)
    dxh = dh * g
    return r * (dxh - xh * jnp.mean(dxh * xh, axis=-1, keepdims=True)), dh * xh


def _cat(parts):
    return jnp.concatenate(parts, axis=1)


def _swap_halves(a):
    half = a.shape[-1] // 2
    return jnp.concatenate([a[..., half:], a[..., :half]], axis=-1)


def _adamw_fn(w, g, m, v):
    m = ADAM_B1 * m + (1.0 - ADAM_B1) * g
    v = ADAM_B2 * v + (1.0 - ADAM_B2) * jnp.square(g)
    m_hat = m / (1.0 - ADAM_B1 ** ADAM_STEP)
    v_hat = v / (1.0 - ADAM_B2 ** ADAM_STEP)
    delta = -ADAM_LR * (m_hat / (jnp.sqrt(v_hat) + ADAM_EPS) + ADAM_WD * w)
    return delta, m, v


def _adamw(name, w, g, m, v):
    shape = w.shape
    width = shape[-1]
    flat = [t.reshape(-1, width) for t in (w, g, m, v)]
    res = _rowwise(name, _adamw_fn, flat, [], [(width, F32)] * 3)
    return [t.reshape(shape) for t in res]


def kernel(x, c, positions, w_ada, b_ada, g_mix_norm, w_in, g_q_lat, w_q_up, g_kv_lat, w_kv_up, w_sb_out, w_mla_out, w_mix_out, g_mlp_norm, w_up, w_down, g_final, loss_target, m_w_ada, m_b_ada, m_g_mix_norm, m_w_in, m_g_q_lat, m_w_q_up, m_g_kv_lat, m_w_kv_up, m_w_sb_out, m_w_mla_out, m_w_mix_out, m_g_mlp_norm, m_w_up, m_w_down, m_g_final, v_w_ada, v_b_ada, v_g_mix_norm, v_w_in, v_g_q_lat, v_w_q_up, v_g_kv_lat, v_w_kv_up, v_w_sb_out, v_w_mla_out, v_w_mix_out, v_g_mlp_norm, v_w_up, v_w_down, v_g_final):
    seq, d = x.shape[1], x.shape[2]
    depth = w_ada.shape[0]
    qr, kvr = g_q_lat.shape[1], g_kv_lat.shape[1]
    sbw, mlaw = w_sb_out.shape[1], w_mla_out.shape[1]
    nh = mlaw // HEAD_DIM
    nhp_sb = sbw // LANES
    dff = w_up.shape[2] * NDEV
    ada_n = w_ada.shape[2]
    gb = min(512, d)
    tq = tk = min(128, seq)
    me = 4 * lax.axis_index("x") + 2 * lax.axis_index("y") + lax.axis_index("c")

    o_qlat = _roundup(3 * sbw, qr)
    o_kvlat = _roundup(o_qlat + qr, kvr)
    o_rope = _roundup(o_kvlat + kvr, 2 * LANES)
    o_gate = _roundup(o_rope + 2 * LANES, gb)
    wp = o_gate + 2 * d

    c_all = _exchange("ag_c", c.reshape(d // LANES, LANES), False).reshape(NDEV, d)
    c_act = _rowwise("silu_c", lambda t: (t * (1.0 / (1.0 + jnp.exp(-t))),), [c_all], [], [(d, F32)])[0]
    parts = jnp.stack([_matmul("ada_fwd", c_act, w_ada[l], "nn") for l in range(depth)])
    parts_all = _exchange("ag_mod", parts.reshape(-1, LANES), False).reshape(NDEV, depth, NDEV, ada_n)
    mine = jnp.transpose(lax.dynamic_index_in_dim(parts_all, me, axis=2, keepdims=False), (1, 0, 2))
    mod = _rowwise("mod_bias", lambda a, b: (a + b,), [mine.reshape(depth, NDEV * ada_n), b_ada], [],
                   [(6 * d, F32)])[0]
    mods = [[mod[l:l + 1, i * d:(i + 1) * d] for i in range(6)] for l in range(depth)]

    col_sharded = [w_in, w_q_up, w_kv_up, w_sb_out, w_mla_out, w_up]
    row_sharded = [w_mix_out, w_down]
    big = [w_in, w_q_up, w_kv_up, w_sb_out, w_mla_out, w_mix_out, w_up, w_down]
    sizes = [w.shape[1] * w.shape[2] for w in big]
    layer_len = _roundup(sum(sizes), PACK_ALIGN)
    layer_rows = layer_len // LANES
    pad = layer_len - sum(sizes)

    def pack_layer(parts_flat, dtype):
        lead = parts_flat[0].shape[:-1]
        cat = jnp.concatenate(parts_flat + [jnp.zeros(lead + (pad,), parts_flat[0].dtype)], axis=-1)
        return cat.astype(dtype).reshape(lead + (layer_rows, LANES))

    packed = jnp.concatenate([pack_layer([w[l].reshape(-1) for w in big], BF16) for l in range(depth)], axis=0)
    gathered = _exchange("ag_weights", packed, False)

    def unpack_layer(flat8):
        out, off = [], 0
        for w, size in zip(big, sizes):
            seg = flat8[:, off:off + size]
            off += size
            rows, cols = w.shape[1], w.shape[2]
            if any(w is t for t in row_sharded):
                out.append(seg.reshape(NDEV * rows, cols))
            else:
                out.append(jnp.transpose(seg.reshape(NDEV, rows, cols), (1, 0, 2)).reshape(rows, NDEV * cols))
        return out

    def derive(full):
        wi, wq, wkv, wsb, wmla, wmix, wu, wd = full
        dt = wi.dtype
        z = lambda r, n: jnp.zeros((r, n), dt)
        o = 3 * sbw
        kr = wi[:, o + qr + kvr:o + qr + kvr + ROPE_DIM]
        g0 = o + qr + kvr + ROPE_DIM
        w_in_pad = _cat([wi[:, :o], z(d, o_qlat - o), wi[:, o:o + qr], z(d, o_kvlat - o_qlat - qr),
                         wi[:, o + qr:o + qr + kvr], z(d, o_rope - o_kvlat - kvr),
                         z(d, HEAD_DIM), kr, z(d, LANES - MLA_QK_DIM),
                         z(d, HEAD_DIM), _swap_halves(kr), z(d, LANES - MLA_QK_DIM),
                         z(d, o_gate - o_rope - 2 * LANES), wi[:, g0:]])
        wq3 = wq.reshape(qr, nh, MLA_QK_DIM)
        z3 = lambda n: jnp.zeros((qr, nh, n), dt)
        rope_w = wq3[:, :, HEAD_DIM:]
        wq_a = jnp.concatenate([wq3[:, :, :HEAD_DIM], rope_w, z3(LANES - MLA_QK_DIM)], axis=2).reshape(qr, nh * LANES)
        wq_b = jnp.concatenate([z3(HEAD_DIM), _swap_halves(rope_w), z3(LANES - MLA_QK_DIM)], axis=2).reshape(qr, nh * LANES)
        wkv3 = wkv.reshape(kvr, nh, 2 * HEAD_DIM)
        wk = jnp.concatenate([wkv3[:, :, :HEAD_DIM], jnp.zeros((kvr, nh, HEAD_DIM), dt)], axis=2).reshape(kvr, nh * LANES)
        wv = wkv3[:, :, HEAD_DIM:].reshape(kvr, nh * HEAD_DIM)
        return dict(w_in=w_in_pad, w_q=_cat([wq_a, wq_b]), w_kv=_cat([wk, wv]), w_sb=wsb, w_mla=wmla,
                    w_mix=wmix, w_up=wu, w_down=wd)

    def fold(gr):
        gi, gq, gkv = gr["w_in"], gr["w_q"], gr["w_kv"]
        o = 3 * sbw
        ra = gi[:, o_rope + HEAD_DIM:o_rope + MLA_QK_DIM]
        rb = gi[:, o_rope + LANES + HEAD_DIM:o_rope + LANES + MLA_QK_DIM]
        g_in = _cat([gi[:, :o], gi[:, o_qlat:o_qlat + qr], gi[:, o_kvlat:o_kvlat + kvr], ra + _swap_halves(rb),
                     gi[:, o_gate:]])
        ga = gq[:, :nh * LANES].reshape(qr, nh, LANES)
        gb_ = gq[:, nh * LANES:].reshape(qr, nh, LANES)
        g_q = jnp.concatenate([ga[:, :, :HEAD_DIM], ga[:, :, HEAD_DIM:MLA_QK_DIM]
                               + _swap_halves(gb_[:, :, HEAD_DIM:MLA_QK_DIM])], axis=2).reshape(qr, nh * MLA_QK_DIM)
        gk = gkv[:, :nh * LANES].reshape(kvr, nh, LANES)[:, :, :HEAD_DIM]
        gv = gkv[:, nh * LANES:].reshape(kvr, nh, HEAD_DIM)
        g_kv = jnp.concatenate([gk, gv], axis=2).reshape(kvr, nh * 2 * HEAD_DIM)
        return [g_in, g_q, g_kv, gr["w_sb"], gr["w_mla"], gr["w_mix"], gr["w_up"], gr["w_down"]]

    weights = [derive(unpack_layer(gathered[:, l * layer_rows:(l + 1) * layer_rows].reshape(NDEV, layer_len)))
               for l in range(depth)]

    inv_freq = 1.0 / (ROPE_THETA ** (jnp.arange(0, ROPE_DIM, 2, dtype=F32) / ROPE_DIM))
    ang = positions[0].astype(F32)[:, None] * inv_freq
    cos, sin = jnp.cos(ang), jnp.sin(ang)
    tail = jnp.zeros((seq, LANES - MLA_QK_DIM), F32)
    rope_c = _cat([jnp.ones((seq, HEAD_DIM), F32), cos, cos, tail])
    rope_s = _cat([jnp.zeros((seq, HEAD_DIM), F32), -sin, sin, tail])
    zero_vec = lambda n: jnp.zeros((1, n), F32)

    def rope_fwd(q2, kvs, pd, tc, ts):
        c8, s8 = _cat([tc] * nh), _cat([ts] * nh)
        qf = q2[:, :nh * LANES] * c8 + q2[:, nh * LANES:] * s8
        kpe = pd[:, :LANES] * tc + pd[:, LANES:] * ts
        return qf, kvs[:, :nh * LANES] + _cat([kpe] * nh), kvs[:, nh * LANES:]

    def rope_bwd(dq, dk, dv, tc, ts):
        c8, s8 = _cat([tc] * nh), _cat([ts] * nh)
        dks = dk[:, :LANES]
        for h in range(1, nh):
            dks = dks + dk[:, h * LANES:(h + 1) * LANES]
        return _cat([dq * c8, dq * s8]), _cat([dk, dv]), _cat([dks * tc, dks * ts])

    def merge_fwd(*a):
        ng = d // gb
        gs, gm, osb, omla = _cat(a[:ng]), _cat(a[ng:2 * ng]), a[2 * ng], a[2 * ng + 1]
        return (osb / (1.0 + jnp.exp(-gs)) + omla / (1.0 + jnp.exp(-gm)),)

    def merge_bwd(*a):
        ng = d // gb
        gs, gm, osb, omla, dm = _cat(a[:ng]), _cat(a[ng:2 * ng]), a[2 * ng], a[2 * ng + 1], a[2 * ng + 2]
        ss, sm = 1.0 / (1.0 + jnp.exp(-gs)), 1.0 / (1.0 + jnp.exp(-gm))
        return ss * dm, sm * dm, _cat([dm * osb * ss * (1.0 - ss), dm * omla * sm * (1.0 - sm)])

    def gate_cols(p):
        ng = d // gb
        return [(p, gb, o_gate // gb + i, 0) for i in range(2 * ng)]

    xs = x[0]
    saved = []
    for l in range(depth):
        w = weights[l]
        sh1, sc1, g1, sh2, sc2, g2 = mods[l]
        h1 = _rowwise("norm1", _rmsmod_fwd, [xs], [g_mix_norm[l:l + 1], sc1, sh1], [(d, BF16)])[0]
        p = _matmul("in_proj", h1, w["w_in"], "nn")
        o_sb, tot_sb = _sb_fwd("sb_fwd", p, nhp_sb, tq, tk)
        y_sb = _matmul("sb_out", o_sb, w["w_sb"], "nn")
        qn = _rowwise("norm_q", _rmsmod_fwd, [(p, qr, o_qlat // qr, 0)],
                      [g_q_lat[l:l + 1], zero_vec(qr), zero_vec(qr)], [(qr, BF16)])[0]
        kvn = _rowwise("norm_kv", _rmsmod_fwd, [(p, kvr, o_kvlat // kvr, 0)],
                       [g_kv_lat[l:l + 1], zero_vec(kvr), zero_vec(kvr)], [(kvr, BF16)])[0]
        q2 = _matmul("q_up", qn, w["w_q"], "nn")
        kvs = _matmul("kv_up", kvn, w["w_kv"], "nn")
        qf, kf, vf = _rowwise("rope_fwd", rope_fwd, [q2, kvs, (p, 2 * LANES, o_rope // (2 * LANES), 0), rope_c, rope_s],
                              [], [(nh * LANES, BF16), (nh * LANES, BF16), (mlaw, BF16)])
        o_mla, lse = _mla_fwd("mla_fwd", qf, kf, vf, tq, tk)
        y_mla = _matmul("mla_out", o_mla, w["w_mla"], "nn")
        merged = _rowwise("merge_fwd", merge_fwd, gate_cols(p) + [y_sb, y_mla], [], [(d, BF16)])[0]
        y1 = _matmul("mix_out", merged, w["w_mix"], "nn")
        x_mid = _rowwise("resid1", lambda a, b, g: (a + g * b,), [xs, y1], [g1], [(d, F32)])[0]
        h2 = _rowwise("norm2", _rmsmod_fwd, [x_mid], [g_mlp_norm[l:l + 1], sc2, sh2], [(d, BF16)])[0]
        u = _matmul("mlp_up", h2, w["w_up"], "nn")
        act = _rowwise("relu2", lambda t: (jnp.square(jnp.maximum(t, 0.0)),), [u], [], [(dff, BF16)])[0]
        y2 = _matmul("mlp_down", act, w["w_down"], "nn")
        x_out = _rowwise("resid2", lambda a, b, g: (a + g * b,), [x_mid, y2], [g2], [(d, F32)])[0]
        saved.append(dict(x=xs, h1=h1, p=p, o_sb=o_sb, tot_sb=tot_sb, y_sb=y_sb, qn=qn, kvn=kvn, qf=qf, kf=kf, vf=vf, o_mla=o_mla,
                          lse=lse, y_mla=y_mla, merged=merged, y1=y1, x_mid=x_mid, h2=h2, u=u, act=act, y2=y2))
        xs = x_out

    def final_fn(xv, tv, g):
        r, xh = _norm_parts(xv)
        diff = xh * g - tv
        dy = diff * (1.0 / d)
        dxh = dy * g
        dx = r * (dxh - xh * jnp.mean(dxh * xh, axis=-1, keepdims=True))
        return dx, diff * diff, dy * xh

    dx, sq, dg_final = _rowwise("loss_head", final_fn, [xs, loss_target[0]], [g_final.reshape(1, d)],
                                [(d, F32)], reds=[d, d])
    loss = lax.psum(0.5 * jnp.sum(sq) / d, ("x", "y", "c"))

    dmods, small, big_grads = [None] * depth, [None] * depth, [None] * depth
    for l in reversed(range(depth)):
        w, sv = weights[l], saved[l]
        sh1, sc1, g1, sh2, sc2, g2 = mods[l]
        gr = {}
        dy2, dgate2 = _rowwise("gate2_bwd", lambda dxv, y, g: (dxv * g, dxv * y), [dx, sv["y2"]], [g2],
                               [(d, BF16)], reds=[d])
        dact = _matmul("mlp_down_dx", dy2, w["w_down"], "nt")
        gr["w_down"] = _matmul("mlp_down_dw", sv["act"], dy2, "tn")
        du = _rowwise("relu2_bwd", lambda da, uv: (da * 2.0 * jnp.maximum(uv, 0.0),), [dact, sv["u"]], [],
                      [(dff, BF16)])[0]
        dh2 = _matmul("mlp_up_dx", du, w["w_up"], "nt")
        gr["w_up"] = _matmul("mlp_up_dw", sv["h2"], du, "tn")
        dx_mid, dsh2, dsc2, dg_mlp = _rowwise("norm2_bwd", _rmsmod_bwd, [dh2, sv["x_mid"], dx],
                                              [g_mlp_norm[l:l + 1], sc2], [(d, F32)], reds=[d, d, d])
        dy1, dgate1 = _rowwise("gate1_bwd", lambda dxv, y, g: (dxv * g, dxv * y), [dx_mid, sv["y1"]], [g1],
                               [(d, BF16)], reds=[d])
        dmerged = _matmul("mix_out_dx", dy1, w["w_mix"], "nt")
        gr["w_mix"] = _matmul("mix_out_dw", sv["merged"], dy1, "tn")
        dy_sb, dy_mla, dgates = _rowwise("merge_bwd", merge_bwd, gate_cols(sv["p"]) + [sv["y_sb"], sv["y_mla"], dmerged],
                                         [], [(d, BF16), (d, BF16), (2 * d, BF16)])
        do_sb = _matmul("sb_out_dx", dy_sb, w["w_sb"], "nt")
        gr["w_sb"] = _matmul("sb_out_dw", sv["o_sb"], dy_sb, "tn")
        do_mla = _matmul("mla_out_dx", dy_mla, w["w_mla"], "nt")
        gr["w_mla"] = _matmul("mla_out_dw", sv["o_mla"], dy_mla, "tn")
        dq_sb, dk_sb, dv_sb = _sb_bwd("sb_bwd", sv["p"], sv["tot_sb"], do_sb, nhp_sb, tq, tk)
        dqf, dkf, dvf = _mla_bwd("mla_bwd", sv["qf"], sv["kf"], sv["vf"], sv["o_mla"], sv["lse"], do_mla, tq, tk)
        dq2, dkvs, drope = _rowwise("rope_bwd", rope_bwd, [dqf, dkf, dvf, rope_c, rope_s], [],
                                    [(2 * nh * LANES, BF16), (nh * LANES + mlaw, BF16), (2 * LANES, BF16)])
        dqn = _matmul("q_up_dx", dq2, w["w_q"], "nt")
        gr["w_q"] = _matmul("q_up_dw", sv["qn"], dq2, "tn")
        dkvn = _matmul("kv_up_dx", dkvs, w["w_kv"], "nt")
        gr["w_kv"] = _matmul("kv_up_dw", sv["kvn"], dkvs, "tn")
        dqlat, dg_q = _rowwise("norm_q_bwd", _rms_bwd_plain, [dqn, (sv["p"], qr, o_qlat // qr, 0)],
                               [g_q_lat[l:l + 1]], [(qr, BF16)], reds=[qr])
        dkvlat, dg_kv = _rowwise("norm_kv_bwd", _rms_bwd_plain, [dkvn, (sv["p"], kvr, o_kvlat // kvr, 0)],
                                 [g_kv_lat[l:l + 1]], [(kvr, BF16)], reds=[kvr])
        zb = lambda n: jnp.zeros((seq, n), BF16)
        dp = _cat([dq_sb.astype(BF16), dk_sb.astype(BF16), dv_sb.astype(BF16), zb(o_qlat - 3 * sbw), dqlat,
                   zb(o_kvlat - o_qlat - qr), dkvlat, zb(o_rope - o_kvlat - kvr), drope,
                   zb(o_gate - o_rope - 2 * LANES), dgates])
        dh1 = _matmul("in_proj_dx", dp, w["w_in"], "nt")
        gr["w_in"] = _matmul("in_proj_dw", sv["h1"], dp, "tn")
        dx, dsh1, dsc1, dg_mix = _rowwise("norm1_bwd", _rmsmod_bwd, [dh1, sv["x"], dx_mid],
                                          [g_mix_norm[l:l + 1], sc1], [(d, F32)], reds=[d, d, d])
        dmods[l] = _cat([dsh1, dsc1, dgate1, dsh2, dsc2, dgate2])
        small[l] = (dg_mix, dg_q, dg_kv, dg_mlp)
        big_grads[l] = fold(gr)

    small_parts = [jnp.concatenate(dmods, axis=0).reshape(-1)]
    small_parts += [jnp.concatenate([small[l][i] for l in range(depth)], axis=0).reshape(-1) for i in range(4)]
    small_parts.append(dg_final.reshape(-1))
    small_sizes = [t.shape[0] for t in small_parts]
    small_len = _roundup(sum(small_sizes), 8 * LANES)
    small_flat = jnp.concatenate(small_parts + [jnp.zeros((small_len - sum(small_sizes),), F32)])
    small_all = _exchange("ag_small", small_flat.reshape(-1, LANES), False)
    small_sum = _sum8("sum_small", small_all).reshape(-1)
    offs = [0]
    for n in small_sizes:
        offs.append(offs[-1] + n)
    g_b_ada = small_sum[offs[0]:offs[1]].reshape(depth, 6 * d)
    g_g_mix = small_sum[offs[1]:offs[2]].reshape(depth, d)
    g_g_q = small_sum[offs[2]:offs[3]].reshape(depth, qr)
    g_g_kv = small_sum[offs[3]:offs[4]].reshape(depth, kvr)
    g_g_mlp = small_sum[offs[4]:offs[5]].reshape(depth, d)
    g_g_final = small_sum[offs[5]:offs[6]].reshape(d)

    dmod_all = small_all.reshape(NDEV, small_len)[:, :depth * 6 * d].reshape(NDEV, depth, 6 * d)
    dmod_mine = lax.dynamic_slice_in_dim(dmod_all, me * ada_n, ada_n, axis=2)
    c_act_t = jnp.transpose(c_act)

    def outer_fn(ct, dm):
        acc = ct[:, 0:1] * dm[0:1, :]
        for b in range(1, NDEV):
            acc = acc + ct[:, b:b + 1] * dm[b:b + 1, :]
        return (acc,)

    g_w_ada = jnp.stack([_rowwise("ada_dw", outer_fn, [c_act_t], [dmod_mine[:, l, :]], [(ada_n, F32)])[0]
                         for l in range(depth)])

    def chunk(gfull, wref):
        rows, cols = wref.shape[1], wref.shape[2]
        if any(wref is t for t in row_sharded):
            return gfull.reshape(NDEV, rows * cols)
        return jnp.transpose(gfull.reshape(rows, NDEV, cols), (1, 0, 2)).reshape(NDEV, rows * cols)

    send = jnp.concatenate([pack_layer([chunk(g, w) for g, w in zip(big_grads[l], big)], BF16)
                            for l in range(depth)], axis=1)
    landed = _exchange("a2a_grads", send, True)
    gsum = _sum8("sum_grads", landed).reshape(depth, layer_len)
    g_big, off = [], 0
    for wref, size in zip(big, sizes):
        g_big.append(gsum[:, off:off + size].reshape(wref.shape))
        off += size
    g_w_in, g_w_q_up, g_w_kv_up, g_w_sb_out, g_w_mla_out, g_w_mix_out, g_w_up, g_w_down = g_big

    names = ["w_ada", "b_ada", "g_mix_norm", "w_in", "g_q_lat", "w_q_up", "g_kv_lat", "w_kv_up", "w_sb_out",
             "w_mla_out", "w_mix_out", "g_mlp_norm", "w_up", "w_down", "g_final"]
    ws = [w_ada, b_ada, g_mix_norm, w_in, g_q_lat, w_q_up, g_kv_lat, w_kv_up, w_sb_out, w_mla_out, w_mix_out,
          g_mlp_norm, w_up, w_down, g_final]
    gs = [g_w_ada, g_b_ada, g_g_mix, g_w_in, g_g_q, g_w_q_up, g_g_kv, g_w_kv_up, g_w_sb_out, g_w_mla_out,
          g_w_mix_out, g_g_mlp, g_w_up, g_w_down, g_g_final]
    ms = [m_w_ada, m_b_ada, m_g_mix_norm, m_w_in, m_g_q_lat, m_w_q_up, m_g_kv_lat, m_w_kv_up, m_w_sb_out,
          m_w_mla_out, m_w_mix_out, m_g_mlp_norm, m_w_up, m_w_down, m_g_final]
    vs = [v_w_ada, v_b_ada, v_g_mix_norm, v_w_in, v_g_q_lat, v_w_q_up, v_g_kv_lat, v_w_kv_up, v_w_sb_out,
          v_w_mla_out, v_w_mix_out, v_g_mlp_norm, v_w_up, v_w_down, v_g_final]
    deltas, new_ms, new_vs = [], [], []
    for name, wt, gt, mt, vt in zip(names, ws, gs, ms, vs):
        shape = wt.shape
        as2d = (lambda t: t.reshape(1, -1)) if wt.ndim == 1 else (lambda t: t)
        dl, nm, nv = _adamw("adamw_" + name, as2d(wt), as2d(gt), as2d(mt), as2d(vt))
        deltas.append(dl.reshape(shape))
        new_ms.append(nm.reshape(shape))
        new_vs.append(nv.reshape(shape))

    return (loss, dx[None], *gs, *deltas, *new_ms, *new_vs)
```

```python
import functools

import jax
import jax.numpy as jnp
from jax import lax
from jax.experimental import pallas as pl
from jax.experimental.pallas import tpu as pltpu

F32 = jnp.float32
BF16 = jnp.bfloat16
NDEV = 8
LANES = 128
HEAD_DIM = 64
ROPE_DIM = 32
MLA_QK_DIM = HEAD_DIM + ROPE_DIM
ROPE_THETA = 10000.0
NORM_EPS = 1e-6
ADAM_LR = 0.001
ADAM_B1 = 0.9
ADAM_B2 = 0.999
ADAM_EPS = 1e-08
ADAM_WD = 0.01
ADAM_STEP = 10
VMEM_LIMIT = 48 * 1024 * 1024


def _pcall(body, **kw):
    return pl.pallas_call(body, **kw)


def _tile(n, pref):
    for t in (512, 384, 256, 128, 64, 32, 16, 8):
        if t <= pref and n % t == 0:
            return t
    return n


def _roundup(n, m):
    return (n + m - 1) // m * m


_CP = pltpu.CompilerParams(vmem_limit_bytes=VMEM_LIMIT)


def _rowwise(name, fn, rows, vecs, outs, reds=(), tb=256):
    rows = [r if isinstance(r, tuple) else (r, r.shape[1], 0, 0) for r in rows]
    nrows = None
    for arr, width, col, roff in rows:
        if roff == 0 and nrows is None:
            nrows = arr.shape[0]
    first_off = [r for r in rows if r[3] != 0]
    if first_off:
        nrows = min(nrows, first_off[0][3])
    tb = _tile(nrows, tb)
    nblk = nrows // tb
    n_in = len(rows) + len(vecs)
    n_out = len(outs)

    def body(*refs):
        vals = [r[...] for r in refs[:n_in]]
        res = fn(*vals)
        if not isinstance(res, (tuple, list)):
            res = (res,)
        for ref, val in zip(refs[n_in:n_in + n_out], res[:n_out]):
            ref[...] = val.astype(ref.dtype)
        for ref, val in zip(refs[n_in + n_out:], res[n_out:]):
            @pl.when(pl.program_id(0) == 0)
            def _(ref=ref):
                ref[...] = jnp.zeros_like(ref)
            ref[...] += jnp.sum(val.astype(F32), axis=0, keepdims=True)

    in_specs = []
    for arr, width, col, roff in rows:
        in_specs.append(pl.BlockSpec((tb, width), functools.partial(
            lambda i, col, rb: (rb + i, col), col=col, rb=roff // tb)))
    for v in vecs:
        in_specs.append(pl.BlockSpec(v.shape, lambda i, nd=v.ndim: (0,) * nd))
    out_specs = [pl.BlockSpec((tb, w), lambda i: (i, 0)) for w, _ in outs]
    out_specs += [pl.BlockSpec((1, w), lambda i: (0, 0)) for w in reds]
    out_shape = [jax.ShapeDtypeStruct((nrows, w), dt) for w, dt in outs]
    out_shape += [jax.ShapeDtypeStruct((1, w), F32) for w in reds]
    res = _pcall(body, name=name, grid=(nblk,), in_specs=in_specs, out_specs=out_specs,
                 out_shape=out_shape, compiler_params=_CP)(*[r[0] for r in rows], *vecs)
    return res


_DIMS = {"nn": (((1,), (0,)), ((), ())), "nt": (((1,), (1,)), ((), ())), "tn": (((0,), (0,)), ((), ()))}


def _matmul(name, a, b, mode, out_dtype=F32):
    if mode == "nn":
        (m, k), n = a.shape, b.shape[1]
    elif mode == "nt":
        (m, k), n = a.shape, b.shape[0]
    else:
        (k, m), n = a.shape, b.shape[1]
    tm, tn = _tile(m, 512), _tile(n, 512)
    dims = _DIMS[mode]

    def body(a_ref, b_ref, o_ref):
        o_ref[...] = lax.dot_general(a_ref[...].astype(BF16), b_ref[...].astype(BF16), dims,
                                     preferred_element_type=F32).astype(o_ref.dtype)

    a_spec = pl.BlockSpec((k, tm), lambda j, i: (0, i)) if mode == "tn" else pl.BlockSpec((tm, k), lambda j, i: (i, 0))
    b_spec = pl.BlockSpec((tn, k), lambda j, i: (j, 0)) if mode == "nt" else pl.BlockSpec((k, tn), lambda j, i: (0, j))
    return _pcall(body, name=name, grid=(n // tn, m // tm), in_specs=[a_spec, b_spec],
                  out_specs=pl.BlockSpec((tm, tn), lambda j, i: (i, j)),
                  out_shape=jax.ShapeDtypeStruct((m, n), out_dtype), compiler_params=_CP)(a, b)


def _exchange(name, srcs, a2a):
    n = len(srcs)

    def body(*refs):
        src_refs, out_refs = refs[:n], refs[n:2 * n]
        send_sems, recv_sems, local_sems = refs[2 * n:]
        x, y, c = lax.axis_index("x"), lax.axis_index("y"), lax.axis_index("c")
        me = 4 * x + 2 * y + c
        copies = []
        for i in range(n):
            local = pltpu.make_async_copy(src_refs[i].at[me] if a2a else src_refs[i], out_refs[i].at[me],
                                          local_sems.at[i])
            local.start()
            copies.append(local)
        for k in range(1, NDEV):
            px = 1 - x if (k >> 2) & 1 else x
            py = 1 - y if (k >> 1) & 1 else y
            pc = 1 - c if k & 1 else c
            for i in range(n):
                cp = pltpu.make_async_remote_copy(
                    src_ref=src_refs[i].at[4 * px + 2 * py + pc] if a2a else src_refs[i],
                    dst_ref=out_refs[i].at[me], send_sem=send_sems.at[k - 1, i], recv_sem=recv_sems.at[k - 1, i],
                    device_id=(px, py, pc), device_id_type=pl.DeviceIdType.MESH)
                cp.start()
                copies.append(cp)
        for cp in copies:
            cp.wait()

    out_shape = [jax.ShapeDtypeStruct(s.shape if a2a else (NDEV,) + s.shape, s.dtype) for s in srcs]
    return _pcall(body, name=name, in_specs=[pl.BlockSpec(memory_space=pl.ANY)] * n,
                  out_specs=[pl.BlockSpec(memory_space=pl.ANY)] * n, out_shape=out_shape,
                  scratch_shapes=[pltpu.SemaphoreType.DMA((NDEV - 1, n)), pltpu.SemaphoreType.DMA((NDEV - 1, n)),
                                  pltpu.SemaphoreType.DMA((n,))])(*srcs)


def _dot_nt(a, b):
    return lax.dot_general(a, b, _DIMS["nt"], preferred_element_type=F32)


def _dot_tn(a, b):
    return lax.dot_general(a, b, _DIMS["tn"], preferred_element_type=F32)


def _dot_nn(a, b):
    return jnp.dot(a, b, preferred_element_type=F32)


def _running_sum(v, tri):
    hi = v.astype(BF16)
    lo = (v - hi.astype(F32)).astype(BF16)
    return _dot_nn(hi, tri) + _dot_nn(lo, tri)


def _tri(tk, rel):
    j = lax.broadcasted_iota(jnp.int32, (tk, tk), 0)
    s = lax.broadcasted_iota(jnp.int32, (tk, tk), 1)
    return {"after": j > s, "upto": j <= s, "before": j < s}[rel].astype(BF16)


def _head_masks():
    lane = lax.broadcasted_iota(jnp.int32, (1, LANES), 1)
    return [(lane // HEAD_DIM) == h for h in range(2)]


def _sb_scores(qh, k, qi, kb, tq, tk, scale):
    z = _dot_nt(qh, k) * scale
    row = lax.broadcasted_iota(jnp.int32, (tq, tk), 0) + qi * tq
    col = lax.broadcasted_iota(jnp.int32, (tq, tk), 1) + kb * tk
    mask = col < row
    e = jnp.exp(-jnp.abs(z))
    log_sig = jnp.minimum(z, 0.0) - jnp.log(1.0 + e)
    log_fail = jnp.where(mask, log_sig - z, 0.0)
    return z, mask, e, log_sig, log_fail


def _sb_fwd(name, p, nhp, tq, tk):
    s = p.shape[0]
    scale = HEAD_DIM ** -0.5
    nq = s // tq

    def body(q_ref, k_ref, v_ref, o_ref, tot_ref):
        qi = pl.program_id(1)
        masks = _head_masks()
        after = _tri(tk, "after")
        nkb = ((qi + 1) * tq + tk - 1) // tk
        q = q_ref[...]
        qhs = [jnp.where(hm, q, 0.0).astype(BF16) for hm in masks]

        def step(j, carry):
            kb = nkb - 1 - j
            ks = pl.multiple_of(kb * tk, tk)
            k = k_ref[pl.ds(ks, tk), :].astype(BF16)
            v = v_ref[pl.ds(ks, tk), :].astype(BF16)
            new = []
            for qh, (later, acc) in zip(qhs, carry):
                _, mask, _, log_sig, log_fail = _sb_scores(qh, k, qi, kb, tq, tk, scale)
                w = jnp.where(mask, jnp.exp(log_sig + _running_sum(log_fail, after) + later), 0.0)
                new.append((later + jnp.sum(log_fail, axis=1, keepdims=True), acc + _dot_nn(w.astype(BF16), v)))
            return tuple(new)

        init = (jnp.zeros((tq, 1), F32), jnp.zeros((tq, LANES), F32))
        (tot0, acc0), (tot1, acc1) = lax.fori_loop(0, nkb, step, (init, init))
        o_ref[...] = jnp.where(masks[0], acc0, acc1)
        tot_ref[...] = jnp.where(masks[0], jnp.broadcast_to(tot0, (tq, LANES)), jnp.broadcast_to(tot1, (tq, LANES)))

    blk = pl.BlockSpec((tq, LANES), lambda h, i: (i, h))
    shape = jax.ShapeDtypeStruct((s, nhp * LANES), F32)
    return _pcall(body, name=name, grid=(nhp, nq),
                  in_specs=[blk, pl.BlockSpec((s, LANES), lambda h, i: (0, nhp + h)),
                            pl.BlockSpec((s, LANES), lambda h, i: (0, 2 * nhp + h))],
                  out_specs=[blk, blk], out_shape=[shape, shape], compiler_params=_CP)(p, p, p)


def _sb_bwd(name, p, tot, do, nhp, tq, tk):
    s = p.shape[0]
    scale = HEAD_DIM ** -0.5
    nq = s // tq

    def body(q_ref, k_ref, v_ref, tot_ref, do_ref, dq_ref, dk_ref, dv_ref):
        qi = pl.program_id(1)

        @pl.when(qi == 0)
        def _():
            dk_ref[...] = jnp.zeros_like(dk_ref)
            dv_ref[...] = jnp.zeros_like(dv_ref)

        masks = _head_masks()
        upto, before = _tri(tk, "upto"), _tri(tk, "before")
        nkb = ((qi + 1) * tq + tk - 1) // tk
        q = q_ref[...]
        qb = q.astype(BF16)
        dout = do_ref[...]
        doutb = dout.astype(BF16)
        qhs = [jnp.where(hm, q, 0.0).astype(BF16) for hm in masks]
        dohs = [jnp.where(hm, dout, 0.0).astype(BF16) for hm in masks]
        totals = [tot_ref[:, h * HEAD_DIM:h * HEAD_DIM + 1] for h in range(2)]

        def step(kb, carry):
            ks = pl.multiple_of(kb * tk, tk)
            k = k_ref[pl.ds(ks, tk), :].astype(BF16)
            v = v_ref[pl.ds(ks, tk), :].astype(BF16)
            new, dks, dvs = [], [], []
            for qh, doh, total, (fail_before, g_before, dq) in zip(qhs, dohs, totals, carry):
                z, mask, e, log_sig, log_fail = _sb_scores(qh, k, qi, kb, tq, tk, scale)
                later = total - fail_before - _running_sum(log_fail, upto)
                w = jnp.where(mask, jnp.exp(log_sig + later), 0.0)
                g = w * _dot_nt(doh, v)
                g_earlier = g_before + _running_sum(g, before)
                inv = 1.0 / (1.0 + e)
                sig = jnp.where(z >= 0.0, 1.0, e) * inv
                sig_neg = jnp.where(z >= 0.0, e, 1.0) * inv
                dzb = (jnp.where(mask, g * sig_neg - sig * g_earlier, 0.0) * scale).astype(BF16)
                dks.append(_dot_tn(dzb, qb))
                dvs.append(_dot_tn(w.astype(BF16), doutb))
                new.append((fail_before + jnp.sum(log_fail, axis=1, keepdims=True),
                            g_before + jnp.sum(g, axis=1, keepdims=True), dq + _dot_nn(dzb, k)))
            dk_ref[pl.ds(ks, tk), :] += jnp.where(masks[0], dks[0], dks[1])
            dv_ref[pl.ds(ks, tk), :] += jnp.where(masks[0], dvs[0], dvs[1])
            return tuple(new)

        zero = jnp.zeros((tq, 1), F32)
        init = (zero, zero, jnp.zeros((tq, LANES), F32))
        (_, _, dq0), (_, _, dq1) = lax.fori_loop(0, nkb, step, (init, init))
        dq_ref[...] = jnp.where(masks[0], dq0, dq1)

    blk = pl.BlockSpec((tq, LANES), lambda h, i: (i, h))
    full = pl.BlockSpec((s, LANES), lambda h, i: (0, h))
    shape = jax.ShapeDtypeStruct((s, nhp * LANES), F32)
    return _pcall(body, name=name, grid=(nhp, nq),
                  in_specs=[blk, pl.BlockSpec((s, LANES), lambda h, i: (0, nhp + h)),
                            pl.BlockSpec((s, LANES), lambda h, i: (0, 2 * nhp + h)), blk, blk],
                  out_specs=[blk, full, full], out_shape=[shape, shape, shape],
                  compiler_params=_CP)(p, p, p, tot, do)


def _mla_mask(qi, kb, tq, tk):
    row = lax.broadcasted_iota(jnp.int32, (tq, tk), 0) + qi * tq
    col = lax.broadcasted_iota(jnp.int32, (tq, tk), 1) + kb * tk
    return col <= row


def _mla_fwd(name, q, k, v, tq, tk):
    s = q.shape[0]
    nhp = v.shape[1] // LANES
    scale = MLA_QK_DIM ** -0.5
    nq = s // tq

    def body(q_ref, k_ref, v_ref, o_ref, lse_ref):
        qi = pl.program_id(1)
        masks = _head_masks()
        nkb = ((qi + 1) * tq + tk - 1) // tk
        qhs = [q_ref[:, h * LANES:(h + 1) * LANES] for h in range(2)]

        def step(kb, carry):
            ks = pl.multiple_of(kb * tk, tk)
            vv = v_ref[pl.ds(ks, tk), :]
            mask = _mla_mask(qi, kb, tq, tk)
            new = []
            for h, (m, l, acc) in enumerate(carry):
                kh = k_ref[pl.ds(ks, tk), h * LANES:(h + 1) * LANES]
                z = jnp.where(mask, _dot_nt(qhs[h], kh) * scale, -1e30)
                m_new = jnp.maximum(m, jnp.max(z, axis=1, keepdims=True))
                a = jnp.exp(m - m_new)
                pr = jnp.exp(z - m_new)
                new.append((m_new, a * l + jnp.sum(pr, axis=1, keepdims=True), a * acc + _dot_nn(pr.astype(BF16), vv)))
            return tuple(new)

        init = (jnp.full((tq, 1), -1e30, F32), jnp.zeros((tq, 1), F32), jnp.zeros((tq, LANES), F32))
        (m0, l0, acc0), (m1, l1, acc1) = lax.fori_loop(0, nkb, step, (init, init))
        o_ref[...] = jnp.where(masks[0], acc0 / l0, acc1 / l1)
        lse_ref[...] = jnp.where(masks[0], jnp.broadcast_to(m0 + jnp.log(l0), (tq, LANES)),
                                 jnp.broadcast_to(m1 + jnp.log(l1), (tq, LANES)))

    shape = jax.ShapeDtypeStruct((s, nhp * LANES), F32)
    blk = pl.BlockSpec((tq, LANES), lambda h, i: (i, h))
    return _pcall(body, name=name, grid=(nhp, nq),
                  in_specs=[pl.BlockSpec((tq, 2 * LANES), lambda h, i: (i, h)),
                            pl.BlockSpec((s, 2 * LANES), lambda h, i: (0, h)),
                            pl.BlockSpec((s, LANES), lambda h, i: (0, h))],
                  out_specs=[blk, blk], out_shape=[shape, shape], compiler_params=_CP)(q, k, v)


def _mla_bwd(name, q, k, v, o, lse, do, tq, tk):
    s = q.shape[0]
    nhp = v.shape[1] // LANES
    scale = MLA_QK_DIM ** -0.5
    nq = s // tq

    def body(q_ref, k_ref, v_ref, o_ref, lse_ref, do_ref, dq_ref, dk_ref, dv_ref):
        qi = pl.program_id(1)

        @pl.when(qi == 0)
        def _():
            dk_ref[...] = jnp.zeros_like(dk_ref)
            dv_ref[...] = jnp.zeros_like(dv_ref)

        masks = _head_masks()
        nkb = ((qi + 1) * tq + tk - 1) // tk
        dout = do_ref[...]
        doutb = dout.astype(BF16)
        prod = dout * o_ref[...]
        qhs = [q_ref[:, h * LANES:(h + 1) * LANES] for h in range(2)]
        dohs = [jnp.where(hm, dout, 0.0).astype(BF16) for hm in masks]
        totals = [jnp.sum(jnp.where(hm, prod, 0.0), axis=1, keepdims=True) for hm in masks]
        lses = [lse_ref[:, h * HEAD_DIM:h * HEAD_DIM + 1] for h in range(2)]

        def step(kb, carry):
            ks = pl.multiple_of(kb * tk, tk)
            vv = v_ref[pl.ds(ks, tk), :]
            mask = _mla_mask(qi, kb, tq, tk)
            new, dvs = [], []
            for h, dq in enumerate(carry):
                kh = k_ref[pl.ds(ks, tk), h * LANES:(h + 1) * LANES]
                pr = jnp.where(mask, jnp.exp(_dot_nt(qhs[h], kh) * scale - lses[h]), 0.0)
                ds = (pr * (_dot_nt(dohs[h], vv) - totals[h]) * scale).astype(BF16)
                dk_ref[pl.ds(ks, tk), h * LANES:(h + 1) * LANES] += _dot_tn(ds, qhs[h])
                dvs.append(_dot_tn(pr.astype(BF16), doutb))
                new.append(dq + _dot_nn(ds, kh))
            dv_ref[pl.ds(ks, tk), :] += jnp.where(masks[0], dvs[0], dvs[1])
            return tuple(new)

        zero = jnp.zeros((tq, LANES), F32)
        dq0, dq1 = lax.fori_loop(0, nkb, step, (zero, zero))
        dq_ref[:, :LANES] = dq0
        dq_ref[:, LANES:] = dq1

    blk = pl.BlockSpec((tq, LANES), lambda h, i: (i, h))
    blk2 = pl.BlockSpec((tq, 2 * LANES), lambda h, i: (i, h))
    full = pl.BlockSpec((s, LANES), lambda h, i: (0, h))
    full2 = pl.BlockSpec((s, 2 * LANES), lambda h, i: (0, h))
    return _pcall(body, name=name, grid=(nhp, nq), in_specs=[blk2, full2, full, blk, blk, blk],
                  out_specs=[blk2, full2, full],
                  out_shape=[jax.ShapeDtypeStruct(q.shape, F32), jax.ShapeDtypeStruct(k.shape, F32),
                             jax.ShapeDtypeStruct(v.shape, F32)], compiler_params=_CP)(q, k, v, o, lse, do)


def _norm_parts(x):
    r = lax.rsqrt(jnp.mean(x * x, axis=-1, keepdims=True) + NORM_EPS)
    return r, x * r


def _rmsmod_fwd(x, g, sc, sh):
    _, xh = _norm_parts(x)
    return ((xh * g) * (1.0 + sc) + sh,)


def _rmsmod_bwd(dh, x, dres, g, sc):
    r, xh = _norm_parts(x)
    dy = dh * (1.0 + sc)
    dxh = dy * g
    dx = r * (dxh - xh * jnp.mean(dxh * xh, axis=-1, keepdims=True)) + dres
    return dx, dh, dh * (xh * g), dy * xh


def _rms_bwd_plain(dh, x, g):
    r, xh = _norm_parts(x)
    dxh = dh * g
    return r * (dxh - xh * jnp.mean(dxh * xh, axis=-1, keepdims=True)), dh * xh


def _cat(parts):
    return jnp.concatenate(parts, axis=1)


def _swap_halves(a):
    half = a.shape[-1] // 2
    return jnp.concatenate([a[..., half:], a[..., :half]], axis=-1)


def _adamw_fn(w, g, m, v):
    m = ADAM_B1 * m + (1.0 - ADAM_B1) * g
    v = ADAM_B2 * v + (1.0 - ADAM_B2) * jnp.square(g)
    m_hat = m / (1.0 - ADAM_B1 ** ADAM_STEP)
    v_hat = v / (1.0 - ADAM_B2 ** ADAM_STEP)
    delta = -ADAM_LR * (m_hat / (jnp.sqrt(v_hat) + ADAM_EPS) + ADAM_WD * w)
    return delta, m, v


def _adamw(name, w, g, m, v):
    shape = w.shape
    width = shape[-1]
    flat = [t.reshape(-1, width) for t in (w, g, m, v)]
    res = _rowwise(name, _adamw_fn, flat, [], [(width, F32)] * 3)
    return [t.reshape(shape) for t in res]


def _sum_adamw(name, land, w, m, v):
    shape = w.shape
    width = shape[-1]
    rows = w.size // width

    def fn(*blocks):
        g = blocks[0].astype(F32)
        for b in blocks[1:NDEV]:
            g = g + b.astype(F32)
        return (g,) + _adamw_fn(blocks[NDEV], g, blocks[NDEV + 1], blocks[NDEV + 2])

    def fn_whole(wb, mb, vb, lb):
        return fn(*[lb[i] for i in range(NDEV)], wb, mb, vb)

    flat = [t.reshape(rows, width) for t in (w, m, v)]
    if rows % 16 == 0:
        views = [(land.reshape(NDEV * rows, width), width, 0, i * rows) for i in range(NDEV)]
        res = _rowwise(name, fn, views + flat, [], [(width, F32)] * 4)
    else:
        res = _rowwise(name, fn_whole, flat, [land.reshape(NDEV, rows, width)], [(width, F32)] * 4)
    return [t.reshape(shape) for t in res]


def kernel(x, c, positions, w_ada, b_ada, g_mix_norm, w_in, g_q_lat, w_q_up, g_kv_lat, w_kv_up, w_sb_out, w_mla_out, w_mix_out, g_mlp_norm, w_up, w_down, g_final, loss_target, m_w_ada, m_b_ada, m_g_mix_norm, m_w_in, m_g_q_lat, m_w_q_up, m_g_kv_lat, m_w_kv_up, m_w_sb_out, m_w_mla_out, m_w_mix_out, m_g_mlp_norm, m_w_up, m_w_down, m_g_final, v_w_ada, v_b_ada, v_g_mix_norm, v_w_in, v_g_q_lat, v_w_q_up, v_g_kv_lat, v_w_kv_up, v_w_sb_out, v_w_mla_out, v_w_mix_out, v_g_mlp_norm, v_w_up, v_w_down, v_g_final):
    seq, d = x.shape[1], x.shape[2]
    depth = w_ada.shape[0]
    qr, kvr = g_q_lat.shape[1], g_kv_lat.shape[1]
    sbw, mlaw = w_sb_out.shape[1], w_mla_out.shape[1]
    nh = mlaw // HEAD_DIM
    nhp_sb = sbw // LANES
    dff = w_up.shape[2] * NDEV
    ada_n = w_ada.shape[2]
    gb = min(512, d)
    tq, tk = min(256, seq), min(128, seq)
    me = 4 * lax.axis_index("x") + 2 * lax.axis_index("y") + lax.axis_index("c")

    o_qlat = _roundup(3 * sbw, qr)
    o_kvlat = _roundup(o_qlat + qr, kvr)
    o_rope = _roundup(o_kvlat + kvr, 2 * LANES)
    o_gate = _roundup(o_rope + 2 * LANES, gb)
    wp = o_gate + 2 * d

    c_all = _exchange("ag_c", [c.reshape(d // LANES, LANES)], False)[0].reshape(NDEV, d)
    c_act = _rowwise("silu_c", lambda t: (t * (1.0 / (1.0 + jnp.exp(-t))),), [c_all], [], [(d, F32)])[0]
    parts = jnp.stack([_matmul("ada_fwd", c_act, w_ada[l], "nn") for l in range(depth)])
    parts_all = _exchange("ag_mod", [parts], False)[0]
    mine = jnp.transpose(lax.dynamic_index_in_dim(parts_all, me, axis=2, keepdims=False), (1, 0, 2))
    mod = _rowwise("mod_bias", lambda a, b: (a + b,), [mine.reshape(depth, NDEV * ada_n), b_ada], [],
                   [(6 * d, F32)])[0]
    mods = [[mod[l:l + 1, i * d:(i + 1) * d] for i in range(6)] for l in range(depth)]

    big = [w_in, w_q_up, w_kv_up, w_sb_out, w_mla_out, w_mix_out, w_up, w_down]
    row_sharded = [False, False, False, False, False, True, False, True]
    gathered = _exchange("ag_weights", [w.astype(BF16) for w in big], False)

    def unpack_layer(l):
        out = []
        for g, by_rows in zip(gathered, row_sharded):
            _, _, rows, cols = g.shape
            if by_rows:
                out.append(g[:, l].reshape(NDEV * rows, cols))
            else:
                out.append(jnp.transpose(g[:, l], (1, 0, 2)).reshape(rows, NDEV * cols))
        return out

    def derive(full):
        wi, wq, wkv, wsb, wmla, wmix, wu, wd = full
        dt = wi.dtype
        z = lambda r, n: jnp.zeros((r, n), dt)
        o = 3 * sbw
        kr = wi[:, o + qr + kvr:o + qr + kvr + ROPE_DIM]
        g0 = o + qr + kvr + ROPE_DIM
        w_in_pad = _cat([wi[:, :o], z(d, o_qlat - o), wi[:, o:o + qr], z(d, o_kvlat - o_qlat - qr),
                         wi[:, o + qr:o + qr + kvr], z(d, o_rope - o_kvlat - kvr),
                         z(d, HEAD_DIM), kr, z(d, LANES - MLA_QK_DIM),
                         z(d, HEAD_DIM), _swap_halves(kr), z(d, LANES - MLA_QK_DIM),
                         z(d, o_gate - o_rope - 2 * LANES), wi[:, g0:]])
        wq3 = wq.reshape(qr, nh, MLA_QK_DIM)
        z3 = lambda n: jnp.zeros((qr, nh, n), dt)
        rope_w = wq3[:, :, HEAD_DIM:]
        wq_a = jnp.concatenate([wq3[:, :, :HEAD_DIM], rope_w, z3(LANES - MLA_QK_DIM)], axis=2).reshape(qr, nh * LANES)
        wq_b = jnp.concatenate([z3(HEAD_DIM), _swap_halves(rope_w), z3(LANES - MLA_QK_DIM)], axis=2).reshape(qr, nh * LANES)
        wkv3 = wkv.reshape(kvr, nh, 2 * HEAD_DIM)
        wk = jnp.concatenate([wkv3[:, :, :HEAD_DIM], jnp.zeros((kvr, nh, HEAD_DIM), dt)], axis=2).reshape(kvr, nh * LANES)
        wv = wkv3[:, :, HEAD_DIM:].reshape(kvr, nh * HEAD_DIM)
        return dict(w_in=w_in_pad, w_q=_cat([wq_a, wq_b]), w_kv=_cat([wk, wv]), w_sb=wsb, w_mla=wmla,
                    w_mix=wmix, w_up=wu, w_down=wd)

    def fold(gr):
        gi, gq, gkv = gr["w_in"], gr["w_q"], gr["w_kv"]
        o = 3 * sbw
        ra = gi[:, o_rope + HEAD_DIM:o_rope + MLA_QK_DIM]
        rb = gi[:, o_rope + LANES + HEAD_DIM:o_rope + LANES + MLA_QK_DIM]
        g_in = _cat([gi[:, :o], gi[:, o_qlat:o_qlat + qr], gi[:, o_kvlat:o_kvlat + kvr], ra + _swap_halves(rb),
                     gi[:, o_gate:]])
        ga = gq[:, :nh * LANES].reshape(qr, nh, LANES)
        gb_ = gq[:, nh * LANES:].reshape(qr, nh, LANES)
        g_q = jnp.concatenate([ga[:, :, :HEAD_DIM], ga[:, :, HEAD_DIM:MLA_QK_DIM]
                               + _swap_halves(gb_[:, :, HEAD_DIM:MLA_QK_DIM])], axis=2).reshape(qr, nh * MLA_QK_DIM)
        gk = gkv[:, :nh * LANES].reshape(kvr, nh, LANES)[:, :, :HEAD_DIM]
        gv = gkv[:, nh * LANES:].reshape(kvr, nh, HEAD_DIM)
        g_kv = jnp.concatenate([gk, gv], axis=2).reshape(kvr, nh * 2 * HEAD_DIM)
        return [g_in, g_q, g_kv, gr["w_sb"], gr["w_mla"], gr["w_mix"], gr["w_up"], gr["w_down"]]

    weights = [derive(unpack_layer(l)) for l in range(depth)]

    inv_freq = 1.0 / (ROPE_THETA ** (jnp.arange(0, ROPE_DIM, 2, dtype=F32) / ROPE_DIM))
    ang = positions[0].astype(F32)[:, None] * inv_freq
    cos, sin = jnp.cos(ang), jnp.sin(ang)
    tail = jnp.zeros((seq, LANES - MLA_QK_DIM), F32)
    rope_c = _cat([jnp.ones((seq, HEAD_DIM), F32), cos, cos, tail])
    rope_s = _cat([jnp.zeros((seq, HEAD_DIM), F32), -sin, sin, tail])
    zero_vec = lambda n: jnp.zeros((1, n), F32)

    def rope_fwd(q2, kvs, pd, tc, ts):
        c8, s8 = _cat([tc] * nh), _cat([ts] * nh)
        qf = q2[:, :nh * LANES] * c8 + q2[:, nh * LANES:] * s8
        kpe = pd[:, :LANES] * tc + pd[:, LANES:] * ts
        return qf, kvs[:, :nh * LANES] + _cat([kpe] * nh), kvs[:, nh * LANES:]

    def rope_bwd(dq, dk, dv, tc, ts):
        c8, s8 = _cat([tc] * nh), _cat([ts] * nh)
        dks = dk[:, :LANES]
        for h in range(1, nh):
            dks = dks + dk[:, h * LANES:(h + 1) * LANES]
        return _cat([dq * c8, dq * s8]), _cat([dk, dv]), _cat([dks * tc, dks * ts])

    def merge_fwd(*a):
        ng = d // gb
        gs, gm, osb, omla = _cat(a[:ng]), _cat(a[ng:2 * ng]), a[2 * ng], a[2 * ng + 1]
        return (osb / (1.0 + jnp.exp(-gs)) + omla / (1.0 + jnp.exp(-gm)),)

    def merge_bwd(*a):
        ng = d // gb
        gs, gm, osb, omla, dm = _cat(a[:ng]), _cat(a[ng:2 * ng]), a[2 * ng], a[2 * ng + 1], a[2 * ng + 2]
        ss, sm = 1.0 / (1.0 + jnp.exp(-gs)), 1.0 / (1.0 + jnp.exp(-gm))
        return ss * dm, sm * dm, _cat([dm * osb * ss * (1.0 - ss), dm * omla * sm * (1.0 - sm)])

    def gate_cols(p):
        ng = d // gb
        return [(p, gb, o_gate // gb + i, 0) for i in range(2 * ng)]

    xs = x[0]
    saved = []
    for l in range(depth):
        w = weights[l]
        sh1, sc1, g1, sh2, sc2, g2 = mods[l]
        h1 = _rowwise("norm1", _rmsmod_fwd, [xs], [g_mix_norm[l:l + 1], sc1, sh1], [(d, BF16)])[0]
        p = _matmul("in_proj", h1, w["w_in"], "nn")
        o_sb, tot_sb = _sb_fwd("sb_fwd", p, nhp_sb, tq, tk)
        y_sb = _matmul("sb_out", o_sb, w["w_sb"], "nn")
        qn = _rowwise("norm_q", _rmsmod_fwd, [(p, qr, o_qlat // qr, 0)],
                      [g_q_lat[l:l + 1], zero_vec(qr), zero_vec(qr)], [(qr, BF16)])[0]
        kvn = _rowwise("norm_kv", _rmsmod_fwd, [(p, kvr, o_kvlat // kvr, 0)],
                       [g_kv_lat[l:l + 1], zero_vec(kvr), zero_vec(kvr)], [(kvr, BF16)])[0]
        q2 = _matmul("q_up", qn, w["w_q"], "nn")
        kvs = _matmul("kv_up", kvn, w["w_kv"], "nn")
        qf, kf, vf = _rowwise("rope_fwd", rope_fwd, [q2, kvs, (p, 2 * LANES, o_rope // (2 * LANES), 0), rope_c, rope_s],
                              [], [(nh * LANES, BF16), (nh * LANES, BF16), (mlaw, BF16)])
        o_mla, lse = _mla_fwd("mla_fwd", qf, kf, vf, tq, tk)
        y_mla = _matmul("mla_out", o_mla, w["w_mla"], "nn")
        merged = _rowwise("merge_fwd", merge_fwd, gate_cols(p) + [y_sb, y_mla], [], [(d, BF16)])[0]
        y1 = _matmul("mix_out", merged, w["w_mix"], "nn")
        x_mid = _rowwise("resid1", lambda a, b, g: (a + g * b,), [xs, y1], [g1], [(d, F32)])[0]
        h2 = _rowwise("norm2", _rmsmod_fwd, [x_mid], [g_mlp_norm[l:l + 1], sc2, sh2], [(d, BF16)])[0]
        u = _matmul("mlp_up", h2, w["w_up"], "nn")
        act = _rowwise("relu2", lambda t: (jnp.square(jnp.maximum(t, 0.0)),), [u], [], [(dff, BF16)])[0]
        y2 = _matmul("mlp_down", act, w["w_down"], "nn")
        x_out = _rowwise("resid2", lambda a, b, g: (a + g * b,), [x_mid, y2], [g2], [(d, F32)])[0]
        saved.append(dict(x=xs, h1=h1, p=p, o_sb=o_sb, tot_sb=tot_sb, y_sb=y_sb, qn=qn, kvn=kvn, qf=qf, kf=kf, vf=vf, o_mla=o_mla,
                          lse=lse, y_mla=y_mla, merged=merged, y1=y1, x_mid=x_mid, h2=h2, u=u, act=act, y2=y2))
        xs = x_out

    def final_fn(xv, tv, g):
        r, xh = _norm_parts(xv)
        diff = xh * g - tv
        dy = diff * (1.0 / d)
        dxh = dy * g
        dx = r * (dxh - xh * jnp.mean(dxh * xh, axis=-1, keepdims=True))
        return dx, diff * diff, dy * xh

    dx, sq, dg_final = _rowwise("loss_head", final_fn, [xs, loss_target[0]], [g_final.reshape(1, d)],
                                [(d, F32)], reds=[d, d])
    loss = lax.psum(0.5 * jnp.sum(sq) / d, ("x", "y", "c"))

    dmods, small, big_grads = [None] * depth, [None] * depth, [None] * depth
    for l in reversed(range(depth)):
        w, sv = weights[l], saved[l]
        sh1, sc1, g1, sh2, sc2, g2 = mods[l]
        gr = {}
        dy2, dgate2 = _rowwise("gate2_bwd", lambda dxv, y, g: (dxv * g, dxv * y), [dx, sv["y2"]], [g2],
                               [(d, BF16)], reds=[d])
        dact = _matmul("mlp_down_dx", dy2, w["w_down"], "nt")
        gr["w_down"] = _matmul("mlp_down_dw", sv["act"], dy2, "tn")
        du = _rowwise("relu2_bwd", lambda da, uv: (da * 2.0 * jnp.maximum(uv, 0.0),), [dact, sv["u"]], [],
                      [(dff, BF16)])[0]
        dh2 = _matmul("mlp_up_dx", du, w["w_up"], "nt")
        gr["w_up"] = _matmul("mlp_up_dw", sv["h2"], du, "tn")
        dx_mid, dsh2, dsc2, dg_mlp = _rowwise("norm2_bwd", _rmsmod_bwd, [dh2, sv["x_mid"], dx],
                                              [g_mlp_norm[l:l + 1], sc2], [(d, F32)], reds=[d, d, d])
        dy1, dgate1 = _rowwise("gate1_bwd", lambda dxv, y, g: (dxv * g, dxv * y), [dx_mid, sv["y1"]], [g1],
                               [(d, BF16)], reds=[d])
        dmerged = _matmul("mix_out_dx", dy1, w["w_mix"], "nt")
        gr["w_mix"] = _matmul("mix_out_dw", sv["merged"], dy1, "tn")
        dy_sb, dy_mla, dgates = _rowwise("merge_bwd", merge_bwd, gate_cols(sv["p"]) + [sv["y_sb"], sv["y_mla"], dmerged],
                                         [], [(d, BF16), (d, BF16), (2 * d, BF16)])
        do_sb = _matmul("sb_out_dx", dy_sb, w["w_sb"], "nt")
        gr["w_sb"] = _matmul("sb_out_dw", sv["o_sb"], dy_sb, "tn")
        do_mla = _matmul("mla_out_dx", dy_mla, w["w_mla"], "nt")
        gr["w_mla"] = _matmul("mla_out_dw", sv["o_mla"], dy_mla, "tn")
        dq_sb, dk_sb, dv_sb = _sb_bwd("sb_bwd", sv["p"], sv["tot_sb"], do_sb, nhp_sb, tq, tk)
        dqf, dkf, dvf = _mla_bwd("mla_bwd", sv["qf"], sv["kf"], sv["vf"], sv["o_mla"], sv["lse"], do_mla, tq, tk)
        dq2, dkvs, drope = _rowwise("rope_bwd", rope_bwd, [dqf, dkf, dvf, rope_c, rope_s], [],
                                    [(2 * nh * LANES, BF16), (nh * LANES + mlaw, BF16), (2 * LANES, BF16)])
        dqn = _matmul("q_up_dx", dq2, w["w_q"], "nt")
        gr["w_q"] = _matmul("q_up_dw", sv["qn"], dq2, "tn")
        dkvn = _matmul("kv_up_dx", dkvs, w["w_kv"], "nt")
        gr["w_kv"] = _matmul("kv_up_dw", sv["kvn"], dkvs, "tn")
        dqlat, dg_q = _rowwise("norm_q_bwd", _rms_bwd_plain, [dqn, (sv["p"], qr, o_qlat // qr, 0)],
                               [g_q_lat[l:l + 1]], [(qr, BF16)], reds=[qr])
        dkvlat, dg_kv = _rowwise("norm_kv_bwd", _rms_bwd_plain, [dkvn, (sv["p"], kvr, o_kvlat // kvr, 0)],
                                 [g_kv_lat[l:l + 1]], [(kvr, BF16)], reds=[kvr])
        zb = lambda n: jnp.zeros((seq, n), BF16)
        dp = _cat([dq_sb.astype(BF16), dk_sb.astype(BF16), dv_sb.astype(BF16), zb(o_qlat - 3 * sbw), dqlat,
                   zb(o_kvlat - o_qlat - qr), dkvlat, zb(o_rope - o_kvlat - kvr), drope,
                   zb(o_gate - o_rope - 2 * LANES), dgates])
        dh1 = _matmul("in_proj_dx", dp, w["w_in"], "nt")
        gr["w_in"] = _matmul("in_proj_dw", sv["h1"], dp, "tn")
        dx, dsh1, dsc1, dg_mix = _rowwise("norm1_bwd", _rmsmod_bwd, [dh1, sv["x"], dx_mid],
                                          [g_mix_norm[l:l + 1], sc1], [(d, F32)], reds=[d, d, d])
        dmods[l] = _cat([dsh1, dsc1, dgate1, dsh2, dsc2, dgate2])
        small[l] = (dg_mix, dg_q, dg_kv, dg_mlp)
        big_grads[l] = fold(gr)

    small_parts = [jnp.concatenate(dmods, axis=0)]
    small_parts += [jnp.concatenate([small[l][i] for l in range(depth)], axis=0) for i in range(4)]
    small_parts.append(dg_final)
    small_all = _exchange("ag_small", small_parts, False)

    dmod_mine = lax.dynamic_slice_in_dim(small_all[0], me * ada_n, ada_n, axis=2)
    c_act_t = jnp.transpose(c_act)

    def outer_fn(ct, dm):
        acc = ct[:, 0:1] * dm[0:1, :]
        for b in range(1, NDEV):
            acc = acc + ct[:, b:b + 1] * dm[b:b + 1, :]
        return (acc,)

    g_w_ada = jnp.stack([_rowwise("ada_dw", outer_fn, [c_act_t], [dmod_mine[:, l, :]], [(ada_n, F32)])[0]
                         for l in range(depth)])

    def chunk(gfull, wref, by_rows):
        rows, cols = wref.shape[1], wref.shape[2]
        if by_rows:
            return gfull.reshape(NDEV, rows, cols).astype(BF16)
        return jnp.transpose(gfull.reshape(rows, NDEV, cols), (1, 0, 2)).astype(BF16)

    send = [jnp.stack([chunk(big_grads[l][i], big[i], row_sharded[i]) for l in range(depth)], axis=1)
            for i in range(len(big))]
    landed = _exchange("a2a_grads", send, True)

    moments = dict(
        w_ada=(w_ada, m_w_ada, v_w_ada), b_ada=(b_ada, m_b_ada, v_b_ada),
        g_mix_norm=(g_mix_norm, m_g_mix_norm, v_g_mix_norm), w_in=(w_in, m_w_in, v_w_in),
        g_q_lat=(g_q_lat, m_g_q_lat, v_g_q_lat), w_q_up=(w_q_up, m_w_q_up, v_w_q_up),
        g_kv_lat=(g_kv_lat, m_g_kv_lat, v_g_kv_lat), w_kv_up=(w_kv_up, m_w_kv_up, v_w_kv_up),
        w_sb_out=(w_sb_out, m_w_sb_out, v_w_sb_out), w_mla_out=(w_mla_out, m_w_mla_out, v_w_mla_out),
        w_mix_out=(w_mix_out, m_w_mix_out, v_w_mix_out), g_mlp_norm=(g_mlp_norm, m_g_mlp_norm, v_g_mlp_norm),
        w_up=(w_up, m_w_up, v_w_up), w_down=(w_down, m_w_down, v_w_down),
        g_final=(g_final.reshape(1, d), m_g_final.reshape(1, d), v_g_final.reshape(1, d)))
    lands = dict(b_ada=small_all[0], g_mix_norm=small_all[1], g_q_lat=small_all[2], g_kv_lat=small_all[3],
                 g_mlp_norm=small_all[4], g_final=small_all[5], w_in=landed[0], w_q_up=landed[1],
                 w_kv_up=landed[2], w_sb_out=landed[3], w_mla_out=landed[4], w_mix_out=landed[5], w_up=landed[6],
                 w_down=landed[7])
    gs, deltas, new_ms, new_vs = [], [], [], []
    for name, (wt, mt, vt) in moments.items():
        if name == "w_ada":
            res = [g_w_ada] + _adamw("adamw_" + name, wt, g_w_ada, mt, vt)
        else:
            res = _sum_adamw("adamw_" + name, lands[name], wt, mt, vt)
        if name == "g_final":
            res = [t.reshape(d) for t in res]
        for lst, t in zip((gs, deltas, new_ms, new_vs), res):
            lst.append(t)

    return (loss, dx[None], *gs, *deltas, *new_ms, *new_vs)
```

```python
import functools

import jax
import jax.numpy as jnp
from jax import lax
from jax.experimental import pallas as pl
from jax.experimental.pallas import tpu as pltpu

F32 = jnp.float32
BF16 = jnp.bfloat16
NDEV = 8
LANES = 128
HEAD_DIM = 64
ROPE_DIM = 32
MLA_QK_DIM = HEAD_DIM + ROPE_DIM
ROPE_THETA = 10000.0
NORM_EPS = 1e-6
ADAM_LR = 0.001
ADAM_B1 = 0.9
ADAM_B2 = 0.999
ADAM_EPS = 1e-08
ADAM_WD = 0.01
ADAM_STEP = 10
VMEM_LIMIT = 48 * 1024 * 1024


def _pcall(body, **kw):
    return pl.pallas_call(body, **kw)


def _tile(n, pref):
    for t in (512, 384, 256, 128, 64, 32, 16, 8):
        if t <= pref and n % t == 0:
            return t
    return n


def _roundup(n, m):
    return (n + m - 1) // m * m


_CP = pltpu.CompilerParams(vmem_limit_bytes=VMEM_LIMIT)


def _rowwise(name, fn, rows, vecs, outs, reds=(), tb=256):
    rows = [r if isinstance(r, tuple) else (r, r.shape[1], 0, 0) for r in rows]
    nrows = None
    for arr, width, col, roff in rows:
        if roff == 0 and nrows is None:
            nrows = arr.shape[0]
    first_off = [r for r in rows if r[3] != 0]
    if first_off:
        nrows = min(nrows, first_off[0][3])
    tb = _tile(nrows, tb)
    nblk = nrows // tb
    n_in = len(rows) + len(vecs)
    n_out = len(outs)

    def body(*refs):
        vals = [r[...] for r in refs[:n_in]]
        res = fn(*vals)
        if not isinstance(res, (tuple, list)):
            res = (res,)
        for ref, val in zip(refs[n_in:n_in + n_out], res[:n_out]):
            ref[...] = val.astype(ref.dtype)
        for ref, val in zip(refs[n_in + n_out:], res[n_out:]):
            @pl.when(pl.program_id(0) == 0)
            def _(ref=ref):
                ref[...] = jnp.zeros_like(ref)
            ref[...] += jnp.sum(val.astype(F32), axis=0, keepdims=True)

    in_specs = []
    for arr, width, col, roff in rows:
        in_specs.append(pl.BlockSpec((tb, width), functools.partial(
            lambda i, col, rb: (rb + i, col), col=col, rb=roff // tb)))
    for v in vecs:
        in_specs.append(pl.BlockSpec(v.shape, lambda i, nd=v.ndim: (0,) * nd))
    out_specs = [pl.BlockSpec((tb, w), lambda i: (i, 0)) for w, _ in outs]
    out_specs += [pl.BlockSpec((1, w), lambda i: (0, 0)) for w in reds]
    out_shape = [jax.ShapeDtypeStruct((nrows, w), dt) for w, dt in outs]
    out_shape += [jax.ShapeDtypeStruct((1, w), F32) for w in reds]
    res = _pcall(body, name=name, grid=(nblk,), in_specs=in_specs, out_specs=out_specs,
                 out_shape=out_shape, compiler_params=_CP)(*[r[0] for r in rows], *vecs)
    return res


_DIMS = {"nn": (((1,), (0,)), ((), ())), "nt": (((1,), (1,)), ((), ())), "tn": (((0,), (0,)), ((), ()))}


def _matmul(name, a, b, mode, out_dtype=F32):
    if mode == "nn":
        (m, k), n = a.shape, b.shape[1]
    elif mode == "nt":
        (m, k), n = a.shape, b.shape[0]
    else:
        (k, m), n = a.shape, b.shape[1]
    tm, tn = _tile(m, 512), _tile(n, 512)
    dims = _DIMS[mode]

    def body(a_ref, b_ref, o_ref):
        o_ref[...] = lax.dot_general(a_ref[...].astype(BF16), b_ref[...].astype(BF16), dims,
                                     preferred_element_type=F32).astype(o_ref.dtype)

    a_spec = pl.BlockSpec((k, tm), lambda j, i: (0, i)) if mode == "tn" else pl.BlockSpec((tm, k), lambda j, i: (i, 0))
    b_spec = pl.BlockSpec((tn, k), lambda j, i: (j, 0)) if mode == "nt" else pl.BlockSpec((k, tn), lambda j, i: (0, j))
    return _pcall(body, name=name, grid=(n // tn, m // tm), in_specs=[a_spec, b_spec],
                  out_specs=pl.BlockSpec((tm, tn), lambda j, i: (i, j)),
                  out_shape=jax.ShapeDtypeStruct((m, n), out_dtype), compiler_params=_CP)(a, b)


def _exchange(name, srcs, a2a):
    n = len(srcs)

    def body(*refs):
        src_refs, out_refs = refs[:n], refs[n:2 * n]
        send_sems, recv_sems, local_sems = refs[2 * n:]
        x, y, c = lax.axis_index("x"), lax.axis_index("y"), lax.axis_index("c")
        me = 4 * x + 2 * y + c
        copies = []
        for i in range(n):
            local = pltpu.make_async_copy(src_refs[i].at[me] if a2a else src_refs[i], out_refs[i].at[me],
                                          local_sems.at[i])
            local.start()
            copies.append(local)
        for k in range(1, NDEV):
            px = 1 - x if (k >> 2) & 1 else x
            py = 1 - y if (k >> 1) & 1 else y
            pc = 1 - c if k & 1 else c
            for i in range(n):
                cp = pltpu.make_async_remote_copy(
                    src_ref=src_refs[i].at[4 * px + 2 * py + pc] if a2a else src_refs[i],
                    dst_ref=out_refs[i].at[me], send_sem=send_sems.at[k - 1, i], recv_sem=recv_sems.at[k - 1, i],
                    device_id=(px, py, pc), device_id_type=pl.DeviceIdType.MESH)
                cp.start()
                copies.append(cp)
        for cp in copies:
            cp.wait()

    out_shape = [jax.ShapeDtypeStruct(s.shape if a2a else (NDEV,) + s.shape, s.dtype) for s in srcs]
    return _pcall(body, name=name, in_specs=[pl.BlockSpec(memory_space=pl.ANY)] * n,
                  out_specs=[pl.BlockSpec(memory_space=pl.ANY)] * n, out_shape=out_shape,
                  scratch_shapes=[pltpu.SemaphoreType.DMA((NDEV - 1, n)), pltpu.SemaphoreType.DMA((NDEV - 1, n)),
                                  pltpu.SemaphoreType.DMA((n,))])(*srcs)


def _dot_nt(a, b):
    return lax.dot_general(a, b, _DIMS["nt"], preferred_element_type=F32)


def _dot_tn(a, b):
    return lax.dot_general(a, b, _DIMS["tn"], preferred_element_type=F32)


def _dot_nn(a, b):
    return jnp.dot(a, b, preferred_element_type=F32)


def _tri(tk, rel):
    j = lax.broadcasted_iota(jnp.int32, (tk, tk), 0)
    s = lax.broadcasted_iota(jnp.int32, (tk, tk), 1)
    return {"after": j > s, "upto": j <= s, "before": j < s}[rel].astype(BF16)


def _head_masks():
    lane = lax.broadcasted_iota(jnp.int32, (1, LANES), 1)
    return [(lane // HEAD_DIM) == h for h in range(2)]


ROW_CHUNK = 32


def _by_rows(fn, n_out, *arrays):
    rows = arrays[0].shape[0]
    step = min(ROW_CHUNK, rows)
    outs = [[] for _ in range(n_out)]
    for r in range(0, rows, step):
        for o, val in zip(outs, fn(r, *[a[r:r + step] for a in arrays])):
            o.append(val)
    return [jnp.concatenate(o, axis=0) for o in outs]


def _causal(r0, k0, rows, tk, strict):
    row = lax.broadcasted_iota(jnp.int32, (rows, tk), 0) + r0
    col = lax.broadcasted_iota(jnp.int32, (rows, tk), 1) + k0
    return col < row if strict else col <= row


def _wide(stat, width):
    return stat if width == LANES else jnp.concatenate([stat] * (width // LANES), axis=1)


def _row_sum(v):
    return jnp.broadcast_to(jnp.sum(v, axis=1, keepdims=True), (v.shape[0], LANES))


def _split_bf16(v):
    hi = v.astype(BF16)
    return hi, (v - hi.astype(F32)).astype(BF16)


def _sb_logs(z, scale, mask):
    z = z * scale
    e = jnp.exp(-jnp.abs(z))
    log_sig = jnp.minimum(z, 0.0) - jnp.log(1.0 + e)
    log_fail = log_sig - z
    return z, log_sig, (log_fail if mask is None else jnp.where(mask, log_fail, 0.0))


def _two_loops(n_full, nkb, near_first, step, carry):
    if near_first:
        carry = lax.fori_loop(0, nkb - n_full, lambda j, c: step(nkb - 1 - j, True, c), carry)
        return lax.fori_loop(0, n_full, lambda j, c: step(n_full - 1 - j, False, c), carry)
    carry = lax.fori_loop(0, n_full, lambda j, c: step(j, False, c), carry)
    return lax.fori_loop(n_full, nkb, lambda j, c: step(j, True, c), carry)


def _sb_fwd(name, p, nhp, tq, tk):
    s = p.shape[0]
    scale = HEAD_DIM ** -0.5
    nq = s // tq

    def body(q_ref, k_ref, v_ref, o_ref, tot_ref):
        qi = pl.program_id(1)
        masks = _head_masks()
        after = _tri(tk, "after")
        nkb = ((qi + 1) * tq + tk - 1) // tk
        q = q_ref[...]
        qhs = [jnp.where(hm, q, 0.0).astype(BF16) for hm in masks]

        def step(kb, masked, carry):
            ks = pl.multiple_of(kb * tk, tk)
            k = k_ref[pl.ds(ks, tk), :].astype(BF16)
            v = v_ref[pl.ds(ks, tk), :].astype(BF16)
            mask_of = lambda r, n: _causal(qi * tq + r, ks, n, tk, True) if masked else None
            new = []
            for qh, (later, acc) in zip(qhs, carry):

                def logs(r, zc):
                    _, log_sig, log_fail = _sb_logs(zc, scale, mask_of(r, zc.shape[0]))
                    return (log_sig,) + _split_bf16(log_fail) + (_row_sum(log_fail),)

                log_sig, hi, lo, fail_sum = _by_rows(logs, 4, _dot_nt(qh, k))
                run = _dot_nn(hi, after) + _dot_nn(lo, after)

                def weights(r, lsc, runc, laterc):
                    w = jnp.exp(lsc + runc + _wide(laterc, tk))
                    return ((jnp.where(mask_of(r, w.shape[0]), w, 0.0) if masked else w).astype(BF16),)

                w = _by_rows(weights, 1, log_sig, run, later)[0]
                new.append((later + fail_sum, acc + _dot_nn(w, v)))
            return tuple(new)

        init = (jnp.zeros((tq, LANES), F32), jnp.zeros((tq, LANES), F32))
        (tot0, acc0), (tot1, acc1) = _two_loops((qi * tq) // tk, nkb, True, step, (init, init))
        o_ref[...] = jnp.where(masks[0], acc0, acc1)
        tot_ref[...] = jnp.where(masks[0], tot0, tot1)

    blk = pl.BlockSpec((tq, LANES), lambda h, i: (i, h))
    shape = jax.ShapeDtypeStruct((s, nhp * LANES), F32)
    return _pcall(body, name=name, grid=(nhp, nq),
                  in_specs=[blk, pl.BlockSpec((s, LANES), lambda h, i: (0, nhp + h)),
                            pl.BlockSpec((s, LANES), lambda h, i: (0, 2 * nhp + h))],
                  out_specs=[blk, blk], out_shape=[shape, shape], compiler_params=_CP)(p, p, p)


def _sb_bwd(name, p, tot, do, nhp, tq, tk):
    s = p.shape[0]
    scale = HEAD_DIM ** -0.5
    nq = s // tq

    def body(q_ref, k_ref, v_ref, tot_ref, do_ref, dq_ref, dk_ref, dv_ref):
        qi = pl.program_id(1)

        @pl.when(qi == 0)
        def _():
            dk_ref[...] = jnp.zeros_like(dk_ref)
            dv_ref[...] = jnp.zeros_like(dv_ref)

        masks = _head_masks()
        upto, before = _tri(tk, "upto"), _tri(tk, "before")
        nkb = ((qi + 1) * tq + tk - 1) // tk
        q = q_ref[...]
        qb = q.astype(BF16)
        dout = do_ref[...]
        doutb = dout.astype(BF16)
        qhs = [jnp.where(hm, q, 0.0).astype(BF16) for hm in masks]
        dohs = [jnp.where(hm, dout, 0.0).astype(BF16) for hm in masks]
        tot = tot_ref[...]
        totals = [jnp.broadcast_to(tot[:, h * HEAD_DIM:h * HEAD_DIM + 1], (tq, LANES)) for h in range(2)]

        def step(kb, masked, carry):
            ks = pl.multiple_of(kb * tk, tk)
            k = k_ref[pl.ds(ks, tk), :].astype(BF16)
            v = v_ref[pl.ds(ks, tk), :].astype(BF16)
            mask_of = lambda r, n: _causal(qi * tq + r, ks, n, tk, True) if masked else None
            new, dks, dvs = [], [], []
            for qh, doh, total, (fail_before, g_before, dq) in zip(qhs, dohs, totals, carry):

                def logs(r, zc):
                    _, log_sig, log_fail = _sb_logs(zc, scale, mask_of(r, zc.shape[0]))
                    return (log_sig,) + _split_bf16(log_fail) + (_row_sum(log_fail),)

                z = _dot_nt(qh, k)
                log_sig, hi, lo, fail_sum = _by_rows(logs, 4, z)
                run = _dot_nn(hi, upto) + _dot_nn(lo, upto)

                def weights(r, lsc, runc, basec, dwc):
                    w = jnp.exp(lsc + (_wide(basec, tk) - runc))
                    if masked:
                        w = jnp.where(mask_of(r, w.shape[0]), w, 0.0)
                    g = w * dwc
                    return (w.astype(BF16), g) + _split_bf16(g) + (_row_sum(g),)

                w, g, ghi, glo, g_sum = _by_rows(weights, 5, log_sig, run, total - fail_before, _dot_nt(doh, v))
                g_run = _dot_nn(ghi, before) + _dot_nn(glo, before)

                def dscore(r, gc, lsc, zc, grc, gbc):
                    dz = gc * jnp.exp(lsc - zc * scale) - jnp.exp(lsc) * (_wide(gbc, tk) + grc)
                    if masked:
                        dz = jnp.where(mask_of(r, dz.shape[0]), dz, 0.0)
                    return ((dz * scale).astype(BF16),)

                dzb = _by_rows(dscore, 1, g, log_sig, z, g_run, g_before)[0]
                dks.append(_dot_tn(dzb, qb))
                dvs.append(_dot_tn(w, doutb))
                new.append((fail_before + fail_sum, g_before + g_sum, dq + _dot_nn(dzb, k)))
            dk_ref[pl.ds(ks, tk), :] += jnp.where(masks[0], dks[0], dks[1])
            dv_ref[pl.ds(ks, tk), :] += jnp.where(masks[0], dvs[0], dvs[1])
            return tuple(new)

        zero = jnp.zeros((tq, LANES), F32)
        (_, _, dq0), (_, _, dq1) = _two_loops((qi * tq) // tk, nkb, False, step, ((zero, zero, zero),) * 2)
        dq_ref[...] = jnp.where(masks[0], dq0, dq1)

    blk = pl.BlockSpec((tq, LANES), lambda h, i: (i, h))
    full = pl.BlockSpec((s, LANES), lambda h, i: (0, h))
    shape = jax.ShapeDtypeStruct((s, nhp * LANES), F32)
    return _pcall(body, name=name, grid=(nhp, nq),
                  in_specs=[blk, pl.BlockSpec((s, LANES), lambda h, i: (0, nhp + h)),
                            pl.BlockSpec((s, LANES), lambda h, i: (0, 2 * nhp + h)), blk, blk],
                  out_specs=[blk, full, full], out_shape=[shape, shape, shape],
                  compiler_params=_CP)(p, p, p, tot, do)


def _mla_fwd(name, q, k, v, tq, tk):
    s = q.shape[0]
    nhp = v.shape[1] // LANES
    scale = MLA_QK_DIM ** -0.5
    nq = s // tq

    def body(q_ref, k_ref, v_ref, o_ref, lse_ref):
        qi = pl.program_id(1)
        masks = _head_masks()
        nkb = ((qi + 1) * tq + tk - 1) // tk
        qhs = [q_ref[:, h * LANES:(h + 1) * LANES] for h in range(2)]

        def step(kb, masked, carry):
            ks = pl.multiple_of(kb * tk, tk)
            vv = v_ref[pl.ds(ks, tk), :]
            new = []
            for h, (m, l, acc) in enumerate(carry):
                kh = k_ref[pl.ds(ks, tk), h * LANES:(h + 1) * LANES]

                def soft(r, zc, mc, lc):
                    zc = zc * scale
                    if masked:
                        zc = jnp.where(_causal(qi * tq + r, ks, zc.shape[0], tk, False), zc, -1e30)
                    m_new = jnp.maximum(mc, jnp.max(zc, axis=1, keepdims=True))
                    a = jnp.exp(mc - m_new)
                    pr = jnp.exp(zc - _wide(m_new, tk))
                    return pr.astype(BF16), m_new, a * lc + _row_sum(pr), a

                pr, m_new, l_new, a = _by_rows(soft, 4, _dot_nt(qhs[h], kh), m, l)
                acc = _by_rows(lambda r, ac, aa, pc: (aa * ac + pc,), 1, acc, a, _dot_nn(pr, vv))[0]
                new.append((m_new, l_new, acc))
            return tuple(new)

        init = (jnp.full((tq, LANES), -1e30, F32), jnp.zeros((tq, LANES), F32), jnp.zeros((tq, LANES), F32))
        (m0, l0, acc0), (m1, l1, acc1) = _two_loops((qi * tq) // tk, nkb, False, step, (init, init))
        o_ref[...] = jnp.where(masks[0], acc0 / l0, acc1 / l1)
        lse_ref[...] = jnp.where(masks[0], m0 + jnp.log(l0), m1 + jnp.log(l1))

    shape = jax.ShapeDtypeStruct((s, nhp * LANES), F32)
    blk = pl.BlockSpec((tq, LANES), lambda h, i: (i, h))
    return _pcall(body, name=name, grid=(nhp, nq),
                  in_specs=[pl.BlockSpec((tq, 2 * LANES), lambda h, i: (i, h)),
                            pl.BlockSpec((s, 2 * LANES), lambda h, i: (0, h)),
                            pl.BlockSpec((s, LANES), lambda h, i: (0, h))],
                  out_specs=[blk, blk], out_shape=[shape, shape], compiler_params=_CP)(q, k, v)


def _mla_bwd(name, q, k, v, o, lse, do, tq, tk):
    s = q.shape[0]
    nhp = v.shape[1] // LANES
    scale = MLA_QK_DIM ** -0.5
    nq = s // tq

    def body(q_ref, k_ref, v_ref, o_ref, lse_ref, do_ref, dq_ref, dk_ref, dv_ref):
        qi = pl.program_id(1)

        @pl.when(qi == 0)
        def _():
            dk_ref[...] = jnp.zeros_like(dk_ref)
            dv_ref[...] = jnp.zeros_like(dv_ref)

        masks = _head_masks()
        nkb = ((qi + 1) * tq + tk - 1) // tk
        dout = do_ref[...]
        doutb = dout.astype(BF16)
        prod = dout * o_ref[...]
        qhs = [q_ref[:, h * LANES:(h + 1) * LANES] for h in range(2)]
        dohs = [jnp.where(hm, dout, 0.0).astype(BF16) for hm in masks]
        totals = [_row_sum(jnp.where(hm, prod, 0.0)) for hm in masks]
        lse = lse_ref[...]
        lses = [jnp.broadcast_to(lse[:, h * HEAD_DIM:h * HEAD_DIM + 1], (tq, LANES)) for h in range(2)]

        def step(kb, masked, carry):
            ks = pl.multiple_of(kb * tk, tk)
            vv = v_ref[pl.ds(ks, tk), :]
            new, dvs = [], []
            for h, dq in enumerate(carry):
                kh = k_ref[pl.ds(ks, tk), h * LANES:(h + 1) * LANES]

                def probs(r, zc, dpc, lsec, totc):
                    pr = jnp.exp(zc * scale - _wide(lsec, tk))
                    if masked:
                        pr = jnp.where(_causal(qi * tq + r, ks, pr.shape[0], tk, False), pr, 0.0)
                    return pr.astype(BF16), (pr * (dpc - _wide(totc, tk)) * scale).astype(BF16)

                pr, ds = _by_rows(probs, 2, _dot_nt(qhs[h], kh), _dot_nt(dohs[h], vv), lses[h], totals[h])
                dk_ref[pl.ds(ks, tk), h * LANES:(h + 1) * LANES] += _dot_tn(ds, qhs[h])
                dvs.append(_dot_tn(pr, doutb))
                new.append(dq + _dot_nn(ds, kh))
            dv_ref[pl.ds(ks, tk), :] += jnp.where(masks[0], dvs[0], dvs[1])
            return tuple(new)

        zero = jnp.zeros((tq, LANES), F32)
        dq0, dq1 = _two_loops((qi * tq) // tk, nkb, False, step, (zero, zero))
        dq_ref[:, :LANES] = dq0
        dq_ref[:, LANES:] = dq1

    blk = pl.BlockSpec((tq, LANES), lambda h, i: (i, h))
    blk2 = pl.BlockSpec((tq, 2 * LANES), lambda h, i: (i, h))
    full = pl.BlockSpec((s, LANES), lambda h, i: (0, h))
    full2 = pl.BlockSpec((s, 2 * LANES), lambda h, i: (0, h))
    return _pcall(body, name=name, grid=(nhp, nq), in_specs=[blk2, full2, full, blk, blk, blk],
                  out_specs=[blk2, full2, full],
                  out_shape=[jax.ShapeDtypeStruct(q.shape, F32), jax.ShapeDtypeStruct(k.shape, F32),
                             jax.ShapeDtypeStruct(v.shape, F32)], compiler_params=_CP)(q, k, v, o, lse, do)


def _norm_parts(x):
    r = lax.rsqrt(jnp.mean(x * x, axis=-1, keepdims=True) + NORM_EPS)
    return r, x * r


def _rmsmod_fwd(x, g, sc, sh):
    _, xh = _norm_parts(x)
    return ((xh * g) * (1.0 + sc) + sh,)


def _rmsmod_bwd(dh, x, dres, g, sc):
    r, xh = _norm_parts(x)
    dy = dh * (1.0 + sc)
    dxh = dy * g
    dx = r * (dxh - xh * jnp.mean(dxh * xh, axis=-1, keepdims=True)) + dres
    return dx, dh, dh * (xh * g), dy * xh


def _rms_bwd_plain(dh, x, g):
    r, xh = _norm_parts(x)
    dxh = dh * g
    return r * (dxh - xh * jnp.mean(dxh * xh, axis=-1, keepdims=True)), dh * xh


def _cat(parts):
    return jnp.concatenate(parts, axis=1)


def _swap_halves(a):
    half = a.shape[-1] // 2
    return jnp.concatenate([a[..., half:], a[..., :half]], axis=-1)


def _adamw_fn(w, g, m, v):
    m = ADAM_B1 * m + (1.0 - ADAM_B1) * g
    v = ADAM_B2 * v + (1.0 - ADAM_B2) * jnp.square(g)
    m_hat = m / (1.0 - ADAM_B1 ** ADAM_STEP)
    v_hat = v / (1.0 - ADAM_B2 ** ADAM_STEP)
    delta = -ADAM_LR * (m_hat / (jnp.sqrt(v_hat) + ADAM_EPS) + ADAM_WD * w)
    return delta, m, v


def _adamw(name, w, g, m, v):
    shape = w.shape
    width = shape[-1]
    flat = [t.reshape(-1, width) for t in (w, g, m, v)]
    res = _rowwise(name, _adamw_fn, flat, [], [(width, F32)] * 3)
    return [t.reshape(shape) for t in res]


def _sum_adamw(name, land, w, m, v):
    shape = w.shape
    width = shape[-1]
    rows = w.size // width

    def fn(*blocks):
        g = blocks[0].astype(F32)
        for b in blocks[1:NDEV]:
            g = g + b.astype(F32)
        return (g,) + _adamw_fn(blocks[NDEV], g, blocks[NDEV + 1], blocks[NDEV + 2])

    def fn_whole(wb, mb, vb, lb):
        return fn(*[lb[i] for i in range(NDEV)], wb, mb, vb)

    flat = [t.reshape(rows, width) for t in (w, m, v)]
    if rows % 16 == 0:
        views = [(land.reshape(NDEV * rows, width), width, 0, i * rows) for i in range(NDEV)]
        res = _rowwise(name, fn, views + flat, [], [(width, F32)] * 4)
    else:
        res = _rowwise(name, fn_whole, flat, [land.reshape(NDEV, rows, width)], [(width, F32)] * 4)
    return [t.reshape(shape) for t in res]


def kernel(x, c, positions, w_ada, b_ada, g_mix_norm, w_in, g_q_lat, w_q_up, g_kv_lat, w_kv_up, w_sb_out, w_mla_out, w_mix_out, g_mlp_norm, w_up, w_down, g_final, loss_target, m_w_ada, m_b_ada, m_g_mix_norm, m_w_in, m_g_q_lat, m_w_q_up, m_g_kv_lat, m_w_kv_up, m_w_sb_out, m_w_mla_out, m_w_mix_out, m_g_mlp_norm, m_w_up, m_w_down, m_g_final, v_w_ada, v_b_ada, v_g_mix_norm, v_w_in, v_g_q_lat, v_w_q_up, v_g_kv_lat, v_w_kv_up, v_w_sb_out, v_w_mla_out, v_w_mix_out, v_g_mlp_norm, v_w_up, v_w_down, v_g_final):
    seq, d = x.shape[1], x.shape[2]
    depth = w_ada.shape[0]
    qr, kvr = g_q_lat.shape[1], g_kv_lat.shape[1]
    sbw, mlaw = w_sb_out.shape[1], w_mla_out.shape[1]
    nh = mlaw // HEAD_DIM
    nhp_sb = sbw // LANES
    dff = w_up.shape[2] * NDEV
    ada_n = w_ada.shape[2]
    gb = min(512, d)
    tq, tk = min(256, seq), min(256, seq)
    me = 4 * lax.axis_index("x") + 2 * lax.axis_index("y") + lax.axis_index("c")

    o_qlat = _roundup(3 * sbw, qr)
    o_kvlat = _roundup(o_qlat + qr, kvr)
    o_rope = _roundup(o_kvlat + kvr, 2 * LANES)
    o_gate = _roundup(o_rope + 2 * LANES, gb)
    wp = o_gate + 2 * d

    c_all = _exchange("ag_c", [c.reshape(d // LANES, LANES)], False)[0].reshape(NDEV, d)
    c_act = _rowwise("silu_c", lambda t: (t * (1.0 / (1.0 + jnp.exp(-t))),), [c_all], [], [(d, F32)])[0]
    parts = jnp.stack([_matmul("ada_fwd", c_act, w_ada[l], "nn") for l in range(depth)])
    parts_all = _exchange("ag_mod", [parts], False)[0]
    mine = jnp.transpose(lax.dynamic_index_in_dim(parts_all, me, axis=2, keepdims=False), (1, 0, 2))
    mod = _rowwise("mod_bias", lambda a, b: (a + b,), [mine.reshape(depth, NDEV * ada_n), b_ada], [],
                   [(6 * d, F32)])[0]
    mods = [[mod[l:l + 1, i * d:(i + 1) * d] for i in range(6)] for l in range(depth)]

    big = [w_in, w_q_up, w_kv_up, w_sb_out, w_mla_out, w_mix_out, w_up, w_down]
    row_sharded = [False, False, False, False, False, True, False, True]
    gathered = _exchange("ag_weights", [w.astype(BF16) for w in big], False)

    def unpack_layer(l):
        out = []
        for g, by_rows in zip(gathered, row_sharded):
            _, _, rows, cols = g.shape
            if by_rows:
                out.append(g[:, l].reshape(NDEV * rows, cols))
            else:
                out.append(jnp.transpose(g[:, l], (1, 0, 2)).reshape(rows, NDEV * cols))
        return out

    def derive(full):
        wi, wq, wkv, wsb, wmla, wmix, wu, wd = full
        dt = wi.dtype
        z = lambda r, n: jnp.zeros((r, n), dt)
        o = 3 * sbw
        kr = wi[:, o + qr + kvr:o + qr + kvr + ROPE_DIM]
        g0 = o + qr + kvr + ROPE_DIM
        w_in_pad = _cat([wi[:, :o], z(d, o_qlat - o), wi[:, o:o + qr], z(d, o_kvlat - o_qlat - qr),
                         wi[:, o + qr:o + qr + kvr], z(d, o_rope - o_kvlat - kvr),
                         z(d, HEAD_DIM), kr, z(d, LANES - MLA_QK_DIM),
                         z(d, HEAD_DIM), _swap_halves(kr), z(d, LANES - MLA_QK_DIM),
                         z(d, o_gate - o_rope - 2 * LANES), wi[:, g0:]])
        wq3 = wq.reshape(qr, nh, MLA_QK_DIM)
        z3 = lambda n: jnp.zeros((qr, nh, n), dt)
        rope_w = wq3[:, :, HEAD_DIM:]
        wq_a = jnp.concatenate([wq3[:, :, :HEAD_DIM], rope_w, z3(LANES - MLA_QK_DIM)], axis=2).reshape(qr, nh * LANES)
        wq_b = jnp.concatenate([z3(HEAD_DIM), _swap_halves(rope_w), z3(LANES - MLA_QK_DIM)], axis=2).reshape(qr, nh * LANES)
        wkv3 = wkv.reshape(kvr, nh, 2 * HEAD_DIM)
        wk = jnp.concatenate([wkv3[:, :, :HEAD_DIM], jnp.zeros((kvr, nh, HEAD_DIM), dt)], axis=2).reshape(kvr, nh * LANES)
        wv = wkv3[:, :, HEAD_DIM:].reshape(kvr, nh * HEAD_DIM)
        return dict(w_in=w_in_pad, w_q=_cat([wq_a, wq_b]), w_kv=_cat([wk, wv]), w_sb=wsb, w_mla=wmla,
                    w_mix=wmix, w_up=wu, w_down=wd)

    def fold(gr):
        gi, gq, gkv = gr["w_in"], gr["w_q"], gr["w_kv"]
        o = 3 * sbw
        ra = gi[:, o_rope + HEAD_DIM:o_rope + MLA_QK_DIM]
        rb = gi[:, o_rope + LANES + HEAD_DIM:o_rope + LANES + MLA_QK_DIM]
        g_in = _cat([gi[:, :o], gi[:, o_qlat:o_qlat + qr], gi[:, o_kvlat:o_kvlat + kvr], ra + _swap_halves(rb),
                     gi[:, o_gate:]])
        ga = gq[:, :nh * LANES].reshape(qr, nh, LANES)
        gb_ = gq[:, nh * LANES:].reshape(qr, nh, LANES)
        g_q = jnp.concatenate([ga[:, :, :HEAD_DIM], ga[:, :, HEAD_DIM:MLA_QK_DIM]
                               + _swap_halves(gb_[:, :, HEAD_DIM:MLA_QK_DIM])], axis=2).reshape(qr, nh * MLA_QK_DIM)
        gk = gkv[:, :nh * LANES].reshape(kvr, nh, LANES)[:, :, :HEAD_DIM]
        gv = gkv[:, nh * LANES:].reshape(kvr, nh, HEAD_DIM)
        g_kv = jnp.concatenate([gk, gv], axis=2).reshape(kvr, nh * 2 * HEAD_DIM)
        return [g_in, g_q, g_kv, gr["w_sb"], gr["w_mla"], gr["w_mix"], gr["w_up"], gr["w_down"]]

    weights = [derive(unpack_layer(l)) for l in range(depth)]

    inv_freq = 1.0 / (ROPE_THETA ** (jnp.arange(0, ROPE_DIM, 2, dtype=F32) / ROPE_DIM))
    ang = positions[0].astype(F32)[:, None] * inv_freq
    cos, sin = jnp.cos(ang), jnp.sin(ang)
    tail = jnp.zeros((seq, LANES - MLA_QK_DIM), F32)
    rope_c = _cat([jnp.ones((seq, HEAD_DIM), F32), cos, cos, tail])
    rope_s = _cat([jnp.zeros((seq, HEAD_DIM), F32), -sin, sin, tail])
    zero_vec = lambda n: jnp.zeros((1, n), F32)

    def rope_fwd(q2, kvs, pd, tc, ts):
        c8, s8 = _cat([tc] * nh), _cat([ts] * nh)
        qf = q2[:, :nh * LANES] * c8 + q2[:, nh * LANES:] * s8
        kpe = pd[:, :LANES] * tc + pd[:, LANES:] * ts
        return qf, kvs[:, :nh * LANES] + _cat([kpe] * nh), kvs[:, nh * LANES:]

    def rope_bwd(dq, dk, dv, tc, ts):
        c8, s8 = _cat([tc] * nh), _cat([ts] * nh)
        dks = dk[:, :LANES]
        for h in range(1, nh):
            dks = dks + dk[:, h * LANES:(h + 1) * LANES]
        return _cat([dq * c8, dq * s8]), _cat([dk, dv]), _cat([dks * tc, dks * ts])

    def merge_fwd(*a):
        ng = d // gb
        gs, gm, osb, omla = _cat(a[:ng]), _cat(a[ng:2 * ng]), a[2 * ng], a[2 * ng + 1]
        return (osb / (1.0 + jnp.exp(-gs)) + omla / (1.0 + jnp.exp(-gm)),)

    def merge_bwd(*a):
        ng = d // gb
        gs, gm, osb, omla, dm = _cat(a[:ng]), _cat(a[ng:2 * ng]), a[2 * ng], a[2 * ng + 1], a[2 * ng + 2]
        ss, sm = 1.0 / (1.0 + jnp.exp(-gs)), 1.0 / (1.0 + jnp.exp(-gm))
        return ss * dm, sm * dm, _cat([dm * osb * ss * (1.0 - ss), dm * omla * sm * (1.0 - sm)])

    def gate_cols(p):
        ng = d // gb
        return [(p, gb, o_gate // gb + i, 0) for i in range(2 * ng)]

    xs = x[0]
    saved = []
    for l in range(depth):
        w = weights[l]
        sh1, sc1, g1, sh2, sc2, g2 = mods[l]
        h1 = _rowwise("norm1", _rmsmod_fwd, [xs], [g_mix_norm[l:l + 1], sc1, sh1], [(d, BF16)])[0]
        p = _matmul("in_proj", h1, w["w_in"], "nn")
        o_sb, tot_sb = _sb_fwd("sb_fwd", p, nhp_sb, tq, tk)
        y_sb = _matmul("sb_out", o_sb, w["w_sb"], "nn")
        qn = _rowwise("norm_q", _rmsmod_fwd, [(p, qr, o_qlat // qr, 0)],
                      [g_q_lat[l:l + 1], zero_vec(qr), zero_vec(qr)], [(qr, BF16)])[0]
        kvn = _rowwise("norm_kv", _rmsmod_fwd, [(p, kvr, o_kvlat // kvr, 0)],
                       [g_kv_lat[l:l + 1], zero_vec(kvr), zero_vec(kvr)], [(kvr, BF16)])[0]
        q2 = _matmul("q_up", qn, w["w_q"], "nn")
        kvs = _matmul("kv_up", kvn, w["w_kv"], "nn")
        qf, kf, vf = _rowwise("rope_fwd", rope_fwd, [q2, kvs, (p, 2 * LANES, o_rope // (2 * LANES), 0), rope_c, rope_s],
                              [], [(nh * LANES, BF16), (nh * LANES, BF16), (mlaw, BF16)])
        o_mla, lse = _mla_fwd("mla_fwd", qf, kf, vf, tq, tk)
        y_mla = _matmul("mla_out", o_mla, w["w_mla"], "nn")
        merged = _rowwise("merge_fwd", merge_fwd, gate_cols(p) + [y_sb, y_mla], [], [(d, BF16)])[0]
        y1 = _matmul("mix_out", merged, w["w_mix"], "nn")
        x_mid = _rowwise("resid1", lambda a, b, g: (a + g * b,), [xs, y1], [g1], [(d, F32)])[0]
        h2 = _rowwise("norm2", _rmsmod_fwd, [x_mid], [g_mlp_norm[l:l + 1], sc2, sh2], [(d, BF16)])[0]
        u = _matmul("mlp_up", h2, w["w_up"], "nn")
        act = _rowwise("relu2", lambda t: (jnp.square(jnp.maximum(t, 0.0)),), [u], [], [(dff, BF16)])[0]
        y2 = _matmul("mlp_down", act, w["w_down"], "nn")
        x_out = _rowwise("resid2", lambda a, b, g: (a + g * b,), [x_mid, y2], [g2], [(d, F32)])[0]
        saved.append(dict(x=xs, h1=h1, p=p, o_sb=o_sb, tot_sb=tot_sb, y_sb=y_sb, qn=qn, kvn=kvn, qf=qf, kf=kf, vf=vf, o_mla=o_mla,
                          lse=lse, y_mla=y_mla, merged=merged, y1=y1, x_mid=x_mid, h2=h2, u=u, act=act, y2=y2))
        xs = x_out

    def final_fn(xv, tv, g):
        r, xh = _norm_parts(xv)
        diff = xh * g - tv
        dy = diff * (1.0 / d)
        dxh = dy * g
        dx = r * (dxh - xh * jnp.mean(dxh * xh, axis=-1, keepdims=True))
        return dx, diff * diff, dy * xh

    dx, sq, dg_final = _rowwise("loss_head", final_fn, [xs, loss_target[0]], [g_final.reshape(1, d)],
                                [(d, F32)], reds=[d, d])
    loss = lax.psum(0.5 * jnp.sum(sq) / d, ("x", "y", "c"))

    dmods, small, big_grads = [None] * depth, [None] * depth, [None] * depth
    for l in reversed(range(depth)):
        w, sv = weights[l], saved[l]
        sh1, sc1, g1, sh2, sc2, g2 = mods[l]
        gr = {}
        dy2, dgate2 = _rowwise("gate2_bwd", lambda dxv, y, g: (dxv * g, dxv * y), [dx, sv["y2"]], [g2],
                               [(d, BF16)], reds=[d])
        dact = _matmul("mlp_down_dx", dy2, w["w_down"], "nt")
        gr["w_down"] = _matmul("mlp_down_dw", sv["act"], dy2, "tn")
        du = _rowwise("relu2_bwd", lambda da, uv: (da * 2.0 * jnp.maximum(uv, 0.0),), [dact, sv["u"]], [],
                      [(dff, BF16)])[0]
        dh2 = _matmul("mlp_up_dx", du, w["w_up"], "nt")
        gr["w_up"] = _matmul("mlp_up_dw", sv["h2"], du, "tn")
        dx_mid, dsh2, dsc2, dg_mlp = _rowwise("norm2_bwd", _rmsmod_bwd, [dh2, sv["x_mid"], dx],
                                              [g_mlp_norm[l:l + 1], sc2], [(d, F32)], reds=[d, d, d])
        dy1, dgate1 = _rowwise("gate1_bwd", lambda dxv, y, g: (dxv * g, dxv * y), [dx_mid, sv["y1"]], [g1],
                               [(d, BF16)], reds=[d])
        dmerged = _matmul("mix_out_dx", dy1, w["w_mix"], "nt")
        gr["w_mix"] = _matmul("mix_out_dw", sv["merged"], dy1, "tn")
        dy_sb, dy_mla, dgates = _rowwise("merge_bwd", merge_bwd, gate_cols(sv["p"]) + [sv["y_sb"], sv["y_mla"], dmerged],
                                         [], [(d, BF16), (d, BF16), (2 * d, BF16)])
        do_sb = _matmul("sb_out_dx", dy_sb, w["w_sb"], "nt")
        gr["w_sb"] = _matmul("sb_out_dw", sv["o_sb"], dy_sb, "tn")
        do_mla = _matmul("mla_out_dx", dy_mla, w["w_mla"], "nt")
        gr["w_mla"] = _matmul("mla_out_dw", sv["o_mla"], dy_mla, "tn")
        dq_sb, dk_sb, dv_sb = _sb_bwd("sb_bwd", sv["p"], sv["tot_sb"], do_sb, nhp_sb, tq, tk)
        dqf, dkf, dvf = _mla_bwd("mla_bwd", sv["qf"], sv["kf"], sv["vf"], sv["o_mla"], sv["lse"], do_mla, tq, tk)
        dq2, dkvs, drope = _rowwise("rope_bwd", rope_bwd, [dqf, dkf, dvf, rope_c, rope_s], [],
                                    [(2 * nh * LANES, BF16), (nh * LANES + mlaw, BF16), (2 * LANES, BF16)])
        dqn = _matmul("q_up_dx", dq2, w["w_q"], "nt")
        gr["w_q"] = _matmul("q_up_dw", sv["qn"], dq2, "tn")
        dkvn = _matmul("kv_up_dx", dkvs, w["w_kv"], "nt")
        gr["w_kv"] = _matmul("kv_up_dw", sv["kvn"], dkvs, "tn")
        dqlat, dg_q = _rowwise("norm_q_bwd", _rms_bwd_plain, [dqn, (sv["p"], qr, o_qlat // qr, 0)],
                               [g_q_lat[l:l + 1]], [(qr, BF16)], reds=[qr])
        dkvlat, dg_kv = _rowwise("norm_kv_bwd", _rms_bwd_plain, [dkvn, (sv["p"], kvr, o_kvlat // kvr, 0)],
                                 [g_kv_lat[l:l + 1]], [(kvr, BF16)], reds=[kvr])
        zb = lambda n: jnp.zeros((seq, n), BF16)
        dp = _cat([dq_sb.astype(BF16), dk_sb.astype(BF16), dv_sb.astype(BF16), zb(o_qlat - 3 * sbw), dqlat,
                   zb(o_kvlat - o_qlat - qr), dkvlat, zb(o_rope - o_kvlat - kvr), drope,
                   zb(o_gate - o_rope - 2 * LANES), dgates])
        dh1 = _matmul("in_proj_dx", dp, w["w_in"], "nt")
        gr["w_in"] = _matmul("in_proj_dw", sv["h1"], dp, "tn")
        dx, dsh1, dsc1, dg_mix = _rowwise("norm1_bwd", _rmsmod_bwd, [dh1, sv["x"], dx_mid],
                                          [g_mix_norm[l:l + 1], sc1], [(d, F32)], reds=[d, d, d])
        dmods[l] = _cat([dsh1, dsc1, dgate1, dsh2, dsc2, dgate2])
        small[l] = (dg_mix, dg_q, dg_kv, dg_mlp)
        big_grads[l] = fold(gr)

    small_parts = [jnp.concatenate(dmods, axis=0)]
    small_parts += [jnp.concatenate([small[l][i] for l in range(depth)], axis=0) for i in range(4)]
    small_parts.append(dg_final)
    small_all = _exchange("ag_small", small_parts, False)

    dmod_mine = lax.dynamic_slice_in_dim(small_all[0], me * ada_n, ada_n, axis=2)
    c_act_t = jnp.transpose(c_act)

    def outer_fn(ct, dm):
        acc = ct[:, 0:1] * dm[0:1, :]
        for b in range(1, NDEV):
            acc = acc + ct[:, b:b + 1] * dm[b:b + 1, :]
        return (acc,)

    g_w_ada = jnp.stack([_rowwise("ada_dw", outer_fn, [c_act_t], [dmod_mine[:, l, :]], [(ada_n, F32)])[0]
                         for l in range(depth)])

    def chunk(gfull, wref, by_rows):
        rows, cols = wref.shape[1], wref.shape[2]
        if by_rows:
            return gfull.reshape(NDEV, rows, cols).astype(BF16)
        return jnp.transpose(gfull.reshape(rows, NDEV, cols), (1, 0, 2)).astype(BF16)

    send = [jnp.stack([chunk(big_grads[l][i], big[i], row_sharded[i]) for l in range(depth)], axis=1)
            for i in range(len(big))]
    landed = _exchange("a2a_grads", send, True)

    moments = dict(
        w_ada=(w_ada, m_w_ada, v_w_ada), b_ada=(b_ada, m_b_ada, v_b_ada),
        g_mix_norm=(g_mix_norm, m_g_mix_norm, v_g_mix_norm), w_in=(w_in, m_w_in, v_w_in),
        g_q_lat=(g_q_lat, m_g_q_lat, v_g_q_lat), w_q_up=(w_q_up, m_w_q_up, v_w_q_up),
        g_kv_lat=(g_kv_lat, m_g_kv_lat, v_g_kv_lat), w_kv_up=(w_kv_up, m_w_kv_up, v_w_kv_up),
        w_sb_out=(w_sb_out, m_w_sb_out, v_w_sb_out), w_mla_out=(w_mla_out, m_w_mla_out, v_w_mla_out),
        w_mix_out=(w_mix_out, m_w_mix_out, v_w_mix_out), g_mlp_norm=(g_mlp_norm, m_g_mlp_norm, v_g_mlp_norm),
        w_up=(w_up, m_w_up, v_w_up), w_down=(w_down, m_w_down, v_w_down),
        g_final=(g_final.reshape(1, d), m_g_final.reshape(1, d), v_g_final.reshape(1, d)))
    lands = dict(b_ada=small_all[0], g_mix_norm=small_all[1], g_q_lat=small_all[2], g_kv_lat=small_all[3],
                 g_mlp_norm=small_all[4], g_final=small_all[5], w_in=landed[0], w_q_up=landed[1],
                 w_kv_up=landed[2], w_sb_out=landed[3], w_mla_out=landed[4], w_mix_out=landed[5], w_up=landed[6],
                 w_down=landed[7])
    gs, deltas, new_ms, new_vs = [], [], [], []
    for name, (wt, mt, vt) in moments.items():
        if name == "w_ada":
            res = [g_w_ada] + _adamw("adamw_" + name, wt, g_w_ada, mt, vt)
        else:
            res = _sum_adamw("adamw_" + name, lands[name], wt, mt, vt)
        if name == "g_final":
            res = [t.reshape(d) for t in res]
        for lst, t in zip((gs, deltas, new_ms, new_vs), res):
            lst.append(t)

    return (loss, dx[None], *gs, *deltas, *new_ms, *new_vs)
```

```python
import functools

import jax
import jax.numpy as jnp
from jax import lax
from jax.experimental import pallas as pl
from jax.experimental.pallas import tpu as pltpu

F32 = jnp.float32
BF16 = jnp.bfloat16
NDEV = 8
LANES = 128
HEAD_DIM = 64
ROPE_DIM = 32
MLA_QK_DIM = HEAD_DIM + ROPE_DIM
ROPE_THETA = 10000.0
NORM_EPS = 1e-6
ADAM_LR = 0.001
ADAM_B1 = 0.9
ADAM_B2 = 0.999
ADAM_EPS = 1e-08
ADAM_WD = 0.01
ADAM_STEP = 10
VMEM_LIMIT = 48 * 1024 * 1024


def _pcall(body, **kw):
    return pl.pallas_call(body, **kw)


def _tile(n, pref):
    for t in (512, 384, 256, 128, 64, 32, 16, 8):
        if t <= pref and n % t == 0:
            return t
    return n


def _roundup(n, m):
    return (n + m - 1) // m * m


_CP = pltpu.CompilerParams(vmem_limit_bytes=VMEM_LIMIT)


def _rowwise(name, fn, rows, vecs, outs, reds=(), tb=256):
    rows = [r if isinstance(r, tuple) else (r, r.shape[1], 0, 0) for r in rows]
    nrows = None
    for arr, width, col, roff in rows:
        if roff == 0 and nrows is None:
            nrows = arr.shape[0]
    first_off = [r for r in rows if r[3] != 0]
    if first_off:
        nrows = min(nrows, first_off[0][3])
    tb = _tile(nrows, tb)
    nblk = nrows // tb
    n_in = len(rows) + len(vecs)
    n_out = len(outs)

    def body(*refs):
        vals = [r[...] for r in refs[:n_in]]
        res = fn(*vals)
        if not isinstance(res, (tuple, list)):
            res = (res,)
        for ref, val in zip(refs[n_in:n_in + n_out], res[:n_out]):
            ref[...] = val.astype(ref.dtype)
        for ref, val in zip(refs[n_in + n_out:], res[n_out:]):
            @pl.when(pl.program_id(0) == 0)
            def _(ref=ref):
                ref[...] = jnp.zeros_like(ref)
            ref[...] += jnp.sum(val.astype(F32), axis=0, keepdims=True)

    in_specs = []
    for arr, width, col, roff in rows:
        in_specs.append(pl.BlockSpec((tb, width), functools.partial(
            lambda i, col, rb: (rb + i, col), col=col, rb=roff // tb)))
    for v in vecs:
        in_specs.append(pl.BlockSpec(v.shape, lambda i, nd=v.ndim: (0,) * nd))
    out_specs = [pl.BlockSpec((tb, w), lambda i: (i, 0)) for w, _ in outs]
    out_specs += [pl.BlockSpec((1, w), lambda i: (0, 0)) for w in reds]
    out_shape = [jax.ShapeDtypeStruct((nrows, w), dt) for w, dt in outs]
    out_shape += [jax.ShapeDtypeStruct((1, w), F32) for w in reds]
    res = _pcall(body, name=name, grid=(nblk,), in_specs=in_specs, out_specs=out_specs,
                 out_shape=out_shape, compiler_params=_CP)(*[r[0] for r in rows], *vecs)
    return res


_DIMS = {"nn": (((1,), (0,)), ((), ())), "nt": (((1,), (1,)), ((), ())), "tn": (((0,), (0,)), ((), ()))}


def _matmul(name, a, b, mode, out_dtype=F32):
    if mode == "nn":
        (m, k), n = a.shape, b.shape[1]
    elif mode == "nt":
        (m, k), n = a.shape, b.shape[0]
    else:
        (k, m), n = a.shape, b.shape[1]
    tm, tn = _tile(m, 512), _tile(n, 512)
    dims = _DIMS[mode]

    def body(a_ref, b_ref, o_ref):
        o_ref[...] = lax.dot_general(a_ref[...].astype(BF16), b_ref[...].astype(BF16), dims,
                                     preferred_element_type=F32).astype(o_ref.dtype)

    a_spec = pl.BlockSpec((k, tm), lambda j, i: (0, i)) if mode == "tn" else pl.BlockSpec((tm, k), lambda j, i: (i, 0))
    b_spec = pl.BlockSpec((tn, k), lambda j, i: (j, 0)) if mode == "nt" else pl.BlockSpec((k, tn), lambda j, i: (0, j))
    return _pcall(body, name=name, grid=(n // tn, m // tm), in_specs=[a_spec, b_spec],
                  out_specs=pl.BlockSpec((tm, tn), lambda j, i: (i, j)),
                  out_shape=jax.ShapeDtypeStruct((m, n), out_dtype), compiler_params=_CP)(a, b)


class _Comm:
    KS = {"gather_all": (1, 2, 3, 4, 5, 6, 7), "gather_own": (1, 2, 4, 6), "gather_fwd": (2, 4, 6),
          "scatter": (1, 2, 3, 4, 5, 6, 7)}

    def __init__(self, kind, srcs, lands=None, layer=None, depth=None):
        self.kind, self.srcs, self.lands, self.layer = kind, list(srcs), lands, layer
        self.n = len(lands) if kind == "gather_fwd" else len(srcs)
        if lands is not None:
            self.out_shapes = [jax.ShapeDtypeStruct(t.shape, t.dtype) for t in lands]
        elif kind == "scatter":
            self.out_shapes = [jax.ShapeDtypeStruct((NDEV, depth) + s.shape[1:], s.dtype) for s in srcs]
        else:
            self.out_shapes = [jax.ShapeDtypeStruct((NDEV,) + s.shape, s.dtype) for s in srcs]
        self.operands = self.srcs + (list(lands) if lands is not None else [])
        self.scratch = [pltpu.SemaphoreType.DMA((NDEV - 1, self.n)), pltpu.SemaphoreType.DMA((NDEV - 1, self.n)),
                        pltpu.SemaphoreType.DMA((self.n,))]

    def aliases(self, first_in, first_out):
        if self.lands is None:
            return {}
        return {first_in + len(self.srcs) + i: first_out + i for i in range(self.n)}

    def copies(self, in_refs, out_refs, send_sems, recv_sems, local_sems):
        x, y, c = lax.axis_index("x"), lax.axis_index("y"), lax.axis_index("c")
        me = 4 * x + 2 * y + c
        cps = []
        if self.kind != "gather_fwd":
            for i in range(self.n):
                src = in_refs[i].at[me] if self.kind == "scatter" else in_refs[i]
                dst = out_refs[i].at[me, self.layer] if self.kind == "scatter" else out_refs[i].at[me]
                cps.append(pltpu.make_async_copy(src, dst, local_sems.at[i]))
        for k in self.KS[self.kind]:
            px = 1 - x if k & 4 else x
            py = 1 - y if k & 2 else y
            pc = 1 - c if k & 1 else c
            peer = 4 * px + 2 * py + pc
            for i in range(self.n):
                if self.kind == "gather_fwd":
                    src, dst, to = in_refs[i].at[peer], out_refs[i].at[peer], (x, y, 1 - c)
                elif self.kind == "scatter":
                    src, dst, to = in_refs[i].at[peer], out_refs[i].at[me, self.layer], (px, py, pc)
                else:
                    src, dst, to = in_refs[i], out_refs[i].at[me], (px, py, pc)
                cps.append(pltpu.make_async_remote_copy(
                    src_ref=src, dst_ref=dst, send_sem=send_sems.at[k - 1, i], recv_sem=recv_sems.at[k - 1, i],
                    device_id=to, device_id_type=pl.DeviceIdType.MESH))
        return cps


_ANY = pl.BlockSpec(memory_space=pl.ANY)


def _exchange(name, comm):
    nci = len(comm.operands)

    def body(*refs):
        cps = comm.copies(refs[:nci], refs[nci:nci + comm.n], *refs[nci + comm.n:])
        for cp in cps:
            cp.start()
        for cp in cps:
            cp.wait()

    return _pcall(body, name=name, in_specs=[_ANY] * nci, out_specs=[_ANY] * comm.n, out_shape=comm.out_shapes,
                  scratch_shapes=comm.scratch, input_output_aliases=comm.aliases(0, 0))(*comm.operands)


def _hosted(name, body, grid, arrays, in_specs, out_shapes, out_specs, comm):
    if comm is None:
        return _pcall(body, name=name, grid=grid, in_specs=in_specs, out_specs=out_specs, out_shape=out_shapes,
                      compiler_params=_CP)(*arrays), []
    ni, no, nci = len(arrays), len(out_shapes), len(comm.operands)

    def full(*refs):
        ins, cin = refs[:ni], refs[ni:ni + nci]
        outs = refs[ni + nci:ni + nci + no]
        cout = refs[ni + nci + no:ni + nci + no + comm.n]
        sems = refs[ni + nci + no + comm.n:]
        first = functools.reduce(jnp.logical_and, [pl.program_id(a) == 0 for a in range(len(grid))])
        last = functools.reduce(jnp.logical_and, [pl.program_id(a) == grid[a] - 1 for a in range(len(grid))])

        @pl.when(first)
        def _():
            for cp in comm.copies(cin, cout, *sems):
                cp.start()

        body(*ins, *outs)

        @pl.when(last)
        def _():
            for cp in comm.copies(cin, cout, *sems):
                cp.wait()

    res = _pcall(full, name=name, grid=grid, in_specs=list(in_specs) + [_ANY] * nci,
                 out_specs=list(out_specs) + [_ANY] * comm.n, out_shape=list(out_shapes) + comm.out_shapes,
                 scratch_shapes=comm.scratch, input_output_aliases=comm.aliases(ni, no),
                 compiler_params=_CP)(*arrays, *comm.operands)
    return res[:no], res[no:]


def _dot_nt(a, b):
    return lax.dot_general(a, b, _DIMS["nt"], preferred_element_type=F32)


def _dot_tn(a, b):
    return lax.dot_general(a, b, _DIMS["tn"], preferred_element_type=F32)


def _dot_nn(a, b):
    return jnp.dot(a, b, preferred_element_type=F32)


def _tri(tk, rel):
    j = lax.broadcasted_iota(jnp.int32, (tk, tk), 0)
    s = lax.broadcasted_iota(jnp.int32, (tk, tk), 1)
    return {"after": j > s, "upto": j <= s, "before": j < s}[rel].astype(BF16)


def _head_masks():
    lane = lax.broadcasted_iota(jnp.int32, (1, LANES), 1)
    return [(lane // HEAD_DIM) == h for h in range(2)]


ROW_CHUNK = 32


def _by_rows(fn, n_out, *arrays):
    rows = arrays[0].shape[0]
    step = min(ROW_CHUNK, rows)
    outs = [[] for _ in range(n_out)]
    for r in range(0, rows, step):
        for o, val in zip(outs, fn(r, *[a[r:r + step] for a in arrays])):
            o.append(val)
    return [jnp.concatenate(o, axis=0) for o in outs]


def _causal(r0, k0, rows, tk, strict):
    row = lax.broadcasted_iota(jnp.int32, (rows, tk), 0) + r0
    col = lax.broadcasted_iota(jnp.int32, (rows, tk), 1) + k0
    return col < row if strict else col <= row


def _wide(stat, width):
    return stat if width == LANES else jnp.concatenate([stat] * (width // LANES), axis=1)


def _row_sum(v):
    return jnp.broadcast_to(jnp.sum(v, axis=1, keepdims=True), (v.shape[0], LANES))


def _split_bf16(v):
    hi = v.astype(BF16)
    return hi, (v - hi.astype(F32)).astype(BF16)


def _sb_logs(z, scale, mask):
    z = z * scale
    e = jnp.exp(-jnp.abs(z))
    log_sig = jnp.minimum(z, 0.0) - jnp.log(1.0 + e)
    log_fail = log_sig - z
    return z, log_sig, (log_fail if mask is None else jnp.where(mask, log_fail, 0.0))


def _two_loops(n_full, nkb, near_first, step, carry):
    if near_first:
        carry = lax.fori_loop(0, nkb - n_full, lambda j, c: step(nkb - 1 - j, True, c), carry)
        return lax.fori_loop(0, n_full, lambda j, c: step(n_full - 1 - j, False, c), carry)
    carry = lax.fori_loop(0, n_full, lambda j, c: step(j, False, c), carry)
    return lax.fori_loop(n_full, nkb, lambda j, c: step(j, True, c), carry)


def _sb_fwd(name, p, nhp, tq, tk, comm=None):
    s = p.shape[0]
    scale = HEAD_DIM ** -0.5
    nq = s // tq

    def body(q_ref, k_ref, v_ref, o_ref, tot_ref):
        qi = pl.program_id(1)
        masks = _head_masks()
        after = _tri(tk, "after")
        nkb = ((qi + 1) * tq + tk - 1) // tk
        q = q_ref[...]
        qhs = [jnp.where(hm, q, 0.0).astype(BF16) for hm in masks]

        def step(kb, masked, carry):
            ks = pl.multiple_of(kb * tk, tk)
            k = k_ref[pl.ds(ks, tk), :].astype(BF16)
            v = v_ref[pl.ds(ks, tk), :].astype(BF16)
            mask_of = lambda r, n: _causal(qi * tq + r, ks, n, tk, True) if masked else None
            new = []
            for qh, (later, acc) in zip(qhs, carry):

                def logs(r, zc):
                    _, log_sig, log_fail = _sb_logs(zc, scale, mask_of(r, zc.shape[0]))
                    return (log_sig,) + _split_bf16(log_fail) + (_row_sum(log_fail),)

                log_sig, hi, lo, fail_sum = _by_rows(logs, 4, _dot_nt(qh, k))
                run = _dot_nn(hi, after) + _dot_nn(lo, after)

                def weights(r, lsc, runc, laterc):
                    w = jnp.exp(lsc + runc + _wide(laterc, tk))
                    return ((jnp.where(mask_of(r, w.shape[0]), w, 0.0) if masked else w).astype(BF16),)

                w = _by_rows(weights, 1, log_sig, run, later)[0]
                new.append((later + fail_sum, acc + _dot_nn(w, v)))
            return tuple(new)

        init = (jnp.zeros((tq, LANES), F32), jnp.zeros((tq, LANES), F32))
        (tot0, acc0), (tot1, acc1) = _two_loops((qi * tq) // tk, nkb, True, step, (init, init))
        o_ref[...] = jnp.where(masks[0], acc0, acc1)
        tot_ref[...] = jnp.where(masks[0], tot0, tot1)

    blk = pl.BlockSpec((tq, LANES), lambda h, i: (i, h))
    shape = jax.ShapeDtypeStruct((s, nhp * LANES), F32)
    return _hosted(name, body, (nhp, nq), [p, p, p],
                   [blk, pl.BlockSpec((s, LANES), lambda h, i: (0, nhp + h)),
                    pl.BlockSpec((s, LANES), lambda h, i: (0, 2 * nhp + h))], [shape, shape], [blk, blk], comm)


def _sb_bwd(name, p, tot, do, nhp, tq, tk, comm=None):
    s = p.shape[0]
    scale = HEAD_DIM ** -0.5
    nq = s // tq

    def body(q_ref, k_ref, v_ref, tot_ref, do_ref, dq_ref, dk_ref, dv_ref):
        qi = pl.program_id(1)

        @pl.when(qi == 0)
        def _():
            dk_ref[...] = jnp.zeros_like(dk_ref)
            dv_ref[...] = jnp.zeros_like(dv_ref)

        masks = _head_masks()
        upto, before = _tri(tk, "upto"), _tri(tk, "before")
        nkb = ((qi + 1) * tq + tk - 1) // tk
        q = q_ref[...]
        qb = q.astype(BF16)
        dout = do_ref[...]
        doutb = dout.astype(BF16)
        qhs = [jnp.where(hm, q, 0.0).astype(BF16) for hm in masks]
        dohs = [jnp.where(hm, dout, 0.0).astype(BF16) for hm in masks]
        tot = tot_ref[...]
        totals = [jnp.broadcast_to(tot[:, h * HEAD_DIM:h * HEAD_DIM + 1], (tq, LANES)) for h in range(2)]

        def step(kb, masked, carry):
            ks = pl.multiple_of(kb * tk, tk)
            k = k_ref[pl.ds(ks, tk), :].astype(BF16)
            v = v_ref[pl.ds(ks, tk), :].astype(BF16)
            mask_of = lambda r, n: _causal(qi * tq + r, ks, n, tk, True) if masked else None
            new, dks, dvs = [], [], []
            for qh, doh, total, (fail_before, g_before, dq) in zip(qhs, dohs, totals, carry):

                def logs(r, zc):
                    _, log_sig, log_fail = _sb_logs(zc, scale, mask_of(r, zc.shape[0]))
                    return (log_sig,) + _split_bf16(log_fail) + (_row_sum(log_fail),)

                z = _dot_nt(qh, k)
                log_sig, hi, lo, fail_sum = _by_rows(logs, 4, z)
                run = _dot_nn(hi, upto) + _dot_nn(lo, upto)

                def weights(r, lsc, runc, basec, dwc):
                    w = jnp.exp(lsc + (_wide(basec, tk) - runc))
                    if masked:
                        w = jnp.where(mask_of(r, w.shape[0]), w, 0.0)
                    g = w * dwc
                    return (w.astype(BF16), g) + _split_bf16(g) + (_row_sum(g),)

                w, g, ghi, glo, g_sum = _by_rows(weights, 5, log_sig, run, total - fail_before, _dot_nt(doh, v))
                g_run = _dot_nn(ghi, before) + _dot_nn(glo, before)

                def dscore(r, gc, lsc, zc, grc, gbc):
                    dz = gc * jnp.exp(lsc - zc * scale) - jnp.exp(lsc) * (_wide(gbc, tk) + grc)
                    if masked:
                        dz = jnp.where(mask_of(r, dz.shape[0]), dz, 0.0)
                    return ((dz * scale).astype(BF16),)

                dzb = _by_rows(dscore, 1, g, log_sig, z, g_run, g_before)[0]
                dks.append(_dot_tn(dzb, qb))
                dvs.append(_dot_tn(w, doutb))
                new.append((fail_before + fail_sum, g_before + g_sum, dq + _dot_nn(dzb, k)))
            dk_ref[pl.ds(ks, tk), :] += jnp.where(masks[0], dks[0], dks[1])
            dv_ref[pl.ds(ks, tk), :] += jnp.where(masks[0], dvs[0], dvs[1])
            return tuple(new)

        zero = jnp.zeros((tq, LANES), F32)
        (_, _, dq0), (_, _, dq1) = _two_loops((qi * tq) // tk, nkb, False, step, ((zero, zero, zero),) * 2)
        dq_ref[...] = jnp.where(masks[0], dq0, dq1)

    blk = pl.BlockSpec((tq, LANES), lambda h, i: (i, h))
    full = pl.BlockSpec((s, LANES), lambda h, i: (0, h))
    shape = jax.ShapeDtypeStruct((s, nhp * LANES), F32)
    return _hosted(name, body, (nhp, nq), [p, p, p, tot, do],
                   [blk, pl.BlockSpec((s, LANES), lambda h, i: (0, nhp + h)),
                    pl.BlockSpec((s, LANES), lambda h, i: (0, 2 * nhp + h)), blk, blk],
                   [shape, shape, shape], [blk, full, full], comm)


def _mla_fwd(name, q, k, v, tq, tk, comm=None):
    s = q.shape[0]
    nhp = v.shape[1] // LANES
    scale = MLA_QK_DIM ** -0.5
    nq = s // tq

    def body(q_ref, k_ref, v_ref, o_ref, lse_ref):
        qi = pl.program_id(1)
        masks = _head_masks()
        nkb = ((qi + 1) * tq + tk - 1) // tk
        qhs = [q_ref[:, h * LANES:(h + 1) * LANES] for h in range(2)]

        def step(kb, masked, carry):
            ks = pl.multiple_of(kb * tk, tk)
            vv = v_ref[pl.ds(ks, tk), :]
            new = []
            for h, (m, l, acc) in enumerate(carry):
                kh = k_ref[pl.ds(ks, tk), h * LANES:(h + 1) * LANES]

                def soft(r, zc, mc, lc):
                    zc = zc * scale
                    if masked:
                        zc = jnp.where(_causal(qi * tq + r, ks, zc.shape[0], tk, False), zc, -1e30)
                    m_new = jnp.maximum(mc, jnp.max(zc, axis=1, keepdims=True))
                    a = jnp.exp(mc - m_new)
                    pr = jnp.exp(zc - _wide(m_new, tk))
                    return pr.astype(BF16), m_new, a * lc + _row_sum(pr), a

                pr, m_new, l_new, a = _by_rows(soft, 4, _dot_nt(qhs[h], kh), m, l)
                acc = _by_rows(lambda r, ac, aa, pc: (aa * ac + pc,), 1, acc, a, _dot_nn(pr, vv))[0]
                new.append((m_new, l_new, acc))
            return tuple(new)

        init = (jnp.full((tq, LANES), -1e30, F32), jnp.zeros((tq, LANES), F32), jnp.zeros((tq, LANES), F32))
        (m0, l0, acc0), (m1, l1, acc1) = _two_loops((qi * tq) // tk, nkb, False, step, (init, init))
        o_ref[...] = jnp.where(masks[0], acc0 / l0, acc1 / l1)
        lse_ref[...] = jnp.where(masks[0], m0 + jnp.log(l0), m1 + jnp.log(l1))

    shape = jax.ShapeDtypeStruct((s, nhp * LANES), F32)
    blk = pl.BlockSpec((tq, LANES), lambda h, i: (i, h))
    return _hosted(name, body, (nhp, nq), [q, k, v],
                   [pl.BlockSpec((tq, 2 * LANES), lambda h, i: (i, h)),
                    pl.BlockSpec((s, 2 * LANES), lambda h, i: (0, h)),
                    pl.BlockSpec((s, LANES), lambda h, i: (0, h))], [shape, shape], [blk, blk], comm)


def _mla_bwd(name, q, k, v, o, lse, do, tq, tk, comm=None):
    s = q.shape[0]
    nhp = v.shape[1] // LANES
    scale = MLA_QK_DIM ** -0.5
    nq = s // tq

    def body(q_ref, k_ref, v_ref, o_ref, lse_ref, do_ref, dq_ref, dk_ref, dv_ref):
        qi = pl.program_id(1)

        @pl.when(qi == 0)
        def _():
            dk_ref[...] = jnp.zeros_like(dk_ref)
            dv_ref[...] = jnp.zeros_like(dv_ref)

        masks = _head_masks()
        nkb = ((qi + 1) * tq + tk - 1) // tk
        dout = do_ref[...]
        doutb = dout.astype(BF16)
        prod = dout * o_ref[...]
        qhs = [q_ref[:, h * LANES:(h + 1) * LANES] for h in range(2)]
        dohs = [jnp.where(hm, dout, 0.0).astype(BF16) for hm in masks]
        totals = [_row_sum(jnp.where(hm, prod, 0.0)) for hm in masks]
        lse = lse_ref[...]
        lses = [jnp.broadcast_to(lse[:, h * HEAD_DIM:h * HEAD_DIM + 1], (tq, LANES)) for h in range(2)]

        def step(kb, masked, carry):
            ks = pl.multiple_of(kb * tk, tk)
            vv = v_ref[pl.ds(ks, tk), :]
            new, dvs = [], []
            for h, dq in enumerate(carry):
                kh = k_ref[pl.ds(ks, tk), h * LANES:(h + 1) * LANES]

                def probs(r, zc, dpc, lsec, totc):
                    pr = jnp.exp(zc * scale - _wide(lsec, tk))
                    if masked:
                        pr = jnp.where(_causal(qi * tq + r, ks, pr.shape[0], tk, False), pr, 0.0)
                    return pr.astype(BF16), (pr * (dpc - _wide(totc, tk)) * scale).astype(BF16)

                pr, ds = _by_rows(probs, 2, _dot_nt(qhs[h], kh), _dot_nt(dohs[h], vv), lses[h], totals[h])
                dk_ref[pl.ds(ks, tk), h * LANES:(h + 1) * LANES] += _dot_tn(ds, qhs[h])
                dvs.append(_dot_tn(pr, doutb))
                new.append(dq + _dot_nn(ds, kh))
            dv_ref[pl.ds(ks, tk), :] += jnp.where(masks[0], dvs[0], dvs[1])
            return tuple(new)

        zero = jnp.zeros((tq, LANES), F32)
        dq0, dq1 = _two_loops((qi * tq) // tk, nkb, False, step, (zero, zero))
        dq_ref[:, :LANES] = dq0
        dq_ref[:, LANES:] = dq1

    blk = pl.BlockSpec((tq, LANES), lambda h, i: (i, h))
    blk2 = pl.BlockSpec((tq, 2 * LANES), lambda h, i: (i, h))
    full = pl.BlockSpec((s, LANES), lambda h, i: (0, h))
    full2 = pl.BlockSpec((s, 2 * LANES), lambda h, i: (0, h))
    return _hosted(name, body, (nhp, nq), [q, k, v, o, lse, do], [blk2, full2, full, blk, blk, blk],
                   [jax.ShapeDtypeStruct(q.shape, F32), jax.ShapeDtypeStruct(k.shape, F32),
                    jax.ShapeDtypeStruct(v.shape, F32)], [blk2, full2, full], comm)


def _norm_parts(x):
    r = lax.rsqrt(jnp.mean(x * x, axis=-1, keepdims=True) + NORM_EPS)
    return r, x * r


def _rmsmod_fwd(x, g, sc, sh):
    _, xh = _norm_parts(x)
    return ((xh * g) * (1.0 + sc) + sh,)


def _rmsmod_bwd(dh, x, dres, g, sc):
    r, xh = _norm_parts(x)
    dy = dh * (1.0 + sc)
    dxh = dy * g
    dx = r * (dxh - xh * jnp.mean(dxh * xh, axis=-1, keepdims=True)) + dres
    return dx, dh, dh * (xh * g), dy * xh


def _rms_bwd_plain(dh, x, g):
    r, xh = _norm_parts(x)
    dxh = dh * g
    return r * (dxh - xh * jnp.mean(dxh * xh, axis=-1, keepdims=True)), dh * xh


def _cat(parts):
    return jnp.concatenate(parts, axis=1)


def _swap_halves(a):
    half = a.shape[-1] // 2
    return jnp.concatenate([a[..., half:], a[..., :half]], axis=-1)


def _adamw_fn(w, g, m, v):
    m = ADAM_B1 * m + (1.0 - ADAM_B1) * g
    v = ADAM_B2 * v + (1.0 - ADAM_B2) * jnp.square(g)
    m_hat = m / (1.0 - ADAM_B1 ** ADAM_STEP)
    v_hat = v / (1.0 - ADAM_B2 ** ADAM_STEP)
    delta = -ADAM_LR * (m_hat / (jnp.sqrt(v_hat) + ADAM_EPS) + ADAM_WD * w)
    return delta, m, v


def _adamw(name, w, g, m, v):
    shape = w.shape
    width = shape[-1]
    flat = [t.reshape(-1, width) for t in (w, g, m, v)]
    res = _rowwise(name, _adamw_fn, flat, [], [(width, F32)] * 3)
    return [t.reshape(shape) for t in res]


def _sum_adamw(name, land, w, m, v):
    shape = w.shape
    width = shape[-1]
    rows = w.size // width

    def fn(*blocks):
        g = blocks[0].astype(F32)
        for b in blocks[1:NDEV]:
            g = g + b.astype(F32)
        return (g,) + _adamw_fn(blocks[NDEV], g, blocks[NDEV + 1], blocks[NDEV + 2])

    def fn_whole(wb, mb, vb, lb):
        return fn(*[lb[i] for i in range(NDEV)], wb, mb, vb)

    flat = [t.reshape(rows, width) for t in (w, m, v)]
    if rows % 16 == 0:
        views = [(land.reshape(NDEV * rows, width), width, 0, i * rows) for i in range(NDEV)]
        res = _rowwise(name, fn, views + flat, [], [(width, F32)] * 4)
    else:
        res = _rowwise(name, fn_whole, flat, [land.reshape(NDEV, rows, width)], [(width, F32)] * 4)
    return [t.reshape(shape) for t in res]


def kernel(x, c, positions, w_ada, b_ada, g_mix_norm, w_in, g_q_lat, w_q_up, g_kv_lat, w_kv_up, w_sb_out, w_mla_out, w_mix_out, g_mlp_norm, w_up, w_down, g_final, loss_target, m_w_ada, m_b_ada, m_g_mix_norm, m_w_in, m_g_q_lat, m_w_q_up, m_g_kv_lat, m_w_kv_up, m_w_sb_out, m_w_mla_out, m_w_mix_out, m_g_mlp_norm, m_w_up, m_w_down, m_g_final, v_w_ada, v_b_ada, v_g_mix_norm, v_w_in, v_g_q_lat, v_w_q_up, v_g_kv_lat, v_w_kv_up, v_w_sb_out, v_w_mla_out, v_w_mix_out, v_g_mlp_norm, v_w_up, v_w_down, v_g_final):
    seq, d = x.shape[1], x.shape[2]
    depth = w_ada.shape[0]
    qr, kvr = g_q_lat.shape[1], g_kv_lat.shape[1]
    sbw, mlaw = w_sb_out.shape[1], w_mla_out.shape[1]
    nh = mlaw // HEAD_DIM
    nhp_sb = sbw // LANES
    dff = w_up.shape[2] * NDEV
    ada_n = w_ada.shape[2]
    gb = min(512, d)
    tq, tk = min(256, seq), min(256, seq)
    me = 4 * lax.axis_index("x") + 2 * lax.axis_index("y") + lax.axis_index("c")

    o_qlat = _roundup(3 * sbw, qr)
    o_kvlat = _roundup(o_qlat + qr, kvr)
    o_rope = _roundup(o_kvlat + kvr, 2 * LANES)
    o_gate = _roundup(o_rope + 2 * LANES, gb)
    wp = o_gate + 2 * d

    c_all = _exchange("ag_c", _Comm("gather_all", [c.reshape(d // LANES, LANES)]))[0].reshape(NDEV, d)
    c_act = _rowwise("silu_c", lambda t: (t * (1.0 / (1.0 + jnp.exp(-t))),), [c_all], [], [(d, F32)])[0]
    parts = jnp.stack([_matmul("ada_fwd", c_act, w_ada[l], "nn") for l in range(depth)])
    parts_all = _exchange("ag_mod", _Comm("gather_all", [parts]))[0]
    mine = jnp.transpose(lax.dynamic_index_in_dim(parts_all, me, axis=2, keepdims=False), (1, 0, 2))
    mod = _rowwise("mod_bias", lambda a, b: (a + b,), [mine.reshape(depth, NDEV * ada_n), b_ada], [],
                   [(6 * d, F32)])[0]
    mods = [[mod[l:l + 1, i * d:(i + 1) * d] for i in range(6)] for l in range(depth)]

    big = [w_in, w_q_up, w_kv_up, w_sb_out, w_mla_out, w_mix_out, w_up, w_down]
    row_sharded = [False, False, False, False, False, True, False, True]
    shards = [[w[l].astype(BF16) for w in big] for l in range(depth)]

    def unpack_layer(gathered):
        out = []
        for g, by_rows in zip(gathered, row_sharded):
            _, rows, cols = g.shape
            if by_rows:
                out.append(g.reshape(NDEV * rows, cols))
            else:
                out.append(jnp.transpose(g, (1, 0, 2)).reshape(rows, NDEV * cols))
        return out

    def derive(full):
        wi, wq, wkv, wsb, wmla, wmix, wu, wd = full
        dt = wi.dtype
        z = lambda r, n: jnp.zeros((r, n), dt)
        o = 3 * sbw
        kr = wi[:, o + qr + kvr:o + qr + kvr + ROPE_DIM]
        g0 = o + qr + kvr + ROPE_DIM
        w_in_pad = _cat([wi[:, :o], z(d, o_qlat - o), wi[:, o:o + qr], z(d, o_kvlat - o_qlat - qr),
                         wi[:, o + qr:o + qr + kvr], z(d, o_rope - o_kvlat - kvr),
                         z(d, HEAD_DIM), kr, z(d, LANES - MLA_QK_DIM),
                         z(d, HEAD_DIM), _swap_halves(kr), z(d, LANES - MLA_QK_DIM),
                         z(d, o_gate - o_rope - 2 * LANES), wi[:, g0:]])
        wq3 = wq.reshape(qr, nh, MLA_QK_DIM)
        z3 = lambda n: jnp.zeros((qr, nh, n), dt)
        rope_w = wq3[:, :, HEAD_DIM:]
        wq_a = jnp.concatenate([wq3[:, :, :HEAD_DIM], rope_w, z3(LANES - MLA_QK_DIM)], axis=2).reshape(qr, nh * LANES)
        wq_b = jnp.concatenate([z3(HEAD_DIM), _swap_halves(rope_w), z3(LANES - MLA_QK_DIM)], axis=2).reshape(qr, nh * LANES)
        wkv3 = wkv.reshape(kvr, nh, 2 * HEAD_DIM)
        wk = jnp.concatenate([wkv3[:, :, :HEAD_DIM], jnp.zeros((kvr, nh, HEAD_DIM), dt)], axis=2).reshape(kvr, nh * LANES)
        wv = wkv3[:, :, HEAD_DIM:].reshape(kvr, nh * HEAD_DIM)
        return dict(w_in=w_in_pad, w_q=_cat([wq_a, wq_b]), w_kv=_cat([wk, wv]), w_sb=wsb, w_mla=wmla,
                    w_mix=wmix, w_up=wu, w_down=wd)

    def fold(gr):
        gi, gq, gkv = gr["w_in"], gr["w_q"], gr["w_kv"]
        o = 3 * sbw
        ra = gi[:, o_rope + HEAD_DIM:o_rope + MLA_QK_DIM]
        rb = gi[:, o_rope + LANES + HEAD_DIM:o_rope + LANES + MLA_QK_DIM]
        g_in = _cat([gi[:, :o], gi[:, o_qlat:o_qlat + qr], gi[:, o_kvlat:o_kvlat + kvr], ra + _swap_halves(rb),
                     gi[:, o_gate:]])
        ga = gq[:, :nh * LANES].reshape(qr, nh, LANES)
        gb_ = gq[:, nh * LANES:].reshape(qr, nh, LANES)
        g_q = jnp.concatenate([ga[:, :, :HEAD_DIM], ga[:, :, HEAD_DIM:MLA_QK_DIM]
                               + _swap_halves(gb_[:, :, HEAD_DIM:MLA_QK_DIM])], axis=2).reshape(qr, nh * MLA_QK_DIM)
        gk = gkv[:, :nh * LANES].reshape(kvr, nh, LANES)[:, :, :HEAD_DIM]
        gv = gkv[:, nh * LANES:].reshape(kvr, nh, HEAD_DIM)
        g_kv = jnp.concatenate([gk, gv], axis=2).reshape(kvr, nh * 2 * HEAD_DIM)
        return [g_in, g_q, g_kv, gr["w_sb"], gr["w_mla"], gr["w_mix"], gr["w_up"], gr["w_down"]]

    first = _exchange("ag_w0_own", _Comm("gather_own", shards[0]))
    weights = [derive(unpack_layer(_exchange("ag_w0_fwd", _Comm("gather_fwd", [], lands=first))))]

    inv_freq = 1.0 / (ROPE_THETA ** (jnp.arange(0, ROPE_DIM, 2, dtype=F32) / ROPE_DIM))
    ang = positions[0].astype(F32)[:, None] * inv_freq
    cos, sin = jnp.cos(ang), jnp.sin(ang)
    tail = jnp.zeros((seq, LANES - MLA_QK_DIM), F32)
    rope_c = _cat([jnp.ones((seq, HEAD_DIM), F32), cos, cos, tail])
    rope_s = _cat([jnp.zeros((seq, HEAD_DIM), F32), -sin, sin, tail])
    zero_vec = lambda n: jnp.zeros((1, n), F32)

    def rope_fwd(q2, kvs, pd, tc, ts):
        c8, s8 = _cat([tc] * nh), _cat([ts] * nh)
        qf = q2[:, :nh * LANES] * c8 + q2[:, nh * LANES:] * s8
        kpe = pd[:, :LANES] * tc + pd[:, LANES:] * ts
        return qf, kvs[:, :nh * LANES] + _cat([kpe] * nh), kvs[:, nh * LANES:]

    def rope_bwd(dq, dk, dv, tc, ts):
        c8, s8 = _cat([tc] * nh), _cat([ts] * nh)
        dks = dk[:, :LANES]
        for h in range(1, nh):
            dks = dks + dk[:, h * LANES:(h + 1) * LANES]
        return _cat([dq * c8, dq * s8]), _cat([dk, dv]), _cat([dks * tc, dks * ts])

    def merge_fwd(*a):
        ng = d // gb
        gs, gm, osb, omla = _cat(a[:ng]), _cat(a[ng:2 * ng]), a[2 * ng], a[2 * ng + 1]
        return (osb / (1.0 + jnp.exp(-gs)) + omla / (1.0 + jnp.exp(-gm)),)

    def merge_bwd(*a):
        ng = d // gb
        gs, gm, osb, omla, dm = _cat(a[:ng]), _cat(a[ng:2 * ng]), a[2 * ng], a[2 * ng + 1], a[2 * ng + 2]
        ss, sm = 1.0 / (1.0 + jnp.exp(-gs)), 1.0 / (1.0 + jnp.exp(-gm))
        return ss * dm, sm * dm, _cat([dm * osb * ss * (1.0 - ss), dm * omla * sm * (1.0 - sm)])

    def gate_cols(p):
        ng = d // gb
        return [(p, gb, o_gate // gb + i, 0) for i in range(2 * ng)]

    xs = x[0]
    saved = []
    for l in range(depth):
        w = weights[l]
        sh1, sc1, g1, sh2, sc2, g2 = mods[l]
        h1 = _rowwise("norm1", _rmsmod_fwd, [xs], [g_mix_norm[l:l + 1], sc1, sh1], [(d, BF16)])[0]
        p = _matmul("in_proj", h1, w["w_in"], "nn")
        nxt = _Comm("gather_own", shards[l + 1]) if l + 1 < depth else None
        (o_sb, tot_sb), part = _sb_fwd("sb_fwd", p, nhp_sb, tq, tk, nxt)
        y_sb = _matmul("sb_out", o_sb, w["w_sb"], "nn")
        qn = _rowwise("norm_q", _rmsmod_fwd, [(p, qr, o_qlat // qr, 0)],
                      [g_q_lat[l:l + 1], zero_vec(qr), zero_vec(qr)], [(qr, BF16)])[0]
        kvn = _rowwise("norm_kv", _rmsmod_fwd, [(p, kvr, o_kvlat // kvr, 0)],
                       [g_kv_lat[l:l + 1], zero_vec(kvr), zero_vec(kvr)], [(kvr, BF16)])[0]
        q2 = _matmul("q_up", qn, w["w_q"], "nn")
        kvs = _matmul("kv_up", kvn, w["w_kv"], "nn")
        qf, kf, vf = _rowwise("rope_fwd", rope_fwd, [q2, kvs, (p, 2 * LANES, o_rope // (2 * LANES), 0), rope_c, rope_s],
                              [], [(nh * LANES, BF16), (nh * LANES, BF16), (mlaw, BF16)])
        (o_mla, lse), full = _mla_fwd("mla_fwd", qf, kf, vf, tq, tk,
                                      _Comm("gather_fwd", [], lands=part) if part else None)
        if full:
            weights.append(derive(unpack_layer(full)))
        y_mla = _matmul("mla_out", o_mla, w["w_mla"], "nn")
        merged = _rowwise("merge_fwd", merge_fwd, gate_cols(p) + [y_sb, y_mla], [], [(d, BF16)])[0]
        y1 = _matmul("mix_out", merged, w["w_mix"], "nn")
        x_mid = _rowwise("resid1", lambda a, b, g: (a + g * b,), [xs, y1], [g1], [(d, F32)])[0]
        h2 = _rowwise("norm2", _rmsmod_fwd, [x_mid], [g_mlp_norm[l:l + 1], sc2, sh2], [(d, BF16)])[0]
        u = _matmul("mlp_up", h2, w["w_up"], "nn")
        act = _rowwise("relu2", lambda t: (jnp.square(jnp.maximum(t, 0.0)),), [u], [], [(dff, BF16)])[0]
        y2 = _matmul("mlp_down", act, w["w_down"], "nn")
        x_out = _rowwise("resid2", lambda a, b, g: (a + g * b,), [x_mid, y2], [g2], [(d, F32)])[0]
        saved.append(dict(x=xs, h1=h1, p=p, o_sb=o_sb, tot_sb=tot_sb, y_sb=y_sb, qn=qn, kvn=kvn, qf=qf, kf=kf, vf=vf, o_mla=o_mla,
                          lse=lse, y_mla=y_mla, merged=merged, y1=y1, x_mid=x_mid, h2=h2, u=u, act=act, y2=y2))
        xs = x_out

    def final_fn(xv, tv, g):
        r, xh = _norm_parts(xv)
        diff = xh * g - tv
        dy = diff * (1.0 / d)
        dxh = dy * g
        dx = r * (dxh - xh * jnp.mean(dxh * xh, axis=-1, keepdims=True))
        return dx, diff * diff, dy * xh

    dx, sq, dg_final = _rowwise("loss_head", final_fn, [xs, loss_target[0]], [g_final.reshape(1, d)],
                                [(d, F32)], reds=[d, d])
    loss = lax.psum(0.5 * jnp.sum(sq) / d, ("x", "y", "c"))

    def chunk(gfull, wref, by_rows):
        rows, cols = wref.shape[1], wref.shape[2]
        if by_rows:
            return gfull.reshape(NDEV, rows, cols).astype(BF16)
        return jnp.transpose(gfull.reshape(rows, NDEV, cols), (1, 0, 2)).astype(BF16)

    dmods, small = [None] * depth, [None] * depth
    pending, lands_a, lands_b = None, None, None
    split = len(big) - 1
    for l in reversed(range(depth)):
        w, sv = weights[l], saved[l]
        sh1, sc1, g1, sh2, sc2, g2 = mods[l]
        gr = {}
        dy2, dgate2 = _rowwise("gate2_bwd", lambda dxv, y, g: (dxv * g, dxv * y), [dx, sv["y2"]], [g2],
                               [(d, BF16)], reds=[d])
        dact = _matmul("mlp_down_dx", dy2, w["w_down"], "nt")
        gr["w_down"] = _matmul("mlp_down_dw", sv["act"], dy2, "tn")
        du = _rowwise("relu2_bwd", lambda da, uv: (da * 2.0 * jnp.maximum(uv, 0.0),), [dact, sv["u"]], [],
                      [(dff, BF16)])[0]
        dh2 = _matmul("mlp_up_dx", du, w["w_up"], "nt")
        gr["w_up"] = _matmul("mlp_up_dw", sv["h2"], du, "tn")
        dx_mid, dsh2, dsc2, dg_mlp = _rowwise("norm2_bwd", _rmsmod_bwd, [dh2, sv["x_mid"], dx],
                                              [g_mlp_norm[l:l + 1], sc2], [(d, F32)], reds=[d, d, d])
        dy1, dgate1 = _rowwise("gate1_bwd", lambda dxv, y, g: (dxv * g, dxv * y), [dx_mid, sv["y1"]], [g1],
                               [(d, BF16)], reds=[d])
        dmerged = _matmul("mix_out_dx", dy1, w["w_mix"], "nt")
        gr["w_mix"] = _matmul("mix_out_dw", sv["merged"], dy1, "tn")
        dy_sb, dy_mla, dgates = _rowwise("merge_bwd", merge_bwd, gate_cols(sv["p"]) + [sv["y_sb"], sv["y_mla"], dmerged],
                                         [], [(d, BF16), (d, BF16), (2 * d, BF16)])
        do_sb = _matmul("sb_out_dx", dy_sb, w["w_sb"], "nt")
        gr["w_sb"] = _matmul("sb_out_dw", sv["o_sb"], dy_sb, "tn")
        do_mla = _matmul("mla_out_dx", dy_mla, w["w_mla"], "nt")
        gr["w_mla"] = _matmul("mla_out_dw", sv["o_mla"], dy_mla, "tn")
        comm_a = comm_b = None
        if pending is not None:
            comm_a = _Comm("scatter", pending[:split], lands_a, l + 1, depth)
            comm_b = _Comm("scatter", pending[split:], lands_b, l + 1, depth)
        (dq_sb, dk_sb, dv_sb), got_a = _sb_bwd("sb_bwd", sv["p"], sv["tot_sb"], do_sb, nhp_sb, tq, tk, comm_a)
        (dqf, dkf, dvf), got_b = _mla_bwd("mla_bwd", sv["qf"], sv["kf"], sv["vf"], sv["o_mla"], sv["lse"], do_mla,
                                          tq, tk, comm_b)
        if pending is not None:
            lands_a, lands_b = got_a, got_b
        dq2, dkvs, drope = _rowwise("rope_bwd", rope_bwd, [dqf, dkf, dvf, rope_c, rope_s], [],
                                    [(2 * nh * LANES, BF16), (nh * LANES + mlaw, BF16), (2 * LANES, BF16)])
        dqn = _matmul("q_up_dx", dq2, w["w_q"], "nt")
        gr["w_q"] = _matmul("q_up_dw", sv["qn"], dq2, "tn")
        dkvn = _matmul("kv_up_dx", dkvs, w["w_kv"], "nt")
        gr["w_kv"] = _matmul("kv_up_dw", sv["kvn"], dkvs, "tn")
        dqlat, dg_q = _rowwise("norm_q_bwd", _rms_bwd_plain, [dqn, (sv["p"], qr, o_qlat // qr, 0)],
                               [g_q_lat[l:l + 1]], [(qr, BF16)], reds=[qr])
        dkvlat, dg_kv = _rowwise("norm_kv_bwd", _rms_bwd_plain, [dkvn, (sv["p"], kvr, o_kvlat // kvr, 0)],
                                 [g_kv_lat[l:l + 1]], [(kvr, BF16)], reds=[kvr])
        zb = lambda n: jnp.zeros((seq, n), BF16)
        dp = _cat([dq_sb.astype(BF16), dk_sb.astype(BF16), dv_sb.astype(BF16), zb(o_qlat - 3 * sbw), dqlat,
                   zb(o_kvlat - o_qlat - qr), dkvlat, zb(o_rope - o_kvlat - kvr), drope,
                   zb(o_gate - o_rope - 2 * LANES), dgates])
        dh1 = _matmul("in_proj_dx", dp, w["w_in"], "nt")
        gr["w_in"] = _matmul("in_proj_dw", sv["h1"], dp, "tn")
        dx, dsh1, dsc1, dg_mix = _rowwise("norm1_bwd", _rmsmod_bwd, [dh1, sv["x"], dx_mid],
                                          [g_mix_norm[l:l + 1], sc1], [(d, F32)], reds=[d, d, d])
        dmods[l] = _cat([dsh1, dsc1, dgate1, dsh2, dsc2, dgate2])
        small[l] = (dg_mix, dg_q, dg_kv, dg_mlp)
        pending = [chunk(g, w, by_rows) for g, w, by_rows in zip(fold(gr), big, row_sharded)]

    small_parts = [jnp.concatenate(dmods, axis=0)]
    small_parts += [jnp.concatenate([small[l][i] for l in range(depth)], axis=0) for i in range(4)]
    small_parts.append(dg_final)
    small_all = _exchange("ag_small", _Comm("gather_all", small_parts))

    dmod_mine = lax.dynamic_slice_in_dim(small_all[0], me * ada_n, ada_n, axis=2)
    c_act_t = jnp.transpose(c_act)

    def outer_fn(ct, dm):
        acc = ct[:, 0:1] * dm[0:1, :]
        for b in range(1, NDEV):
            acc = acc + ct[:, b:b + 1] * dm[b:b + 1, :]
        return (acc,)

    g_w_ada = jnp.stack([_rowwise("ada_dw", outer_fn, [c_act_t], [dmod_mine[:, l, :]], [(ada_n, F32)])[0]
                         for l in range(depth)])

    landed = _exchange("a2a_last", _Comm("scatter", pending, None if lands_a is None else list(lands_a) + list(lands_b),
                                         0, depth))

    moments = dict(
        w_ada=(w_ada, m_w_ada, v_w_ada), b_ada=(b_ada, m_b_ada, v_b_ada),
        g_mix_norm=(g_mix_norm, m_g_mix_norm, v_g_mix_norm), w_in=(w_in, m_w_in, v_w_in),
        g_q_lat=(g_q_lat, m_g_q_lat, v_g_q_lat), w_q_up=(w_q_up, m_w_q_up, v_w_q_up),
        g_kv_lat=(g_kv_lat, m_g_kv_lat, v_g_kv_lat), w_kv_up=(w_kv_up, m_w_kv_up, v_w_kv_up),
        w_sb_out=(w_sb_out, m_w_sb_out, v_w_sb_out), w_mla_out=(w_mla_out, m_w_mla_out, v_w_mla_out),
        w_mix_out=(w_mix_out, m_w_mix_out, v_w_mix_out), g_mlp_norm=(g_mlp_norm, m_g_mlp_norm, v_g_mlp_norm),
        w_up=(w_up, m_w_up, v_w_up), w_down=(w_down, m_w_down, v_w_down),
        g_final=(g_final.reshape(1, d), m_g_final.reshape(1, d), v_g_final.reshape(1, d)))
    lands = dict(b_ada=small_all[0], g_mix_norm=small_all[1], g_q_lat=small_all[2], g_kv_lat=small_all[3],
                 g_mlp_norm=small_all[4], g_final=small_all[5], w_in=landed[0], w_q_up=landed[1],
                 w_kv_up=landed[2], w_sb_out=landed[3], w_mla_out=landed[4], w_mix_out=landed[5], w_up=landed[6],
                 w_down=landed[7])
    gs, deltas, new_ms, new_vs = [], [], [], []
    for name, (wt, mt, vt) in moments.items():
        if name == "w_ada":
            res = [g_w_ada] + _adamw("adamw_" + name, wt, g_w_ada, mt, vt)
        else:
            res = _sum_adamw("adamw_" + name, lands[name], wt, mt, vt)
        if name == "g_final":
            res = [t.reshape(d) for t in res]
        for lst, t in zip((gs, deltas, new_ms, new_vs), res):
            lst.append(t)

    return (loss, dx[None], *gs, *deltas, *new_ms, *new_vs)
```

```python
import functools

import jax
import jax.numpy as jnp
from jax import lax
from jax.experimental import pallas as pl
from jax.experimental.pallas import tpu as pltpu

F32 = jnp.float32
BF16 = jnp.bfloat16
NDEV = 8
LANES = 128
HEAD_DIM = 64
ROPE_DIM = 32
MLA_QK_DIM = HEAD_DIM + ROPE_DIM
ROPE_THETA = 10000.0
NORM_EPS = 1e-6
ADAM_LR = 0.001
ADAM_B1 = 0.9
ADAM_B2 = 0.999
ADAM_EPS = 1e-08
ADAM_WD = 0.01
ADAM_STEP = 10
VMEM_LIMIT = 48 * 1024 * 1024


def _pcall(body, **kw):
    return pl.pallas_call(body, **kw)


def _tile(n, pref):
    for t in (512, 384, 256, 128, 64, 32, 16, 8):
        if t <= pref and n % t == 0:
            return t
    return n


def _roundup(n, m):
    return (n + m - 1) // m * m


_CP = pltpu.CompilerParams(vmem_limit_bytes=VMEM_LIMIT)


def _rowwise(name, fn, rows, vecs, outs, reds=(), tb=256):
    rows = [r if isinstance(r, tuple) else (r, r.shape[1], 0, 0) for r in rows]
    nrows = None
    for arr, width, col, roff in rows:
        if roff == 0 and nrows is None:
            nrows = arr.shape[0]
    first_off = [r for r in rows if r[3] != 0]
    if first_off:
        nrows = min(nrows, first_off[0][3])
    tb = _tile(nrows, tb)
    nblk = nrows // tb
    n_in = len(rows) + len(vecs)
    n_out = len(outs)

    def body(*refs):
        vals = [r[...] for r in refs[:n_in]]
        res = fn(*vals)
        if not isinstance(res, (tuple, list)):
            res = (res,)
        for ref, val in zip(refs[n_in:n_in + n_out], res[:n_out]):
            ref[...] = val.astype(ref.dtype)
        for ref, val in zip(refs[n_in + n_out:], res[n_out:]):
            @pl.when(pl.program_id(0) == 0)
            def _(ref=ref):
                ref[...] = jnp.zeros_like(ref)
            ref[...] += jnp.sum(val.astype(F32), axis=0, keepdims=True)

    in_specs = []
    for arr, width, col, roff in rows:
        in_specs.append(pl.BlockSpec((tb, width), functools.partial(
            lambda i, col, rb: (rb + i, col), col=col, rb=roff // tb)))
    for v in vecs:
        in_specs.append(pl.BlockSpec(v.shape, lambda i, nd=v.ndim: (0,) * nd))
    out_specs = [pl.BlockSpec((tb, w), lambda i: (i, 0)) for w, _ in outs]
    out_specs += [pl.BlockSpec((1, w), lambda i: (0, 0)) for w in reds]
    out_shape = [jax.ShapeDtypeStruct((nrows, w), dt) for w, dt in outs]
    out_shape += [jax.ShapeDtypeStruct((1, w), F32) for w in reds]
    res = _pcall(body, name=name, grid=(nblk,), in_specs=in_specs, out_specs=out_specs,
                 out_shape=out_shape, compiler_params=_CP)(*[r[0] for r in rows], *vecs)
    return res


_DIMS = {"nn": (((1,), (0,)), ((), ())), "nt": (((1,), (1,)), ((), ())), "tn": (((0,), (0,)), ((), ()))}


def _matmul(name, a, b, mode, out_dtype=F32, epilogue=None, rows=(), vecs=(), outs=None):
    if mode == "nn":
        (m, k), n = a.shape, b.shape[1]
    elif mode == "nt":
        (m, k), n = a.shape, b.shape[0]
    else:
        (k, m), n = a.shape, b.shape[1]
    tm = _tile(m, 512)
    tn = next((t for t in (1536, 1024) if n % t == 0 and n > t), _tile(n, 512))
    dims = _DIMS[mode]
    outs = [out_dtype] if outs is None else outs
    n_extra = len(rows) + len(vecs)

    def body(a_ref, b_ref, *refs):
        acc = lax.dot_general(a_ref[...].astype(BF16), b_ref[...].astype(BF16), dims, preferred_element_type=F32)
        res = (acc,) if epilogue is None else epilogue(acc, *[r[...] for r in refs[:n_extra]])
        for o_ref, val in zip(refs[n_extra:], res):
            o_ref[...] = val.astype(o_ref.dtype)

    a_spec = pl.BlockSpec((k, tm), lambda j, i: (0, i)) if mode == "tn" else pl.BlockSpec((tm, k), lambda j, i: (i, 0))
    b_spec = pl.BlockSpec((tn, k), lambda j, i: (j, 0)) if mode == "nt" else pl.BlockSpec((k, tn), lambda j, i: (0, j))
    blk = pl.BlockSpec((tm, tn), lambda j, i: (i, j))
    res = _pcall(body, name=name, grid=(n // tn, m // tm),
                 in_specs=[a_spec, b_spec] + [blk] * len(rows) + [pl.BlockSpec((1, tn), lambda j, i: (0, j))] * len(vecs),
                 out_specs=[blk] * len(outs), out_shape=[jax.ShapeDtypeStruct((m, n), dt) for dt in outs],
                 compiler_params=_CP)(a, b, *rows, *vecs)
    return res[0] if len(outs) == 1 else res


class _Comm:
    KS = {"gather_all": (1, 2, 3, 4, 5, 6, 7), "gather_own": (1, 2, 4, 6), "gather_fwd": (2, 4, 6),
          "scatter": (1, 2, 3, 4, 5, 6, 7)}

    def __init__(self, kind, srcs, lands=None, layers=None, depth=None):
        self.kind, self.srcs, self.layers = kind, list(srcs), layers
        self.n = len(lands) if kind == "gather_fwd" else len(srcs)
        self.lands = list(lands) if lands is not None else [None] * self.n
        self.out_shapes = []
        for i, land in enumerate(self.lands):
            if land is not None:
                self.out_shapes.append(jax.ShapeDtypeStruct(land.shape, land.dtype))
            elif kind == "scatter":
                self.out_shapes.append(jax.ShapeDtypeStruct((NDEV, depth) + srcs[i].shape[1:], srcs[i].dtype))
            else:
                self.out_shapes.append(jax.ShapeDtypeStruct((NDEV,) + srcs[i].shape, srcs[i].dtype))
        self.operands = self.srcs + [t for t in self.lands if t is not None]
        self.scratch = [pltpu.SemaphoreType.DMA((NDEV - 1, self.n)), pltpu.SemaphoreType.DMA((NDEV - 1, self.n)),
                        pltpu.SemaphoreType.DMA((self.n,))]

    def aliases(self, first_in, first_out):
        given = [i for i, t in enumerate(self.lands) if t is not None]
        return {first_in + len(self.srcs) + pos: first_out + i for pos, i in enumerate(given)}

    def copies(self, in_refs, out_refs, send_sems, recv_sems, local_sems):
        x, y, c = lax.axis_index("x"), lax.axis_index("y"), lax.axis_index("c")
        me = 4 * x + 2 * y + c
        cps = []
        if self.kind != "gather_fwd":
            for i in range(self.n):
                src = in_refs[i].at[me] if self.kind == "scatter" else in_refs[i]
                dst = out_refs[i].at[me, self.layers[i]] if self.kind == "scatter" else out_refs[i].at[me]
                cps.append(pltpu.make_async_copy(src, dst, local_sems.at[i]))
        for k in self.KS[self.kind]:
            px = 1 - x if k & 4 else x
            py = 1 - y if k & 2 else y
            pc = 1 - c if k & 1 else c
            peer = 4 * px + 2 * py + pc
            for i in range(self.n):
                if self.kind == "gather_fwd":
                    src, dst, to = in_refs[i].at[peer], out_refs[i].at[peer], (x, y, 1 - c)
                elif self.kind == "scatter":
                    src, dst, to = in_refs[i].at[peer], out_refs[i].at[me, self.layers[i]], (px, py, pc)
                else:
                    src, dst, to = in_refs[i], out_refs[i].at[me], (px, py, pc)
                cps.append(pltpu.make_async_remote_copy(
                    src_ref=src, dst_ref=dst, send_sem=send_sems.at[k - 1, i], recv_sem=recv_sems.at[k - 1, i],
                    device_id=to, device_id_type=pl.DeviceIdType.MESH))
        return cps


_ANY = pl.BlockSpec(memory_space=pl.ANY)


def _exchange(name, comm):
    nci = len(comm.operands)

    def body(*refs):
        cps = comm.copies(refs[:nci], refs[nci:nci + comm.n], *refs[nci + comm.n:])
        for cp in cps:
            cp.start()
        for cp in cps:
            cp.wait()

    return _pcall(body, name=name, in_specs=[_ANY] * nci, out_specs=[_ANY] * comm.n, out_shape=comm.out_shapes,
                  scratch_shapes=comm.scratch, input_output_aliases=comm.aliases(0, 0))(*comm.operands)


def _hosted(name, body, grid, arrays, in_specs, out_shapes, out_specs, comm):
    if comm is None:
        return _pcall(body, name=name, grid=grid, in_specs=in_specs, out_specs=out_specs, out_shape=out_shapes,
                      compiler_params=_CP)(*arrays), []
    ni, no, nci = len(arrays), len(out_shapes), len(comm.operands)

    def full(*refs):
        ins, cin = refs[:ni], refs[ni:ni + nci]
        outs = refs[ni + nci:ni + nci + no]
        cout = refs[ni + nci + no:ni + nci + no + comm.n]
        sems = refs[ni + nci + no + comm.n:]
        first = functools.reduce(jnp.logical_and, [pl.program_id(a) == 0 for a in range(len(grid))])
        last = functools.reduce(jnp.logical_and, [pl.program_id(a) == grid[a] - 1 for a in range(len(grid))])

        @pl.when(first)
        def _():
            for cp in comm.copies(cin, cout, *sems):
                cp.start()

        body(*ins, *outs)

        @pl.when(last)
        def _():
            for cp in comm.copies(cin, cout, *sems):
                cp.wait()

    res = _pcall(full, name=name, grid=grid, in_specs=list(in_specs) + [_ANY] * nci,
                 out_specs=list(out_specs) + [_ANY] * comm.n, out_shape=list(out_shapes) + comm.out_shapes,
                 scratch_shapes=comm.scratch, input_output_aliases=comm.aliases(ni, no),
                 compiler_params=_CP)(*arrays, *comm.operands)
    return res[:no], res[no:]


def _dot_nt(a, b):
    return lax.dot_general(a, b, _DIMS["nt"], preferred_element_type=F32)


def _dot_tn(a, b):
    return lax.dot_general(a, b, _DIMS["tn"], preferred_element_type=F32)


def _dot_nn(a, b):
    return jnp.dot(a, b, preferred_element_type=F32)


def _tri(tk, rel):
    j = lax.broadcasted_iota(jnp.int32, (tk, tk), 0)
    s = lax.broadcasted_iota(jnp.int32, (tk, tk), 1)
    return {"after": j > s, "upto": j <= s, "before": j < s}[rel].astype(BF16)


def _head_masks():
    lane = lax.broadcasted_iota(jnp.int32, (1, LANES), 1)
    return [(lane // HEAD_DIM) == h for h in range(2)]


ROW_CHUNK = 32


def _by_rows(fn, n_out, *arrays):
    rows = arrays[0].shape[0]
    step = min(ROW_CHUNK, rows)
    outs = [[] for _ in range(n_out)]
    for r in range(0, rows, step):
        for o, val in zip(outs, fn(r, *[a[r:r + step] for a in arrays])):
            o.append(val)
    return [jnp.concatenate(o, axis=0) for o in outs]


def _causal(r0, k0, rows, tk, strict):
    row = lax.broadcasted_iota(jnp.int32, (rows, tk), 0) + r0
    col = lax.broadcasted_iota(jnp.int32, (rows, tk), 1) + k0
    return col < row if strict else col <= row


def _wide(stat, width):
    return stat if width == LANES else jnp.concatenate([stat] * (width // LANES), axis=1)


def _row_sum(v):
    return jnp.broadcast_to(jnp.sum(v, axis=1, keepdims=True), (v.shape[0], LANES))


def _split_bf16(v):
    hi = v.astype(BF16)
    return hi, (v - hi.astype(F32)).astype(BF16)


def _sb_logs(z, scale, mask):
    z = z * scale
    e = jnp.exp(-jnp.abs(z))
    log_sig = jnp.minimum(z, 0.0) - jnp.log(1.0 + e)
    log_fail = log_sig - z
    return z, log_sig, (log_fail if mask is None else jnp.where(mask, log_fail, 0.0))


def _two_loops(n_full, nkb, near_first, step, carry):
    if near_first:
        carry = lax.fori_loop(0, nkb - n_full, lambda j, c: step(nkb - 1 - j, True, c), carry)
        return lax.fori_loop(0, n_full, lambda j, c: step(n_full - 1 - j, False, c), carry)
    carry = lax.fori_loop(0, n_full, lambda j, c: step(j, False, c), carry)
    return lax.fori_loop(n_full, nkb, lambda j, c: step(j, True, c), carry)


def _sb_fwd(name, p, nhp, tq, tk, comm=None):
    s = p.shape[0]
    scale = HEAD_DIM ** -0.5
    nq = s // tq

    def body(q_ref, k_ref, v_ref, o_ref, tot_ref):
        qi = pl.program_id(1)
        masks = _head_masks()
        after = _tri(tk, "after")
        nkb = ((qi + 1) * tq + tk - 1) // tk
        q = q_ref[...]
        qhs = [jnp.where(hm, q, 0.0).astype(BF16) for hm in masks]

        def step(kb, masked, carry):
            ks = pl.multiple_of(kb * tk, tk)
            k = k_ref[pl.ds(ks, tk), :].astype(BF16)
            v = v_ref[pl.ds(ks, tk), :].astype(BF16)
            mask_of = lambda r, n: _causal(qi * tq + r, ks, n, tk, True) if masked else None
            new = []
            for qh, (later, acc) in zip(qhs, carry):

                def logs(r, zc):
                    _, log_sig, log_fail = _sb_logs(zc, scale, mask_of(r, zc.shape[0]))
                    return (log_sig,) + _split_bf16(log_fail) + (_row_sum(log_fail),)

                log_sig, hi, lo, fail_sum = _by_rows(logs, 4, _dot_nt(qh, k))
                run = _dot_nn(hi, after) + _dot_nn(lo, after)

                def weights(r, lsc, runc, laterc):
                    w = jnp.exp(lsc + runc + _wide(laterc, tk))
                    return ((jnp.where(mask_of(r, w.shape[0]), w, 0.0) if masked else w).astype(BF16),)

                w = _by_rows(weights, 1, log_sig, run, later)[0]
                new.append((later + fail_sum, acc + _dot_nn(w, v)))
            return tuple(new)

        init = (jnp.zeros((tq, LANES), F32), jnp.zeros((tq, LANES), F32))
        (tot0, acc0), (tot1, acc1) = _two_loops((qi * tq) // tk, nkb, True, step, (init, init))
        o_ref[...] = jnp.where(masks[0], acc0, acc1)
        tot_ref[...] = jnp.where(masks[0], tot0, tot1)

    blk = pl.BlockSpec((tq, LANES), lambda h, i: (i, h))
    shape = jax.ShapeDtypeStruct((s, nhp * LANES), F32)
    return _hosted(name, body, (nhp, nq), [p, p, p],
                   [blk, pl.BlockSpec((s, LANES), lambda h, i: (0, nhp + h)),
                    pl.BlockSpec((s, LANES), lambda h, i: (0, 2 * nhp + h))], [shape, shape], [blk, blk], comm)


def _sb_bwd(name, p, tot, do, nhp, tq, tk, comm=None):
    s = p.shape[0]
    scale = HEAD_DIM ** -0.5
    nq = s // tq

    def body(q_ref, k_ref, v_ref, tot_ref, do_ref, dq_ref, dk_ref, dv_ref):
        qi = pl.program_id(1)

        @pl.when(qi == 0)
        def _():
            dk_ref[...] = jnp.zeros_like(dk_ref)
            dv_ref[...] = jnp.zeros_like(dv_ref)

        masks = _head_masks()
        upto, before = _tri(tk, "upto"), _tri(tk, "before")
        nkb = ((qi + 1) * tq + tk - 1) // tk
        q = q_ref[...]
        qb = q.astype(BF16)
        dout = do_ref[...]
        doutb = dout.astype(BF16)
        qhs = [jnp.where(hm, q, 0.0).astype(BF16) for hm in masks]
        dohs = [jnp.where(hm, dout, 0.0).astype(BF16) for hm in masks]
        tot = tot_ref[...]
        totals = [jnp.broadcast_to(tot[:, h * HEAD_DIM:h * HEAD_DIM + 1], (tq, LANES)) for h in range(2)]

        def step(kb, masked, carry):
            ks = pl.multiple_of(kb * tk, tk)
            k = k_ref[pl.ds(ks, tk), :].astype(BF16)
            v = v_ref[pl.ds(ks, tk), :].astype(BF16)
            mask_of = lambda r, n: _causal(qi * tq + r, ks, n, tk, True) if masked else None
            new, dks, dvs = [], [], []
            for qh, doh, total, (fail_before, g_before, dq) in zip(qhs, dohs, totals, carry):

                def logs(r, zc):
                    _, log_sig, log_fail = _sb_logs(zc, scale, mask_of(r, zc.shape[0]))
                    return (log_sig,) + _split_bf16(log_fail) + (_row_sum(log_fail),)

                z = _dot_nt(qh, k)
                log_sig, hi, lo, fail_sum = _by_rows(logs, 4, z)
                run = _dot_nn(hi, upto) + _dot_nn(lo, upto)

                def weights(r, lsc, runc, basec, dwc):
                    w = jnp.exp(lsc + (_wide(basec, tk) - runc))
                    if masked:
                        w = jnp.where(mask_of(r, w.shape[0]), w, 0.0)
                    g = w * dwc
                    return (w.astype(BF16), g) + _split_bf16(g) + (_row_sum(g),)

                w, g, ghi, glo, g_sum = _by_rows(weights, 5, log_sig, run, total - fail_before, _dot_nt(doh, v))
                g_run = _dot_nn(ghi, before) + _dot_nn(glo, before)

                def dscore(r, gc, lsc, zc, grc, gbc):
                    dz = gc * jnp.exp(lsc - zc * scale) - jnp.exp(lsc) * (_wide(gbc, tk) + grc)
                    if masked:
                        dz = jnp.where(mask_of(r, dz.shape[0]), dz, 0.0)
                    return ((dz * scale).astype(BF16),)

                dzb = _by_rows(dscore, 1, g, log_sig, z, g_run, g_before)[0]
                dks.append(_dot_tn(dzb, qb))
                dvs.append(_dot_tn(w, doutb))
                new.append((fail_before + fail_sum, g_before + g_sum, dq + _dot_nn(dzb, k)))
            dk_ref[pl.ds(ks, tk), :] += jnp.where(masks[0], dks[0], dks[1])
            dv_ref[pl.ds(ks, tk), :] += jnp.where(masks[0], dvs[0], dvs[1])
            return tuple(new)

        zero = jnp.zeros((tq, LANES), F32)
        (_, _, dq0), (_, _, dq1) = _two_loops((qi * tq) // tk, nkb, False, step, ((zero, zero, zero),) * 2)
        dq_ref[...] = jnp.where(masks[0], dq0, dq1)

    blk = pl.BlockSpec((tq, LANES), lambda h, i: (i, h))
    full = pl.BlockSpec((s, LANES), lambda h, i: (0, h))
    shape = jax.ShapeDtypeStruct((s, nhp * LANES), F32)
    return _hosted(name, body, (nhp, nq), [p, p, p, tot, do],
                   [blk, pl.BlockSpec((s, LANES), lambda h, i: (0, nhp + h)),
                    pl.BlockSpec((s, LANES), lambda h, i: (0, 2 * nhp + h)), blk, blk],
                   [shape, shape, shape], [blk, full, full], comm)


def _mla_fwd(name, q, k, v, tq, tk, comm=None):
    s = q.shape[0]
    nhp = v.shape[1] // LANES
    scale = MLA_QK_DIM ** -0.5
    nq = s // tq

    def body(q_ref, k_ref, v_ref, o_ref, lse_ref):
        qi = pl.program_id(1)
        masks = _head_masks()
        nkb = ((qi + 1) * tq + tk - 1) // tk
        qhs = [q_ref[:, h * LANES:(h + 1) * LANES] for h in range(2)]

        def step(kb, masked, carry):
            ks = pl.multiple_of(kb * tk, tk)
            vv = v_ref[pl.ds(ks, tk), :]
            new = []
            for h, (m, l, acc) in enumerate(carry):
                kh = k_ref[pl.ds(ks, tk), h * LANES:(h + 1) * LANES]

                def soft(r, zc, mc, lc):
                    zc = zc * scale
                    if masked:
                        zc = jnp.where(_causal(qi * tq + r, ks, zc.shape[0], tk, False), zc, -1e30)
                    m_new = jnp.maximum(mc, jnp.max(zc, axis=1, keepdims=True))
                    a = jnp.exp(mc - m_new)
                    pr = jnp.exp(zc - _wide(m_new, tk))
                    return pr.astype(BF16), m_new, a * lc + _row_sum(pr), a

                pr, m_new, l_new, a = _by_rows(soft, 4, _dot_nt(qhs[h], kh), m, l)
                acc = _by_rows(lambda r, ac, aa, pc: (aa * ac + pc,), 1, acc, a, _dot_nn(pr, vv))[0]
                new.append((m_new, l_new, acc))
            return tuple(new)

        init = (jnp.full((tq, LANES), -1e30, F32), jnp.zeros((tq, LANES), F32), jnp.zeros((tq, LANES), F32))
        (m0, l0, acc0), (m1, l1, acc1) = _two_loops((qi * tq) // tk, nkb, False, step, (init, init))
        o_ref[...] = jnp.where(masks[0], acc0 / l0, acc1 / l1)
        lse_ref[...] = jnp.where(masks[0], m0 + jnp.log(l0), m1 + jnp.log(l1))

    shape = jax.ShapeDtypeStruct((s, nhp * LANES), F32)
    blk = pl.BlockSpec((tq, LANES), lambda h, i: (i, h))
    return _hosted(name, body, (nhp, nq), [q, k, v],
                   [pl.BlockSpec((tq, 2 * LANES), lambda h, i: (i, h)),
                    pl.BlockSpec((s, 2 * LANES), lambda h, i: (0, h)),
                    pl.BlockSpec((s, LANES), lambda h, i: (0, h))], [shape, shape], [blk, blk], comm)


def _mla_bwd(name, q, k, v, o, lse, do, tq, tk, comm=None):
    s = q.shape[0]
    nhp = v.shape[1] // LANES
    scale = MLA_QK_DIM ** -0.5
    nq = s // tq

    def body(q_ref, k_ref, v_ref, o_ref, lse_ref, do_ref, dq_ref, dk_ref, dv_ref):
        qi = pl.program_id(1)

        @pl.when(qi == 0)
        def _():
            dk_ref[...] = jnp.zeros_like(dk_ref)
            dv_ref[...] = jnp.zeros_like(dv_ref)

        masks = _head_masks()
        nkb = ((qi + 1) * tq + tk - 1) // tk
        dout = do_ref[...]
        doutb = dout.astype(BF16)
        prod = dout * o_ref[...]
        qhs = [q_ref[:, h * LANES:(h + 1) * LANES] for h in range(2)]
        dohs = [jnp.where(hm, dout, 0.0).astype(BF16) for hm in masks]
        totals = [_row_sum(jnp.where(hm, prod, 0.0)) for hm in masks]
        lse = lse_ref[...]
        lses = [jnp.broadcast_to(lse[:, h * HEAD_DIM:h * HEAD_DIM + 1], (tq, LANES)) for h in range(2)]

        def step(kb, masked, carry):
            ks = pl.multiple_of(kb * tk, tk)
            vv = v_ref[pl.ds(ks, tk), :]
            new, dvs = [], []
            for h, dq in enumerate(carry):
                kh = k_ref[pl.ds(ks, tk), h * LANES:(h + 1) * LANES]

                def probs(r, zc, dpc, lsec, totc):
                    pr = jnp.exp(zc * scale - _wide(lsec, tk))
                    if masked:
                        pr = jnp.where(_causal(qi * tq + r, ks, pr.shape[0], tk, False), pr, 0.0)
                    return pr.astype(BF16), (pr * (dpc - _wide(totc, tk)) * scale).astype(BF16)

                pr, ds = _by_rows(probs, 2, _dot_nt(qhs[h], kh), _dot_nt(dohs[h], vv), lses[h], totals[h])
                dk_ref[pl.ds(ks, tk), h * LANES:(h + 1) * LANES] += _dot_tn(ds, qhs[h])
                dvs.append(_dot_tn(pr, doutb))
                new.append(dq + _dot_nn(ds, kh))
            dv_ref[pl.ds(ks, tk), :] += jnp.where(masks[0], dvs[0], dvs[1])
            return tuple(new)

        zero = jnp.zeros((tq, LANES), F32)
        dq0, dq1 = _two_loops((qi * tq) // tk, nkb, False, step, (zero, zero))
        dq_ref[:, :LANES] = dq0
        dq_ref[:, LANES:] = dq1

    blk = pl.BlockSpec((tq, LANES), lambda h, i: (i, h))
    blk2 = pl.BlockSpec((tq, 2 * LANES), lambda h, i: (i, h))
    full = pl.BlockSpec((s, LANES), lambda h, i: (0, h))
    full2 = pl.BlockSpec((s, 2 * LANES), lambda h, i: (0, h))
    return _hosted(name, body, (nhp, nq), [q, k, v, o, lse, do], [blk2, full2, full, blk, blk, blk],
                   [jax.ShapeDtypeStruct(q.shape, F32), jax.ShapeDtypeStruct(k.shape, F32),
                    jax.ShapeDtypeStruct(v.shape, F32)], [blk2, full2, full], comm)


def _norm_parts(x):
    r = lax.rsqrt(jnp.mean(x * x, axis=-1, keepdims=True) + NORM_EPS)
    return r, x * r


def _rmsmod_fwd(x, g, sc, sh):
    _, xh = _norm_parts(x)
    return ((xh * g) * (1.0 + sc) + sh,)


def _rmsmod_bwd(dh, x, dres, g, sc):
    r, xh = _norm_parts(x)
    dy = dh * (1.0 + sc)
    dxh = dy * g
    dx = r * (dxh - xh * jnp.mean(dxh * xh, axis=-1, keepdims=True)) + dres
    return dx, dh, dh * (xh * g), dy * xh


def _rms_bwd_plain(dh, x, g):
    r, xh = _norm_parts(x)
    dxh = dh * g
    return r * (dxh - xh * jnp.mean(dxh * xh, axis=-1, keepdims=True)), dh * xh


def _cat(parts):
    return jnp.concatenate(parts, axis=1)


def _swap_halves(a):
    half = a.shape[-1] // 2
    return jnp.concatenate([a[..., half:], a[..., :half]], axis=-1)


def _adamw_fn(w, g, m, v):
    m = ADAM_B1 * m + (1.0 - ADAM_B1) * g
    v = ADAM_B2 * v + (1.0 - ADAM_B2) * jnp.square(g)
    m_hat = m / (1.0 - ADAM_B1 ** ADAM_STEP)
    v_hat = v / (1.0 - ADAM_B2 ** ADAM_STEP)
    delta = -ADAM_LR * (m_hat / (jnp.sqrt(v_hat) + ADAM_EPS) + ADAM_WD * w)
    return delta, m, v


def _adamw(name, w, g, m, v):
    shape = w.shape
    width = shape[-1]
    flat = [t.reshape(-1, width) for t in (w, g, m, v)]
    res = _rowwise(name, _adamw_fn, flat, [], [(width, F32)] * 3)
    return [t.reshape(shape) for t in res]


def _sum_adamw(name, land, w, m, v):
    shape = w.shape
    width = shape[-1]
    rows = w.size // width

    def fn(*blocks):
        g = blocks[0].astype(F32)
        for b in blocks[1:NDEV]:
            g = g + b.astype(F32)
        return (g,) + _adamw_fn(blocks[NDEV], g, blocks[NDEV + 1], blocks[NDEV + 2])

    def fn_whole(wb, mb, vb, lb):
        return fn(*[lb[i] for i in range(NDEV)], wb, mb, vb)

    flat = [t.reshape(rows, width) for t in (w, m, v)]
    if rows % 16 == 0:
        views = [(land.reshape(NDEV * rows, width), width, 0, i * rows) for i in range(NDEV)]
        res = _rowwise(name, fn, views + flat, [], [(width, F32)] * 4)
    else:
        res = _rowwise(name, fn_whole, flat, [land.reshape(NDEV, rows, width)], [(width, F32)] * 4)
    return [t.reshape(shape) for t in res]


def kernel(x, c, positions, w_ada, b_ada, g_mix_norm, w_in, g_q_lat, w_q_up, g_kv_lat, w_kv_up, w_sb_out, w_mla_out, w_mix_out, g_mlp_norm, w_up, w_down, g_final, loss_target, m_w_ada, m_b_ada, m_g_mix_norm, m_w_in, m_g_q_lat, m_w_q_up, m_g_kv_lat, m_w_kv_up, m_w_sb_out, m_w_mla_out, m_w_mix_out, m_g_mlp_norm, m_w_up, m_w_down, m_g_final, v_w_ada, v_b_ada, v_g_mix_norm, v_w_in, v_g_q_lat, v_w_q_up, v_g_kv_lat, v_w_kv_up, v_w_sb_out, v_w_mla_out, v_w_mix_out, v_g_mlp_norm, v_w_up, v_w_down, v_g_final):
    seq, d = x.shape[1], x.shape[2]
    depth = w_ada.shape[0]
    qr, kvr = g_q_lat.shape[1], g_kv_lat.shape[1]
    sbw, mlaw = w_sb_out.shape[1], w_mla_out.shape[1]
    nh = mlaw // HEAD_DIM
    nhp_sb = sbw // LANES
    dff = w_up.shape[2] * NDEV
    ada_n = w_ada.shape[2]
    gb = min(512, d)
    tq, tk = min(256, seq), min(256, seq)
    me = 4 * lax.axis_index("x") + 2 * lax.axis_index("y") + lax.axis_index("c")

    o_qlat = _roundup(3 * sbw, qr)
    o_kvlat = _roundup(o_qlat + qr, kvr)
    o_rope = _roundup(o_kvlat + kvr, 2 * LANES)
    o_gate = _roundup(o_rope + 2 * LANES, gb)
    wp = o_gate + 2 * d

    c_all = _exchange("ag_c", _Comm("gather_all", [c.reshape(d // LANES, LANES)]))[0].reshape(NDEV, d)
    c_act = _rowwise("silu_c", lambda t: (t * (1.0 / (1.0 + jnp.exp(-t))),), [c_all], [], [(d, F32)])[0]
    parts = jnp.stack([_matmul("ada_fwd", c_act, w_ada[l], "nn") for l in range(depth)])
    parts_all = _exchange("ag_mod", _Comm("gather_all", [parts]))[0]
    mine = jnp.transpose(lax.dynamic_index_in_dim(parts_all, me, axis=2, keepdims=False), (1, 0, 2))
    mod = _rowwise("mod_bias", lambda a, b: (a + b,), [mine.reshape(depth, NDEV * ada_n), b_ada], [],
                   [(6 * d, F32)])[0]
    mods = [[mod[l:l + 1, i * d:(i + 1) * d] for i in range(6)] for l in range(depth)]

    big = [w_in, w_q_up, w_kv_up, w_sb_out, w_mla_out, w_mix_out, w_up, w_down]
    row_sharded = [False, False, False, False, False, True, False, True]
    shards = [[w[l].astype(BF16) for w in big] for l in range(depth)]

    def unpack_layer(gathered):
        out = []
        for g, by_rows in zip(gathered, row_sharded):
            _, rows, cols = g.shape
            if by_rows:
                out.append(g.reshape(NDEV * rows, cols))
            else:
                out.append(jnp.transpose(g, (1, 0, 2)).reshape(rows, NDEV * cols))
        return out

    def derive(full):
        wi, wq, wkv, wsb, wmla, wmix, wu, wd = full
        dt = wi.dtype
        z = lambda r, n: jnp.zeros((r, n), dt)
        o = 3 * sbw
        kr = wi[:, o + qr + kvr:o + qr + kvr + ROPE_DIM]
        g0 = o + qr + kvr + ROPE_DIM
        w_in_pad = _cat([wi[:, :o], z(d, o_qlat - o), wi[:, o:o + qr], z(d, o_kvlat - o_qlat - qr),
                         wi[:, o + qr:o + qr + kvr], z(d, o_rope - o_kvlat - kvr),
                         z(d, HEAD_DIM), kr, z(d, LANES - MLA_QK_DIM),
                         z(d, HEAD_DIM), _swap_halves(kr), z(d, LANES - MLA_QK_DIM),
                         z(d, o_gate - o_rope - 2 * LANES), wi[:, g0:]])
        wq3 = wq.reshape(qr, nh, MLA_QK_DIM)
        z3 = lambda n: jnp.zeros((qr, nh, n), dt)
        rope_w = wq3[:, :, HEAD_DIM:]
        wq_a = jnp.concatenate([wq3[:, :, :HEAD_DIM], rope_w, z3(LANES - MLA_QK_DIM)], axis=2).reshape(qr, nh * LANES)
        wq_b = jnp.concatenate([z3(HEAD_DIM), _swap_halves(rope_w), z3(LANES - MLA_QK_DIM)], axis=2).reshape(qr, nh * LANES)
        wkv3 = wkv.reshape(kvr, nh, 2 * HEAD_DIM)
        wk = jnp.concatenate([wkv3[:, :, :HEAD_DIM], jnp.zeros((kvr, nh, HEAD_DIM), dt)], axis=2).reshape(kvr, nh * LANES)
        wv = wkv3[:, :, HEAD_DIM:].reshape(kvr, nh * HEAD_DIM)
        return dict(w_in=w_in_pad, w_q=_cat([wq_a, wq_b]), w_kv=_cat([wk, wv]), w_sb=wsb, w_mla=wmla,
                    w_mix=wmix, w_up=wu, w_down=wd)

    def fold(gr):
        gi, gq, gkv = gr["w_in"], gr["w_q"], gr["w_kv"]
        o = 3 * sbw
        ra = gi[:, o_rope + HEAD_DIM:o_rope + MLA_QK_DIM]
        rb = gi[:, o_rope + LANES + HEAD_DIM:o_rope + LANES + MLA_QK_DIM]
        g_in = _cat([gi[:, :o], gi[:, o_qlat:o_qlat + qr], gi[:, o_kvlat:o_kvlat + kvr], ra + _swap_halves(rb),
                     gi[:, o_gate:]])
        ga = gq[:, :nh * LANES].reshape(qr, nh, LANES)
        gb_ = gq[:, nh * LANES:].reshape(qr, nh, LANES)
        g_q = jnp.concatenate([ga[:, :, :HEAD_DIM], ga[:, :, HEAD_DIM:MLA_QK_DIM]
                               + _swap_halves(gb_[:, :, HEAD_DIM:MLA_QK_DIM])], axis=2).reshape(qr, nh * MLA_QK_DIM)
        gk = gkv[:, :nh * LANES].reshape(kvr, nh, LANES)[:, :, :HEAD_DIM]
        gv = gkv[:, nh * LANES:].reshape(kvr, nh, HEAD_DIM)
        g_kv = jnp.concatenate([gk, gv], axis=2).reshape(kvr, nh * 2 * HEAD_DIM)
        return [g_in, g_q, g_kv, gr["w_sb"], gr["w_mla"], gr["w_mix"], gr["w_up"], gr["w_down"]]

    first = _exchange("ag_w0_own", _Comm("gather_own", shards[0]))
    weights = [derive(unpack_layer(_exchange("ag_w0_fwd", _Comm("gather_fwd", [], lands=first))))]

    inv_freq = 1.0 / (ROPE_THETA ** (jnp.arange(0, ROPE_DIM, 2, dtype=F32) / ROPE_DIM))
    ang = positions[0].astype(F32)[:, None] * inv_freq
    cos, sin = jnp.cos(ang), jnp.sin(ang)
    tail = jnp.zeros((seq, LANES - MLA_QK_DIM), F32)
    rope_c = _cat([jnp.ones((seq, HEAD_DIM), F32), cos, cos, tail])
    rope_s = _cat([jnp.zeros((seq, HEAD_DIM), F32), -sin, sin, tail])
    zero_vec = lambda n: jnp.zeros((1, n), F32)

    def rope_fwd(q2, kvs, pd, tc, ts):
        c8, s8 = _cat([tc] * nh), _cat([ts] * nh)
        qf = q2[:, :nh * LANES] * c8 + q2[:, nh * LANES:] * s8
        kpe = pd[:, :LANES] * tc + pd[:, LANES:] * ts
        return qf, kvs[:, :nh * LANES] + _cat([kpe] * nh), kvs[:, nh * LANES:]

    def rope_bwd(dq, dk, dv, tc, ts):
        c8, s8 = _cat([tc] * nh), _cat([ts] * nh)
        dks = dk[:, :LANES]
        for h in range(1, nh):
            dks = dks + dk[:, h * LANES:(h + 1) * LANES]
        return _cat([dq * c8, dq * s8]), _cat([dk, dv]), _cat([dks * tc, dks * ts])

    def merge_fwd(*a):
        ng = d // gb
        gs, gm, osb, omla = _cat(a[:ng]), _cat(a[ng:2 * ng]), a[2 * ng], a[2 * ng + 1]
        return (osb / (1.0 + jnp.exp(-gs)) + omla / (1.0 + jnp.exp(-gm)),)

    def merge_bwd(*a):
        ng = d // gb
        gs, gm, osb, omla, dm = _cat(a[:ng]), _cat(a[ng:2 * ng]), a[2 * ng], a[2 * ng + 1], a[2 * ng + 2]
        ss, sm = 1.0 / (1.0 + jnp.exp(-gs)), 1.0 / (1.0 + jnp.exp(-gm))
        return ss * dm, sm * dm, _cat([dm * osb * ss * (1.0 - ss), dm * omla * sm * (1.0 - sm)])

    def gate_cols(p):
        ng = d // gb
        return [(p, gb, o_gate // gb + i, 0) for i in range(2 * ng)]

    xs = x[0]
    saved = []
    for l in range(depth):
        w = weights[l]
        sh1, sc1, g1, sh2, sc2, g2 = mods[l]
        h1 = _rowwise("norm1", _rmsmod_fwd, [xs], [g_mix_norm[l:l + 1], sc1, sh1], [(d, BF16)])[0]
        p = _matmul("in_proj", h1, w["w_in"], "nn")
        nxt = _Comm("gather_own", shards[l + 1]) if l + 1 < depth else None
        (o_sb, tot_sb), part = _sb_fwd("sb_fwd", p, nhp_sb, tq, tk, nxt)
        y_sb = _matmul("sb_out", o_sb, w["w_sb"], "nn")
        qn = _rowwise("norm_q", _rmsmod_fwd, [(p, qr, o_qlat // qr, 0)],
                      [g_q_lat[l:l + 1], zero_vec(qr), zero_vec(qr)], [(qr, BF16)])[0]
        kvn = _rowwise("norm_kv", _rmsmod_fwd, [(p, kvr, o_kvlat // kvr, 0)],
                       [g_kv_lat[l:l + 1], zero_vec(kvr), zero_vec(kvr)], [(kvr, BF16)])[0]
        q2 = _matmul("q_up", qn, w["w_q"], "nn")
        kvs = _matmul("kv_up", kvn, w["w_kv"], "nn")
        qf, kf, vf = _rowwise("rope_fwd", rope_fwd, [q2, kvs, (p, 2 * LANES, o_rope // (2 * LANES), 0), rope_c, rope_s],
                              [], [(nh * LANES, BF16), (nh * LANES, BF16), (mlaw, BF16)])
        (o_mla, lse), full = _mla_fwd("mla_fwd", qf, kf, vf, tq, tk,
                                      _Comm("gather_fwd", [], lands=part) if part else None)
        if full:
            weights.append(derive(unpack_layer(full)))
        y_mla = _matmul("mla_out", o_mla, w["w_mla"], "nn")
        merged = _rowwise("merge_fwd", merge_fwd, gate_cols(p) + [y_sb, y_mla], [], [(d, BF16)])[0]
        resid = lambda acc, xv, g: (acc, xv + g * acc)
        y1, x_mid = _matmul("mix_out", merged, w["w_mix"], "nn", epilogue=resid, rows=[xs], vecs=[g1], outs=[F32, F32])
        h2 = _rowwise("norm2", _rmsmod_fwd, [x_mid], [g_mlp_norm[l:l + 1], sc2, sh2], [(d, BF16)])[0]
        u, act = _matmul("mlp_up", h2, w["w_up"], "nn", outs=[F32, BF16],
                         epilogue=lambda acc: (acc, jnp.square(jnp.maximum(acc, 0.0))))
        y2, x_out = _matmul("mlp_down", act, w["w_down"], "nn", epilogue=resid, rows=[x_mid], vecs=[g2], outs=[F32, F32])
        saved.append(dict(x=xs, h1=h1, p=p, o_sb=o_sb, tot_sb=tot_sb, y_sb=y_sb, qn=qn, kvn=kvn, qf=qf, kf=kf, vf=vf, o_mla=o_mla,
                          lse=lse, y_mla=y_mla, merged=merged, y1=y1, x_mid=x_mid, h2=h2, u=u, act=act, y2=y2))
        xs = x_out

    def final_fn(xv, tv, g):
        r, xh = _norm_parts(xv)
        diff = xh * g - tv
        dy = diff * (1.0 / d)
        dxh = dy * g
        dx = r * (dxh - xh * jnp.mean(dxh * xh, axis=-1, keepdims=True))
        return dx, diff * diff, dy * xh

    dx, sq, dg_final = _rowwise("loss_head", final_fn, [xs, loss_target[0]], [g_final.reshape(1, d)],
                                [(d, F32)], reds=[d, d])
    loss = lax.psum(0.5 * jnp.sum(sq) / d, ("x", "y", "c"))

    def chunk(gfull, wref, by_rows):
        rows, cols = wref.shape[1], wref.shape[2]
        if by_rows:
            return gfull.reshape(NDEV, rows, cols).astype(BF16)
        return jnp.transpose(gfull.reshape(rows, NDEV, cols), (1, 0, 2)).astype(BF16)

    dmods, small = [None] * depth, [None] * depth
    late, lands = None, [None] * len(big)
    for l in reversed(range(depth)):
        w, sv = weights[l], saved[l]
        sh1, sc1, g1, sh2, sc2, g2 = mods[l]
        gr = {}
        dy2, dgate2 = _rowwise("gate2_bwd", lambda dxv, y, g: (dxv * g, dxv * y), [dx, sv["y2"]], [g2],
                               [(d, BF16)], reds=[d])
        du = _matmul("mlp_down_dx", dy2, w["w_down"], "nt", outs=[BF16], rows=[sv["u"]],
                     epilogue=lambda acc, uv: (acc * 2.0 * jnp.maximum(uv, 0.0),))
        gr["w_down"] = _matmul("mlp_down_dw", sv["act"], dy2, "tn")
        dh2 = _matmul("mlp_up_dx", du, w["w_up"], "nt")
        gr["w_up"] = _matmul("mlp_up_dw", sv["h2"], du, "tn")
        dx_mid, dsh2, dsc2, dg_mlp = _rowwise("norm2_bwd", _rmsmod_bwd, [dh2, sv["x_mid"], dx],
                                              [g_mlp_norm[l:l + 1], sc2], [(d, F32)], reds=[d, d, d])
        dy1, dgate1 = _rowwise("gate1_bwd", lambda dxv, y, g: (dxv * g, dxv * y), [dx_mid, sv["y1"]], [g1],
                               [(d, BF16)], reds=[d])
        dmerged = _matmul("mix_out_dx", dy1, w["w_mix"], "nt")
        gr["w_mix"] = _matmul("mix_out_dw", sv["merged"], dy1, "tn")
        dy_sb, dy_mla, dgates = _rowwise("merge_bwd", merge_bwd, gate_cols(sv["p"]) + [sv["y_sb"], sv["y_mla"], dmerged],
                                         [], [(d, BF16), (d, BF16), (2 * d, BF16)])
        do_sb = _matmul("sb_out_dx", dy_sb, w["w_sb"], "nt")
        gr["w_sb"] = _matmul("sb_out_dw", sv["o_sb"], dy_sb, "tn")
        do_mla = _matmul("mla_out_dx", dy_mla, w["w_mla"], "nt")
        gr["w_mla"] = _matmul("mla_out_dw", sv["o_mla"], dy_mla, "tn")
        ready = {3: gr["w_sb"], 4: gr["w_mla"], 5: gr["w_mix"], 6: gr["w_up"], 7: gr["w_down"]}
        ready = {i: chunk(g, big[i], row_sharded[i]) for i, g in ready.items()}
        ids_a = [3, 4, 5, 6] + ([0, 1, 2] if late is not None else [])
        comm_a = _Comm("scatter", [ready[i] for i in (3, 4, 5, 6)] + (late or []), [lands[i] for i in ids_a],
                       [l] * 4 + [l + 1] * 3, depth)
        comm_b = _Comm("scatter", [ready[7]], [lands[7]], [l], depth)
        (dq_sb, dk_sb, dv_sb), got_a = _sb_bwd("sb_bwd", sv["p"], sv["tot_sb"], do_sb, nhp_sb, tq, tk, comm_a)
        (dqf, dkf, dvf), got_b = _mla_bwd("mla_bwd", sv["qf"], sv["kf"], sv["vf"], sv["o_mla"], sv["lse"], do_mla,
                                          tq, tk, comm_b)
        for i, t in zip(ids_a + [7], list(got_a) + list(got_b)):
            lands[i] = t
        dq2, dkvs, drope = _rowwise("rope_bwd", rope_bwd, [dqf, dkf, dvf, rope_c, rope_s], [],
                                    [(2 * nh * LANES, BF16), (nh * LANES + mlaw, BF16), (2 * LANES, BF16)])
        dqn = _matmul("q_up_dx", dq2, w["w_q"], "nt")
        gr["w_q"] = _matmul("q_up_dw", sv["qn"], dq2, "tn")
        dkvn = _matmul("kv_up_dx", dkvs, w["w_kv"], "nt")
        gr["w_kv"] = _matmul("kv_up_dw", sv["kvn"], dkvs, "tn")
        dqlat, dg_q = _rowwise("norm_q_bwd", _rms_bwd_plain, [dqn, (sv["p"], qr, o_qlat // qr, 0)],
                               [g_q_lat[l:l + 1]], [(qr, BF16)], reds=[qr])
        dkvlat, dg_kv = _rowwise("norm_kv_bwd", _rms_bwd_plain, [dkvn, (sv["p"], kvr, o_kvlat // kvr, 0)],
                                 [g_kv_lat[l:l + 1]], [(kvr, BF16)], reds=[kvr])
        zb = lambda n: jnp.zeros((seq, n), BF16)
        dp = _cat([dq_sb.astype(BF16), dk_sb.astype(BF16), dv_sb.astype(BF16), zb(o_qlat - 3 * sbw), dqlat,
                   zb(o_kvlat - o_qlat - qr), dkvlat, zb(o_rope - o_kvlat - kvr), drope,
                   zb(o_gate - o_rope - 2 * LANES), dgates])
        dh1 = _matmul("in_proj_dx", dp, w["w_in"], "nt")
        gr["w_in"] = _matmul("in_proj_dw", sv["h1"], dp, "tn")
        dx, dsh1, dsc1, dg_mix = _rowwise("norm1_bwd", _rmsmod_bwd, [dh1, sv["x"], dx_mid],
                                          [g_mix_norm[l:l + 1], sc1], [(d, F32)], reds=[d, d, d])
        dmods[l] = _cat([dsh1, dsc1, dgate1, dsh2, dsc2, dgate2])
        small[l] = (dg_mix, dg_q, dg_kv, dg_mlp)
        late = [chunk(g, big[i], row_sharded[i]) for i, g in enumerate(fold(gr)[:3])]

    small_parts = [jnp.concatenate(dmods, axis=0)]
    small_parts += [jnp.concatenate([small[l][i] for l in range(depth)], axis=0) for i in range(4)]
    small_parts.append(dg_final)
    small_all = _exchange("ag_small", _Comm("gather_all", small_parts))

    dmod_mine = lax.dynamic_slice_in_dim(small_all[0], me * ada_n, ada_n, axis=2)
    c_act_t = jnp.transpose(c_act)

    def outer_fn(ct, dm):
        acc = ct[:, 0:1] * dm[0:1, :]
        for b in range(1, NDEV):
            acc = acc + ct[:, b:b + 1] * dm[b:b + 1, :]
        return (acc,)

    g_w_ada = jnp.stack([_rowwise("ada_dw", outer_fn, [c_act_t], [dmod_mine[:, l, :]], [(ada_n, F32)])[0]
                         for l in range(depth)])

    landed = list(_exchange("a2a_last", _Comm("scatter", late, lands[:3], [0] * 3, depth))) + lands[3:]

    moments = dict(
        w_ada=(w_ada, m_w_ada, v_w_ada), b_ada=(b_ada, m_b_ada, v_b_ada),
        g_mix_norm=(g_mix_norm, m_g_mix_norm, v_g_mix_norm), w_in=(w_in, m_w_in, v_w_in),
        g_q_lat=(g_q_lat, m_g_q_lat, v_g_q_lat), w_q_up=(w_q_up, m_w_q_up, v_w_q_up),
        g_kv_lat=(g_kv_lat, m_g_kv_lat, v_g_kv_lat), w_kv_up=(w_kv_up, m_w_kv_up, v_w_kv_up),
        w_sb_out=(w_sb_out, m_w_sb_out, v_w_sb_out), w_mla_out=(w_mla_out, m_w_mla_out, v_w_mla_out),
        w_mix_out=(w_mix_out, m_w_mix_out, v_w_mix_out), g_mlp_norm=(g_mlp_norm, m_g_mlp_norm, v_g_mlp_norm),
        w_up=(w_up, m_w_up, v_w_up), w_down=(w_down, m_w_down, v_w_down),
        g_final=(g_final.reshape(1, d), m_g_final.reshape(1, d), v_g_final.reshape(1, d)))
    lands = dict(b_ada=small_all[0], g_mix_norm=small_all[1], g_q_lat=small_all[2], g_kv_lat=small_all[3],
                 g_mlp_norm=small_all[4], g_final=small_all[5], w_in=landed[0], w_q_up=landed[1],
                 w_kv_up=landed[2], w_sb_out=landed[3], w_mla_out=landed[4], w_mix_out=landed[5], w_up=landed[6],
                 w_down=landed[7])
    gs, deltas, new_ms, new_vs = [], [], [], []
    for name, (wt, mt, vt) in moments.items():
        if name == "w_ada":
            res = [g_w_ada] + _adamw("adamw_" + name, wt, g_w_ada, mt, vt)
        else:
            res = _sum_adamw("adamw_" + name, lands[name], wt, mt, vt)
        if name == "g_final":
            res = [t.reshape(d) for t in res]
        for lst, t in zip((gs, deltas, new_ms, new_vs), res):
            lst.append(t)

    return (loss, dx[None], *gs, *deltas, *new_ms, *new_vs)
```

```python
import functools

import jax
import jax.numpy as jnp
from jax import lax
from jax.experimental import pallas as pl
from jax.experimental.pallas import tpu as pltpu

F32 = jnp.float32
BF16 = jnp.bfloat16
NDEV = 8
LANES = 128
HEAD_DIM = 64
ROPE_DIM = 32
MLA_QK_DIM = HEAD_DIM + ROPE_DIM
ROPE_THETA = 10000.0
NORM_EPS = 1e-6
ADAM_LR = 0.001
ADAM_B1 = 0.9
ADAM_B2 = 0.999
ADAM_EPS = 1e-08
ADAM_WD = 0.01
ADAM_STEP = 10
VMEM_LIMIT = 48 * 1024 * 1024


def _pcall(body, **kw):
    return pl.pallas_call(body, **kw)


def _tile(n, pref):
    for t in (512, 384, 256, 128, 64, 32, 16, 8):
        if t <= pref and n % t == 0:
            return t
    return n


def _roundup(n, m):
    return (n + m - 1) // m * m


_CP = pltpu.CompilerParams(vmem_limit_bytes=VMEM_LIMIT)


def _rowwise(name, fn, rows, vecs, outs, reds=(), tb=256):
    rows = [r if isinstance(r, tuple) else (r, r.shape[1], 0, 0) for r in rows]
    nrows = None
    for arr, width, col, roff in rows:
        if roff == 0 and nrows is None:
            nrows = arr.shape[0]
    first_off = [r for r in rows if r[3] != 0]
    if first_off:
        nrows = min(nrows, first_off[0][3])
    tb = _tile(nrows, tb)
    nblk = nrows // tb
    n_in = len(rows) + len(vecs)
    n_out = len(outs)

    def body(*refs):
        vals = [r[...] for r in refs[:n_in]]
        res = fn(*vals)
        if not isinstance(res, (tuple, list)):
            res = (res,)
        for ref, val in zip(refs[n_in:n_in + n_out], res[:n_out]):
            ref[...] = val.astype(ref.dtype)
        for ref, val in zip(refs[n_in + n_out:], res[n_out:]):
            @pl.when(pl.program_id(0) == 0)
            def _(ref=ref):
                ref[...] = jnp.zeros_like(ref)
            ref[...] += jnp.sum(val.astype(F32), axis=0, keepdims=True)

    in_specs = []
    for arr, width, col, roff in rows:
        in_specs.append(pl.BlockSpec((tb, width), functools.partial(
            lambda i, col, rb: (rb + i, col), col=col, rb=roff // tb)))
    for v in vecs:
        in_specs.append(pl.BlockSpec(v.shape, lambda i, nd=v.ndim: (0,) * nd))
    out_specs = [pl.BlockSpec((tb, w), lambda i: (i, 0)) for w, _ in outs]
    out_specs += [pl.BlockSpec((1, w), lambda i: (0, 0)) for w in reds]
    out_shape = [jax.ShapeDtypeStruct((nrows, w), dt) for w, dt in outs]
    out_shape += [jax.ShapeDtypeStruct((1, w), F32) for w in reds]
    res = _pcall(body, name=name, grid=(nblk,), in_specs=in_specs, out_specs=out_specs,
                 out_shape=out_shape, compiler_params=_CP)(*[r[0] for r in rows], *vecs)
    return res


_DIMS = {"nn": (((1,), (0,)), ((), ())), "nt": (((1,), (1,)), ((), ())), "tn": (((0,), (0,)), ((), ()))}


def _matmul(name, a, b, mode, out_dtype=F32, epilogue=None, rows=(), vecs=(), outs=None):
    if mode == "nn":
        (m, k), n = a.shape, b.shape[1]
    elif mode == "nt":
        (m, k), n = a.shape, b.shape[0]
    else:
        (k, m), n = a.shape, b.shape[1]
    tm = _tile(m, 512)
    tn = next((t for t in (1536, 1024) if n % t == 0 and n > t), _tile(n, 512))
    dims = _DIMS[mode]
    outs = [out_dtype] if outs is None else outs
    n_extra = len(rows) + len(vecs)

    def body(a_ref, b_ref, *refs):
        acc = lax.dot_general(a_ref[...].astype(BF16), b_ref[...].astype(BF16), dims, preferred_element_type=F32)
        res = (acc,) if epilogue is None else epilogue(acc, *[r[...] for r in refs[:n_extra]])
        for o_ref, val in zip(refs[n_extra:], res):
            o_ref[...] = val.astype(o_ref.dtype)

    a_spec = pl.BlockSpec((k, tm), lambda j, i: (0, i)) if mode == "tn" else pl.BlockSpec((tm, k), lambda j, i: (i, 0))
    b_spec = pl.BlockSpec((tn, k), lambda j, i: (j, 0)) if mode == "nt" else pl.BlockSpec((k, tn), lambda j, i: (0, j))
    blk = pl.BlockSpec((tm, tn), lambda j, i: (i, j))
    res = _pcall(body, name=name, grid=(n // tn, m // tm),
                 in_specs=[a_spec, b_spec] + [blk] * len(rows) + [pl.BlockSpec((1, tn), lambda j, i: (0, j))] * len(vecs),
                 out_specs=[blk] * len(outs), out_shape=[jax.ShapeDtypeStruct((m, n), dt) for dt in outs],
                 compiler_params=_CP)(a, b, *rows, *vecs)
    return res[0] if len(outs) == 1 else res


class _Comm:
    KS = {"gather_all": (1, 2, 3, 4, 5, 6, 7), "gather_own": (1, 2, 4, 6), "gather_fwd": (2, 4, 6),
          "scatter": (1, 2, 3, 4, 5, 6, 7)}

    def __init__(self, kind, srcs, lands=None, layers=None, depth=None):
        self.kind, self.srcs, self.layers = kind, list(srcs), layers
        self.n = len(lands) if kind == "gather_fwd" else len(srcs)
        self.lands = list(lands) if lands is not None else [None] * self.n
        self.out_shapes = []
        for i, land in enumerate(self.lands):
            if land is not None:
                self.out_shapes.append(jax.ShapeDtypeStruct(land.shape, land.dtype))
            elif kind == "scatter":
                self.out_shapes.append(jax.ShapeDtypeStruct((NDEV, depth) + srcs[i].shape[1:], srcs[i].dtype))
            else:
                self.out_shapes.append(jax.ShapeDtypeStruct((NDEV,) + srcs[i].shape, srcs[i].dtype))
        self.operands = self.srcs + [t for t in self.lands if t is not None]
        self.scratch = [pltpu.SemaphoreType.DMA((NDEV - 1, self.n)), pltpu.SemaphoreType.DMA((NDEV - 1, self.n)),
                        pltpu.SemaphoreType.DMA((self.n,))]

    def aliases(self, first_in, first_out):
        given = [i for i, t in enumerate(self.lands) if t is not None]
        return {first_in + len(self.srcs) + pos: first_out + i for pos, i in enumerate(given)}

    def copies(self, in_refs, out_refs, send_sems, recv_sems, local_sems):
        x, y, c = lax.axis_index("x"), lax.axis_index("y"), lax.axis_index("c")
        me = 4 * x + 2 * y + c
        cps = []
        if self.kind != "gather_fwd":
            for i in range(self.n):
                src = in_refs[i].at[me] if self.kind == "scatter" else in_refs[i]
                dst = out_refs[i].at[me, self.layers[i]] if self.kind == "scatter" else out_refs[i].at[me]
                cps.append(pltpu.make_async_copy(src, dst, local_sems.at[i]))
        for k in self.KS[self.kind]:
            px = 1 - x if k & 4 else x
            py = 1 - y if k & 2 else y
            pc = 1 - c if k & 1 else c
            peer = 4 * px + 2 * py + pc
            for i in range(self.n):
                if self.kind == "gather_fwd":
                    src, dst, to = in_refs[i].at[peer], out_refs[i].at[peer], (x, y, 1 - c)
                elif self.kind == "scatter":
                    src, dst, to = in_refs[i].at[peer], out_refs[i].at[me, self.layers[i]], (px, py, pc)
                else:
                    src, dst, to = in_refs[i], out_refs[i].at[me], (px, py, pc)
                cps.append(pltpu.make_async_remote_copy(
                    src_ref=src, dst_ref=dst, send_sem=send_sems.at[k - 1, i], recv_sem=recv_sems.at[k - 1, i],
                    device_id=to, device_id_type=pl.DeviceIdType.MESH))
        return cps


_ANY = pl.BlockSpec(memory_space=pl.ANY)


def _exchange(name, comm):
    nci = len(comm.operands)

    def body(*refs):
        cps = comm.copies(refs[:nci], refs[nci:nci + comm.n], *refs[nci + comm.n:])
        for cp in cps:
            cp.start()
        for cp in cps:
            cp.wait()

    return _pcall(body, name=name, in_specs=[_ANY] * nci, out_specs=[_ANY] * comm.n, out_shape=comm.out_shapes,
                  scratch_shapes=comm.scratch, input_output_aliases=comm.aliases(0, 0))(*comm.operands)


def _hosted(name, body, grid, arrays, in_specs, out_shapes, out_specs, comm):
    if comm is None:
        return _pcall(body, name=name, grid=grid, in_specs=in_specs, out_specs=out_specs, out_shape=out_shapes,
                      compiler_params=_CP)(*arrays), []
    ni, no, nci = len(arrays), len(out_shapes), len(comm.operands)

    def full(*refs):
        ins, cin = refs[:ni], refs[ni:ni + nci]
        outs = refs[ni + nci:ni + nci + no]
        cout = refs[ni + nci + no:ni + nci + no + comm.n]
        sems = refs[ni + nci + no + comm.n:]
        first = functools.reduce(jnp.logical_and, [pl.program_id(a) == 0 for a in range(len(grid))])
        last = functools.reduce(jnp.logical_and, [pl.program_id(a) == grid[a] - 1 for a in range(len(grid))])

        @pl.when(first)
        def _():
            for cp in comm.copies(cin, cout, *sems):
                cp.start()

        body(*ins, *outs)

        @pl.when(last)
        def _():
            for cp in comm.copies(cin, cout, *sems):
                cp.wait()

    res = _pcall(full, name=name, grid=grid, in_specs=list(in_specs) + [_ANY] * nci,
                 out_specs=list(out_specs) + [_ANY] * comm.n, out_shape=list(out_shapes) + comm.out_shapes,
                 scratch_shapes=comm.scratch, input_output_aliases=comm.aliases(ni, no),
                 compiler_params=_CP)(*arrays, *comm.operands)
    return res[:no], res[no:]


def _dot_nt(a, b):
    return lax.dot_general(a, b, _DIMS["nt"], preferred_element_type=F32)


def _dot_tn(a, b):
    return lax.dot_general(a, b, _DIMS["tn"], preferred_element_type=F32)


def _dot_nn(a, b):
    return jnp.dot(a, b, preferred_element_type=F32)


def _tri(tk, rel):
    j = lax.broadcasted_iota(jnp.int32, (tk, tk), 0)
    s = lax.broadcasted_iota(jnp.int32, (tk, tk), 1)
    return {"after": j > s, "upto": j <= s, "before": j < s}[rel].astype(BF16)


def _pairs_per_step(nhp):
    return 2 if nhp % 2 == 0 else 1


def _pair(a, pr):
    return a[:, pr * LANES:(pr + 1) * LANES]


def _head_masks():
    lane = lax.broadcasted_iota(jnp.int32, (1, LANES), 1)
    return [(lane // HEAD_DIM) == h for h in range(2)]


ROW_CHUNK = 32


def _by_rows(fn, n_out, *arrays):
    rows = arrays[0].shape[0]
    step = min(ROW_CHUNK, rows)
    outs = [[] for _ in range(n_out)]
    for r in range(0, rows, step):
        for o, val in zip(outs, fn(r, *[a[r:r + step] for a in arrays])):
            o.append(val)
    return [jnp.concatenate(o, axis=0) for o in outs]


def _causal(r0, k0, rows, tk, strict):
    row = lax.broadcasted_iota(jnp.int32, (rows, tk), 0) + r0
    col = lax.broadcasted_iota(jnp.int32, (rows, tk), 1) + k0
    return col < row if strict else col <= row


def _wide(stat, width):
    return stat if width == LANES else jnp.concatenate([stat] * (width // LANES), axis=1)


def _row_sum(v):
    return jnp.broadcast_to(jnp.sum(v, axis=1, keepdims=True), (v.shape[0], LANES))


def _split_bf16(v):
    hi = v.astype(BF16)
    return hi, (v - hi.astype(F32)).astype(BF16)


def _sb_logs(z, scale, mask):
    z = z * scale
    e = jnp.exp(-jnp.abs(z))
    log_sig = jnp.minimum(z, 0.0) - jnp.log(1.0 + e)
    log_fail = log_sig - z
    return z, log_sig, (log_fail if mask is None else jnp.where(mask, log_fail, 0.0))


def _two_loops(n_full, nkb, near_first, step, carry):
    if near_first:
        carry = lax.fori_loop(0, nkb - n_full, lambda j, c: step(nkb - 1 - j, True, c), carry)
        return lax.fori_loop(0, n_full, lambda j, c: step(n_full - 1 - j, False, c), carry)
    carry = lax.fori_loop(0, n_full, lambda j, c: step(j, False, c), carry)
    return lax.fori_loop(n_full, nkb, lambda j, c: step(j, True, c), carry)


def _sb_fwd(name, p, nhp, tq, tk, comm=None):
    s = p.shape[0]
    scale = HEAD_DIM ** -0.5
    nq = s // tq
    pp = _pairs_per_step(nhp)
    wide = pp * LANES

    def body(q_ref, k_ref, v_ref, o_ref, tot_ref):
        qi = pl.program_id(1)
        masks = _head_masks()
        after = _tri(tk, "after")
        nkb = ((qi + 1) * tq + tk - 1) // tk
        q = q_ref[...]
        qhs = [jnp.where(hm, _pair(q, pr), 0.0).astype(BF16) for pr in range(pp) for hm in masks]

        def step(kb, masked, carry):
            ks = pl.multiple_of(kb * tk, tk)
            ks_all = k_ref[pl.ds(ks, tk), :].astype(BF16)
            vs_all = v_ref[pl.ds(ks, tk), :].astype(BF16)
            mask_of = lambda r, n: _causal(qi * tq + r, ks, n, tk, True) if masked else None
            heads = range(len(qhs))

            def logs(r, zc):
                _, log_sig, log_fail = _sb_logs(zc, scale, mask_of(r, zc.shape[0]))
                return (log_sig,) + _split_bf16(log_fail) + (_row_sum(log_fail),)

            def weights(r, lsc, runc, laterc):
                w = jnp.exp(lsc + runc + _wide(laterc, tk))
                return ((jnp.where(mask_of(r, w.shape[0]), w, 0.0) if masked else w).astype(BF16),)

            zs = [_dot_nt(qhs[i], _pair(ks_all, i // 2)) for i in heads]
            first = [_by_rows(logs, 4, zs[i]) for i in heads]
            runs = [_dot_nn(first[i][1], after) + _dot_nn(first[i][2], after) for i in heads]
            ws = [_by_rows(weights, 1, first[i][0], runs[i], carry[i][0])[0] for i in heads]
            pvs = [_dot_nn(ws[i], _pair(vs_all, i // 2)) for i in heads]
            return tuple((carry[i][0] + first[i][3], carry[i][1] + pvs[i]) for i in heads)

        init = (jnp.zeros((tq, LANES), F32), jnp.zeros((tq, LANES), F32))
        res = _two_loops((qi * tq) // tk, nkb, True, step, (init,) * (2 * pp))
        for pr in range(pp):
            (tot0, acc0), (tot1, acc1) = res[2 * pr], res[2 * pr + 1]
            o_ref[:, pr * LANES:(pr + 1) * LANES] = jnp.where(masks[0], acc0, acc1)
            tot_ref[:, pr * LANES:(pr + 1) * LANES] = jnp.where(masks[0], tot0, tot1)

    ng = nhp // pp
    blk = pl.BlockSpec((tq, wide), lambda h, i: (i, h))
    shape = jax.ShapeDtypeStruct((s, nhp * LANES), F32)
    return _hosted(name, body, (ng, nq), [p, p, p],
                   [blk, pl.BlockSpec((s, wide), lambda h, i: (0, ng + h)),
                    pl.BlockSpec((s, wide), lambda h, i: (0, 2 * ng + h))], [shape, shape], [blk, blk], comm)


def _sb_bwd(name, p, tot, do, nhp, tq, tk, comm=None):
    s = p.shape[0]
    scale = HEAD_DIM ** -0.5
    nq = s // tq
    pp = _pairs_per_step(nhp)
    wide = pp * LANES

    def body(q_ref, k_ref, v_ref, tot_ref, do_ref, dq_ref, dk_ref, dv_ref):
        qi = pl.program_id(1)

        @pl.when(qi == 0)
        def _():
            dk_ref[...] = jnp.zeros_like(dk_ref)
            dv_ref[...] = jnp.zeros_like(dv_ref)

        masks = _head_masks()
        upto, before = _tri(tk, "upto"), _tri(tk, "before")
        nkb = ((qi + 1) * tq + tk - 1) // tk
        q = q_ref[...]
        qbs = q.astype(BF16)
        dout = do_ref[...]
        doutbs = dout.astype(BF16)
        qhs = [jnp.where(hm, _pair(q, pr), 0.0).astype(BF16) for pr in range(pp) for hm in masks]
        dohs = [jnp.where(hm, _pair(dout, pr), 0.0).astype(BF16) for pr in range(pp) for hm in masks]
        tot = tot_ref[...]
        totals = [jnp.broadcast_to(tot[:, h * HEAD_DIM:h * HEAD_DIM + 1], (tq, LANES)) for h in range(2 * pp)]

        def step(kb, masked, carry):
            ks = pl.multiple_of(kb * tk, tk)
            ks_all = k_ref[pl.ds(ks, tk), :].astype(BF16)
            vs_all = v_ref[pl.ds(ks, tk), :].astype(BF16)
            mask_of = lambda r, n: _causal(qi * tq + r, ks, n, tk, True) if masked else None
            heads = range(len(qhs))

            def logs(r, zc):
                _, log_sig, log_fail = _sb_logs(zc, scale, mask_of(r, zc.shape[0]))
                return (log_sig,) + _split_bf16(log_fail) + (_row_sum(log_fail),)

            def weights(r, lsc, runc, basec, dwc):
                w = jnp.exp(lsc + (_wide(basec, tk) - runc))
                if masked:
                    w = jnp.where(mask_of(r, w.shape[0]), w, 0.0)
                g = w * dwc
                return (w.astype(BF16), g) + _split_bf16(g) + (_row_sum(g),)

            def dscore(r, gc, lsc, zc, grc, gbc):
                dz = gc * jnp.exp(lsc - zc * scale) - jnp.exp(lsc) * (_wide(gbc, tk) + grc)
                if masked:
                    dz = jnp.where(mask_of(r, dz.shape[0]), dz, 0.0)
                return ((dz * scale).astype(BF16),)

            zs = [_dot_nt(qhs[i], _pair(ks_all, i // 2)) for i in heads]
            dws = [_dot_nt(dohs[i], _pair(vs_all, i // 2)) for i in heads]
            first = [_by_rows(logs, 4, zs[i]) for i in heads]
            runs = [_dot_nn(first[i][1], upto) + _dot_nn(first[i][2], upto) for i in heads]
            second = [_by_rows(weights, 5, first[i][0], runs[i], totals[i] - carry[i][0], dws[i])
                      for i in heads]
            g_runs = [_dot_nn(second[i][2], before) + _dot_nn(second[i][3], before) for i in heads]
            dzs = [_by_rows(dscore, 1, second[i][1], first[i][0], zs[i], g_runs[i], carry[i][1])[0] for i in heads]
            dks = [_dot_tn(dzs[i], _pair(qbs, i // 2)) for i in heads]
            dvs = [_dot_tn(second[i][0], _pair(doutbs, i // 2)) for i in heads]
            dqs = [_dot_nn(dzs[i], _pair(ks_all, i // 2)) for i in heads]
            for pr in range(pp):
                cols = slice(pr * LANES, (pr + 1) * LANES)
                dk_ref[pl.ds(ks, tk), cols] += jnp.where(masks[0], dks[2 * pr], dks[2 * pr + 1])
                dv_ref[pl.ds(ks, tk), cols] += jnp.where(masks[0], dvs[2 * pr], dvs[2 * pr + 1])
            return tuple((carry[i][0] + first[i][3], carry[i][1] + second[i][4], carry[i][2] + dqs[i]) for i in heads)

        zero = jnp.zeros((tq, LANES), F32)
        res = _two_loops((qi * tq) // tk, nkb, False, step, ((zero, zero, zero),) * (2 * pp))
        for pr in range(pp):
            dq_ref[:, pr * LANES:(pr + 1) * LANES] = jnp.where(masks[0], res[2 * pr][2], res[2 * pr + 1][2])

    ng = nhp // pp
    blk = pl.BlockSpec((tq, wide), lambda h, i: (i, h))
    full = pl.BlockSpec((s, wide), lambda h, i: (0, h))
    shape = jax.ShapeDtypeStruct((s, nhp * LANES), F32)
    return _hosted(name, body, (ng, nq), [p, p, p, tot, do],
                   [blk, pl.BlockSpec((s, wide), lambda h, i: (0, ng + h)),
                    pl.BlockSpec((s, wide), lambda h, i: (0, 2 * ng + h)), blk, blk],
                   [shape, shape, shape], [blk, full, full], comm)


def _mla_fwd(name, q, k, v, tq, tk, comm=None):
    s = q.shape[0]
    nhp = v.shape[1] // LANES
    scale = MLA_QK_DIM ** -0.5
    nq = s // tq
    pp = _pairs_per_step(nhp)

    def body(q_ref, k_ref, v_ref, o_ref, lse_ref):
        qi = pl.program_id(1)
        masks = _head_masks()
        nkb = ((qi + 1) * tq + tk - 1) // tk
        qhs = [q_ref[:, h * LANES:(h + 1) * LANES] for h in range(2 * pp)]

        def step(kb, masked, carry):
            ks = pl.multiple_of(kb * tk, tk)
            heads = range(len(qhs))

            def soft(r, zc, mc, lc):
                zc = zc * scale
                if masked:
                    zc = jnp.where(_causal(qi * tq + r, ks, zc.shape[0], tk, False), zc, -1e30)
                m_new = jnp.maximum(mc, jnp.max(zc, axis=1, keepdims=True))
                a = jnp.exp(mc - m_new)
                pr = jnp.exp(zc - _wide(m_new, tk))
                return pr.astype(BF16), m_new, a * lc + _row_sum(pr), a

            zs = [_dot_nt(qhs[h], k_ref[pl.ds(ks, tk), h * LANES:(h + 1) * LANES]) for h in heads]
            first = [_by_rows(soft, 4, zs[h], carry[h][0], carry[h][1]) for h in heads]
            pvs = [_dot_nn(first[h][0], v_ref[pl.ds(ks, tk), (h // 2) * LANES:(h // 2 + 1) * LANES]) for h in heads]
            accs = [_by_rows(lambda r, ac, aa, pc: (aa * ac + pc,), 1, carry[h][2], first[h][3], pvs[h])[0]
                    for h in heads]
            return tuple((first[h][1], first[h][2], accs[h]) for h in heads)

        init = (jnp.full((tq, LANES), -1e30, F32), jnp.zeros((tq, LANES), F32), jnp.zeros((tq, LANES), F32))
        res = _two_loops((qi * tq) // tk, nkb, False, step, (init,) * (2 * pp))
        for pr in range(pp):
            (m0, l0, acc0), (m1, l1, acc1) = res[2 * pr], res[2 * pr + 1]
            o_ref[:, pr * LANES:(pr + 1) * LANES] = jnp.where(masks[0], acc0 / l0, acc1 / l1)
            lse_ref[:, pr * LANES:(pr + 1) * LANES] = jnp.where(masks[0], m0 + jnp.log(l0), m1 + jnp.log(l1))

    shape = jax.ShapeDtypeStruct((s, nhp * LANES), F32)
    blk = pl.BlockSpec((tq, pp * LANES), lambda h, i: (i, h))
    return _hosted(name, body, (nhp // pp, nq), [q, k, v],
                   [pl.BlockSpec((tq, 2 * pp * LANES), lambda h, i: (i, h)),
                    pl.BlockSpec((s, 2 * pp * LANES), lambda h, i: (0, h)),
                    pl.BlockSpec((s, pp * LANES), lambda h, i: (0, h))], [shape, shape], [blk, blk], comm)


def _mla_bwd(name, q, k, v, o, lse, do, tq, tk, comm=None):
    s = q.shape[0]
    nhp = v.shape[1] // LANES
    scale = MLA_QK_DIM ** -0.5
    nq = s // tq
    pp = _pairs_per_step(nhp)

    def body(q_ref, k_ref, v_ref, o_ref, lse_ref, do_ref, dq_ref, dk_ref, dv_ref):
        qi = pl.program_id(1)

        @pl.when(qi == 0)
        def _():
            dk_ref[...] = jnp.zeros_like(dk_ref)
            dv_ref[...] = jnp.zeros_like(dv_ref)

        masks = _head_masks()
        nkb = ((qi + 1) * tq + tk - 1) // tk
        dout = do_ref[...]
        doutbs = dout.astype(BF16)
        prod = dout * o_ref[...]
        qhs = [q_ref[:, h * LANES:(h + 1) * LANES] for h in range(2 * pp)]
        dohs = [jnp.where(hm, _pair(dout, pr), 0.0).astype(BF16) for pr in range(pp) for hm in masks]
        totals = [_row_sum(jnp.where(hm, _pair(prod, pr), 0.0)) for pr in range(pp) for hm in masks]
        lse = lse_ref[...]
        lses = [jnp.broadcast_to(lse[:, h * HEAD_DIM:h * HEAD_DIM + 1], (tq, LANES)) for h in range(2 * pp)]

        def step(kb, masked, carry):
            ks = pl.multiple_of(kb * tk, tk)
            heads = range(len(qhs))

            def probs(r, zc, dpc, lsec, totc):
                pr = jnp.exp(zc * scale - _wide(lsec, tk))
                if masked:
                    pr = jnp.where(_causal(qi * tq + r, ks, pr.shape[0], tk, False), pr, 0.0)
                return pr.astype(BF16), (pr * (dpc - _wide(totc, tk)) * scale).astype(BF16)

            khs = [k_ref[pl.ds(ks, tk), h * LANES:(h + 1) * LANES] for h in heads]
            vvs = [v_ref[pl.ds(ks, tk), (h // 2) * LANES:(h // 2 + 1) * LANES] for h in heads]
            zs = [_dot_nt(qhs[h], khs[h]) for h in heads]
            dps = [_dot_nt(dohs[h], vvs[h]) for h in heads]
            both = [_by_rows(probs, 2, zs[h], dps[h], lses[h], totals[h]) for h in heads]
            dks = [_dot_tn(both[h][1], qhs[h]) for h in heads]
            dvs = [_dot_tn(both[h][0], _pair(doutbs, h // 2)) for h in heads]
            dqs = [_dot_nn(both[h][1], khs[h]) for h in heads]
            for h in heads:
                dk_ref[pl.ds(ks, tk), h * LANES:(h + 1) * LANES] += dks[h]
            for pr in range(pp):
                dv_ref[pl.ds(ks, tk), pr * LANES:(pr + 1) * LANES] += jnp.where(masks[0], dvs[2 * pr], dvs[2 * pr + 1])
            return tuple(carry[h] + dqs[h] for h in heads)

        zero = jnp.zeros((tq, LANES), F32)
        dqs = _two_loops((qi * tq) // tk, nkb, False, step, (zero,) * (2 * pp))
        for h in range(2 * pp):
            dq_ref[:, h * LANES:(h + 1) * LANES] = dqs[h]

    blk = pl.BlockSpec((tq, pp * LANES), lambda h, i: (i, h))
    blk2 = pl.BlockSpec((tq, 2 * pp * LANES), lambda h, i: (i, h))
    full = pl.BlockSpec((s, pp * LANES), lambda h, i: (0, h))
    full2 = pl.BlockSpec((s, 2 * pp * LANES), lambda h, i: (0, h))
    return _hosted(name, body, (nhp // pp, nq), [q, k, v, o, lse, do], [blk2, full2, full, blk, blk, blk],
                   [jax.ShapeDtypeStruct(q.shape, F32), jax.ShapeDtypeStruct(k.shape, F32),
                    jax.ShapeDtypeStruct(v.shape, F32)], [blk2, full2, full], comm)


def _norm_parts(x):
    r = lax.rsqrt(jnp.mean(x * x, axis=-1, keepdims=True) + NORM_EPS)
    return r, x * r


def _rmsmod_fwd(x, g, sc, sh):
    _, xh = _norm_parts(x)
    return ((xh * g) * (1.0 + sc) + sh,)


def _rmsmod_bwd(dh, x, dres, g, sc):
    r, xh = _norm_parts(x)
    dy = dh * (1.0 + sc)
    dxh = dy * g
    dx = r * (dxh - xh * jnp.mean(dxh * xh, axis=-1, keepdims=True)) + dres
    return dx, dh, dh * (xh * g), dy * xh


def _rms_bwd_plain(dh, x, g):
    r, xh = _norm_parts(x)
    dxh = dh * g
    return r * (dxh - xh * jnp.mean(dxh * xh, axis=-1, keepdims=True)), dh * xh


def _cat(parts):
    return jnp.concatenate(parts, axis=1)


def _swap_halves(a):
    half = a.shape[-1] // 2
    return jnp.concatenate([a[..., half:], a[..., :half]], axis=-1)


def _adamw_fn(w, g, m, v):
    m = ADAM_B1 * m + (1.0 - ADAM_B1) * g
    v = ADAM_B2 * v + (1.0 - ADAM_B2) * jnp.square(g)
    m_hat = m / (1.0 - ADAM_B1 ** ADAM_STEP)
    v_hat = v / (1.0 - ADAM_B2 ** ADAM_STEP)
    delta = -ADAM_LR * (m_hat / (jnp.sqrt(v_hat) + ADAM_EPS) + ADAM_WD * w)
    return delta, m, v


def _adamw(name, w, g, m, v):
    shape = w.shape
    width = shape[-1]
    flat = [t.reshape(-1, width) for t in (w, g, m, v)]
    res = _rowwise(name, _adamw_fn, flat, [], [(width, F32)] * 3)
    return [t.reshape(shape) for t in res]


def _sum_adamw(name, land, w, m, v):
    shape = w.shape
    width = shape[-1]
    rows = w.size // width

    def fn(*blocks):
        g = blocks[0].astype(F32)
        for b in blocks[1:NDEV]:
            g = g + b.astype(F32)
        return (g,) + _adamw_fn(blocks[NDEV], g, blocks[NDEV + 1], blocks[NDEV + 2])

    def fn_whole(wb, mb, vb, lb):
        return fn(*[lb[i] for i in range(NDEV)], wb, mb, vb)

    flat = [t.reshape(rows, width) for t in (w, m, v)]
    if rows % 16 == 0:
        views = [(land.reshape(NDEV * rows, width), width, 0, i * rows) for i in range(NDEV)]
        res = _rowwise(name, fn, views + flat, [], [(width, F32)] * 4)
    else:
        res = _rowwise(name, fn_whole, flat, [land.reshape(NDEV, rows, width)], [(width, F32)] * 4)
    return [t.reshape(shape) for t in res]


def kernel(x, c, positions, w_ada, b_ada, g_mix_norm, w_in, g_q_lat, w_q_up, g_kv_lat, w_kv_up, w_sb_out, w_mla_out, w_mix_out, g_mlp_norm, w_up, w_down, g_final, loss_target, m_w_ada, m_b_ada, m_g_mix_norm, m_w_in, m_g_q_lat, m_w_q_up, m_g_kv_lat, m_w_kv_up, m_w_sb_out, m_w_mla_out, m_w_mix_out, m_g_mlp_norm, m_w_up, m_w_down, m_g_final, v_w_ada, v_b_ada, v_g_mix_norm, v_w_in, v_g_q_lat, v_w_q_up, v_g_kv_lat, v_w_kv_up, v_w_sb_out, v_w_mla_out, v_w_mix_out, v_g_mlp_norm, v_w_up, v_w_down, v_g_final):
    seq, d = x.shape[1], x.shape[2]
    depth = w_ada.shape[0]
    qr, kvr = g_q_lat.shape[1], g_kv_lat.shape[1]
    sbw, mlaw = w_sb_out.shape[1], w_mla_out.shape[1]
    nh = mlaw // HEAD_DIM
    nhp_sb = sbw // LANES
    dff = w_up.shape[2] * NDEV
    ada_n = w_ada.shape[2]
    gb = min(512, d)
    tq, tk = min(256, seq), min(256, seq)
    me = 4 * lax.axis_index("x") + 2 * lax.axis_index("y") + lax.axis_index("c")

    o_qlat = _roundup(3 * sbw, qr)
    o_kvlat = _roundup(o_qlat + qr, kvr)
    o_rope = _roundup(o_kvlat + kvr, 2 * LANES)
    o_gate = _roundup(o_rope + 2 * LANES, gb)
    wp = o_gate + 2 * d

    c_all = _exchange("ag_c", _Comm("gather_all", [c.reshape(d // LANES, LANES)]))[0].reshape(NDEV, d)
    c_act = _rowwise("silu_c", lambda t: (t * (1.0 / (1.0 + jnp.exp(-t))),), [c_all], [], [(d, F32)])[0]
    parts = jnp.stack([_matmul("ada_fwd", c_act, w_ada[l], "nn") for l in range(depth)])
    parts_all = _exchange("ag_mod", _Comm("gather_all", [parts]))[0]
    mine = jnp.transpose(lax.dynamic_index_in_dim(parts_all, me, axis=2, keepdims=False), (1, 0, 2))
    mod = _rowwise("mod_bias", lambda a, b: (a + b,), [mine.reshape(depth, NDEV * ada_n), b_ada], [],
                   [(6 * d, F32)])[0]
    mods = [[mod[l:l + 1, i * d:(i + 1) * d] for i in range(6)] for l in range(depth)]

    big = [w_in, w_q_up, w_kv_up, w_sb_out, w_mla_out, w_mix_out, w_up, w_down]
    row_sharded = [False, False, False, False, False, True, False, True]
    shards = [[w[l].astype(BF16) for w in big] for l in range(depth)]

    def unpack_layer(gathered):
        out = []
        for g, by_rows in zip(gathered, row_sharded):
            _, rows, cols = g.shape
            if by_rows:
                out.append(g.reshape(NDEV * rows, cols))
            else:
                out.append(jnp.transpose(g, (1, 0, 2)).reshape(rows, NDEV * cols))
        return out

    def derive(full):
        wi, wq, wkv, wsb, wmla, wmix, wu, wd = full
        dt = wi.dtype
        z = lambda r, n: jnp.zeros((r, n), dt)
        o = 3 * sbw
        kr = wi[:, o + qr + kvr:o + qr + kvr + ROPE_DIM]
        g0 = o + qr + kvr + ROPE_DIM
        w_in_pad = _cat([wi[:, :o], z(d, o_qlat - o), wi[:, o:o + qr], z(d, o_kvlat - o_qlat - qr),
                         wi[:, o + qr:o + qr + kvr], z(d, o_rope - o_kvlat - kvr),
                         z(d, HEAD_DIM), kr, z(d, LANES - MLA_QK_DIM),
                         z(d, HEAD_DIM), _swap_halves(kr), z(d, LANES - MLA_QK_DIM),
                         z(d, o_gate - o_rope - 2 * LANES), wi[:, g0:]])
        wq3 = wq.reshape(qr, nh, MLA_QK_DIM)
        z3 = lambda n: jnp.zeros((qr, nh, n), dt)
        rope_w = wq3[:, :, HEAD_DIM:]
        wq_a = jnp.concatenate([wq3[:, :, :HEAD_DIM], rope_w, z3(LANES - MLA_QK_DIM)], axis=2).reshape(qr, nh * LANES)
        wq_b = jnp.concatenate([z3(HEAD_DIM), _swap_halves(rope_w), z3(LANES - MLA_QK_DIM)], axis=2).reshape(qr, nh * LANES)
        wkv3 = wkv.reshape(kvr, nh, 2 * HEAD_DIM)
        wk = jnp.concatenate([wkv3[:, :, :HEAD_DIM], jnp.zeros((kvr, nh, HEAD_DIM), dt)], axis=2).reshape(kvr, nh * LANES)
        wv = wkv3[:, :, HEAD_DIM:].reshape(kvr, nh * HEAD_DIM)
        return dict(w_in=w_in_pad, w_q=_cat([wq_a, wq_b]), w_kv=_cat([wk, wv]), w_sb=wsb, w_mla=wmla,
                    w_mix=wmix, w_up=wu, w_down=wd)

    def fold(gr):
        gi, gq, gkv = gr["w_in"], gr["w_q"], gr["w_kv"]
        o = 3 * sbw
        ra = gi[:, o_rope + HEAD_DIM:o_rope + MLA_QK_DIM]
        rb = gi[:, o_rope + LANES + HEAD_DIM:o_rope + LANES + MLA_QK_DIM]
        g_in = _cat([gi[:, :o], gi[:, o_qlat:o_qlat + qr], gi[:, o_kvlat:o_kvlat + kvr], ra + _swap_halves(rb),
                     gi[:, o_gate:]])
        ga = gq[:, :nh * LANES].reshape(qr, nh, LANES)
        gb_ = gq[:, nh * LANES:].reshape(qr, nh, LANES)
        g_q = jnp.concatenate([ga[:, :, :HEAD_DIM], ga[:, :, HEAD_DIM:MLA_QK_DIM]
                               + _swap_halves(gb_[:, :, HEAD_DIM:MLA_QK_DIM])], axis=2).reshape(qr, nh * MLA_QK_DIM)
        gk = gkv[:, :nh * LANES].reshape(kvr, nh, LANES)[:, :, :HEAD_DIM]
        gv = gkv[:, nh * LANES:].reshape(kvr, nh, HEAD_DIM)
        g_kv = jnp.concatenate([gk, gv], axis=2).reshape(kvr, nh * 2 * HEAD_DIM)
        return [g_in, g_q, g_kv, gr["w_sb"], gr["w_mla"], gr["w_mix"], gr["w_up"], gr["w_down"]]

    first = _exchange("ag_w0_own", _Comm("gather_own", shards[0]))
    weights = [derive(unpack_layer(_exchange("ag_w0_fwd", _Comm("gather_fwd", [], lands=first))))]

    inv_freq = 1.0 / (ROPE_THETA ** (jnp.arange(0, ROPE_DIM, 2, dtype=F32) / ROPE_DIM))
    ang = positions[0].astype(F32)[:, None] * inv_freq
    cos, sin = jnp.cos(ang), jnp.sin(ang)
    tail = jnp.zeros((seq, LANES - MLA_QK_DIM), F32)
    rope_c = _cat([jnp.ones((seq, HEAD_DIM), F32), cos, cos, tail])
    rope_s = _cat([jnp.zeros((seq, HEAD_DIM), F32), -sin, sin, tail])
    zero_vec = lambda n: jnp.zeros((1, n), F32)

    def rope_fwd(q2, kvs, pd, tc, ts):
        c8, s8 = _cat([tc] * nh), _cat([ts] * nh)
        qf = q2[:, :nh * LANES] * c8 + q2[:, nh * LANES:] * s8
        kpe = pd[:, :LANES] * tc + pd[:, LANES:] * ts
        return qf, kvs[:, :nh * LANES] + _cat([kpe] * nh), kvs[:, nh * LANES:]

    def rope_bwd(dq, dk, dv, tc, ts):
        c8, s8 = _cat([tc] * nh), _cat([ts] * nh)
        dks = dk[:, :LANES]
        for h in range(1, nh):
            dks = dks + dk[:, h * LANES:(h + 1) * LANES]
        return _cat([dq * c8, dq * s8]), _cat([dk, dv]), _cat([dks * tc, dks * ts])

    def merge_fwd(*a):
        ng = d // gb
        gs, gm, osb, omla = _cat(a[:ng]), _cat(a[ng:2 * ng]), a[2 * ng], a[2 * ng + 1]
        return (osb / (1.0 + jnp.exp(-gs)) + omla / (1.0 + jnp.exp(-gm)),)

    def merge_bwd(*a):
        ng = d // gb
        gs, gm, osb, omla, dm = _cat(a[:ng]), _cat(a[ng:2 * ng]), a[2 * ng], a[2 * ng + 1], a[2 * ng + 2]
        ss, sm = 1.0 / (1.0 + jnp.exp(-gs)), 1.0 / (1.0 + jnp.exp(-gm))
        return ss * dm, sm * dm, _cat([dm * osb * ss * (1.0 - ss), dm * omla * sm * (1.0 - sm)])

    def gate_cols(p):
        ng = d // gb
        return [(p, gb, o_gate // gb + i, 0) for i in range(2 * ng)]

    xs = x[0]
    saved = []
    for l in range(depth):
        w = weights[l]
        sh1, sc1, g1, sh2, sc2, g2 = mods[l]
        h1 = _rowwise("norm1", _rmsmod_fwd, [xs], [g_mix_norm[l:l + 1], sc1, sh1], [(d, BF16)])[0]
        p = _matmul("in_proj", h1, w["w_in"], "nn")
        nxt = _Comm("gather_own", shards[l + 1]) if l + 1 < depth else None
        (o_sb, tot_sb), part = _sb_fwd("sb_fwd", p, nhp_sb, tq, tk, nxt)
        y_sb = _matmul("sb_out", o_sb, w["w_sb"], "nn")
        qn = _rowwise("norm_q", _rmsmod_fwd, [(p, qr, o_qlat // qr, 0)],
                      [g_q_lat[l:l + 1], zero_vec(qr), zero_vec(qr)], [(qr, BF16)])[0]
        kvn = _rowwise("norm_kv", _rmsmod_fwd, [(p, kvr, o_kvlat // kvr, 0)],
                       [g_kv_lat[l:l + 1], zero_vec(kvr), zero_vec(kvr)], [(kvr, BF16)])[0]
        q2 = _matmul("q_up", qn, w["w_q"], "nn")
        kvs = _matmul("kv_up", kvn, w["w_kv"], "nn")
        qf, kf, vf = _rowwise("rope_fwd", rope_fwd, [q2, kvs, (p, 2 * LANES, o_rope // (2 * LANES), 0), rope_c, rope_s],
                              [], [(nh * LANES, BF16), (nh * LANES, BF16), (mlaw, BF16)])
        (o_mla, lse), full = _mla_fwd("mla_fwd", qf, kf, vf, tq, tk,
                                      _Comm("gather_fwd", [], lands=part) if part else None)
        if full:
            weights.append(derive(unpack_layer(full)))
        y_mla = _matmul("mla_out", o_mla, w["w_mla"], "nn")
        merged = _rowwise("merge_fwd", merge_fwd, gate_cols(p) + [y_sb, y_mla], [], [(d, BF16)])[0]
        resid = lambda acc, xv, g: (acc, xv + g * acc)
        y1, x_mid = _matmul("mix_out", merged, w["w_mix"], "nn", epilogue=resid, rows=[xs], vecs=[g1], outs=[F32, F32])
        h2 = _rowwise("norm2", _rmsmod_fwd, [x_mid], [g_mlp_norm[l:l + 1], sc2, sh2], [(d, BF16)])[0]
        u, act = _matmul("mlp_up", h2, w["w_up"], "nn", outs=[F32, BF16],
                         epilogue=lambda acc: (acc, jnp.square(jnp.maximum(acc, 0.0))))
        y2, x_out = _matmul("mlp_down", act, w["w_down"], "nn", epilogue=resid, rows=[x_mid], vecs=[g2], outs=[F32, F32])
        saved.append(dict(x=xs, h1=h1, p=p, o_sb=o_sb, tot_sb=tot_sb, y_sb=y_sb, qn=qn, kvn=kvn, qf=qf, kf=kf, vf=vf, o_mla=o_mla,
                          lse=lse, y_mla=y_mla, merged=merged, y1=y1, x_mid=x_mid, h2=h2, u=u, act=act, y2=y2))
        xs = x_out

    def final_fn(xv, tv, g):
        r, xh = _norm_parts(xv)
        diff = xh * g - tv
        dy = diff * (1.0 / d)
        dxh = dy * g
        dx = r * (dxh - xh * jnp.mean(dxh * xh, axis=-1, keepdims=True))
        return dx, diff * diff, dy * xh

    dx, sq, dg_final = _rowwise("loss_head", final_fn, [xs, loss_target[0]], [g_final.reshape(1, d)],
                                [(d, F32)], reds=[d, d])
    loss = lax.psum(0.5 * jnp.sum(sq) / d, ("x", "y", "c"))

    def chunk(gfull, wref, by_rows):
        rows, cols = wref.shape[1], wref.shape[2]
        if by_rows:
            return gfull.reshape(NDEV, rows, cols).astype(BF16)
        return jnp.transpose(gfull.reshape(rows, NDEV, cols), (1, 0, 2)).astype(BF16)

    dmods, small = [None] * depth, [None] * depth
    late, lands = None, [None] * len(big)
    for l in reversed(range(depth)):
        w, sv = weights[l], saved[l]
        sh1, sc1, g1, sh2, sc2, g2 = mods[l]
        gr = {}
        dy2, dgate2 = _rowwise("gate2_bwd", lambda dxv, y, g: (dxv * g, dxv * y), [dx, sv["y2"]], [g2],
                               [(d, BF16)], reds=[d])
        du = _matmul("mlp_down_dx", dy2, w["w_down"], "nt", outs=[BF16], rows=[sv["u"]],
                     epilogue=lambda acc, uv: (acc * 2.0 * jnp.maximum(uv, 0.0),))
        gr["w_down"] = _matmul("mlp_down_dw", sv["act"], dy2, "tn", BF16)
        dh2 = _matmul("mlp_up_dx", du, w["w_up"], "nt")
        gr["w_up"] = _matmul("mlp_up_dw", sv["h2"], du, "tn", BF16)
        dx_mid, dsh2, dsc2, dg_mlp = _rowwise("norm2_bwd", _rmsmod_bwd, [dh2, sv["x_mid"], dx],
                                              [g_mlp_norm[l:l + 1], sc2], [(d, F32)], reds=[d, d, d])
        dy1, dgate1 = _rowwise("gate1_bwd", lambda dxv, y, g: (dxv * g, dxv * y), [dx_mid, sv["y1"]], [g1],
                               [(d, BF16)], reds=[d])
        dmerged = _matmul("mix_out_dx", dy1, w["w_mix"], "nt")
        gr["w_mix"] = _matmul("mix_out_dw", sv["merged"], dy1, "tn", BF16)
        dy_sb, dy_mla, dgates = _rowwise("merge_bwd", merge_bwd, gate_cols(sv["p"]) + [sv["y_sb"], sv["y_mla"], dmerged],
                                         [], [(d, BF16), (d, BF16), (2 * d, BF16)])
        do_sb = _matmul("sb_out_dx", dy_sb, w["w_sb"], "nt")
        gr["w_sb"] = _matmul("sb_out_dw", sv["o_sb"], dy_sb, "tn", BF16)
        do_mla = _matmul("mla_out_dx", dy_mla, w["w_mla"], "nt")
        gr["w_mla"] = _matmul("mla_out_dw", sv["o_mla"], dy_mla, "tn", BF16)
        ready = {3: gr["w_sb"], 4: gr["w_mla"], 5: gr["w_mix"], 6: gr["w_up"], 7: gr["w_down"]}
        ready = {i: chunk(g, big[i], row_sharded[i]) for i, g in ready.items()}
        ids_a = [3, 4, 5, 6] + ([0, 1, 2] if late is not None else [])
        comm_a = _Comm("scatter", [ready[i] for i in (3, 4, 5, 6)] + (late or []), [lands[i] for i in ids_a],
                       [l] * 4 + [l + 1] * 3, depth)
        comm_b = _Comm("scatter", [ready[7]], [lands[7]], [l], depth)
        (dq_sb, dk_sb, dv_sb), got_a = _sb_bwd("sb_bwd", sv["p"], sv["tot_sb"], do_sb, nhp_sb, tq, tk, comm_a)
        (dqf, dkf, dvf), got_b = _mla_bwd("mla_bwd", sv["qf"], sv["kf"], sv["vf"], sv["o_mla"], sv["lse"], do_mla,
                                          tq, tk, comm_b)
        for i, t in zip(ids_a + [7], list(got_a) + list(got_b)):
            lands[i] = t
        dq2, dkvs, drope = _rowwise("rope_bwd", rope_bwd, [dqf, dkf, dvf, rope_c, rope_s], [],
                                    [(2 * nh * LANES, BF16), (nh * LANES + mlaw, BF16), (2 * LANES, BF16)])
        dqn = _matmul("q_up_dx", dq2, w["w_q"], "nt")
        gr["w_q"] = _matmul("q_up_dw", sv["qn"], dq2, "tn")
        dkvn = _matmul("kv_up_dx", dkvs, w["w_kv"], "nt")
        gr["w_kv"] = _matmul("kv_up_dw", sv["kvn"], dkvs, "tn")
        dqlat, dg_q = _rowwise("norm_q_bwd", _rms_bwd_plain, [dqn, (sv["p"], qr, o_qlat // qr, 0)],
                               [g_q_lat[l:l + 1]], [(qr, BF16)], reds=[qr])
        dkvlat, dg_kv = _rowwise("norm_kv_bwd", _rms_bwd_plain, [dkvn, (sv["p"], kvr, o_kvlat // kvr, 0)],
                                 [g_kv_lat[l:l + 1]], [(kvr, BF16)], reds=[kvr])
        zb = lambda n: jnp.zeros((seq, n), BF16)
        dp = _cat([dq_sb.astype(BF16), dk_sb.astype(BF16), dv_sb.astype(BF16), zb(o_qlat - 3 * sbw), dqlat,
                   zb(o_kvlat - o_qlat - qr), dkvlat, zb(o_rope - o_kvlat - kvr), drope,
                   zb(o_gate - o_rope - 2 * LANES), dgates])
        dh1 = _matmul("in_proj_dx", dp, w["w_in"], "nt")
        gr["w_in"] = _matmul("in_proj_dw", sv["h1"], dp, "tn")
        dx, dsh1, dsc1, dg_mix = _rowwise("norm1_bwd", _rmsmod_bwd, [dh1, sv["x"], dx_mid],
                                          [g_mix_norm[l:l + 1], sc1], [(d, F32)], reds=[d, d, d])
        dmods[l] = _cat([dsh1, dsc1, dgate1, dsh2, dsc2, dgate2])
        small[l] = (dg_mix, dg_q, dg_kv, dg_mlp)
        late = [chunk(g, big[i], row_sharded[i]) for i, g in enumerate(fold(gr)[:3])]

    small_parts = [jnp.concatenate(dmods, axis=0)]
    small_parts += [jnp.concatenate([small[l][i] for l in range(depth)], axis=0) for i in range(4)]
    small_parts.append(dg_final)
    small_all = _exchange("ag_small", _Comm("gather_all", small_parts))

    dmod_mine = lax.dynamic_slice_in_dim(small_all[0], me * ada_n, ada_n, axis=2)
    c_act_t = jnp.transpose(c_act)

    def outer_fn(ct, dm):
        acc = ct[:, 0:1] * dm[0:1, :]
        for b in range(1, NDEV):
            acc = acc + ct[:, b:b + 1] * dm[b:b + 1, :]
        return (acc,)

    g_w_ada = jnp.stack([_rowwise("ada_dw", outer_fn, [c_act_t], [dmod_mine[:, l, :]], [(ada_n, F32)])[0]
                         for l in range(depth)])

    landed = list(_exchange("a2a_last", _Comm("scatter", late, lands[:3], [0] * 3, depth))) + lands[3:]

    moments = dict(
        w_ada=(w_ada, m_w_ada, v_w_ada), b_ada=(b_ada, m_b_ada, v_b_ada),
        g_mix_norm=(g_mix_norm, m_g_mix_norm, v_g_mix_norm), w_in=(w_in, m_w_in, v_w_in),
        g_q_lat=(g_q_lat, m_g_q_lat, v_g_q_lat), w_q_up=(w_q_up, m_w_q_up, v_w_q_up),
        g_kv_lat=(g_kv_lat, m_g_kv_lat, v_g_kv_lat), w_kv_up=(w_kv_up, m_w_kv_up, v_w_kv_up),
        w_sb_out=(w_sb_out, m_w_sb_out, v_w_sb_out), w_mla_out=(w_mla_out, m_w_mla_out, v_w_mla_out),
        w_mix_out=(w_mix_out, m_w_mix_out, v_w_mix_out), g_mlp_norm=(g_mlp_norm, m_g_mlp_norm, v_g_mlp_norm),
        w_up=(w_up, m_w_up, v_w_up), w_down=(w_down, m_w_down, v_w_down),
        g_final=(g_final.reshape(1, d), m_g_final.reshape(1, d), v_g_final.reshape(1, d)))
    lands = dict(b_ada=small_all[0], g_mix_norm=small_all[1], g_q_lat=small_all[2], g_kv_lat=small_all[3],
                 g_mlp_norm=small_all[4], g_final=small_all[5], w_in=landed[0], w_q_up=landed[1],
                 w_kv_up=landed[2], w_sb_out=landed[3], w_mla_out=landed[4], w_mix_out=landed[5], w_up=landed[6],
                 w_down=landed[7])
    gs, deltas, new_ms, new_vs = [], [], [], []
    for name, (wt, mt, vt) in moments.items():
        if name == "w_ada":
            res = [g_w_ada] + _adamw("adamw_" + name, wt, g_w_ada, mt, vt)
        else:
            res = _sum_adamw("adamw_" + name, lands[name], wt, mt, vt)
        if name == "g_final":
            res = [t.reshape(d) for t in res]
        for lst, t in zip((gs, deltas, new_ms, new_vs), res):
            lst.append(t)

    return (loss, dx[None], *gs, *deltas, *new_ms, *new_vs)
```

```python
import functools

import jax
import jax.numpy as jnp
from jax import lax
from jax.experimental import pallas as pl
from jax.experimental.pallas import tpu as pltpu

F32 = jnp.float32
BF16 = jnp.bfloat16
NDEV = 8
LANES = 128
HEAD_DIM = 64
ROPE_DIM = 32
MLA_QK_DIM = HEAD_DIM + ROPE_DIM
ROPE_THETA = 10000.0
NORM_EPS = 1e-6
ADAM_LR = 0.001
ADAM_B1 = 0.9
ADAM_B2 = 0.999
ADAM_EPS = 1e-08
ADAM_WD = 0.01
ADAM_STEP = 10
VMEM_LIMIT = 48 * 1024 * 1024


def _pcall(body, **kw):
    return pl.pallas_call(body, **kw)


def _tile(n, pref):
    for t in (512, 384, 256, 128, 64, 32, 16, 8):
        if t <= pref and n % t == 0:
            return t
    return n


def _roundup(n, m):
    return (n + m - 1) // m * m


_CP = pltpu.CompilerParams(vmem_limit_bytes=VMEM_LIMIT)


def _rowwise(name, fn, rows, vecs, outs, reds=(), tb=256):
    rows = [r if isinstance(r, tuple) else (r, r.shape[1], 0, 0) for r in rows]
    nrows = None
    for arr, width, col, roff in rows:
        if roff == 0 and nrows is None:
            nrows = arr.shape[0]
    first_off = [r for r in rows if r[3] != 0]
    if first_off:
        nrows = min(nrows, first_off[0][3])
    tb = _tile(nrows, tb)
    nblk = nrows // tb
    n_in = len(rows) + len(vecs)
    n_out = len(outs)

    def body(*refs):
        vals = [r[...] for r in refs[:n_in]]
        res = fn(*vals)
        if not isinstance(res, (tuple, list)):
            res = (res,)
        for ref, val in zip(refs[n_in:n_in + n_out], res[:n_out]):
            ref[...] = val.astype(ref.dtype)
        for ref, val in zip(refs[n_in + n_out:], res[n_out:]):
            @pl.when(pl.program_id(0) == 0)
            def _(ref=ref):
                ref[...] = jnp.zeros_like(ref)
            ref[...] += jnp.sum(val.astype(F32), axis=0, keepdims=True)

    in_specs = []
    for arr, width, col, roff in rows:
        in_specs.append(pl.BlockSpec((tb, width), functools.partial(
            lambda i, col, rb: (rb + i, col), col=col, rb=roff // tb)))
    for v in vecs:
        in_specs.append(pl.BlockSpec(v.shape, lambda i, nd=v.ndim: (0,) * nd))
    out_specs = [pl.BlockSpec((tb, w), lambda i: (i, 0)) for w, _ in outs]
    out_specs += [pl.BlockSpec((1, w), lambda i: (0, 0)) for w in reds]
    out_shape = [jax.ShapeDtypeStruct((nrows, w), dt) for w, dt in outs]
    out_shape += [jax.ShapeDtypeStruct((1, w), F32) for w in reds]
    res = _pcall(body, name=name, grid=(nblk,), in_specs=in_specs, out_specs=out_specs,
                 out_shape=out_shape, compiler_params=_CP)(*[r[0] for r in rows], *vecs)
    return res


_DIMS = {"nn": (((1,), (0,)), ((), ())), "nt": (((1,), (1,)), ((), ())), "tn": (((0,), (0,)), ((), ()))}


def _matmul(name, a, b, mode, out_dtype=F32, epilogue=None, rows=(), vecs=(), outs=None):
    if mode == "nn":
        (m, k), n = a.shape, b.shape[1]
    elif mode == "nt":
        (m, k), n = a.shape, b.shape[0]
    else:
        (k, m), n = a.shape, b.shape[1]
    tm = _tile(m, 512)
    tn = next((t for t in (1536, 1024) if n % t == 0 and n > t), _tile(n, 512))
    dims = _DIMS[mode]
    outs = [out_dtype] if outs is None else outs
    n_extra = len(rows) + len(vecs)

    def body(a_ref, b_ref, *refs):
        acc = lax.dot_general(a_ref[...].astype(BF16), b_ref[...].astype(BF16), dims, preferred_element_type=F32)
        res = (acc,) if epilogue is None else epilogue(acc, *[r[...] for r in refs[:n_extra]])
        for o_ref, val in zip(refs[n_extra:], res):
            o_ref[...] = val.astype(o_ref.dtype)

    a_spec = pl.BlockSpec((k, tm), lambda j, i: (0, i)) if mode == "tn" else pl.BlockSpec((tm, k), lambda j, i: (i, 0))
    b_spec = pl.BlockSpec((tn, k), lambda j, i: (j, 0)) if mode == "nt" else pl.BlockSpec((k, tn), lambda j, i: (0, j))
    blk = pl.BlockSpec((tm, tn), lambda j, i: (i, j))
    res = _pcall(body, name=name, grid=(n // tn, m // tm),
                 in_specs=[a_spec, b_spec] + [blk] * len(rows) + [pl.BlockSpec((1, tn), lambda j, i: (0, j))] * len(vecs),
                 out_specs=[blk] * len(outs), out_shape=[jax.ShapeDtypeStruct((m, n), dt) for dt in outs],
                 compiler_params=_CP)(a, b, *rows, *vecs)
    return res[0] if len(outs) == 1 else res


class _Comm:
    KS = {"gather_all": (1, 2, 3, 4, 5, 6, 7), "gather_own": (1, 2, 4, 6), "gather_fwd": (2, 4, 6),
          "scatter": (1, 2, 3, 4, 5, 6, 7)}

    def __init__(self, kind, srcs, lands=None, layers=None, depth=None):
        self.kind, self.srcs, self.layers = kind, list(srcs), layers
        self.n = len(lands) if kind == "gather_fwd" else len(srcs)
        self.lands = list(lands) if lands is not None else [None] * self.n
        self.out_shapes = []
        for i, land in enumerate(self.lands):
            if land is not None:
                self.out_shapes.append(jax.ShapeDtypeStruct(land.shape, land.dtype))
            elif kind == "scatter":
                self.out_shapes.append(jax.ShapeDtypeStruct((NDEV, depth) + srcs[i].shape[1:], srcs[i].dtype))
            else:
                self.out_shapes.append(jax.ShapeDtypeStruct((NDEV,) + srcs[i].shape, srcs[i].dtype))
        self.operands = self.srcs + [t for t in self.lands if t is not None]
        self.scratch = [pltpu.SemaphoreType.DMA((NDEV - 1, self.n)), pltpu.SemaphoreType.DMA((NDEV - 1, self.n)),
                        pltpu.SemaphoreType.DMA((self.n,))]

    def aliases(self, first_in, first_out):
        given = [i for i, t in enumerate(self.lands) if t is not None]
        return {first_in + len(self.srcs) + pos: first_out + i for pos, i in enumerate(given)}

    def copies(self, in_refs, out_refs, send_sems, recv_sems, local_sems):
        x, y, c = lax.axis_index("x"), lax.axis_index("y"), lax.axis_index("c")
        me = 4 * x + 2 * y + c
        cps = []
        if self.kind != "gather_fwd":
            for i in range(self.n):
                src = in_refs[i].at[me] if self.kind == "scatter" else in_refs[i]
                dst = out_refs[i].at[me, self.layers[i]] if self.kind == "scatter" else out_refs[i].at[me]
                cps.append(pltpu.make_async_copy(src, dst, local_sems.at[i]))
        for k in self.KS[self.kind]:
            px = 1 - x if k & 4 else x
            py = 1 - y if k & 2 else y
            pc = 1 - c if k & 1 else c
            peer = 4 * px + 2 * py + pc
            for i in range(self.n):
                if self.kind == "gather_fwd":
                    src, dst, to = in_refs[i].at[peer], out_refs[i].at[peer], (x, y, 1 - c)
                elif self.kind == "scatter":
                    src, dst, to = in_refs[i].at[peer], out_refs[i].at[me, self.layers[i]], (px, py, pc)
                else:
                    src, dst, to = in_refs[i], out_refs[i].at[me], (px, py, pc)
                cps.append(pltpu.make_async_remote_copy(
                    src_ref=src, dst_ref=dst, send_sem=send_sems.at[k - 1, i], recv_sem=recv_sems.at[k - 1, i],
                    device_id=to, device_id_type=pl.DeviceIdType.MESH))
        return cps


class _CommGroup:
    def __init__(self, comms):
        self.comms = comms
        self.n = sum(cm.n for cm in comms)
        self.operands = [t for cm in comms for t in cm.operands]
        self.out_shapes = [t for cm in comms for t in cm.out_shapes]
        self.scratch = [t for cm in comms for t in cm.scratch]

    def aliases(self, first_in, first_out):
        out = {}
        for cm in self.comms:
            out.update(cm.aliases(first_in, first_out))
            first_in, first_out = first_in + len(cm.operands), first_out + cm.n
        return out

    def copies(self, in_refs, out_refs, *sems):
        cps, i, o = [], 0, 0
        for j, cm in enumerate(self.comms):
            cps += cm.copies(in_refs[i:i + len(cm.operands)], out_refs[o:o + cm.n], *sems[3 * j:3 * j + 3])
            i, o = i + len(cm.operands), o + cm.n
        return cps

    def split(self, outs):
        res, o = [], 0
        for cm in self.comms:
            res.append(list(outs[o:o + cm.n]))
            o += cm.n
        return res


_ANY = pl.BlockSpec(memory_space=pl.ANY)


def _exchange(name, comm):
    nci = len(comm.operands)

    def body(*refs):
        cps = comm.copies(refs[:nci], refs[nci:nci + comm.n], *refs[nci + comm.n:])
        for cp in cps:
            cp.start()
        for cp in cps:
            cp.wait()

    return _pcall(body, name=name, in_specs=[_ANY] * nci, out_specs=[_ANY] * comm.n, out_shape=comm.out_shapes,
                  scratch_shapes=comm.scratch, input_output_aliases=comm.aliases(0, 0))(*comm.operands)


def _hosted(name, body, grid, arrays, in_specs, out_shapes, out_specs, comm):
    if comm is None:
        return _pcall(body, name=name, grid=grid, in_specs=in_specs, out_specs=out_specs, out_shape=out_shapes,
                      compiler_params=_CP)(*arrays), []
    ni, no, nci = len(arrays), len(out_shapes), len(comm.operands)

    def full(*refs):
        ins, cin = refs[:ni], refs[ni:ni + nci]
        outs = refs[ni + nci:ni + nci + no]
        cout = refs[ni + nci + no:ni + nci + no + comm.n]
        sems = refs[ni + nci + no + comm.n:]
        first = functools.reduce(jnp.logical_and, [pl.program_id(a) == 0 for a in range(len(grid))])
        last = functools.reduce(jnp.logical_and, [pl.program_id(a) == grid[a] - 1 for a in range(len(grid))])

        @pl.when(first)
        def _():
            for cp in comm.copies(cin, cout, *sems):
                cp.start()

        body(*ins, *outs)

        @pl.when(last)
        def _():
            for cp in comm.copies(cin, cout, *sems):
                cp.wait()

    res = _pcall(full, name=name, grid=grid, in_specs=list(in_specs) + [_ANY] * nci,
                 out_specs=list(out_specs) + [_ANY] * comm.n, out_shape=list(out_shapes) + comm.out_shapes,
                 scratch_shapes=comm.scratch, input_output_aliases=comm.aliases(ni, no),
                 compiler_params=_CP)(*arrays, *comm.operands)
    return res[:no], res[no:]


def _dot_nt(a, b):
    return lax.dot_general(a, b, _DIMS["nt"], preferred_element_type=F32)


def _dot_tn(a, b):
    return lax.dot_general(a, b, _DIMS["tn"], preferred_element_type=F32)


def _dot_nn(a, b):
    return jnp.dot(a, b, preferred_element_type=F32)


def _tri(tk, rel):
    j = lax.broadcasted_iota(jnp.int32, (tk, tk), 0)
    s = lax.broadcasted_iota(jnp.int32, (tk, tk), 1)
    return {"after": j > s, "upto": j <= s, "before": j < s}[rel].astype(BF16)


def _pairs_per_step(nhp):
    return 2 if nhp % 2 == 0 else 1


def _pair(a, pr):
    return a[:, pr * LANES:(pr + 1) * LANES]


def _head_masks():
    lane = lax.broadcasted_iota(jnp.int32, (1, LANES), 1)
    return [(lane // HEAD_DIM) == h for h in range(2)]


ROW_CHUNK = 32


def _by_rows(fn, n_out, *arrays):
    rows = arrays[0].shape[0]
    step = min(ROW_CHUNK, rows)
    outs = [[] for _ in range(n_out)]
    for r in range(0, rows, step):
        for o, val in zip(outs, fn(r, *[a[r:r + step] for a in arrays])):
            o.append(val)
    return [jnp.concatenate(o, axis=0) for o in outs]


def _causal(r0, k0, rows, tk, strict):
    row = lax.broadcasted_iota(jnp.int32, (rows, tk), 0) + r0
    col = lax.broadcasted_iota(jnp.int32, (rows, tk), 1) + k0
    return col < row if strict else col <= row


def _wide(stat, width):
    return stat if width == LANES else jnp.concatenate([stat] * (width // LANES), axis=1)


def _row_sum(v):
    return jnp.broadcast_to(jnp.sum(v, axis=1, keepdims=True), (v.shape[0], LANES))


def _split_bf16(v):
    hi = v.astype(BF16)
    return hi, (v - hi.astype(F32)).astype(BF16)


def _sb_logs(z, scale, mask):
    z = z * scale
    e = jnp.exp(-jnp.abs(z))
    log_sig = jnp.minimum(z, 0.0) - jnp.log(1.0 + e)
    log_fail = log_sig - z
    return z, log_sig, (log_fail if mask is None else jnp.where(mask, log_fail, 0.0))


def _two_loops(n_full, nkb, near_first, step, carry):
    if near_first:
        carry = lax.fori_loop(0, nkb - n_full, lambda j, c: step(nkb - 1 - j, True, c), carry)
        return lax.fori_loop(0, n_full, lambda j, c: step(n_full - 1 - j, False, c), carry)
    carry = lax.fori_loop(0, n_full, lambda j, c: step(j, False, c), carry)
    return lax.fori_loop(n_full, nkb, lambda j, c: step(j, True, c), carry)


def _sb_fwd(name, p, nhp, tq, tk, comm=None):
    s = p.shape[0]
    scale = HEAD_DIM ** -0.5
    nq = s // tq
    pp = _pairs_per_step(nhp)
    wide = pp * LANES

    def body(q_ref, k_ref, v_ref, o_ref, tot_ref):
        qi = pl.program_id(1)
        masks = _head_masks()
        after = _tri(tk, "after")
        nkb = ((qi + 1) * tq + tk - 1) // tk
        q = q_ref[...]
        qhs = [jnp.where(hm, _pair(q, pr), 0.0).astype(BF16) for pr in range(pp) for hm in masks]

        def step(kb, masked, carry):
            ks = pl.multiple_of(kb * tk, tk)
            ks_all = k_ref[pl.ds(ks, tk), :].astype(BF16)
            vs_all = v_ref[pl.ds(ks, tk), :].astype(BF16)
            mask_of = lambda r, n: _causal(qi * tq + r, ks, n, tk, True) if masked else None
            heads = range(len(qhs))

            def logs(r, zc):
                _, log_sig, log_fail = _sb_logs(zc, scale, mask_of(r, zc.shape[0]))
                return (log_sig,) + _split_bf16(log_fail) + (_row_sum(log_fail),)

            def weights(r, lsc, runc, laterc):
                w = jnp.exp(lsc + runc + _wide(laterc, tk))
                return ((jnp.where(mask_of(r, w.shape[0]), w, 0.0) if masked else w).astype(BF16),)

            zs = [_dot_nt(qhs[i], _pair(ks_all, i // 2)) for i in heads]
            first = [_by_rows(logs, 4, zs[i]) for i in heads]
            runs = [_dot_nn(first[i][1], after) + _dot_nn(first[i][2], after) for i in heads]
            ws = [_by_rows(weights, 1, first[i][0], runs[i], carry[i][0])[0] for i in heads]
            pvs = [_dot_nn(ws[i], _pair(vs_all, i // 2)) for i in heads]
            return tuple((carry[i][0] + first[i][3], carry[i][1] + pvs[i]) for i in heads)

        init = (jnp.zeros((tq, LANES), F32), jnp.zeros((tq, LANES), F32))
        res = _two_loops((qi * tq) // tk, nkb, True, step, (init,) * (2 * pp))
        for pr in range(pp):
            (tot0, acc0), (tot1, acc1) = res[2 * pr], res[2 * pr + 1]
            o_ref[:, pr * LANES:(pr + 1) * LANES] = jnp.where(masks[0], acc0, acc1)
            tot_ref[:, pr * LANES:(pr + 1) * LANES] = jnp.where(masks[0], tot0, tot1)

    ng = nhp // pp
    blk = pl.BlockSpec((tq, wide), lambda h, i: (i, h))
    shape = jax.ShapeDtypeStruct((s, nhp * LANES), F32)
    return _hosted(name, body, (ng, nq), [p, p, p],
                   [blk, pl.BlockSpec((s, wide), lambda h, i: (0, ng + h)),
                    pl.BlockSpec((s, wide), lambda h, i: (0, 2 * ng + h))], [shape, shape], [blk, blk], comm)


def _sb_bwd(name, p, tot, do, nhp, tq, tk, comm=None):
    s = p.shape[0]
    scale = HEAD_DIM ** -0.5
    nq = s // tq
    pp = _pairs_per_step(nhp)
    wide = pp * LANES

    def body(q_ref, k_ref, v_ref, tot_ref, do_ref, dq_ref, dk_ref, dv_ref):
        qi = pl.program_id(1)

        @pl.when(qi == 0)
        def _():
            dk_ref[...] = jnp.zeros_like(dk_ref)
            dv_ref[...] = jnp.zeros_like(dv_ref)

        masks = _head_masks()
        upto, before = _tri(tk, "upto"), _tri(tk, "before")
        nkb = ((qi + 1) * tq + tk - 1) // tk
        q = q_ref[...]
        qbs = q.astype(BF16)
        dout = do_ref[...]
        doutbs = dout.astype(BF16)
        qhs = [jnp.where(hm, _pair(q, pr), 0.0).astype(BF16) for pr in range(pp) for hm in masks]
        dohs = [jnp.where(hm, _pair(dout, pr), 0.0).astype(BF16) for pr in range(pp) for hm in masks]
        tot = tot_ref[...]
        totals = [jnp.broadcast_to(tot[:, h * HEAD_DIM:h * HEAD_DIM + 1], (tq, LANES)) for h in range(2 * pp)]

        def step(kb, masked, carry):
            ks = pl.multiple_of(kb * tk, tk)
            ks_all = k_ref[pl.ds(ks, tk), :].astype(BF16)
            vs_all = v_ref[pl.ds(ks, tk), :].astype(BF16)
            mask_of = lambda r, n: _causal(qi * tq + r, ks, n, tk, True) if masked else None
            heads = range(len(qhs))

            def logs(r, zc):
                _, log_sig, log_fail = _sb_logs(zc, scale, mask_of(r, zc.shape[0]))
                return (log_sig,) + _split_bf16(log_fail) + (_row_sum(log_fail),)

            def weights(r, lsc, runc, basec, dwc):
                w = jnp.exp(lsc + (_wide(basec, tk) - runc))
                if masked:
                    w = jnp.where(mask_of(r, w.shape[0]), w, 0.0)
                g = w * dwc
                return (w.astype(BF16), g) + _split_bf16(g) + (_row_sum(g),)

            def dscore(r, gc, lsc, zc, grc, gbc):
                dz = gc * jnp.exp(lsc - zc * scale) - jnp.exp(lsc) * (_wide(gbc, tk) + grc)
                if masked:
                    dz = jnp.where(mask_of(r, dz.shape[0]), dz, 0.0)
                return ((dz * scale).astype(BF16),)

            zs = [_dot_nt(qhs[i], _pair(ks_all, i // 2)) for i in heads]
            dws = [_dot_nt(dohs[i], _pair(vs_all, i // 2)) for i in heads]
            first = [_by_rows(logs, 4, zs[i]) for i in heads]
            runs = [_dot_nn(first[i][1], upto) + _dot_nn(first[i][2], upto) for i in heads]
            second = [_by_rows(weights, 5, first[i][0], runs[i], totals[i] - carry[i][0], dws[i])
                      for i in heads]
            g_runs = [_dot_nn(second[i][2], before) + _dot_nn(second[i][3], before) for i in heads]
            dzs = [_by_rows(dscore, 1, second[i][1], first[i][0], zs[i], g_runs[i], carry[i][1])[0] for i in heads]
            dks = [_dot_tn(dzs[i], _pair(qbs, i // 2)) for i in heads]
            dvs = [_dot_tn(second[i][0], _pair(doutbs, i // 2)) for i in heads]
            dqs = [_dot_nn(dzs[i], _pair(ks_all, i // 2)) for i in heads]
            for pr in range(pp):
                cols = slice(pr * LANES, (pr + 1) * LANES)
                dk_ref[pl.ds(ks, tk), cols] += jnp.where(masks[0], dks[2 * pr], dks[2 * pr + 1])
                dv_ref[pl.ds(ks, tk), cols] += jnp.where(masks[0], dvs[2 * pr], dvs[2 * pr + 1])
            return tuple((carry[i][0] + first[i][3], carry[i][1] + second[i][4], carry[i][2] + dqs[i]) for i in heads)

        zero = jnp.zeros((tq, LANES), F32)
        res = _two_loops((qi * tq) // tk, nkb, False, step, ((zero, zero, zero),) * (2 * pp))
        for pr in range(pp):
            dq_ref[:, pr * LANES:(pr + 1) * LANES] = jnp.where(masks[0], res[2 * pr][2], res[2 * pr + 1][2])

    ng = nhp // pp
    blk = pl.BlockSpec((tq, wide), lambda h, i: (i, h))
    full = pl.BlockSpec((s, wide), lambda h, i: (0, h))
    shape = jax.ShapeDtypeStruct((s, nhp * LANES), F32)
    return _hosted(name, body, (ng, nq), [p, p, p, tot, do],
                   [blk, pl.BlockSpec((s, wide), lambda h, i: (0, ng + h)),
                    pl.BlockSpec((s, wide), lambda h, i: (0, 2 * ng + h)), blk, blk],
                   [shape, shape, shape], [blk, full, full], comm)


def _mla_fwd(name, q, k, v, tq, tk, comm=None):
    s = q.shape[0]
    nhp = v.shape[1] // LANES
    scale = MLA_QK_DIM ** -0.5
    nq = s // tq
    pp = _pairs_per_step(nhp)

    def body(q_ref, k_ref, v_ref, o_ref, lse_ref):
        qi = pl.program_id(1)
        masks = _head_masks()
        nkb = ((qi + 1) * tq + tk - 1) // tk
        qhs = [q_ref[:, h * LANES:(h + 1) * LANES] for h in range(2 * pp)]

        def step(kb, masked, carry):
            ks = pl.multiple_of(kb * tk, tk)
            heads = range(len(qhs))

            def soft(r, zc, mc, lc):
                zc = zc * scale
                if masked:
                    zc = jnp.where(_causal(qi * tq + r, ks, zc.shape[0], tk, False), zc, -1e30)
                m_new = jnp.maximum(mc, jnp.max(zc, axis=1, keepdims=True))
                a = jnp.exp(mc - m_new)
                pr = jnp.exp(zc - _wide(m_new, tk))
                return pr.astype(BF16), m_new, a * lc + _row_sum(pr), a

            zs = [_dot_nt(qhs[h], k_ref[pl.ds(ks, tk), h * LANES:(h + 1) * LANES]) for h in heads]
            first = [_by_rows(soft, 4, zs[h], carry[h][0], carry[h][1]) for h in heads]
            pvs = [_dot_nn(first[h][0], v_ref[pl.ds(ks, tk), (h // 2) * LANES:(h // 2 + 1) * LANES]) for h in heads]
            accs = [_by_rows(lambda r, ac, aa, pc: (aa * ac + pc,), 1, carry[h][2], first[h][3], pvs[h])[0]
                    for h in heads]
            return tuple((first[h][1], first[h][2], accs[h]) for h in heads)

        init = (jnp.full((tq, LANES), -1e30, F32), jnp.zeros((tq, LANES), F32), jnp.zeros((tq, LANES), F32))
        res = _two_loops((qi * tq) // tk, nkb, False, step, (init,) * (2 * pp))
        for pr in range(pp):
            (m0, l0, acc0), (m1, l1, acc1) = res[2 * pr], res[2 * pr + 1]
            o_ref[:, pr * LANES:(pr + 1) * LANES] = jnp.where(masks[0], acc0 / l0, acc1 / l1)
            lse_ref[:, pr * LANES:(pr + 1) * LANES] = jnp.where(masks[0], m0 + jnp.log(l0), m1 + jnp.log(l1))

    shape = jax.ShapeDtypeStruct((s, nhp * LANES), F32)
    blk = pl.BlockSpec((tq, pp * LANES), lambda h, i: (i, h))
    return _hosted(name, body, (nhp // pp, nq), [q, k, v],
                   [pl.BlockSpec((tq, 2 * pp * LANES), lambda h, i: (i, h)),
                    pl.BlockSpec((s, 2 * pp * LANES), lambda h, i: (0, h)),
                    pl.BlockSpec((s, pp * LANES), lambda h, i: (0, h))], [shape, shape], [blk, blk], comm)


def _mla_bwd(name, q, k, v, o, lse, do, tq, tk, comm=None):
    s = q.shape[0]
    nhp = v.shape[1] // LANES
    scale = MLA_QK_DIM ** -0.5
    nq = s // tq
    pp = _pairs_per_step(nhp)

    def body(q_ref, k_ref, v_ref, o_ref, lse_ref, do_ref, dq_ref, dk_ref, dv_ref):
        qi = pl.program_id(1)

        @pl.when(qi == 0)
        def _():
            dk_ref[...] = jnp.zeros_like(dk_ref)
            dv_ref[...] = jnp.zeros_like(dv_ref)

        masks = _head_masks()
        nkb = ((qi + 1) * tq + tk - 1) // tk
        dout = do_ref[...]
        doutbs = dout.astype(BF16)
        prod = dout * o_ref[...]
        qhs = [q_ref[:, h * LANES:(h + 1) * LANES] for h in range(2 * pp)]
        dohs = [jnp.where(hm, _pair(dout, pr), 0.0).astype(BF16) for pr in range(pp) for hm in masks]
        totals = [_row_sum(jnp.where(hm, _pair(prod, pr), 0.0)) for pr in range(pp) for hm in masks]
        lse = lse_ref[...]
        lses = [jnp.broadcast_to(lse[:, h * HEAD_DIM:h * HEAD_DIM + 1], (tq, LANES)) for h in range(2 * pp)]

        def step(kb, masked, carry):
            ks = pl.multiple_of(kb * tk, tk)
            heads = range(len(qhs))

            def probs(r, zc, dpc, lsec, totc):
                pr = jnp.exp(zc * scale - _wide(lsec, tk))
                if masked:
                    pr = jnp.where(_causal(qi * tq + r, ks, pr.shape[0], tk, False), pr, 0.0)
                return pr.astype(BF16), (pr * (dpc - _wide(totc, tk)) * scale).astype(BF16)

            khs = [k_ref[pl.ds(ks, tk), h * LANES:(h + 1) * LANES] for h in heads]
            vvs = [v_ref[pl.ds(ks, tk), (h // 2) * LANES:(h // 2 + 1) * LANES] for h in heads]
            zs = [_dot_nt(qhs[h], khs[h]) for h in heads]
            dps = [_dot_nt(dohs[h], vvs[h]) for h in heads]
            both = [_by_rows(probs, 2, zs[h], dps[h], lses[h], totals[h]) for h in heads]
            dks = [_dot_tn(both[h][1], qhs[h]) for h in heads]
            dvs = [_dot_tn(both[h][0], _pair(doutbs, h // 2)) for h in heads]
            dqs = [_dot_nn(both[h][1], khs[h]) for h in heads]
            for h in heads:
                dk_ref[pl.ds(ks, tk), h * LANES:(h + 1) * LANES] += dks[h]
            for pr in range(pp):
                dv_ref[pl.ds(ks, tk), pr * LANES:(pr + 1) * LANES] += jnp.where(masks[0], dvs[2 * pr], dvs[2 * pr + 1])
            return tuple(carry[h] + dqs[h] for h in heads)

        zero = jnp.zeros((tq, LANES), F32)
        dqs = _two_loops((qi * tq) // tk, nkb, False, step, (zero,) * (2 * pp))
        for h in range(2 * pp):
            dq_ref[:, h * LANES:(h + 1) * LANES] = dqs[h]

    blk = pl.BlockSpec((tq, pp * LANES), lambda h, i: (i, h))
    blk2 = pl.BlockSpec((tq, 2 * pp * LANES), lambda h, i: (i, h))
    full = pl.BlockSpec((s, pp * LANES), lambda h, i: (0, h))
    full2 = pl.BlockSpec((s, 2 * pp * LANES), lambda h, i: (0, h))
    return _hosted(name, body, (nhp // pp, nq), [q, k, v, o, lse, do], [blk2, full2, full, blk, blk, blk],
                   [jax.ShapeDtypeStruct(q.shape, F32), jax.ShapeDtypeStruct(k.shape, F32),
                    jax.ShapeDtypeStruct(v.shape, F32)], [blk2, full2, full], comm)


def _norm_parts(x):
    r = lax.rsqrt(jnp.mean(x * x, axis=-1, keepdims=True) + NORM_EPS)
    return r, x * r


def _rmsmod_fwd(x, g, sc, sh):
    _, xh = _norm_parts(x)
    return ((xh * g) * (1.0 + sc) + sh,)


def _rmsmod_bwd(dh, x, dres, g, sc):
    r, xh = _norm_parts(x)
    dy = dh * (1.0 + sc)
    dxh = dy * g
    dx = r * (dxh - xh * jnp.mean(dxh * xh, axis=-1, keepdims=True)) + dres
    return dx, dh, dh * (xh * g), dy * xh


def _rms_bwd_plain(dh, x, g):
    r, xh = _norm_parts(x)
    dxh = dh * g
    return r * (dxh - xh * jnp.mean(dxh * xh, axis=-1, keepdims=True)), dh * xh


def _cat(parts):
    return jnp.concatenate(parts, axis=1)


def _swap_halves(a):
    half = a.shape[-1] // 2
    return jnp.concatenate([a[..., half:], a[..., :half]], axis=-1)


def _adamw_fn(w, g, m, v):
    m = ADAM_B1 * m + (1.0 - ADAM_B1) * g
    v = ADAM_B2 * v + (1.0 - ADAM_B2) * jnp.square(g)
    m_hat = m / (1.0 - ADAM_B1 ** ADAM_STEP)
    v_hat = v / (1.0 - ADAM_B2 ** ADAM_STEP)
    delta = -ADAM_LR * (m_hat / (jnp.sqrt(v_hat) + ADAM_EPS) + ADAM_WD * w)
    return delta, m, v


def _adamw(name, w, g, m, v):
    shape = w.shape
    width = shape[-1]
    flat = [t.reshape(-1, width) for t in (w, g, m, v)]
    res = _rowwise(name, _adamw_fn, flat, [], [(width, F32)] * 3)
    return [t.reshape(shape) for t in res]


def _sum_adamw(name, land, w, m, v):
    shape = w.shape
    width = shape[-1]
    rows = w.size // width

    def fn(*blocks):
        g = blocks[0].astype(F32)
        for b in blocks[1:NDEV]:
            g = g + b.astype(F32)
        return (g,) + _adamw_fn(blocks[NDEV], g, blocks[NDEV + 1], blocks[NDEV + 2])

    def fn_whole(wb, mb, vb, lb):
        return fn(*[lb[i] for i in range(NDEV)], wb, mb, vb)

    flat = [t.reshape(rows, width) for t in (w, m, v)]
    if rows % 16 == 0:
        views = [(land.reshape(NDEV * rows, width), width, 0, i * rows) for i in range(NDEV)]
        res = _rowwise(name, fn, views + flat, [], [(width, F32)] * 4)
    else:
        res = _rowwise(name, fn_whole, flat, [land.reshape(NDEV, rows, width)], [(width, F32)] * 4)
    return [t.reshape(shape) for t in res]


def kernel(x, c, positions, w_ada, b_ada, g_mix_norm, w_in, g_q_lat, w_q_up, g_kv_lat, w_kv_up, w_sb_out, w_mla_out, w_mix_out, g_mlp_norm, w_up, w_down, g_final, loss_target, m_w_ada, m_b_ada, m_g_mix_norm, m_w_in, m_g_q_lat, m_w_q_up, m_g_kv_lat, m_w_kv_up, m_w_sb_out, m_w_mla_out, m_w_mix_out, m_g_mlp_norm, m_w_up, m_w_down, m_g_final, v_w_ada, v_b_ada, v_g_mix_norm, v_w_in, v_g_q_lat, v_w_q_up, v_g_kv_lat, v_w_kv_up, v_w_sb_out, v_w_mla_out, v_w_mix_out, v_g_mlp_norm, v_w_up, v_w_down, v_g_final):
    seq, d = x.shape[1], x.shape[2]
    depth = w_ada.shape[0]
    qr, kvr = g_q_lat.shape[1], g_kv_lat.shape[1]
    sbw, mlaw = w_sb_out.shape[1], w_mla_out.shape[1]
    nh = mlaw // HEAD_DIM
    nhp_sb = sbw // LANES
    dff = w_up.shape[2] * NDEV
    ada_n = w_ada.shape[2]
    gb = min(512, d)
    tq, tk = min(256, seq), min(256, seq)
    me = 4 * lax.axis_index("x") + 2 * lax.axis_index("y") + lax.axis_index("c")

    o_qlat = _roundup(3 * sbw, qr)
    o_kvlat = _roundup(o_qlat + qr, kvr)
    o_rope = _roundup(o_kvlat + kvr, 2 * LANES)
    o_gate = _roundup(o_rope + 2 * LANES, gb)
    wp = o_gate + 2 * d

    c_all = _exchange("ag_c", _Comm("gather_all", [c.reshape(d // LANES, LANES)]))[0].reshape(NDEV, d)
    c_act = _rowwise("silu_c", lambda t: (t * (1.0 / (1.0 + jnp.exp(-t))),), [c_all], [], [(d, F32)])[0]
    parts = jnp.stack([_matmul("ada_fwd", c_act, w_ada[l], "nn") for l in range(depth)])
    parts_all = _exchange("ag_mod", _Comm("gather_all", [parts]))[0]
    mine = jnp.transpose(lax.dynamic_index_in_dim(parts_all, me, axis=2, keepdims=False), (1, 0, 2))
    mod = _rowwise("mod_bias", lambda a, b: (a + b,), [mine.reshape(depth, NDEV * ada_n), b_ada], [],
                   [(6 * d, F32)])[0]
    mods = [[mod[l:l + 1, i * d:(i + 1) * d] for i in range(6)] for l in range(depth)]

    big = [w_in, w_q_up, w_kv_up, w_sb_out, w_mla_out, w_mix_out, w_up, w_down]
    row_sharded = [False, False, False, False, False, True, False, True]
    shards = [[w[l].astype(BF16) for w in big] for l in range(depth)]

    ids_a, ids_b = [0, 1, 2, 3, 4, 5], [6, 7]
    pick = lambda l, ids: [shards[l][i] for i in ids]

    def unpack(gathered, ids):
        out = []
        for g, i in zip(gathered, ids):
            _, rows, cols = g.shape
            if i == 0:
                out.append(g)
            elif row_sharded[i]:
                out.append(g.reshape(NDEV * rows, cols))
            else:
                out.append(jnp.transpose(g, (1, 0, 2)).reshape(rows, NDEV * cols))
        return out

    n_in = w_in.shape[2]
    r0 = 3 * sbw + qr + kvr
    g0 = r0 + ROPE_DIM
    runs = [(0, 3 * sbw, 0), (3 * sbw, 3 * sbw + qr, o_qlat), (3 * sbw + qr, r0, o_kvlat), (g0, g0 + 2 * d, o_gate)]

    def shard_cols(g, a, b):
        return [g[j][:, max(a, n_in * j) - n_in * j:min(b, n_in * (j + 1)) - n_in * j]
                for j in range(a // n_in, (b - 1) // n_in + 1)]

    def derive(full):
        wi, wq, wkv, wsb, wmla, wmix = full
        dt = wi.dtype
        z = lambda r, n: jnp.zeros((r, n), dt)
        kr = _cat(shard_cols(wi, r0, g0))
        pieces, at = [], 0
        for a, b, start in runs[:3]:
            pieces += [z(d, start - at)] + shard_cols(wi, a, b)
            at = start + b - a
        pieces += [z(d, o_rope - at), z(d, HEAD_DIM), kr, z(d, LANES - MLA_QK_DIM),
                   z(d, HEAD_DIM), _swap_halves(kr), z(d, LANES - MLA_QK_DIM),
                   z(d, o_gate - o_rope - 2 * LANES)] + shard_cols(wi, g0, g0 + 2 * d)
        w_in_pad = _cat([t for t in pieces if t.shape[1]])
        wq3 = wq.reshape(qr, nh, MLA_QK_DIM)
        z3 = lambda n: jnp.zeros((qr, nh, n), dt)
        rope_w = wq3[:, :, HEAD_DIM:]
        wq_a = jnp.concatenate([wq3[:, :, :HEAD_DIM], rope_w, z3(LANES - MLA_QK_DIM)], axis=2).reshape(qr, nh * LANES)
        wq_b = jnp.concatenate([z3(HEAD_DIM), _swap_halves(rope_w), z3(LANES - MLA_QK_DIM)], axis=2).reshape(qr, nh * LANES)
        wkv3 = wkv.reshape(kvr, nh, 2 * HEAD_DIM)
        wk = jnp.concatenate([wkv3[:, :, :HEAD_DIM], jnp.zeros((kvr, nh, HEAD_DIM), dt)], axis=2).reshape(kvr, nh * LANES)
        wv = wkv3[:, :, HEAD_DIM:].reshape(kvr, nh * HEAD_DIM)
        return dict(w_in=w_in_pad, w_q=_cat([wq_a, wq_b]), w_kv=_cat([wk, wv]), w_sb=wsb, w_mla=wmla, w_mix=wmix)

    def fold(gr):
        gi, gq, gkv = gr["w_in"], gr["w_q"], gr["w_kv"]
        ra = gi[:, o_rope + HEAD_DIM:o_rope + MLA_QK_DIM]
        rb = gi[:, o_rope + LANES + HEAD_DIM:o_rope + LANES + MLA_QK_DIM]
        rope = ra + _swap_halves(rb)

        def cols(a, b):
            out = []
            for s0, s1, start in runs[:3] + [(r0, g0, None)] + runs[3:]:
                lo, hi = max(a, s0), min(b, s1)
                if lo < hi:
                    out.append(rope[:, lo - r0:hi - r0] if start is None else gi[:, start + lo - s0:start + hi - s0])
            return out

        g_in = jnp.stack([_cat(cols(n_in * j, n_in * (j + 1))) for j in range(NDEV)]).astype(BF16)
        ga = gq[:, :nh * LANES].reshape(qr, nh, LANES)
        gb_ = gq[:, nh * LANES:].reshape(qr, nh, LANES)
        g_q = jnp.concatenate([ga[:, :, :HEAD_DIM], ga[:, :, HEAD_DIM:MLA_QK_DIM]
                               + _swap_halves(gb_[:, :, HEAD_DIM:MLA_QK_DIM])], axis=2).reshape(qr, nh * MLA_QK_DIM)
        gk = gkv[:, :nh * LANES].reshape(kvr, nh, LANES)[:, :, :HEAD_DIM]
        gv = gkv[:, nh * LANES:].reshape(kvr, nh, HEAD_DIM)
        g_kv = jnp.concatenate([gk, gv], axis=2).reshape(kvr, nh * 2 * HEAD_DIM)
        return [g_in, g_q, g_kv, gr["w_sb"], gr["w_mla"], gr["w_mix"], gr["w_up"], gr["w_down"]]

    part_a = _exchange("ag_w0_own", _Comm("gather_own", pick(0, ids_a)))
    ready_a = _exchange("ag_w0_fwd", _Comm("gather_fwd", [], lands=part_a))
    part_b, weights = None, []

    inv_freq = 1.0 / (ROPE_THETA ** (jnp.arange(0, ROPE_DIM, 2, dtype=F32) / ROPE_DIM))
    ang = positions[0].astype(F32)[:, None] * inv_freq
    cos, sin = jnp.cos(ang), jnp.sin(ang)
    tail = jnp.zeros((seq, LANES - MLA_QK_DIM), F32)
    rope_c = _cat([jnp.ones((seq, HEAD_DIM), F32), cos, cos, tail])
    rope_s = _cat([jnp.zeros((seq, HEAD_DIM), F32), -sin, sin, tail])
    zero_vec = lambda n: jnp.zeros((1, n), F32)

    def rope_fwd(q2, kvs, pd, tc, ts):
        c8, s8 = _cat([tc] * nh), _cat([ts] * nh)
        qf = q2[:, :nh * LANES] * c8 + q2[:, nh * LANES:] * s8
        kpe = pd[:, :LANES] * tc + pd[:, LANES:] * ts
        return qf, kvs[:, :nh * LANES] + _cat([kpe] * nh), kvs[:, nh * LANES:]

    def rope_bwd(dq, dk, dv, tc, ts):
        c8, s8 = _cat([tc] * nh), _cat([ts] * nh)
        dks = dk[:, :LANES]
        for h in range(1, nh):
            dks = dks + dk[:, h * LANES:(h + 1) * LANES]
        return _cat([dq * c8, dq * s8]), _cat([dk, dv]), _cat([dks * tc, dks * ts])

    def merge_fwd(*a):
        ng = d // gb
        gs, gm, osb, omla = _cat(a[:ng]), _cat(a[ng:2 * ng]), a[2 * ng], a[2 * ng + 1]
        return (osb / (1.0 + jnp.exp(-gs)) + omla / (1.0 + jnp.exp(-gm)),)

    def merge_bwd(*a):
        ng = d // gb
        gs, gm, osb, omla, dm = _cat(a[:ng]), _cat(a[ng:2 * ng]), a[2 * ng], a[2 * ng + 1], a[2 * ng + 2]
        ss, sm = 1.0 / (1.0 + jnp.exp(-gs)), 1.0 / (1.0 + jnp.exp(-gm))
        return ss * dm, sm * dm, _cat([dm * osb * ss * (1.0 - ss), dm * omla * sm * (1.0 - sm)])

    def gate_cols(p):
        ng = d // gb
        return [(p, gb, o_gate // gb + i, 0) for i in range(2 * ng)]

    xs = x[0]
    saved = []
    for l in range(depth):
        w = derive(unpack(ready_a, ids_a))
        sh1, sc1, g1, sh2, sc2, g2 = mods[l]
        h1 = _rowwise("norm1", _rmsmod_fwd, [xs], [g_mix_norm[l:l + 1], sc1, sh1], [(d, BF16)])[0]
        p = _matmul("in_proj", h1, w["w_in"], "nn")
        comms = [_Comm("gather_own", pick(0, ids_b)) if l == 0 else _Comm("gather_fwd", [], lands=part_b)]
        if l + 1 < depth:
            comms.append(_Comm("gather_own", pick(l + 1, ids_a)))
        group = _CommGroup(comms)
        (o_sb, tot_sb), got = _sb_fwd("sb_fwd", p, nhp_sb, tq, tk, group)
        got = group.split(got)
        full_b, part_a = got[0], (got[1] if l + 1 < depth else None)
        y_sb = _matmul("sb_out", o_sb, w["w_sb"], "nn")
        qn = _rowwise("norm_q", _rmsmod_fwd, [(p, qr, o_qlat // qr, 0)],
                      [g_q_lat[l:l + 1], zero_vec(qr), zero_vec(qr)], [(qr, BF16)])[0]
        kvn = _rowwise("norm_kv", _rmsmod_fwd, [(p, kvr, o_kvlat // kvr, 0)],
                       [g_kv_lat[l:l + 1], zero_vec(kvr), zero_vec(kvr)], [(kvr, BF16)])[0]
        q2 = _matmul("q_up", qn, w["w_q"], "nn")
        kvs = _matmul("kv_up", kvn, w["w_kv"], "nn")
        qf, kf, vf = _rowwise("rope_fwd", rope_fwd, [q2, kvs, (p, 2 * LANES, o_rope // (2 * LANES), 0), rope_c, rope_s],
                              [], [(nh * LANES, BF16), (nh * LANES, BF16), (mlaw, BF16)])
        comms = [_Comm("gather_fwd", [], lands=full_b)] if l == 0 else []
        if l + 1 < depth:
            comms += [_Comm("gather_fwd", [], lands=part_a), _Comm("gather_own", pick(l + 1, ids_b))]
        group = _CommGroup(comms) if comms else None
        (o_mla, lse), got = _mla_fwd("mla_fwd", qf, kf, vf, tq, tk, group)
        got = group.split(got) if comms else []
        if l == 0:
            full_b = got.pop(0)
        if l + 1 < depth:
            ready_a, part_b = got
        w["w_up"], w["w_down"] = unpack(full_b, ids_b)
        weights.append(w)
        y_mla = _matmul("mla_out", o_mla, w["w_mla"], "nn")
        merged = _rowwise("merge_fwd", merge_fwd, gate_cols(p) + [y_sb, y_mla], [], [(d, BF16)])[0]
        resid = lambda acc, xv, g: (acc, xv + g * acc)
        y1, x_mid = _matmul("mix_out", merged, w["w_mix"], "nn", epilogue=resid, rows=[xs], vecs=[g1], outs=[F32, F32])
        h2 = _rowwise("norm2", _rmsmod_fwd, [x_mid], [g_mlp_norm[l:l + 1], sc2, sh2], [(d, BF16)])[0]
        u, act = _matmul("mlp_up", h2, w["w_up"], "nn", outs=[F32, BF16],
                         epilogue=lambda acc: (acc, jnp.square(jnp.maximum(acc, 0.0))))
        y2, x_out = _matmul("mlp_down", act, w["w_down"], "nn", epilogue=resid, rows=[x_mid], vecs=[g2], outs=[F32, F32])
        saved.append(dict(x=xs, h1=h1, p=p, o_sb=o_sb, tot_sb=tot_sb, y_sb=y_sb, qn=qn, kvn=kvn, qf=qf, kf=kf, vf=vf, o_mla=o_mla,
                          lse=lse, y_mla=y_mla, merged=merged, y1=y1, x_mid=x_mid, h2=h2, u=u, act=act, y2=y2))
        xs = x_out

    def final_fn(xv, tv, g):
        r, xh = _norm_parts(xv)
        diff = xh * g - tv
        dy = diff * (1.0 / d)
        dxh = dy * g
        dx = r * (dxh - xh * jnp.mean(dxh * xh, axis=-1, keepdims=True))
        return dx, diff * diff, dy * xh

    dx, sq, dg_final = _rowwise("loss_head", final_fn, [xs, loss_target[0]], [g_final.reshape(1, d)],
                                [(d, F32)], reds=[d, d])
    loss = lax.psum(0.5 * jnp.sum(sq) / d, ("x", "y", "c"))

    def chunk(gfull, wref, by_rows):
        rows, cols = wref.shape[1], wref.shape[2]
        if by_rows:
            return gfull.reshape(NDEV, rows, cols).astype(BF16)
        return jnp.transpose(gfull.reshape(rows, NDEV, cols), (1, 0, 2)).astype(BF16)

    dmods, small = [None] * depth, [None] * depth
    late, lands = None, [None] * len(big)
    for l in reversed(range(depth)):
        w, sv = weights[l], saved[l]
        sh1, sc1, g1, sh2, sc2, g2 = mods[l]
        gr = {}
        dy2, dgate2 = _rowwise("gate2_bwd", lambda dxv, y, g: (dxv * g, dxv * y), [dx, sv["y2"]], [g2],
                               [(d, BF16)], reds=[d])
        du = _matmul("mlp_down_dx", dy2, w["w_down"], "nt", outs=[BF16], rows=[sv["u"]],
                     epilogue=lambda acc, uv: (acc * 2.0 * jnp.maximum(uv, 0.0),))
        gr["w_down"] = _matmul("mlp_down_dw", sv["act"], dy2, "tn", BF16)
        dh2 = _matmul("mlp_up_dx", du, w["w_up"], "nt")
        gr["w_up"] = _matmul("mlp_up_dw", sv["h2"], du, "tn", BF16)
        dx_mid, dsh2, dsc2, dg_mlp = _rowwise("norm2_bwd", _rmsmod_bwd, [dh2, sv["x_mid"], dx],
                                              [g_mlp_norm[l:l + 1], sc2], [(d, F32)], reds=[d, d, d])
        dy1, dgate1 = _rowwise("gate1_bwd", lambda dxv, y, g: (dxv * g, dxv * y), [dx_mid, sv["y1"]], [g1],
                               [(d, BF16)], reds=[d])
        dmerged = _matmul("mix_out_dx", dy1, w["w_mix"], "nt")
        gr["w_mix"] = _matmul("mix_out_dw", sv["merged"], dy1, "tn", BF16)
        dy_sb, dy_mla, dgates = _rowwise("merge_bwd", merge_bwd, gate_cols(sv["p"]) + [sv["y_sb"], sv["y_mla"], dmerged],
                                         [], [(d, BF16), (d, BF16), (2 * d, BF16)])
        do_sb = _matmul("sb_out_dx", dy_sb, w["w_sb"], "nt")
        gr["w_sb"] = _matmul("sb_out_dw", sv["o_sb"], dy_sb, "tn", BF16)
        do_mla = _matmul("mla_out_dx", dy_mla, w["w_mla"], "nt")
        gr["w_mla"] = _matmul("mla_out_dw", sv["o_mla"], dy_mla, "tn", BF16)
        ready = {3: gr["w_sb"], 4: gr["w_mla"], 5: gr["w_mix"], 6: gr["w_up"], 7: gr["w_down"]}
        ready = {i: chunk(g, big[i], row_sharded[i]) for i, g in ready.items()}
        ids_a = [3, 4, 5, 6] + ([0, 1, 2] if late is not None else [])
        comm_a = _Comm("scatter", [ready[i] for i in (3, 4, 5, 6)] + (late or []), [lands[i] for i in ids_a],
                       [l] * 4 + [l + 1] * 3, depth)
        comm_b = _Comm("scatter", [ready[7]], [lands[7]], [l], depth)
        (dq_sb, dk_sb, dv_sb), got_a = _sb_bwd("sb_bwd", sv["p"], sv["tot_sb"], do_sb, nhp_sb, tq, tk, comm_a)
        (dqf, dkf, dvf), got_b = _mla_bwd("mla_bwd", sv["qf"], sv["kf"], sv["vf"], sv["o_mla"], sv["lse"], do_mla,
                                          tq, tk, comm_b)
        for i, t in zip(ids_a + [7], list(got_a) + list(got_b)):
            lands[i] = t
        dq2, dkvs, drope = _rowwise("rope_bwd", rope_bwd, [dqf, dkf, dvf, rope_c, rope_s], [],
                                    [(2 * nh * LANES, BF16), (nh * LANES + mlaw, BF16), (2 * LANES, BF16)])
        dqn = _matmul("q_up_dx", dq2, w["w_q"], "nt")
        gr["w_q"] = _matmul("q_up_dw", sv["qn"], dq2, "tn")
        dkvn = _matmul("kv_up_dx", dkvs, w["w_kv"], "nt")
        gr["w_kv"] = _matmul("kv_up_dw", sv["kvn"], dkvs, "tn")
        dqlat, dg_q = _rowwise("norm_q_bwd", _rms_bwd_plain, [dqn, (sv["p"], qr, o_qlat // qr, 0)],
                               [g_q_lat[l:l + 1]], [(qr, BF16)], reds=[qr])
        dkvlat, dg_kv = _rowwise("norm_kv_bwd", _rms_bwd_plain, [dkvn, (sv["p"], kvr, o_kvlat // kvr, 0)],
                                 [g_kv_lat[l:l + 1]], [(kvr, BF16)], reds=[kvr])
        zb = lambda n: jnp.zeros((seq, n), BF16)
        dp = _cat([dq_sb.astype(BF16), dk_sb.astype(BF16), dv_sb.astype(BF16), zb(o_qlat - 3 * sbw), dqlat,
                   zb(o_kvlat - o_qlat - qr), dkvlat, zb(o_rope - o_kvlat - kvr), drope,
                   zb(o_gate - o_rope - 2 * LANES), dgates])
        dh1 = _matmul("in_proj_dx", dp, w["w_in"], "nt")
        gr["w_in"] = _matmul("in_proj_dw", sv["h1"], dp, "tn")
        dx, dsh1, dsc1, dg_mix = _rowwise("norm1_bwd", _rmsmod_bwd, [dh1, sv["x"], dx_mid],
                                          [g_mix_norm[l:l + 1], sc1], [(d, F32)], reds=[d, d, d])
        dmods[l] = _cat([dsh1, dsc1, dgate1, dsh2, dsc2, dgate2])
        small[l] = (dg_mix, dg_q, dg_kv, dg_mlp)
        g_in, g_q, g_kv = fold(gr)[:3]
        late = [g_in, chunk(g_q, big[1], False), chunk(g_kv, big[2], False)]

    small_parts = [jnp.concatenate(dmods, axis=0)]
    small_parts += [jnp.concatenate([small[l][i] for l in range(depth)], axis=0) for i in range(4)]
    small_parts.append(dg_final)
    small_all = _exchange("ag_small", _Comm("gather_all", small_parts))

    dmod_mine = lax.dynamic_slice_in_dim(small_all[0], me * ada_n, ada_n, axis=2)
    c_act_t = jnp.transpose(c_act)

    def outer_fn(ct, dm):
        acc = ct[:, 0:1] * dm[0:1, :]
        for b in range(1, NDEV):
            acc = acc + ct[:, b:b + 1] * dm[b:b + 1, :]
        return (acc,)

    g_w_ada = jnp.stack([_rowwise("ada_dw", outer_fn, [c_act_t], [dmod_mine[:, l, :]], [(ada_n, F32)])[0]
                         for l in range(depth)])

    landed = list(_exchange("a2a_last", _Comm("scatter", late, lands[:3], [0] * 3, depth))) + lands[3:]

    moments = dict(
        w_ada=(w_ada, m_w_ada, v_w_ada), b_ada=(b_ada, m_b_ada, v_b_ada),
        g_mix_norm=(g_mix_norm, m_g_mix_norm, v_g_mix_norm), w_in=(w_in, m_w_in, v_w_in),
        g_q_lat=(g_q_lat, m_g_q_lat, v_g_q_lat), w_q_up=(w_q_up, m_w_q_up, v_w_q_up),
        g_kv_lat=(g_kv_lat, m_g_kv_lat, v_g_kv_lat), w_kv_up=(w_kv_up, m_w_kv_up, v_w_kv_up),
        w_sb_out=(w_sb_out, m_w_sb_out, v_w_sb_out), w_mla_out=(w_mla_out, m_w_mla_out, v_w_mla_out),
        w_mix_out=(w_mix_out, m_w_mix_out, v_w_mix_out), g_mlp_norm=(g_mlp_norm, m_g_mlp_norm, v_g_mlp_norm),
        w_up=(w_up, m_w_up, v_w_up), w_down=(w_down, m_w_down, v_w_down),
        g_final=(g_final.reshape(1, d), m_g_final.reshape(1, d), v_g_final.reshape(1, d)))
    lands = dict(b_ada=small_all[0], g_mix_norm=small_all[1], g_q_lat=small_all[2], g_kv_lat=small_all[3],
                 g_mlp_norm=small_all[4], g_final=small_all[5], w_in=landed[0], w_q_up=landed[1],
                 w_kv_up=landed[2], w_sb_out=landed[3], w_mla_out=landed[4], w_mix_out=landed[5], w_up=landed[6],
                 w_down=landed[7])
    gs, deltas, new_ms, new_vs = [], [], [], []
    for name, (wt, mt, vt) in moments.items():
        if name == "w_ada":
            res = [g_w_ada] + _adamw("adamw_" + name, wt, g_w_ada, mt, vt)
        else:
            res = _sum_adamw("adamw_" + name, lands[name], wt, mt, vt)
        if name == "g_final":
            res = [t.reshape(d) for t in res]
        for lst, t in zip((gs, deltas, new_ms, new_vs), res):
            lst.append(t)

    return (loss, dx[None], *gs, *deltas, *new_ms, *new_vs)
```

```python
import functools

import jax
import jax.numpy as jnp
from jax import lax
from jax.experimental import pallas as pl
from jax.experimental.pallas import tpu as pltpu

F32 = jnp.float32
BF16 = jnp.bfloat16
NDEV = 8
LANES = 128
HEAD_DIM = 64
ROPE_DIM = 32
MLA_QK_DIM = HEAD_DIM + ROPE_DIM
ROPE_THETA = 10000.0
NORM_EPS = 1e-6
ADAM_LR = 0.001
ADAM_B1 = 0.9
ADAM_B2 = 0.999
ADAM_EPS = 1e-08
ADAM_WD = 0.01
ADAM_STEP = 10
VMEM_LIMIT = 48 * 1024 * 1024


def _pcall(body, **kw):
    return pl.pallas_call(body, **kw)


def _tile(n, pref):
    for t in (512, 384, 256, 128, 64, 32, 16, 8):
        if t <= pref and n % t == 0:
            return t
    return n


def _roundup(n, m):
    return (n + m - 1) // m * m


_CP = pltpu.CompilerParams(vmem_limit_bytes=VMEM_LIMIT)


def _rowwise(name, fn, rows, vecs, outs, reds=(), tb=256):
    rows = [r if isinstance(r, tuple) else (r, r.shape[1], 0, 0) for r in rows]
    nrows = None
    for arr, width, col, roff in rows:
        if roff == 0 and nrows is None:
            nrows = arr.shape[0]
    first_off = [r for r in rows if r[3] != 0]
    if first_off:
        nrows = min(nrows, first_off[0][3])
    tb = _tile(nrows, tb)
    nblk = nrows // tb
    n_in = len(rows) + len(vecs)
    n_out = len(outs)

    def body(*refs):
        vals = [r[...] for r in refs[:n_in]]
        res = fn(*vals)
        if not isinstance(res, (tuple, list)):
            res = (res,)
        for ref, val in zip(refs[n_in:n_in + n_out], res[:n_out]):
            ref[...] = val.astype(ref.dtype)
        for ref, val in zip(refs[n_in + n_out:], res[n_out:]):
            @pl.when(pl.program_id(0) == 0)
            def _(ref=ref):
                ref[...] = jnp.zeros_like(ref)
            ref[...] += jnp.sum(val.astype(F32), axis=0, keepdims=True)

    in_specs = []
    for arr, width, col, roff in rows:
        in_specs.append(pl.BlockSpec((tb, width), functools.partial(
            lambda i, col, rb: (rb + i, col), col=col, rb=roff // tb)))
    for v in vecs:
        in_specs.append(pl.BlockSpec(v.shape, lambda i, nd=v.ndim: (0,) * nd))
    out_specs = [pl.BlockSpec((tb, w), lambda i: (i, 0)) for w, _ in outs]
    out_specs += [pl.BlockSpec((1, w), lambda i: (0, 0)) for w in reds]
    out_shape = [jax.ShapeDtypeStruct((nrows, w), dt) for w, dt in outs]
    out_shape += [jax.ShapeDtypeStruct((1, w), F32) for w in reds]
    res = _pcall(body, name=name, grid=(nblk,), in_specs=in_specs, out_specs=out_specs,
                 out_shape=out_shape, compiler_params=_CP)(*[r[0] for r in rows], *vecs)
    return res


_DIMS = {"nn": (((1,), (0,)), ((), ())), "nt": (((1,), (1,)), ((), ())), "tn": (((0,), (0,)), ((), ()))}


def _matmul(name, a, b, mode, out_dtype=F32, epilogue=None, rows=(), vecs=(), outs=None, comm=None):
    if mode == "nn":
        (m, k), n = a.shape, b.shape[1]
    elif mode == "nt":
        (m, k), n = a.shape, b.shape[0]
    else:
        (k, m), n = a.shape, b.shape[1]
    tm = _tile(m, 512)
    tn = next((t for t in (1536, 1024) if n % t == 0 and n > t), _tile(n, 512))
    dims = _DIMS[mode]
    outs = [out_dtype] if outs is None else outs
    n_extra = len(rows) + len(vecs)

    def body(a_ref, b_ref, *refs):
        acc = lax.dot_general(a_ref[...].astype(BF16), b_ref[...].astype(BF16), dims, preferred_element_type=F32)
        res = (acc,) if epilogue is None else epilogue(acc, *[r[...] for r in refs[:n_extra]])
        for o_ref, val in zip(refs[n_extra:], res):
            o_ref[...] = val.astype(o_ref.dtype)

    a_spec = pl.BlockSpec((k, tm), lambda j, i: (0, i)) if mode == "tn" else pl.BlockSpec((tm, k), lambda j, i: (i, 0))
    b_spec = pl.BlockSpec((tn, k), lambda j, i: (j, 0)) if mode == "nt" else pl.BlockSpec((k, tn), lambda j, i: (0, j))
    blk = pl.BlockSpec((tm, tn), lambda j, i: (i, j))
    res, got = _hosted(name, body, (n // tn, m // tm), [a, b, *rows, *vecs],
                       [a_spec, b_spec] + [blk] * len(rows) + [pl.BlockSpec((1, tn), lambda j, i: (0, j))] * len(vecs),
                       [jax.ShapeDtypeStruct((m, n), dt) for dt in outs], [blk] * len(outs), comm)
    res = res[0] if len(outs) == 1 else res
    return res if comm is None else (res, got)


class _Comm:
    KS = {"gather_all": (1, 2, 3, 4, 5, 6, 7), "gather_own": (1, 2, 4, 6), "gather_fwd": (2, 4, 6),
          "scatter": (1, 2, 3, 4, 5, 6, 7)}

    def __init__(self, kind, srcs, lands=None, layers=None, depth=None):
        self.kind, self.srcs, self.layers = kind, list(srcs), layers
        self.n = len(lands) if kind == "gather_fwd" else len(srcs)
        self.lands = list(lands) if lands is not None else [None] * self.n
        self.out_shapes = []
        for i, land in enumerate(self.lands):
            if land is not None:
                self.out_shapes.append(jax.ShapeDtypeStruct(land.shape, land.dtype))
            elif kind == "scatter":
                self.out_shapes.append(jax.ShapeDtypeStruct((NDEV, depth) + srcs[i].shape[1:], srcs[i].dtype))
            else:
                self.out_shapes.append(jax.ShapeDtypeStruct((NDEV,) + srcs[i].shape, srcs[i].dtype))
        self.operands = self.srcs + [t for t in self.lands if t is not None]
        self.scratch = [pltpu.SemaphoreType.DMA((NDEV - 1, self.n)), pltpu.SemaphoreType.DMA((NDEV - 1, self.n)),
                        pltpu.SemaphoreType.DMA((self.n,))]

    def aliases(self, first_in, first_out):
        given = [i for i, t in enumerate(self.lands) if t is not None]
        return {first_in + len(self.srcs) + pos: first_out + i for pos, i in enumerate(given)}

    def copies(self, in_refs, out_refs, send_sems, recv_sems, local_sems):
        x, y, c = lax.axis_index("x"), lax.axis_index("y"), lax.axis_index("c")
        me = 4 * x + 2 * y + c
        cps = []
        if self.kind != "gather_fwd":
            for i in range(self.n):
                src = in_refs[i].at[me] if self.kind == "scatter" else in_refs[i]
                dst = out_refs[i].at[me, self.layers[i]] if self.kind == "scatter" else out_refs[i].at[me]
                cps.append(pltpu.make_async_copy(src, dst, local_sems.at[i]))
        for k in self.KS[self.kind]:
            px = 1 - x if k & 4 else x
            py = 1 - y if k & 2 else y
            pc = 1 - c if k & 1 else c
            peer = 4 * px + 2 * py + pc
            for i in range(self.n):
                if self.kind == "gather_fwd":
                    src, dst, to = in_refs[i].at[peer], out_refs[i].at[peer], (x, y, 1 - c)
                elif self.kind == "scatter":
                    src, dst, to = in_refs[i].at[peer], out_refs[i].at[me, self.layers[i]], (px, py, pc)
                else:
                    src, dst, to = in_refs[i], out_refs[i].at[me], (px, py, pc)
                cps.append(pltpu.make_async_remote_copy(
                    src_ref=src, dst_ref=dst, send_sem=send_sems.at[k - 1, i], recv_sem=recv_sems.at[k - 1, i],
                    device_id=to, device_id_type=pl.DeviceIdType.MESH))
        return cps


class _CommGroup:
    def __init__(self, comms):
        self.comms = comms
        self.n = sum(cm.n for cm in comms)
        self.operands = [t for cm in comms for t in cm.operands]
        self.out_shapes = [t for cm in comms for t in cm.out_shapes]
        self.scratch = [t for cm in comms for t in cm.scratch]

    def aliases(self, first_in, first_out):
        out = {}
        for cm in self.comms:
            out.update(cm.aliases(first_in, first_out))
            first_in, first_out = first_in + len(cm.operands), first_out + cm.n
        return out

    def copies(self, in_refs, out_refs, *sems):
        cps, i, o = [], 0, 0
        for j, cm in enumerate(self.comms):
            cps += cm.copies(in_refs[i:i + len(cm.operands)], out_refs[o:o + cm.n], *sems[3 * j:3 * j + 3])
            i, o = i + len(cm.operands), o + cm.n
        return cps

    def split(self, outs):
        res, o = [], 0
        for cm in self.comms:
            res.append(list(outs[o:o + cm.n]))
            o += cm.n
        return res


_ANY = pl.BlockSpec(memory_space=pl.ANY)


def _exchange(name, comm):
    nci = len(comm.operands)

    def body(*refs):
        cps = comm.copies(refs[:nci], refs[nci:nci + comm.n], *refs[nci + comm.n:])
        for cp in cps:
            cp.start()
        for cp in cps:
            cp.wait()

    return _pcall(body, name=name, in_specs=[_ANY] * nci, out_specs=[_ANY] * comm.n, out_shape=comm.out_shapes,
                  scratch_shapes=comm.scratch, input_output_aliases=comm.aliases(0, 0))(*comm.operands)


def _hosted(name, body, grid, arrays, in_specs, out_shapes, out_specs, comm):
    if comm is None:
        return _pcall(body, name=name, grid=grid, in_specs=in_specs, out_specs=out_specs, out_shape=out_shapes,
                      compiler_params=_CP)(*arrays), []
    ni, no, nci = len(arrays), len(out_shapes), len(comm.operands)

    def full(*refs):
        ins, cin = refs[:ni], refs[ni:ni + nci]
        outs = refs[ni + nci:ni + nci + no]
        cout = refs[ni + nci + no:ni + nci + no + comm.n]
        sems = refs[ni + nci + no + comm.n:]
        first = functools.reduce(jnp.logical_and, [pl.program_id(a) == 0 for a in range(len(grid))])
        last = functools.reduce(jnp.logical_and, [pl.program_id(a) == grid[a] - 1 for a in range(len(grid))])

        @pl.when(first)
        def _():
            for cp in comm.copies(cin, cout, *sems):
                cp.start()

        body(*ins, *outs)

        @pl.when(last)
        def _():
            for cp in comm.copies(cin, cout, *sems):
                cp.wait()

    res = _pcall(full, name=name, grid=grid, in_specs=list(in_specs) + [_ANY] * nci,
                 out_specs=list(out_specs) + [_ANY] * comm.n, out_shape=list(out_shapes) + comm.out_shapes,
                 scratch_shapes=comm.scratch, input_output_aliases=comm.aliases(ni, no),
                 compiler_params=_CP)(*arrays, *comm.operands)
    return res[:no], res[no:]


def _dot_nt(a, b):
    return lax.dot_general(a, b, _DIMS["nt"], preferred_element_type=F32)


def _dot_tn(a, b):
    return lax.dot_general(a, b, _DIMS["tn"], preferred_element_type=F32)


def _dot_nn(a, b):
    return jnp.dot(a, b, preferred_element_type=F32)


def _tri(tk, rel):
    j = lax.broadcasted_iota(jnp.int32, (tk, tk), 0)
    s = lax.broadcasted_iota(jnp.int32, (tk, tk), 1)
    return {"after": j > s, "upto": j <= s, "before": j < s}[rel].astype(BF16)


def _pairs_per_step(nhp):
    return 2 if nhp % 2 == 0 else 1


def _pair(a, pr):
    return a[:, pr * LANES:(pr + 1) * LANES]


def _head_masks():
    lane = lax.broadcasted_iota(jnp.int32, (1, LANES), 1)
    return [(lane // HEAD_DIM) == h for h in range(2)]


ROW_CHUNK = 32


def _by_rows(fn, n_out, *arrays):
    rows = arrays[0].shape[0]
    step = min(ROW_CHUNK, rows)
    outs = [[] for _ in range(n_out)]
    for r in range(0, rows, step):
        for o, val in zip(outs, fn(r, *[a[r:r + step] for a in arrays])):
            o.append(val)
    return [jnp.concatenate(o, axis=0) for o in outs]


def _causal(r0, k0, rows, tk, strict):
    row = lax.broadcasted_iota(jnp.int32, (rows, tk), 0) + r0
    col = lax.broadcasted_iota(jnp.int32, (rows, tk), 1) + k0
    return col < row if strict else col <= row


def _wide(stat, width):
    return stat if width == LANES else jnp.concatenate([stat] * (width // LANES), axis=1)


def _row_sum(v):
    return jnp.broadcast_to(jnp.sum(v, axis=1, keepdims=True), (v.shape[0], LANES))


def _split_bf16(v):
    hi = v.astype(BF16)
    return hi, (v - hi.astype(F32)).astype(BF16)


def _sb_logs(z, scale, mask):
    z = z * scale
    e = jnp.exp(-jnp.abs(z))
    log_sig = jnp.minimum(z, 0.0) - jnp.log(1.0 + e)
    log_fail = log_sig - z
    return z, log_sig, (log_fail if mask is None else jnp.where(mask, log_fail, 0.0))


def _two_loops(n_full, nkb, near_first, step, carry):
    if near_first:
        carry = lax.fori_loop(0, nkb - n_full, lambda j, c: step(nkb - 1 - j, True, c), carry)
        return lax.fori_loop(0, n_full, lambda j, c: step(n_full - 1 - j, False, c), carry)
    carry = lax.fori_loop(0, n_full, lambda j, c: step(j, False, c), carry)
    return lax.fori_loop(n_full, nkb, lambda j, c: step(j, True, c), carry)


def _sb_fwd(name, p, nhp, tq, tk, comm=None):
    s = p.shape[0]
    scale = HEAD_DIM ** -0.5
    nq = s // tq
    pp = _pairs_per_step(nhp)
    wide = pp * LANES

    def body(q_ref, k_ref, v_ref, o_ref, tot_ref):
        qi = pl.program_id(1)
        masks = _head_masks()
        after = _tri(tk, "after")
        nkb = ((qi + 1) * tq + tk - 1) // tk
        q = q_ref[...]
        qhs = [jnp.where(hm, _pair(q, pr), 0.0).astype(BF16) for pr in range(pp) for hm in masks]

        def step(kb, masked, carry):
            ks = pl.multiple_of(kb * tk, tk)
            ks_all = k_ref[pl.ds(ks, tk), :].astype(BF16)
            vs_all = v_ref[pl.ds(ks, tk), :].astype(BF16)
            mask_of = lambda r, n: _causal(qi * tq + r, ks, n, tk, True) if masked else None
            heads = range(len(qhs))

            def logs(r, zc):
                _, log_sig, log_fail = _sb_logs(zc, scale, mask_of(r, zc.shape[0]))
                return (log_sig,) + _split_bf16(log_fail) + (_row_sum(log_fail),)

            def weights(r, lsc, runc, laterc):
                w = jnp.exp(lsc + runc + _wide(laterc, tk))
                return ((jnp.where(mask_of(r, w.shape[0]), w, 0.0) if masked else w).astype(BF16),)

            zs = [_dot_nt(qhs[i], _pair(ks_all, i // 2)) for i in heads]
            first = [_by_rows(logs, 4, zs[i]) for i in heads]
            runs = [_dot_nn(first[i][1], after) + _dot_nn(first[i][2], after) for i in heads]
            ws = [_by_rows(weights, 1, first[i][0], runs[i], carry[i][0])[0] for i in heads]
            pvs = [_dot_nn(ws[i], _pair(vs_all, i // 2)) for i in heads]
            return tuple((carry[i][0] + first[i][3], carry[i][1] + pvs[i]) for i in heads)

        init = (jnp.zeros((tq, LANES), F32), jnp.zeros((tq, LANES), F32))
        res = _two_loops((qi * tq) // tk, nkb, True, step, (init,) * (2 * pp))
        for pr in range(pp):
            (tot0, acc0), (tot1, acc1) = res[2 * pr], res[2 * pr + 1]
            o_ref[:, pr * LANES:(pr + 1) * LANES] = jnp.where(masks[0], acc0, acc1)
            tot_ref[:, pr * LANES:(pr + 1) * LANES] = jnp.where(masks[0], tot0, tot1)

    ng = nhp // pp
    blk = pl.BlockSpec((tq, wide), lambda h, i: (i, h))
    shape = jax.ShapeDtypeStruct((s, nhp * LANES), F32)
    return _hosted(name, body, (ng, nq), [p, p, p],
                   [blk, pl.BlockSpec((s, wide), lambda h, i: (0, ng + h)),
                    pl.BlockSpec((s, wide), lambda h, i: (0, 2 * ng + h))], [shape, shape], [blk, blk], comm)


def _sb_bwd(name, p, tot, do, nhp, tq, tk, comm=None):
    s = p.shape[0]
    scale = HEAD_DIM ** -0.5
    nq = s // tq
    pp = _pairs_per_step(nhp)
    wide = pp * LANES

    def body(q_ref, k_ref, v_ref, tot_ref, do_ref, dq_ref, dk_ref, dv_ref):
        qi = pl.program_id(1)

        @pl.when(qi == 0)
        def _():
            dk_ref[...] = jnp.zeros_like(dk_ref)
            dv_ref[...] = jnp.zeros_like(dv_ref)

        masks = _head_masks()
        upto, before = _tri(tk, "upto"), _tri(tk, "before")
        nkb = ((qi + 1) * tq + tk - 1) // tk
        q = q_ref[...]
        qbs = q.astype(BF16)
        dout = do_ref[...]
        doutbs = dout.astype(BF16)
        qhs = [jnp.where(hm, _pair(q, pr), 0.0).astype(BF16) for pr in range(pp) for hm in masks]
        dohs = [jnp.where(hm, _pair(dout, pr), 0.0).astype(BF16) for pr in range(pp) for hm in masks]
        tot = tot_ref[...]
        totals = [jnp.broadcast_to(tot[:, h * HEAD_DIM:h * HEAD_DIM + 1], (tq, LANES)) for h in range(2 * pp)]

        def step(kb, masked, carry):
            ks = pl.multiple_of(kb * tk, tk)
            ks_all = k_ref[pl.ds(ks, tk), :].astype(BF16)
            vs_all = v_ref[pl.ds(ks, tk), :].astype(BF16)
            mask_of = lambda r, n: _causal(qi * tq + r, ks, n, tk, True) if masked else None
            heads = range(len(qhs))

            def logs(r, zc):
                _, log_sig, log_fail = _sb_logs(zc, scale, mask_of(r, zc.shape[0]))
                return (log_sig,) + _split_bf16(log_fail) + (_row_sum(log_fail),)

            def weights(r, lsc, runc, basec, dwc):
                w = jnp.exp(lsc + (_wide(basec, tk) - runc))
                if masked:
                    w = jnp.where(mask_of(r, w.shape[0]), w, 0.0)
                g = w * dwc
                return (w.astype(BF16), g) + _split_bf16(g) + (_row_sum(g),)

            def dscore(r, gc, lsc, zc, grc, gbc):
                dz = gc * jnp.exp(lsc - zc * scale) - jnp.exp(lsc) * (_wide(gbc, tk) + grc)
                if masked:
                    dz = jnp.where(mask_of(r, dz.shape[0]), dz, 0.0)
                return ((dz * scale).astype(BF16),)

            zs = [_dot_nt(qhs[i], _pair(ks_all, i // 2)) for i in heads]
            dws = [_dot_nt(dohs[i], _pair(vs_all, i // 2)) for i in heads]
            first = [_by_rows(logs, 4, zs[i]) for i in heads]
            runs = [_dot_nn(first[i][1], upto) + _dot_nn(first[i][2], upto) for i in heads]
            second = [_by_rows(weights, 5, first[i][0], runs[i], totals[i] - carry[i][0], dws[i])
                      for i in heads]
            g_runs = [_dot_nn(second[i][2], before) + _dot_nn(second[i][3], before) for i in heads]
            dzs = [_by_rows(dscore, 1, second[i][1], first[i][0], zs[i], g_runs[i], carry[i][1])[0] for i in heads]
            dks = [_dot_tn(dzs[i], _pair(qbs, i // 2)) for i in heads]
            dvs = [_dot_tn(second[i][0], _pair(doutbs, i // 2)) for i in heads]
            dqs = [_dot_nn(dzs[i], _pair(ks_all, i // 2)) for i in heads]
            for pr in range(pp):
                cols = slice(pr * LANES, (pr + 1) * LANES)
                dk_ref[pl.ds(ks, tk), cols] += jnp.where(masks[0], dks[2 * pr], dks[2 * pr + 1])
                dv_ref[pl.ds(ks, tk), cols] += jnp.where(masks[0], dvs[2 * pr], dvs[2 * pr + 1])
            return tuple((carry[i][0] + first[i][3], carry[i][1] + second[i][4], carry[i][2] + dqs[i]) for i in heads)

        zero = jnp.zeros((tq, LANES), F32)
        res = _two_loops((qi * tq) // tk, nkb, False, step, ((zero, zero, zero),) * (2 * pp))
        for pr in range(pp):
            dq_ref[:, pr * LANES:(pr + 1) * LANES] = jnp.where(masks[0], res[2 * pr][2], res[2 * pr + 1][2])

    ng = nhp // pp
    blk = pl.BlockSpec((tq, wide), lambda h, i: (i, h))
    full = pl.BlockSpec((s, wide), lambda h, i: (0, h))
    shape = jax.ShapeDtypeStruct((s, nhp * LANES), F32)
    return _hosted(name, body, (ng, nq), [p, p, p, tot, do],
                   [blk, pl.BlockSpec((s, wide), lambda h, i: (0, ng + h)),
                    pl.BlockSpec((s, wide), lambda h, i: (0, 2 * ng + h)), blk, blk],
                   [shape, shape, shape], [blk, full, full], comm)


def _mla_fwd(name, q, k, v, tq, tk, comm=None):
    s = q.shape[0]
    nhp = v.shape[1] // LANES
    scale = MLA_QK_DIM ** -0.5
    nq = s // tq
    pp = _pairs_per_step(nhp)

    def body(q_ref, k_ref, v_ref, o_ref, lse_ref):
        qi = pl.program_id(1)
        masks = _head_masks()
        nkb = ((qi + 1) * tq + tk - 1) // tk
        qhs = [q_ref[:, h * LANES:(h + 1) * LANES] for h in range(2 * pp)]

        def step(kb, masked, carry):
            ks = pl.multiple_of(kb * tk, tk)
            heads = range(len(qhs))

            def soft(r, zc, mc, lc):
                zc = zc * scale
                if masked:
                    zc = jnp.where(_causal(qi * tq + r, ks, zc.shape[0], tk, False), zc, -1e30)
                m_new = jnp.maximum(mc, jnp.max(zc, axis=1, keepdims=True))
                a = jnp.exp(mc - m_new)
                pr = jnp.exp(zc - _wide(m_new, tk))
                return pr.astype(BF16), m_new, a * lc + _row_sum(pr), a

            zs = [_dot_nt(qhs[h], k_ref[pl.ds(ks, tk), h * LANES:(h + 1) * LANES]) for h in heads]
            first = [_by_rows(soft, 4, zs[h], carry[h][0], carry[h][1]) for h in heads]
            pvs = [_dot_nn(first[h][0], v_ref[pl.ds(ks, tk), (h // 2) * LANES:(h // 2 + 1) * LANES]) for h in heads]
            accs = [_by_rows(lambda r, ac, aa, pc: (aa * ac + pc,), 1, carry[h][2], first[h][3], pvs[h])[0]
                    for h in heads]
            return tuple((first[h][1], first[h][2], accs[h]) for h in heads)

        init = (jnp.full((tq, LANES), -1e30, F32), jnp.zeros((tq, LANES), F32), jnp.zeros((tq, LANES), F32))
        res = _two_loops((qi * tq) // tk, nkb, False, step, (init,) * (2 * pp))
        for pr in range(pp):
            (m0, l0, acc0), (m1, l1, acc1) = res[2 * pr], res[2 * pr + 1]
            o_ref[:, pr * LANES:(pr + 1) * LANES] = jnp.where(masks[0], acc0 / l0, acc1 / l1)
            lse_ref[:, pr * LANES:(pr + 1) * LANES] = jnp.where(masks[0], m0 + jnp.log(l0), m1 + jnp.log(l1))

    shape = jax.ShapeDtypeStruct((s, nhp * LANES), F32)
    blk = pl.BlockSpec((tq, pp * LANES), lambda h, i: (i, h))
    return _hosted(name, body, (nhp // pp, nq), [q, k, v],
                   [pl.BlockSpec((tq, 2 * pp * LANES), lambda h, i: (i, h)),
                    pl.BlockSpec((s, 2 * pp * LANES), lambda h, i: (0, h)),
                    pl.BlockSpec((s, pp * LANES), lambda h, i: (0, h))], [shape, shape], [blk, blk], comm)


def _mla_bwd(name, q, k, v, o, lse, do, tq, tk, comm=None):
    s = q.shape[0]
    nhp = v.shape[1] // LANES
    scale = MLA_QK_DIM ** -0.5
    nq = s // tq
    pp = _pairs_per_step(nhp)

    def body(q_ref, k_ref, v_ref, o_ref, lse_ref, do_ref, dq_ref, dk_ref, dv_ref):
        qi = pl.program_id(1)

        @pl.when(qi == 0)
        def _():
            dk_ref[...] = jnp.zeros_like(dk_ref)
            dv_ref[...] = jnp.zeros_like(dv_ref)

        masks = _head_masks()
        nkb = ((qi + 1) * tq + tk - 1) // tk
        dout = do_ref[...]
        doutbs = dout.astype(BF16)
        prod = dout * o_ref[...]
        qhs = [q_ref[:, h * LANES:(h + 1) * LANES] for h in range(2 * pp)]
        dohs = [jnp.where(hm, _pair(dout, pr), 0.0).astype(BF16) for pr in range(pp) for hm in masks]
        totals = [_row_sum(jnp.where(hm, _pair(prod, pr), 0.0)) for pr in range(pp) for hm in masks]
        lse = lse_ref[...]
        lses = [jnp.broadcast_to(lse[:, h * HEAD_DIM:h * HEAD_DIM + 1], (tq, LANES)) for h in range(2 * pp)]

        def step(kb, masked, carry):
            ks = pl.multiple_of(kb * tk, tk)
            heads = range(len(qhs))

            def probs(r, zc, dpc, lsec, totc):
                pr = jnp.exp(zc * scale - _wide(lsec, tk))
                if masked:
                    pr = jnp.where(_causal(qi * tq + r, ks, pr.shape[0], tk, False), pr, 0.0)
                return pr.astype(BF16), (pr * (dpc - _wide(totc, tk)) * scale).astype(BF16)

            khs = [k_ref[pl.ds(ks, tk), h * LANES:(h + 1) * LANES] for h in heads]
            vvs = [v_ref[pl.ds(ks, tk), (h // 2) * LANES:(h // 2 + 1) * LANES] for h in heads]
            zs = [_dot_nt(qhs[h], khs[h]) for h in heads]
            dps = [_dot_nt(dohs[h], vvs[h]) for h in heads]
            both = [_by_rows(probs, 2, zs[h], dps[h], lses[h], totals[h]) for h in heads]
            dks = [_dot_tn(both[h][1], qhs[h]) for h in heads]
            dvs = [_dot_tn(both[h][0], _pair(doutbs, h // 2)) for h in heads]
            dqs = [_dot_nn(both[h][1], khs[h]) for h in heads]
            for h in heads:
                dk_ref[pl.ds(ks, tk), h * LANES:(h + 1) * LANES] += dks[h]
            for pr in range(pp):
                dv_ref[pl.ds(ks, tk), pr * LANES:(pr + 1) * LANES] += jnp.where(masks[0], dvs[2 * pr], dvs[2 * pr + 1])
            return tuple(carry[h] + dqs[h] for h in heads)

        zero = jnp.zeros((tq, LANES), F32)
        dqs = _two_loops((qi * tq) // tk, nkb, False, step, (zero,) * (2 * pp))
        for h in range(2 * pp):
            dq_ref[:, h * LANES:(h + 1) * LANES] = dqs[h]

    blk = pl.BlockSpec((tq, pp * LANES), lambda h, i: (i, h))
    blk2 = pl.BlockSpec((tq, 2 * pp * LANES), lambda h, i: (i, h))
    full = pl.BlockSpec((s, pp * LANES), lambda h, i: (0, h))
    full2 = pl.BlockSpec((s, 2 * pp * LANES), lambda h, i: (0, h))
    return _hosted(name, body, (nhp // pp, nq), [q, k, v, o, lse, do], [blk2, full2, full, blk, blk, blk],
                   [jax.ShapeDtypeStruct(q.shape, F32), jax.ShapeDtypeStruct(k.shape, F32),
                    jax.ShapeDtypeStruct(v.shape, F32)], [blk2, full2, full], comm)


def _norm_parts(x):
    r = lax.rsqrt(jnp.mean(x * x, axis=-1, keepdims=True) + NORM_EPS)
    return r, x * r


def _rmsmod_fwd(x, g, sc, sh):
    _, xh = _norm_parts(x)
    return ((xh * g) * (1.0 + sc) + sh,)


def _rmsmod_bwd(dh, x, dres, g, sc):
    r, xh = _norm_parts(x)
    dy = dh * (1.0 + sc)
    dxh = dy * g
    dx = r * (dxh - xh * jnp.mean(dxh * xh, axis=-1, keepdims=True)) + dres
    return dx, dh, dh * (xh * g), dy * xh


def _rms_bwd_plain(dh, x, g):
    r, xh = _norm_parts(x)
    dxh = dh * g
    return r * (dxh - xh * jnp.mean(dxh * xh, axis=-1, keepdims=True)), dh * xh


def _cat(parts):
    return jnp.concatenate(parts, axis=1)


def _swap_halves(a):
    half = a.shape[-1] // 2
    return jnp.concatenate([a[..., half:], a[..., :half]], axis=-1)


def _adamw_fn(w, g, m, v):
    m = ADAM_B1 * m + (1.0 - ADAM_B1) * g
    v = ADAM_B2 * v + (1.0 - ADAM_B2) * jnp.square(g)
    m_hat = m / (1.0 - ADAM_B1 ** ADAM_STEP)
    v_hat = v / (1.0 - ADAM_B2 ** ADAM_STEP)
    delta = -ADAM_LR * (m_hat / (jnp.sqrt(v_hat) + ADAM_EPS) + ADAM_WD * w)
    return delta, m, v


def _adamw(name, w, g, m, v):
    shape = w.shape
    width = shape[-1]
    flat = [t.reshape(-1, width) for t in (w, g, m, v)]
    res = _rowwise(name, _adamw_fn, flat, [], [(width, F32)] * 3)
    return [t.reshape(shape) for t in res]


def _sum_adamw(name, land, w, m, v):
    shape = w.shape
    width = shape[-1]
    rows = w.size // width

    def fn(*blocks):
        g = blocks[0].astype(F32)
        for b in blocks[1:NDEV]:
            g = g + b.astype(F32)
        return (g,) + _adamw_fn(blocks[NDEV], g, blocks[NDEV + 1], blocks[NDEV + 2])

    def fn_whole(wb, mb, vb, lb):
        return fn(*[lb[i] for i in range(NDEV)], wb, mb, vb)

    flat = [t.reshape(rows, width) for t in (w, m, v)]
    if rows % 16 == 0:
        views = [(land.reshape(NDEV * rows, width), width, 0, i * rows) for i in range(NDEV)]
        res = _rowwise(name, fn, views + flat, [], [(width, F32)] * 4)
    else:
        res = _rowwise(name, fn_whole, flat, [land.reshape(NDEV, rows, width)], [(width, F32)] * 4)
    return [t.reshape(shape) for t in res]


def kernel(x, c, positions, w_ada, b_ada, g_mix_norm, w_in, g_q_lat, w_q_up, g_kv_lat, w_kv_up, w_sb_out, w_mla_out, w_mix_out, g_mlp_norm, w_up, w_down, g_final, loss_target, m_w_ada, m_b_ada, m_g_mix_norm, m_w_in, m_g_q_lat, m_w_q_up, m_g_kv_lat, m_w_kv_up, m_w_sb_out, m_w_mla_out, m_w_mix_out, m_g_mlp_norm, m_w_up, m_w_down, m_g_final, v_w_ada, v_b_ada, v_g_mix_norm, v_w_in, v_g_q_lat, v_w_q_up, v_g_kv_lat, v_w_kv_up, v_w_sb_out, v_w_mla_out, v_w_mix_out, v_g_mlp_norm, v_w_up, v_w_down, v_g_final):
    seq, d = x.shape[1], x.shape[2]
    depth = w_ada.shape[0]
    qr, kvr = g_q_lat.shape[1], g_kv_lat.shape[1]
    sbw, mlaw = w_sb_out.shape[1], w_mla_out.shape[1]
    nh = mlaw // HEAD_DIM
    nhp_sb = sbw // LANES
    dff = w_up.shape[2] * NDEV
    ada_n = w_ada.shape[2]
    gb = min(512, d)
    tq, tk = min(256, seq), min(256, seq)
    me = 4 * lax.axis_index("x") + 2 * lax.axis_index("y") + lax.axis_index("c")

    o_qlat = _roundup(3 * sbw, qr)
    o_kvlat = _roundup(o_qlat + qr, kvr)
    o_rope = _roundup(o_kvlat + kvr, 2 * LANES)
    o_gate = _roundup(o_rope + 2 * LANES, gb)
    wp = o_gate + 2 * d

    c_all = _exchange("ag_c", _Comm("gather_all", [c.reshape(d // LANES, LANES)]))[0].reshape(NDEV, d)
    c_act = _rowwise("silu_c", lambda t: (t * (1.0 / (1.0 + jnp.exp(-t))),), [c_all], [], [(d, F32)])[0]
    parts = jnp.stack([_matmul("ada_fwd", c_act, w_ada[l], "nn") for l in range(depth)])
    parts_all = _exchange("ag_mod", _Comm("gather_all", [parts]))[0]
    mine = jnp.transpose(lax.dynamic_index_in_dim(parts_all, me, axis=2, keepdims=False), (1, 0, 2))
    mod = _rowwise("mod_bias", lambda a, b: (a + b,), [mine.reshape(depth, NDEV * ada_n), b_ada], [],
                   [(6 * d, F32)])[0]
    mods = [[mod[l:l + 1, i * d:(i + 1) * d] for i in range(6)] for l in range(depth)]

    big = [w_in, w_q_up, w_kv_up, w_sb_out, w_mla_out, w_mix_out, w_up, w_down]
    row_sharded = [False, False, False, False, False, True, False, True]
    shards = [[w[l].astype(BF16) for w in big] for l in range(depth)]

    ids_a, ids_b = [0, 1, 2, 3, 4, 5], [6, 7]
    pick = lambda l, ids: [shards[l][i] for i in ids]

    def unpack(gathered, ids):
        out = []
        for g, i in zip(gathered, ids):
            _, rows, cols = g.shape
            if i == 0:
                out.append(g)
            elif row_sharded[i]:
                out.append(g.reshape(NDEV * rows, cols))
            else:
                out.append(jnp.transpose(g, (1, 0, 2)).reshape(rows, NDEV * cols))
        return out

    n_in = w_in.shape[2]
    r0 = 3 * sbw + qr + kvr
    g0 = r0 + ROPE_DIM
    runs = [(0, 3 * sbw, 0), (3 * sbw, 3 * sbw + qr, o_qlat), (3 * sbw + qr, r0, o_kvlat), (g0, g0 + 2 * d, o_gate)]

    def shard_cols(g, a, b):
        return [g[j][:, max(a, n_in * j) - n_in * j:min(b, n_in * (j + 1)) - n_in * j]
                for j in range(a // n_in, (b - 1) // n_in + 1)]

    def derive(full):
        wi, wq, wkv, wsb, wmla, wmix = full
        dt = wi.dtype
        z = lambda r, n: jnp.zeros((r, n), dt)
        kr = _cat(shard_cols(wi, r0, g0))
        pieces, at = [], 0
        for a, b, start in runs[:3]:
            pieces += [z(d, start - at)] + shard_cols(wi, a, b)
            at = start + b - a
        pieces += [z(d, o_rope - at), z(d, HEAD_DIM), kr, z(d, LANES - MLA_QK_DIM),
                   z(d, HEAD_DIM), _swap_halves(kr), z(d, LANES - MLA_QK_DIM),
                   z(d, o_gate - o_rope - 2 * LANES)] + shard_cols(wi, g0, g0 + 2 * d)
        w_in_pad = _cat([t for t in pieces if t.shape[1]])
        wq3 = wq.reshape(qr, nh, MLA_QK_DIM)
        z3 = lambda n: jnp.zeros((qr, nh, n), dt)
        rope_w = wq3[:, :, HEAD_DIM:]
        wq_a = jnp.concatenate([wq3[:, :, :HEAD_DIM], rope_w, z3(LANES - MLA_QK_DIM)], axis=2).reshape(qr, nh * LANES)
        wq_b = jnp.concatenate([z3(HEAD_DIM), _swap_halves(rope_w), z3(LANES - MLA_QK_DIM)], axis=2).reshape(qr, nh * LANES)
        wkv3 = wkv.reshape(kvr, nh, 2 * HEAD_DIM)
        wk = jnp.concatenate([wkv3[:, :, :HEAD_DIM], jnp.zeros((kvr, nh, HEAD_DIM), dt)], axis=2).reshape(kvr, nh * LANES)
        wv = wkv3[:, :, HEAD_DIM:].reshape(kvr, nh * HEAD_DIM)
        return dict(w_in=w_in_pad, w_q=_cat([wq_a, wq_b]), w_kv=_cat([wk, wv]), w_sb=wsb, w_mla=wmla, w_mix=wmix)

    def fold(gr):
        gi, gq, gkv = gr["w_in"], gr["w_q"], gr["w_kv"]
        ra = gi[:, o_rope + HEAD_DIM:o_rope + MLA_QK_DIM]
        rb = gi[:, o_rope + LANES + HEAD_DIM:o_rope + LANES + MLA_QK_DIM]
        rope = ra + _swap_halves(rb)

        def cols(a, b):
            out = []
            for s0, s1, start in runs[:3] + [(r0, g0, None)] + runs[3:]:
                lo, hi = max(a, s0), min(b, s1)
                if lo < hi:
                    out.append(rope[:, lo - r0:hi - r0] if start is None else gi[:, start + lo - s0:start + hi - s0])
            return out

        g_in = jnp.stack([_cat(cols(n_in * j, n_in * (j + 1))) for j in range(NDEV)]).astype(BF16)
        ga = gq[:, :nh * LANES].reshape(qr, nh, LANES)
        gb_ = gq[:, nh * LANES:].reshape(qr, nh, LANES)
        g_q = jnp.concatenate([ga[:, :, :HEAD_DIM], ga[:, :, HEAD_DIM:MLA_QK_DIM]
                               + _swap_halves(gb_[:, :, HEAD_DIM:MLA_QK_DIM])], axis=2).reshape(qr, nh * MLA_QK_DIM)
        gk = gkv[:, :nh * LANES].reshape(kvr, nh, LANES)[:, :, :HEAD_DIM]
        gv = gkv[:, nh * LANES:].reshape(kvr, nh, HEAD_DIM)
        g_kv = jnp.concatenate([gk, gv], axis=2).reshape(kvr, nh * 2 * HEAD_DIM)
        return [g_in, g_q, g_kv, gr["w_sb"], gr["w_mla"], gr["w_mix"], gr["w_up"], gr["w_down"]]

    part_a = _exchange("ag_w0_own", _Comm("gather_own", pick(0, ids_a)))
    ready_a = _exchange("ag_w0_fwd", _Comm("gather_fwd", [], lands=part_a))
    part_b, weights = None, []

    inv_freq = 1.0 / (ROPE_THETA ** (jnp.arange(0, ROPE_DIM, 2, dtype=F32) / ROPE_DIM))
    ang = positions[0].astype(F32)[:, None] * inv_freq
    cos, sin = jnp.cos(ang), jnp.sin(ang)
    tail = jnp.zeros((seq, LANES - MLA_QK_DIM), F32)
    rope_c = _cat([jnp.ones((seq, HEAD_DIM), F32), cos, cos, tail])
    rope_s = _cat([jnp.zeros((seq, HEAD_DIM), F32), -sin, sin, tail])
    zero_vec = lambda n: jnp.zeros((1, n), F32)

    def rope_fwd(q2, kvs, pd, tc, ts):
        c8, s8 = _cat([tc] * nh), _cat([ts] * nh)
        qf = q2[:, :nh * LANES] * c8 + q2[:, nh * LANES:] * s8
        kpe = pd[:, :LANES] * tc + pd[:, LANES:] * ts
        return qf, kvs[:, :nh * LANES] + _cat([kpe] * nh), kvs[:, nh * LANES:]

    def rope_bwd(dq, dk, dv, tc, ts):
        c8, s8 = _cat([tc] * nh), _cat([ts] * nh)
        dks = dk[:, :LANES]
        for h in range(1, nh):
            dks = dks + dk[:, h * LANES:(h + 1) * LANES]
        return _cat([dq * c8, dq * s8]), _cat([dk, dv]), _cat([dks * tc, dks * ts])

    def merge_fwd(*a):
        ng = d // gb
        gs, gm, osb, omla = _cat(a[:ng]), _cat(a[ng:2 * ng]), a[2 * ng], a[2 * ng + 1]
        return (osb / (1.0 + jnp.exp(-gs)) + omla / (1.0 + jnp.exp(-gm)),)

    def merge_bwd(*a):
        ng = d // gb
        gs, gm, osb, omla, dm = _cat(a[:ng]), _cat(a[ng:2 * ng]), a[2 * ng], a[2 * ng + 1], a[2 * ng + 2]
        ss, sm = 1.0 / (1.0 + jnp.exp(-gs)), 1.0 / (1.0 + jnp.exp(-gm))
        return ss * dm, sm * dm, _cat([dm * osb * ss * (1.0 - ss), dm * omla * sm * (1.0 - sm)])

    def gate_cols(p):
        ng = d // gb
        return [(p, gb, o_gate // gb + i, 0) for i in range(2 * ng)]

    xs = x[0]
    saved = []
    for l in range(depth):
        w = derive(unpack(ready_a, ids_a))
        sh1, sc1, g1, sh2, sc2, g2 = mods[l]
        h1 = _rowwise("norm1", _rmsmod_fwd, [xs], [g_mix_norm[l:l + 1], sc1, sh1], [(d, BF16)])[0]
        p = _matmul("in_proj", h1, w["w_in"], "nn")
        (o_sb, tot_sb), part_b = _sb_fwd("sb_fwd", p, nhp_sb, tq, tk, _Comm("gather_own", pick(l, ids_b)))
        y_sb = _matmul("sb_out", o_sb, w["w_sb"], "nn")
        qn = _rowwise("norm_q", _rmsmod_fwd, [(p, qr, o_qlat // qr, 0)],
                      [g_q_lat[l:l + 1], zero_vec(qr), zero_vec(qr)], [(qr, BF16)])[0]
        kvn = _rowwise("norm_kv", _rmsmod_fwd, [(p, kvr, o_kvlat // kvr, 0)],
                       [g_kv_lat[l:l + 1], zero_vec(kvr), zero_vec(kvr)], [(kvr, BF16)])[0]
        q2 = _matmul("q_up", qn, w["w_q"], "nn")
        kvs = _matmul("kv_up", kvn, w["w_kv"], "nn")
        qf, kf, vf = _rowwise("rope_fwd", rope_fwd, [q2, kvs, (p, 2 * LANES, o_rope // (2 * LANES), 0), rope_c, rope_s],
                              [], [(nh * LANES, BF16), (nh * LANES, BF16), (mlaw, BF16)])
        comms = [_Comm("gather_fwd", [], lands=part_b)]
        if l + 1 < depth:
            comms.append(_Comm("gather_own", pick(l + 1, ids_a)))
        group = _CommGroup(comms)
        (o_mla, lse), got = _mla_fwd("mla_fwd", qf, kf, vf, tq, tk, group)
        got = group.split(got)
        w["w_up"], w["w_down"] = unpack(got[0], ids_b)
        weights.append(w)
        y_mla = _matmul("mla_out", o_mla, w["w_mla"], "nn")
        merged = _rowwise("merge_fwd", merge_fwd, gate_cols(p) + [y_sb, y_mla], [], [(d, BF16)])[0]
        resid = lambda acc, xv, g: (acc, xv + g * acc)
        y1, x_mid = _matmul("mix_out", merged, w["w_mix"], "nn", epilogue=resid, rows=[xs], vecs=[g1], outs=[F32, F32])
        h2 = _rowwise("norm2", _rmsmod_fwd, [x_mid], [g_mlp_norm[l:l + 1], sc2, sh2], [(d, BF16)])[0]
        relu2 = lambda acc: (acc, jnp.square(jnp.maximum(acc, 0.0)))
        if l + 1 < depth:
            (u, act), ready_a = _matmul("mlp_up", h2, w["w_up"], "nn", outs=[F32, BF16], epilogue=relu2,
                                        comm=_Comm("gather_fwd", [], lands=got[1]))
        else:
            u, act = _matmul("mlp_up", h2, w["w_up"], "nn", outs=[F32, BF16], epilogue=relu2)
        y2, x_out = _matmul("mlp_down", act, w["w_down"], "nn", epilogue=resid, rows=[x_mid], vecs=[g2], outs=[F32, F32])
        saved.append(dict(x=xs, h1=h1, p=p, o_sb=o_sb, tot_sb=tot_sb, y_sb=y_sb, qn=qn, kvn=kvn, qf=qf, kf=kf, vf=vf, o_mla=o_mla,
                          lse=lse, y_mla=y_mla, merged=merged, y1=y1, x_mid=x_mid, h2=h2, u=u, act=act, y2=y2))
        xs = x_out

    def final_fn(xv, tv, g):
        r, xh = _norm_parts(xv)
        diff = xh * g - tv
        dy = diff * (1.0 / d)
        dxh = dy * g
        dx = r * (dxh - xh * jnp.mean(dxh * xh, axis=-1, keepdims=True))
        return dx, diff * diff, dy * xh

    dx, sq, dg_final = _rowwise("loss_head", final_fn, [xs, loss_target[0]], [g_final.reshape(1, d)],
                                [(d, F32)], reds=[d, d])
    loss = lax.psum(0.5 * jnp.sum(sq) / d, ("x", "y", "c"))

    def chunk(gfull, wref, by_rows):
        rows, cols = wref.shape[1], wref.shape[2]
        if by_rows:
            return gfull.reshape(NDEV, rows, cols).astype(BF16)
        return jnp.transpose(gfull.reshape(rows, NDEV, cols), (1, 0, 2)).astype(BF16)

    dmods, small = [None] * depth, [None] * depth
    late, lands = None, [None] * len(big)
    for l in reversed(range(depth)):
        w, sv = weights[l], saved[l]
        sh1, sc1, g1, sh2, sc2, g2 = mods[l]
        gr = {}
        dy2, dgate2 = _rowwise("gate2_bwd", lambda dxv, y, g: (dxv * g, dxv * y), [dx, sv["y2"]], [g2],
                               [(d, BF16)], reds=[d])
        du = _matmul("mlp_down_dx", dy2, w["w_down"], "nt", outs=[BF16], rows=[sv["u"]],
                     epilogue=lambda acc, uv: (acc * 2.0 * jnp.maximum(uv, 0.0),))
        gr["w_down"] = _matmul("mlp_down_dw", sv["act"], dy2, "tn", BF16)
        dh2 = _matmul("mlp_up_dx", du, w["w_up"], "nt")
        gr["w_up"] = _matmul("mlp_up_dw", sv["h2"], du, "tn", BF16)
        dx_mid, dsh2, dsc2, dg_mlp = _rowwise("norm2_bwd", _rmsmod_bwd, [dh2, sv["x_mid"], dx],
                                              [g_mlp_norm[l:l + 1], sc2], [(d, F32)], reds=[d, d, d])
        dy1, dgate1 = _rowwise("gate1_bwd", lambda dxv, y, g: (dxv * g, dxv * y), [dx_mid, sv["y1"]], [g1],
                               [(d, BF16)], reds=[d])
        dmerged = _matmul("mix_out_dx", dy1, w["w_mix"], "nt")
        gr["w_mix"] = _matmul("mix_out_dw", sv["merged"], dy1, "tn", BF16)
        dy_sb, dy_mla, dgates = _rowwise("merge_bwd", merge_bwd, gate_cols(sv["p"]) + [sv["y_sb"], sv["y_mla"], dmerged],
                                         [], [(d, BF16), (d, BF16), (2 * d, BF16)])
        do_sb = _matmul("sb_out_dx", dy_sb, w["w_sb"], "nt")
        gr["w_sb"] = _matmul("sb_out_dw", sv["o_sb"], dy_sb, "tn", BF16)
        do_mla = _matmul("mla_out_dx", dy_mla, w["w_mla"], "nt")
        gr["w_mla"] = _matmul("mla_out_dw", sv["o_mla"], dy_mla, "tn", BF16)
        ready = {3: gr["w_sb"], 4: gr["w_mla"], 5: gr["w_mix"], 6: gr["w_up"], 7: gr["w_down"]}
        ready = {i: chunk(g, big[i], row_sharded[i]) for i, g in ready.items()}
        ids_a = [6] + ([0, 1, 2] if late is not None else [])
        comm_a = _Comm("scatter", [ready[6]] + (late or []), [lands[i] for i in ids_a], [l] + [l + 1] * 3, depth)
        comm_b = _Comm("scatter", [ready[7]], [lands[7]], [l], depth)
        (dq_sb, dk_sb, dv_sb), got_a = _sb_bwd("sb_bwd", sv["p"], sv["tot_sb"], do_sb, nhp_sb, tq, tk, comm_a)
        (dqf, dkf, dvf), got_b = _mla_bwd("mla_bwd", sv["qf"], sv["kf"], sv["vf"], sv["o_mla"], sv["lse"], do_mla,
                                          tq, tk, comm_b)
        for i, t in zip(ids_a + [7], list(got_a) + list(got_b)):
            lands[i] = t
        dq2, dkvs, drope = _rowwise("rope_bwd", rope_bwd, [dqf, dkf, dvf, rope_c, rope_s], [],
                                    [(2 * nh * LANES, BF16), (nh * LANES + mlaw, BF16), (2 * LANES, BF16)])
        dqn = _matmul("q_up_dx", dq2, w["w_q"], "nt")
        gr["w_q"] = _matmul("q_up_dw", sv["qn"], dq2, "tn")
        dkvn = _matmul("kv_up_dx", dkvs, w["w_kv"], "nt")
        gr["w_kv"] = _matmul("kv_up_dw", sv["kvn"], dkvs, "tn")
        dqlat, dg_q = _rowwise("norm_q_bwd", _rms_bwd_plain, [dqn, (sv["p"], qr, o_qlat // qr, 0)],
                               [g_q_lat[l:l + 1]], [(qr, BF16)], reds=[qr])
        dkvlat, dg_kv = _rowwise("norm_kv_bwd", _rms_bwd_plain, [dkvn, (sv["p"], kvr, o_kvlat // kvr, 0)],
                                 [g_kv_lat[l:l + 1]], [(kvr, BF16)], reds=[kvr])
        zb = lambda n: jnp.zeros((seq, n), BF16)
        dp = _cat([dq_sb.astype(BF16), dk_sb.astype(BF16), dv_sb.astype(BF16), zb(o_qlat - 3 * sbw), dqlat,
                   zb(o_kvlat - o_qlat - qr), dkvlat, zb(o_rope - o_kvlat - kvr), drope,
                   zb(o_gate - o_rope - 2 * LANES), dgates])
        dh1, got = _matmul("in_proj_dx", dp, w["w_in"], "nt", comm=_Comm("scatter", [ready[5]], [lands[5]], [l], depth))
        lands[5] = got[0]
        gr["w_in"], got = _matmul("in_proj_dw", sv["h1"], dp, "tn",
                                  comm=_Comm("scatter", [ready[3], ready[4]], [lands[3], lands[4]], [l, l], depth))
        lands[3], lands[4] = got
        dx, dsh1, dsc1, dg_mix = _rowwise("norm1_bwd", _rmsmod_bwd, [dh1, sv["x"], dx_mid],
                                          [g_mix_norm[l:l + 1], sc1], [(d, F32)], reds=[d, d, d])
        dmods[l] = _cat([dsh1, dsc1, dgate1, dsh2, dsc2, dgate2])
        small[l] = (dg_mix, dg_q, dg_kv, dg_mlp)
        g_in, g_q, g_kv = fold(gr)[:3]
        late = [g_in, chunk(g_q, big[1], False), chunk(g_kv, big[2], False)]

    small_parts = [jnp.concatenate(dmods, axis=0)]
    small_parts += [jnp.concatenate([small[l][i] for l in range(depth)], axis=0) for i in range(4)]
    small_parts.append(dg_final)
    small_all = _exchange("ag_small", _Comm("gather_all", small_parts))

    dmod_mine = lax.dynamic_slice_in_dim(small_all[0], me * ada_n, ada_n, axis=2)
    c_act_t = jnp.transpose(c_act)

    def outer_fn(ct, dm):
        acc = ct[:, 0:1] * dm[0:1, :]
        for b in range(1, NDEV):
            acc = acc + ct[:, b:b + 1] * dm[b:b + 1, :]
        return (acc,)

    g_w_ada = jnp.stack([_rowwise("ada_dw", outer_fn, [c_act_t], [dmod_mine[:, l, :]], [(ada_n, F32)])[0]
                         for l in range(depth)])

    landed = list(_exchange("a2a_last", _Comm("scatter", late, lands[:3], [0] * 3, depth))) + lands[3:]

    moments = dict(
        w_ada=(w_ada, m_w_ada, v_w_ada), b_ada=(b_ada, m_b_ada, v_b_ada),
        g_mix_norm=(g_mix_norm, m_g_mix_norm, v_g_mix_norm), w_in=(w_in, m_w_in, v_w_in),
        g_q_lat=(g_q_lat, m_g_q_lat, v_g_q_lat), w_q_up=(w_q_up, m_w_q_up, v_w_q_up),
        g_kv_lat=(g_kv_lat, m_g_kv_lat, v_g_kv_lat), w_kv_up=(w_kv_up, m_w_kv_up, v_w_kv_up),
        w_sb_out=(w_sb_out, m_w_sb_out, v_w_sb_out), w_mla_out=(w_mla_out, m_w_mla_out, v_w_mla_out),
        w_mix_out=(w_mix_out, m_w_mix_out, v_w_mix_out), g_mlp_norm=(g_mlp_norm, m_g_mlp_norm, v_g_mlp_norm),
        w_up=(w_up, m_w_up, v_w_up), w_down=(w_down, m_w_down, v_w_down),
        g_final=(g_final.reshape(1, d), m_g_final.reshape(1, d), v_g_final.reshape(1, d)))
    lands = dict(b_ada=small_all[0], g_mix_norm=small_all[1], g_q_lat=small_all[2], g_kv_lat=small_all[3],
                 g_mlp_norm=small_all[4], g_final=small_all[5], w_in=landed[0], w_q_up=landed[1],
                 w_kv_up=landed[2], w_sb_out=landed[3], w_mla_out=landed[4], w_mix_out=landed[5], w_up=landed[6],
                 w_down=landed[7])
    gs, deltas, new_ms, new_vs = [], [], [], []
    for name, (wt, mt, vt) in moments.items():
        if name == "w_ada":
            res = [g_w_ada] + _adamw("adamw_" + name, wt, g_w_ada, mt, vt)
        else:
            res = _sum_adamw("adamw_" + name, lands[name], wt, mt, vt)
        if name == "g_final":
            res = [t.reshape(d) for t in res]
        for lst, t in zip((gs, deltas, new_ms, new_vs), res):
            lst.append(t)

    return (loss, dx[None], *gs, *deltas, *new_ms, *new_vs)
```

```python
import functools

import jax
import jax.numpy as jnp
from jax import lax
from jax.experimental import pallas as pl
from jax.experimental.pallas import tpu as pltpu

F32 = jnp.float32
BF16 = jnp.bfloat16
NDEV = 8
LANES = 128
HEAD_DIM = 64
ROPE_DIM = 32
MLA_QK_DIM = HEAD_DIM + ROPE_DIM
ROPE_THETA = 10000.0
NORM_EPS = 1e-6
ADAM_LR = 0.001
ADAM_B1 = 0.9
ADAM_B2 = 0.999
ADAM_EPS = 1e-08
ADAM_WD = 0.01
ADAM_STEP = 10
VMEM_LIMIT = 48 * 1024 * 1024


def _pcall(body, **kw):
    return pl.pallas_call(body, **kw)


def _tile(n, pref):
    for t in (512, 384, 256, 128, 64, 32, 16, 8):
        if t <= pref and n % t == 0:
            return t
    return n


def _roundup(n, m):
    return (n + m - 1) // m * m


_CP = pltpu.CompilerParams(vmem_limit_bytes=VMEM_LIMIT)


def _rowwise(name, fn, rows, vecs, outs, reds=(), tb=256):
    rows = [r if isinstance(r, tuple) else (r, r.shape[1], 0, 0) for r in rows]
    nrows = None
    for arr, width, col, roff in rows:
        if roff == 0 and nrows is None:
            nrows = arr.shape[0]
    first_off = [r for r in rows if r[3] != 0]
    if first_off:
        nrows = min(nrows, first_off[0][3])
    tb = _tile(nrows, tb)
    nblk = nrows // tb
    n_in = len(rows) + len(vecs)
    n_out = len(outs)

    def body(*refs):
        vals = [r[...] for r in refs[:n_in]]
        res = fn(*vals)
        if not isinstance(res, (tuple, list)):
            res = (res,)
        for ref, val in zip(refs[n_in:n_in + n_out], res[:n_out]):
            ref[...] = val.astype(ref.dtype)
        for ref, val in zip(refs[n_in + n_out:], res[n_out:]):
            @pl.when(pl.program_id(0) == 0)
            def _(ref=ref):
                ref[...] = jnp.zeros_like(ref)
            ref[...] += jnp.sum(val.astype(F32), axis=0, keepdims=True)

    in_specs = []
    for arr, width, col, roff in rows:
        in_specs.append(pl.BlockSpec((tb, width), functools.partial(
            lambda i, col, rb: (rb + i, col), col=col, rb=roff // tb)))
    for v in vecs:
        in_specs.append(pl.BlockSpec(v.shape, lambda i, nd=v.ndim: (0,) * nd))
    out_specs = [pl.BlockSpec((tb, w), lambda i: (i, 0)) for w, _ in outs]
    out_specs += [pl.BlockSpec((1, w), lambda i: (0, 0)) for w in reds]
    out_shape = [jax.ShapeDtypeStruct((nrows, w), dt) for w, dt in outs]
    out_shape += [jax.ShapeDtypeStruct((1, w), F32) for w in reds]
    res = _pcall(body, name=name, grid=(nblk,), in_specs=in_specs, out_specs=out_specs,
                 out_shape=out_shape, compiler_params=_CP)(*[r[0] for r in rows], *vecs)
    return res


_DIMS = {"nn": (((1,), (0,)), ((), ())), "nt": (((1,), (1,)), ((), ())), "tn": (((0,), (0,)), ((), ()))}


def _matmul(name, a, b, mode, out_dtype=F32, epilogue=None, rows=(), vecs=(), outs=None, comm=None):
    if mode == "nn":
        (m, k), n = a.shape, b.shape[1]
    elif mode == "nt":
        (m, k), n = a.shape, b.shape[0]
    else:
        (k, m), n = a.shape, b.shape[1]
    tm = _tile(m, 512)
    tn = next((t for t in (1536, 1024) if n % t == 0 and n > t), _tile(n, 512))
    dims = _DIMS[mode]
    outs = [out_dtype] if outs is None else outs
    n_extra = len(rows) + len(vecs)
    rows = [r if isinstance(r, tuple) else (r, 0) for r in rows]

    def body(a_ref, b_ref, *refs):
        acc = lax.dot_general(a_ref[...].astype(BF16), b_ref[...].astype(BF16), dims, preferred_element_type=F32)
        res = (acc,) if epilogue is None else epilogue(acc, *[r[...] for r in refs[:n_extra]])
        for o_ref, val in zip(refs[n_extra:], res):
            o_ref[...] = val.astype(o_ref.dtype)

    a_spec = pl.BlockSpec((k, tm), lambda j, i: (0, i)) if mode == "tn" else pl.BlockSpec((tm, k), lambda j, i: (i, 0))
    b_spec = pl.BlockSpec((tn, k), lambda j, i: (j, 0)) if mode == "nt" else pl.BlockSpec((k, tn), lambda j, i: (0, j))
    blk = pl.BlockSpec((tm, tn), lambda j, i: (i, j))
    assert all(off % tn == 0 for _, off in rows), (name, tn)
    row_specs = [pl.BlockSpec((tm, tn), functools.partial(lambda j, i, first: (i, first + j), first=off // tn))
                 for _, off in rows]
    res, got = _hosted(name, body, (n // tn, m // tm), [a, b, *[r for r, _ in rows], *vecs],
                       [a_spec, b_spec] + row_specs + [pl.BlockSpec((1, tn), lambda j, i: (0, j))] * len(vecs),
                       [jax.ShapeDtypeStruct((m, n), dt) for dt in outs], [blk] * len(outs), comm)
    res = res[0] if len(outs) == 1 else res
    return res if comm is None else (res, got)


class _Comm:
    KS = {"gather_all": (1, 2, 3, 4, 5, 6, 7), "gather_own": (1, 2, 4, 6), "gather_fwd": (2, 4, 6),
          "scatter": (1, 2, 3, 4, 5, 6, 7)}

    def __init__(self, kind, srcs, lands=None, layers=None, depth=None):
        self.kind, self.srcs, self.layers = kind, list(srcs), layers
        self.n = len(lands) if kind == "gather_fwd" else len(srcs)
        self.lands = list(lands) if lands is not None else [None] * self.n
        self.out_shapes = []
        for i, land in enumerate(self.lands):
            if land is not None:
                self.out_shapes.append(jax.ShapeDtypeStruct(land.shape, land.dtype))
            elif kind == "scatter":
                self.out_shapes.append(jax.ShapeDtypeStruct((NDEV, depth) + srcs[i].shape[1:], srcs[i].dtype))
            else:
                self.out_shapes.append(jax.ShapeDtypeStruct((NDEV,) + srcs[i].shape, srcs[i].dtype))
        self.operands = self.srcs + [t for t in self.lands if t is not None]
        self.scratch = [pltpu.SemaphoreType.DMA((NDEV - 1, self.n)), pltpu.SemaphoreType.DMA((NDEV - 1, self.n)),
                        pltpu.SemaphoreType.DMA((self.n,))]

    def aliases(self, first_in, first_out):
        given = [i for i, t in enumerate(self.lands) if t is not None]
        return {first_in + len(self.srcs) + pos: first_out + i for pos, i in enumerate(given)}

    def copies(self, in_refs, out_refs, send_sems, recv_sems, local_sems):
        x, y, c = lax.axis_index("x"), lax.axis_index("y"), lax.axis_index("c")
        me = 4 * x + 2 * y + c
        cps = []
        if self.kind != "gather_fwd":
            for i in range(self.n):
                src = in_refs[i].at[me] if self.kind == "scatter" else in_refs[i]
                dst = out_refs[i].at[me, self.layers[i]] if self.kind == "scatter" else out_refs[i].at[me]
                cps.append(pltpu.make_async_copy(src, dst, local_sems.at[i]))
        for k in self.KS[self.kind]:
            px = 1 - x if k & 4 else x
            py = 1 - y if k & 2 else y
            pc = 1 - c if k & 1 else c
            peer = 4 * px + 2 * py + pc
            for i in range(self.n):
                if self.kind == "gather_fwd":
                    src, dst, to = in_refs[i].at[peer], out_refs[i].at[peer], (x, y, 1 - c)
                elif self.kind == "scatter":
                    src, dst, to = in_refs[i].at[peer], out_refs[i].at[me, self.layers[i]], (px, py, pc)
                else:
                    src, dst, to = in_refs[i], out_refs[i].at[me], (px, py, pc)
                cps.append(pltpu.make_async_remote_copy(
                    src_ref=src, dst_ref=dst, send_sem=send_sems.at[k - 1, i], recv_sem=recv_sems.at[k - 1, i],
                    device_id=to, device_id_type=pl.DeviceIdType.MESH))
        return cps


class _CommGroup:
    def __init__(self, comms):
        self.comms = comms
        self.n = sum(cm.n for cm in comms)
        self.operands = [t for cm in comms for t in cm.operands]
        self.out_shapes = [t for cm in comms for t in cm.out_shapes]
        self.scratch = [t for cm in comms for t in cm.scratch]

    def aliases(self, first_in, first_out):
        out = {}
        for cm in self.comms:
            out.update(cm.aliases(first_in, first_out))
            first_in, first_out = first_in + len(cm.operands), first_out + cm.n
        return out

    def copies(self, in_refs, out_refs, *sems):
        cps, i, o = [], 0, 0
        for j, cm in enumerate(self.comms):
            cps += cm.copies(in_refs[i:i + len(cm.operands)], out_refs[o:o + cm.n], *sems[3 * j:3 * j + 3])
            i, o = i + len(cm.operands), o + cm.n
        return cps

    def split(self, outs):
        res, o = [], 0
        for cm in self.comms:
            res.append(list(outs[o:o + cm.n]))
            o += cm.n
        return res


_ANY = pl.BlockSpec(memory_space=pl.ANY)


def _exchange(name, comm):
    nci = len(comm.operands)

    def body(*refs):
        cps = comm.copies(refs[:nci], refs[nci:nci + comm.n], *refs[nci + comm.n:])
        for cp in cps:
            cp.start()
        for cp in cps:
            cp.wait()

    return _pcall(body, name=name, in_specs=[_ANY] * nci, out_specs=[_ANY] * comm.n, out_shape=comm.out_shapes,
                  scratch_shapes=comm.scratch, input_output_aliases=comm.aliases(0, 0))(*comm.operands)


def _hosted(name, body, grid, arrays, in_specs, out_shapes, out_specs, comm):
    if comm is None:
        return _pcall(body, name=name, grid=grid, in_specs=in_specs, out_specs=out_specs, out_shape=out_shapes,
                      compiler_params=_CP)(*arrays), []
    ni, no, nci = len(arrays), len(out_shapes), len(comm.operands)

    def full(*refs):
        ins, cin = refs[:ni], refs[ni:ni + nci]
        outs = refs[ni + nci:ni + nci + no]
        cout = refs[ni + nci + no:ni + nci + no + comm.n]
        sems = refs[ni + nci + no + comm.n:]
        first = functools.reduce(jnp.logical_and, [pl.program_id(a) == 0 for a in range(len(grid))])
        last = functools.reduce(jnp.logical_and, [pl.program_id(a) == grid[a] - 1 for a in range(len(grid))])

        @pl.when(first)
        def _():
            for cp in comm.copies(cin, cout, *sems):
                cp.start()

        body(*ins, *outs)

        @pl.when(last)
        def _():
            for cp in comm.copies(cin, cout, *sems):
                cp.wait()

    res = _pcall(full, name=name, grid=grid, in_specs=list(in_specs) + [_ANY] * nci,
                 out_specs=list(out_specs) + [_ANY] * comm.n, out_shape=list(out_shapes) + comm.out_shapes,
                 scratch_shapes=comm.scratch, input_output_aliases=comm.aliases(ni, no),
                 compiler_params=_CP)(*arrays, *comm.operands)
    return res[:no], res[no:]


def _dot_nt(a, b):
    return lax.dot_general(a, b, _DIMS["nt"], preferred_element_type=F32)


def _dot_tn(a, b):
    return lax.dot_general(a, b, _DIMS["tn"], preferred_element_type=F32)


def _dot_nn(a, b):
    return jnp.dot(a, b, preferred_element_type=F32)


def _tri(tk, rel):
    j = lax.broadcasted_iota(jnp.int32, (tk, tk), 0)
    s = lax.broadcasted_iota(jnp.int32, (tk, tk), 1)
    return {"after": j > s, "upto": j <= s, "before": j < s}[rel].astype(BF16)


def _pairs_per_step(nhp):
    return 2 if nhp % 2 == 0 else 1


def _pair(a, pr):
    return a[:, pr * LANES:(pr + 1) * LANES]


def _head_masks():
    lane = lax.broadcasted_iota(jnp.int32, (1, LANES), 1)
    return [(lane // HEAD_DIM) == h for h in range(2)]


ROW_CHUNK = 32


def _by_rows(fn, n_out, *arrays):
    rows = arrays[0].shape[0]
    step = min(ROW_CHUNK, rows)
    outs = [[] for _ in range(n_out)]
    for r in range(0, rows, step):
        for o, val in zip(outs, fn(r, *[a[r:r + step] for a in arrays])):
            o.append(val)
    return [jnp.concatenate(o, axis=0) for o in outs]


def _causal(r0, k0, rows, tk, strict):
    row = lax.broadcasted_iota(jnp.int32, (rows, tk), 0) + r0
    col = lax.broadcasted_iota(jnp.int32, (rows, tk), 1) + k0
    return col < row if strict else col <= row


def _wide(stat, width):
    return stat if width == LANES else jnp.concatenate([stat] * (width // LANES), axis=1)


def _row_sum(v):
    return jnp.broadcast_to(jnp.sum(v, axis=1, keepdims=True), (v.shape[0], LANES))


def _split_bf16(v):
    hi = v.astype(BF16)
    return hi, (v - hi.astype(F32)).astype(BF16)


def _sb_logs(z, scale, mask):
    z = z * scale
    e = jnp.exp(-jnp.abs(z))
    log_sig = jnp.minimum(z, 0.0) - jnp.log(1.0 + e)
    log_fail = log_sig - z
    return z, log_sig, (log_fail if mask is None else jnp.where(mask, log_fail, 0.0))


def _two_loops(n_full, nkb, near_first, step, carry):
    if near_first:
        carry = lax.fori_loop(0, nkb - n_full, lambda j, c: step(nkb - 1 - j, True, c), carry)
        return lax.fori_loop(0, n_full, lambda j, c: step(n_full - 1 - j, False, c), carry)
    carry = lax.fori_loop(0, n_full, lambda j, c: step(j, False, c), carry)
    return lax.fori_loop(n_full, nkb, lambda j, c: step(j, True, c), carry)


def _sb_fwd(name, p, nhp, tq, tk, comm=None):
    s = p.shape[0]
    scale = HEAD_DIM ** -0.5
    nq = s // tq
    pp = _pairs_per_step(nhp)
    wide = pp * LANES

    def body(q_ref, k_ref, v_ref, o_ref, tot_ref):
        qi = pl.program_id(1)
        masks = _head_masks()
        after = _tri(tk, "after")
        nkb = ((qi + 1) * tq + tk - 1) // tk
        q = q_ref[...]
        qhs = [jnp.where(hm, _pair(q, pr), 0.0).astype(BF16) for pr in range(pp) for hm in masks]

        def step(kb, masked, carry):
            ks = pl.multiple_of(kb * tk, tk)
            ks_all = k_ref[pl.ds(ks, tk), :].astype(BF16)
            vs_all = v_ref[pl.ds(ks, tk), :].astype(BF16)
            mask_of = lambda r, n: _causal(qi * tq + r, ks, n, tk, True) if masked else None
            heads = range(len(qhs))

            def logs(r, zc):
                _, log_sig, log_fail = _sb_logs(zc, scale, mask_of(r, zc.shape[0]))
                return (log_sig,) + _split_bf16(log_fail) + (_row_sum(log_fail),)

            def weights(r, lsc, runc, laterc):
                w = jnp.exp(lsc + runc + _wide(laterc, tk))
                return ((jnp.where(mask_of(r, w.shape[0]), w, 0.0) if masked else w).astype(BF16),)

            zs = [_dot_nt(qhs[i], _pair(ks_all, i // 2)) for i in heads]
            first = [_by_rows(logs, 4, zs[i]) for i in heads]
            runs = [_dot_nn(first[i][1], after) + _dot_nn(first[i][2], after) for i in heads]
            ws = [_by_rows(weights, 1, first[i][0], runs[i], carry[i][0])[0] for i in heads]
            pvs = [_dot_nn(ws[i], _pair(vs_all, i // 2)) for i in heads]
            return tuple((carry[i][0] + first[i][3], carry[i][1] + pvs[i]) for i in heads)

        init = (jnp.zeros((tq, LANES), F32), jnp.zeros((tq, LANES), F32))
        res = _two_loops((qi * tq) // tk, nkb, True, step, (init,) * (2 * pp))
        for pr in range(pp):
            (tot0, acc0), (tot1, acc1) = res[2 * pr], res[2 * pr + 1]
            o_ref[:, pr * LANES:(pr + 1) * LANES] = jnp.where(masks[0], acc0, acc1)
            tot_ref[:, pr * LANES:(pr + 1) * LANES] = jnp.where(masks[0], tot0, tot1)

    ng = nhp // pp
    blk = pl.BlockSpec((tq, wide), lambda h, i: (i, h))
    shape = jax.ShapeDtypeStruct((s, nhp * LANES), F32)
    return _hosted(name, body, (ng, nq), [p, p, p],
                   [blk, pl.BlockSpec((s, wide), lambda h, i: (0, ng + h)),
                    pl.BlockSpec((s, wide), lambda h, i: (0, 2 * ng + h))], [shape, shape], [blk, blk], comm)


def _sb_bwd(name, p, tot, do, nhp, tq, tk, comm=None):
    s = p.shape[0]
    scale = HEAD_DIM ** -0.5
    nq = s // tq
    pp = _pairs_per_step(nhp)
    wide = pp * LANES

    def body(q_ref, k_ref, v_ref, tot_ref, do_ref, dq_ref, dk_ref, dv_ref):
        qi = pl.program_id(1)

        @pl.when(qi == 0)
        def _():
            dk_ref[...] = jnp.zeros_like(dk_ref)
            dv_ref[...] = jnp.zeros_like(dv_ref)

        masks = _head_masks()
        upto, before = _tri(tk, "upto"), _tri(tk, "before")
        nkb = ((qi + 1) * tq + tk - 1) // tk
        q = q_ref[...]
        qbs = q.astype(BF16)
        dout = do_ref[...]
        doutbs = dout.astype(BF16)
        qhs = [jnp.where(hm, _pair(q, pr), 0.0).astype(BF16) for pr in range(pp) for hm in masks]
        dohs = [jnp.where(hm, _pair(dout, pr), 0.0).astype(BF16) for pr in range(pp) for hm in masks]
        tot = tot_ref[...]
        totals = [jnp.broadcast_to(tot[:, h * HEAD_DIM:h * HEAD_DIM + 1], (tq, LANES)) for h in range(2 * pp)]

        def step(kb, masked, carry):
            ks = pl.multiple_of(kb * tk, tk)
            ks_all = k_ref[pl.ds(ks, tk), :].astype(BF16)
            vs_all = v_ref[pl.ds(ks, tk), :].astype(BF16)
            mask_of = lambda r, n: _causal(qi * tq + r, ks, n, tk, True) if masked else None
            heads = range(len(qhs))

            def logs(r, zc):
                _, log_sig, log_fail = _sb_logs(zc, scale, mask_of(r, zc.shape[0]))
                return (log_sig,) + _split_bf16(log_fail) + (_row_sum(log_fail),)

            def weights(r, lsc, runc, basec, dwc):
                w = jnp.exp(lsc + (_wide(basec, tk) - runc))
                if masked:
                    w = jnp.where(mask_of(r, w.shape[0]), w, 0.0)
                g = w * dwc
                return (w.astype(BF16), g) + _split_bf16(g) + (_row_sum(g),)

            def dscore(r, gc, lsc, zc, grc, gbc):
                dz = gc * jnp.exp(lsc - zc * scale) - jnp.exp(lsc) * (_wide(gbc, tk) + grc)
                if masked:
                    dz = jnp.where(mask_of(r, dz.shape[0]), dz, 0.0)
                return ((dz * scale).astype(BF16),)

            zs = [_dot_nt(qhs[i], _pair(ks_all, i // 2)) for i in heads]
            dws = [_dot_nt(dohs[i], _pair(vs_all, i // 2)) for i in heads]
            first = [_by_rows(logs, 4, zs[i]) for i in heads]
            runs = [_dot_nn(first[i][1], upto) + _dot_nn(first[i][2], upto) for i in heads]
            second = [_by_rows(weights, 5, first[i][0], runs[i], totals[i] - carry[i][0], dws[i])
                      for i in heads]
            g_runs = [_dot_nn(second[i][2], before) + _dot_nn(second[i][3], before) for i in heads]
            dzs = [_by_rows(dscore, 1, second[i][1], first[i][0], zs[i], g_runs[i], carry[i][1])[0] for i in heads]
            dks = [_dot_tn(dzs[i], _pair(qbs, i // 2)) for i in heads]
            dvs = [_dot_tn(second[i][0], _pair(doutbs, i // 2)) for i in heads]
            dqs = [_dot_nn(dzs[i], _pair(ks_all, i // 2)) for i in heads]
            for pr in range(pp):
                cols = slice(pr * LANES, (pr + 1) * LANES)
                dk_ref[pl.ds(ks, tk), cols] += jnp.where(masks[0], dks[2 * pr], dks[2 * pr + 1])
                dv_ref[pl.ds(ks, tk), cols] += jnp.where(masks[0], dvs[2 * pr], dvs[2 * pr + 1])
            return tuple((carry[i][0] + first[i][3], carry[i][1] + second[i][4], carry[i][2] + dqs[i]) for i in heads)

        zero = jnp.zeros((tq, LANES), F32)
        res = _two_loops((qi * tq) // tk, nkb, False, step, ((zero, zero, zero),) * (2 * pp))
        for pr in range(pp):
            dq_ref[:, pr * LANES:(pr + 1) * LANES] = jnp.where(masks[0], res[2 * pr][2], res[2 * pr + 1][2])

    ng = nhp // pp
    blk = pl.BlockSpec((tq, wide), lambda h, i: (i, h))
    full = pl.BlockSpec((s, wide), lambda h, i: (0, h))
    shape = jax.ShapeDtypeStruct((s, nhp * LANES), F32)
    return _hosted(name, body, (ng, nq), [p, p, p, tot, do],
                   [blk, pl.BlockSpec((s, wide), lambda h, i: (0, ng + h)),
                    pl.BlockSpec((s, wide), lambda h, i: (0, 2 * ng + h)), blk, blk],
                   [shape, shape, shape], [blk, full, full], comm)


def _mla_fwd(name, q, k, v, tq, tk, comm=None):
    s = q.shape[0]
    nhp = v.shape[1] // LANES
    scale = MLA_QK_DIM ** -0.5
    nq = s // tq
    pp = _pairs_per_step(nhp)

    def body(q_ref, k_ref, v_ref, o_ref, lse_ref):
        qi = pl.program_id(1)
        masks = _head_masks()
        nkb = ((qi + 1) * tq + tk - 1) // tk
        qhs = [q_ref[:, h * LANES:(h + 1) * LANES] for h in range(2 * pp)]

        def step(kb, masked, carry):
            ks = pl.multiple_of(kb * tk, tk)
            heads = range(len(qhs))

            def soft(r, zc, mc, lc):
                zc = zc * scale
                if masked:
                    zc = jnp.where(_causal(qi * tq + r, ks, zc.shape[0], tk, False), zc, -1e30)
                m_new = jnp.maximum(mc, jnp.max(zc, axis=1, keepdims=True))
                a = jnp.exp(mc - m_new)
                pr = jnp.exp(zc - _wide(m_new, tk))
                return pr.astype(BF16), m_new, a * lc + _row_sum(pr), a

            zs = [_dot_nt(qhs[h], k_ref[pl.ds(ks, tk), h * LANES:(h + 1) * LANES]) for h in heads]
            first = [_by_rows(soft, 4, zs[h], carry[h][0], carry[h][1]) for h in heads]
            pvs = [_dot_nn(first[h][0], v_ref[pl.ds(ks, tk), (h // 2) * LANES:(h // 2 + 1) * LANES]) for h in heads]
            accs = [_by_rows(lambda r, ac, aa, pc: (aa * ac + pc,), 1, carry[h][2], first[h][3], pvs[h])[0]
                    for h in heads]
            return tuple((first[h][1], first[h][2], accs[h]) for h in heads)

        init = (jnp.full((tq, LANES), -1e30, F32), jnp.zeros((tq, LANES), F32), jnp.zeros((tq, LANES), F32))
        res = _two_loops((qi * tq) // tk, nkb, False, step, (init,) * (2 * pp))
        for pr in range(pp):
            (m0, l0, acc0), (m1, l1, acc1) = res[2 * pr], res[2 * pr + 1]
            o_ref[:, pr * LANES:(pr + 1) * LANES] = jnp.where(masks[0], acc0 / l0, acc1 / l1)
            lse_ref[:, pr * LANES:(pr + 1) * LANES] = jnp.where(masks[0], m0 + jnp.log(l0), m1 + jnp.log(l1))

    shape = jax.ShapeDtypeStruct((s, nhp * LANES), F32)
    blk = pl.BlockSpec((tq, pp * LANES), lambda h, i: (i, h))
    return _hosted(name, body, (nhp // pp, nq), [q, k, v],
                   [pl.BlockSpec((tq, 2 * pp * LANES), lambda h, i: (i, h)),
                    pl.BlockSpec((s, 2 * pp * LANES), lambda h, i: (0, h)),
                    pl.BlockSpec((s, pp * LANES), lambda h, i: (0, h))], [shape, shape], [blk, blk], comm)


def _mla_bwd(name, q, k, v, o, lse, do, tq, tk, comm=None):
    s = q.shape[0]
    nhp = v.shape[1] // LANES
    scale = MLA_QK_DIM ** -0.5
    nq = s // tq
    pp = _pairs_per_step(nhp)

    def body(q_ref, k_ref, v_ref, o_ref, lse_ref, do_ref, dq_ref, dk_ref, dv_ref):
        qi = pl.program_id(1)

        @pl.when(qi == 0)
        def _():
            dk_ref[...] = jnp.zeros_like(dk_ref)
            dv_ref[...] = jnp.zeros_like(dv_ref)

        masks = _head_masks()
        nkb = ((qi + 1) * tq + tk - 1) // tk
        dout = do_ref[...]
        doutbs = dout.astype(BF16)
        prod = dout * o_ref[...]
        qhs = [q_ref[:, h * LANES:(h + 1) * LANES] for h in range(2 * pp)]
        dohs = [jnp.where(hm, _pair(dout, pr), 0.0).astype(BF16) for pr in range(pp) for hm in masks]
        totals = [_row_sum(jnp.where(hm, _pair(prod, pr), 0.0)) for pr in range(pp) for hm in masks]
        lse = lse_ref[...]
        lses = [jnp.broadcast_to(lse[:, h * HEAD_DIM:h * HEAD_DIM + 1], (tq, LANES)) for h in range(2 * pp)]

        def step(kb, masked, carry):
            ks = pl.multiple_of(kb * tk, tk)
            heads = range(len(qhs))

            def probs(r, zc, dpc, lsec, totc):
                pr = jnp.exp(zc * scale - _wide(lsec, tk))
                if masked:
                    pr = jnp.where(_causal(qi * tq + r, ks, pr.shape[0], tk, False), pr, 0.0)
                return pr.astype(BF16), (pr * (dpc - _wide(totc, tk)) * scale).astype(BF16)

            khs = [k_ref[pl.ds(ks, tk), h * LANES:(h + 1) * LANES] for h in heads]
            vvs = [v_ref[pl.ds(ks, tk), (h // 2) * LANES:(h // 2 + 1) * LANES] for h in heads]
            zs = [_dot_nt(qhs[h], khs[h]) for h in heads]
            dps = [_dot_nt(dohs[h], vvs[h]) for h in heads]
            both = [_by_rows(probs, 2, zs[h], dps[h], lses[h], totals[h]) for h in heads]
            dks = [_dot_tn(both[h][1], qhs[h]) for h in heads]
            dvs = [_dot_tn(both[h][0], _pair(doutbs, h // 2)) for h in heads]
            dqs = [_dot_nn(both[h][1], khs[h]) for h in heads]
            for h in heads:
                dk_ref[pl.ds(ks, tk), h * LANES:(h + 1) * LANES] += dks[h]
            for pr in range(pp):
                dv_ref[pl.ds(ks, tk), pr * LANES:(pr + 1) * LANES] += jnp.where(masks[0], dvs[2 * pr], dvs[2 * pr + 1])
            return tuple(carry[h] + dqs[h] for h in heads)

        zero = jnp.zeros((tq, LANES), F32)
        dqs = _two_loops((qi * tq) // tk, nkb, False, step, (zero,) * (2 * pp))
        for h in range(2 * pp):
            dq_ref[:, h * LANES:(h + 1) * LANES] = dqs[h]

    blk = pl.BlockSpec((tq, pp * LANES), lambda h, i: (i, h))
    blk2 = pl.BlockSpec((tq, 2 * pp * LANES), lambda h, i: (i, h))
    full = pl.BlockSpec((s, pp * LANES), lambda h, i: (0, h))
    full2 = pl.BlockSpec((s, 2 * pp * LANES), lambda h, i: (0, h))
    return _hosted(name, body, (nhp // pp, nq), [q, k, v, o, lse, do], [blk2, full2, full, blk, blk, blk],
                   [jax.ShapeDtypeStruct(q.shape, F32), jax.ShapeDtypeStruct(k.shape, F32),
                    jax.ShapeDtypeStruct(v.shape, F32)], [blk2, full2, full], comm)


def _norm_parts(x):
    r = lax.rsqrt(jnp.mean(x * x, axis=-1, keepdims=True) + NORM_EPS)
    return r, x * r


def _rmsmod_fwd(x, g, sc, sh):
    _, xh = _norm_parts(x)
    return ((xh * g) * (1.0 + sc) + sh,)


def _rmsmod_bwd(dh, x, dres, g, sc):
    r, xh = _norm_parts(x)
    dy = dh * (1.0 + sc)
    dxh = dy * g
    dx = r * (dxh - xh * jnp.mean(dxh * xh, axis=-1, keepdims=True)) + dres
    return dx, dh, dh * (xh * g), dy * xh


def _rms_bwd_plain(dh, x, g):
    r, xh = _norm_parts(x)
    dxh = dh * g
    return r * (dxh - xh * jnp.mean(dxh * xh, axis=-1, keepdims=True)), dh * xh


def _cat(parts):
    return jnp.concatenate(parts, axis=1)


def _swap_halves(a):
    half = a.shape[-1] // 2
    return jnp.concatenate([a[..., half:], a[..., :half]], axis=-1)


def _adamw_fn(w, g, m, v):
    m = ADAM_B1 * m + (1.0 - ADAM_B1) * g
    v = ADAM_B2 * v + (1.0 - ADAM_B2) * jnp.square(g)
    m_hat = m / (1.0 - ADAM_B1 ** ADAM_STEP)
    v_hat = v / (1.0 - ADAM_B2 ** ADAM_STEP)
    delta = -ADAM_LR * (m_hat / (jnp.sqrt(v_hat) + ADAM_EPS) + ADAM_WD * w)
    return delta, m, v


def _adamw(name, w, g, m, v):
    shape = w.shape
    width = shape[-1]
    flat = [t.reshape(-1, width) for t in (w, g, m, v)]
    res = _rowwise(name, _adamw_fn, flat, [], [(width, F32)] * 3)
    return [t.reshape(shape) for t in res]


def _sum_adamw(name, land, w, m, v):
    shape = w.shape
    width = shape[-1]
    rows = w.size // width

    def fn(*blocks):
        g = blocks[0].astype(F32)
        for b in blocks[1:NDEV]:
            g = g + b.astype(F32)
        return (g,) + _adamw_fn(blocks[NDEV], g, blocks[NDEV + 1], blocks[NDEV + 2])

    def fn_whole(wb, mb, vb, lb):
        return fn(*[lb[i] for i in range(NDEV)], wb, mb, vb)

    flat = [t.reshape(rows, width) for t in (w, m, v)]
    if rows % 16 == 0:
        views = [(land.reshape(NDEV * rows, width), width, 0, i * rows) for i in range(NDEV)]
        res = _rowwise(name, fn, views + flat, [], [(width, F32)] * 4)
    else:
        res = _rowwise(name, fn_whole, flat, [land.reshape(NDEV, rows, width)], [(width, F32)] * 4)
    return [t.reshape(shape) for t in res]


def kernel(x, c, positions, w_ada, b_ada, g_mix_norm, w_in, g_q_lat, w_q_up, g_kv_lat, w_kv_up, w_sb_out, w_mla_out, w_mix_out, g_mlp_norm, w_up, w_down, g_final, loss_target, m_w_ada, m_b_ada, m_g_mix_norm, m_w_in, m_g_q_lat, m_w_q_up, m_g_kv_lat, m_w_kv_up, m_w_sb_out, m_w_mla_out, m_w_mix_out, m_g_mlp_norm, m_w_up, m_w_down, m_g_final, v_w_ada, v_b_ada, v_g_mix_norm, v_w_in, v_g_q_lat, v_w_q_up, v_g_kv_lat, v_w_kv_up, v_w_sb_out, v_w_mla_out, v_w_mix_out, v_g_mlp_norm, v_w_up, v_w_down, v_g_final):
    seq, d = x.shape[1], x.shape[2]
    depth = w_ada.shape[0]
    qr, kvr = g_q_lat.shape[1], g_kv_lat.shape[1]
    sbw, mlaw = w_sb_out.shape[1], w_mla_out.shape[1]
    nh = mlaw // HEAD_DIM
    nhp_sb = sbw // LANES
    dff = w_up.shape[2] * NDEV
    ada_n = w_ada.shape[2]
    gb = min(512, d)
    tq, tk = min(256, seq), min(256, seq)
    me = 4 * lax.axis_index("x") + 2 * lax.axis_index("y") + lax.axis_index("c")

    o_qlat = _roundup(3 * sbw, qr)
    o_kvlat = _roundup(o_qlat + qr, kvr)
    o_rope = _roundup(o_kvlat + kvr, 2 * LANES)
    o_gate = _roundup(o_rope + 2 * LANES, gb)
    wp = o_gate + 2 * d

    c_all = _exchange("ag_c", _Comm("gather_all", [c.reshape(d // LANES, LANES)]))[0].reshape(NDEV, d)
    c_act = _rowwise("silu_c", lambda t: (t * (1.0 / (1.0 + jnp.exp(-t))),), [c_all], [], [(d, F32)])[0]
    parts = jnp.stack([_matmul("ada_fwd", c_act, w_ada[l], "nn") for l in range(depth)])
    parts_all = _exchange("ag_mod", _Comm("gather_all", [parts]))[0]
    mine = jnp.transpose(lax.dynamic_index_in_dim(parts_all, me, axis=2, keepdims=False), (1, 0, 2))
    mod = _rowwise("mod_bias", lambda a, b: (a + b,), [mine.reshape(depth, NDEV * ada_n), b_ada], [],
                   [(6 * d, F32)])[0]
    mods = [[mod[l:l + 1, i * d:(i + 1) * d] for i in range(6)] for l in range(depth)]

    big = [w_in, w_q_up, w_kv_up, w_sb_out, w_mla_out, w_mix_out, w_up, w_down]
    row_sharded = [False, False, False, False, False, True, False, True]
    shards = [[w[l].astype(BF16) for w in big] for l in range(depth)]

    ids_a, ids_b = [0, 1, 2, 3, 4, 5], [6, 7]
    pick = lambda l, ids: [shards[l][i] for i in ids]

    def unpack(gathered, ids):
        out = []
        for g, i in zip(gathered, ids):
            _, rows, cols = g.shape
            if i == 0:
                out.append(g)
            elif row_sharded[i]:
                out.append(g.reshape(NDEV * rows, cols))
            else:
                out.append(jnp.transpose(g, (1, 0, 2)).reshape(rows, NDEV * cols))
        return out

    n_in = w_in.shape[2]
    r0 = 3 * sbw + qr + kvr
    g0 = r0 + ROPE_DIM
    runs = [(0, 3 * sbw, 0), (3 * sbw, 3 * sbw + qr, o_qlat), (3 * sbw + qr, r0, o_kvlat), (g0, g0 + 2 * d, o_gate)]

    def shard_cols(g, a, b):
        return [g[j][:, max(a, n_in * j) - n_in * j:min(b, n_in * (j + 1)) - n_in * j]
                for j in range(a // n_in, (b - 1) // n_in + 1)]

    def derive(full):
        wi, wq, wkv, wsb, wmla, wmix = full
        dt = wi.dtype
        z = lambda r, n: jnp.zeros((r, n), dt)
        kr = _cat(shard_cols(wi, r0, g0))
        pieces, at = [], 0
        for a, b, start in runs[:3]:
            pieces += [z(d, start - at)] + shard_cols(wi, a, b)
            at = start + b - a
        pieces += [z(d, o_rope - at), z(d, HEAD_DIM), kr, z(d, LANES - MLA_QK_DIM),
                   z(d, HEAD_DIM), _swap_halves(kr), z(d, LANES - MLA_QK_DIM),
                   z(d, o_gate - o_rope - 2 * LANES)] + shard_cols(wi, g0, g0 + 2 * d)
        w_in_pad = _cat([t for t in pieces if t.shape[1]])
        wq3 = wq.reshape(qr, nh, MLA_QK_DIM)
        z3 = lambda n: jnp.zeros((qr, nh, n), dt)
        rope_w = wq3[:, :, HEAD_DIM:]
        wq_a = jnp.concatenate([wq3[:, :, :HEAD_DIM], rope_w, z3(LANES - MLA_QK_DIM)], axis=2).reshape(qr, nh * LANES)
        wq_b = jnp.concatenate([z3(HEAD_DIM), _swap_halves(rope_w), z3(LANES - MLA_QK_DIM)], axis=2).reshape(qr, nh * LANES)
        wkv3 = wkv.reshape(kvr, nh, 2 * HEAD_DIM)
        wk = jnp.concatenate([wkv3[:, :, :HEAD_DIM], jnp.zeros((kvr, nh, HEAD_DIM), dt)], axis=2).reshape(kvr, nh * LANES)
        wv = wkv3[:, :, HEAD_DIM:].reshape(kvr, nh * HEAD_DIM)
        return dict(w_in=w_in_pad, w_q=_cat([wq_a, wq_b]), w_kv=_cat([wk, wv]), w_sb=wsb, w_mla=wmla, w_mix=wmix)

    def fold(gr):
        gi, gq, gkv = gr["w_in"], gr["w_q"], gr["w_kv"]
        ra = gi[:, o_rope + HEAD_DIM:o_rope + MLA_QK_DIM]
        rb = gi[:, o_rope + LANES + HEAD_DIM:o_rope + LANES + MLA_QK_DIM]
        rope = ra + _swap_halves(rb)

        def cols(a, b):
            out = []
            for s0, s1, start in runs[:3] + [(r0, g0, None)] + runs[3:]:
                lo, hi = max(a, s0), min(b, s1)
                if lo < hi:
                    out.append(rope[:, lo - r0:hi - r0] if start is None else gi[:, start + lo - s0:start + hi - s0])
            return out

        g_in = jnp.stack([_cat(cols(n_in * j, n_in * (j + 1))) for j in range(NDEV)]).astype(BF16)
        ga = gq[:, :nh * LANES].reshape(qr, nh, LANES)
        gb_ = gq[:, nh * LANES:].reshape(qr, nh, LANES)
        g_q = jnp.concatenate([ga[:, :, :HEAD_DIM], ga[:, :, HEAD_DIM:MLA_QK_DIM]
                               + _swap_halves(gb_[:, :, HEAD_DIM:MLA_QK_DIM])], axis=2).reshape(qr, nh * MLA_QK_DIM)
        gk = gkv[:, :nh * LANES].reshape(kvr, nh, LANES)[:, :, :HEAD_DIM]
        gv = gkv[:, nh * LANES:].reshape(kvr, nh, HEAD_DIM)
        g_kv = jnp.concatenate([gk, gv], axis=2).reshape(kvr, nh * 2 * HEAD_DIM)
        return [g_in, g_q, g_kv, gr["w_sb"], gr["w_mla"], gr["w_mix"], gr["w_up"], gr["w_down"]]

    part_a = _exchange("ag_w0_own", _Comm("gather_own", pick(0, ids_a)))
    ready_a = _exchange("ag_w0_fwd", _Comm("gather_fwd", [], lands=part_a))
    part_b, weights = None, []

    inv_freq = 1.0 / (ROPE_THETA ** (jnp.arange(0, ROPE_DIM, 2, dtype=F32) / ROPE_DIM))
    ang = positions[0].astype(F32)[:, None] * inv_freq
    cos, sin = jnp.cos(ang), jnp.sin(ang)
    tail = jnp.zeros((seq, LANES - MLA_QK_DIM), F32)
    rope_c = _cat([jnp.ones((seq, HEAD_DIM), F32), cos, cos, tail])
    rope_s = _cat([jnp.zeros((seq, HEAD_DIM), F32), -sin, sin, tail])
    zero_vec = lambda n: jnp.zeros((1, n), F32)

    def rope_fwd(q2, kvs, pd, tc, ts):
        c8, s8 = _cat([tc] * nh), _cat([ts] * nh)
        qf = q2[:, :nh * LANES] * c8 + q2[:, nh * LANES:] * s8
        kpe = pd[:, :LANES] * tc + pd[:, LANES:] * ts
        return qf, kvs[:, :nh * LANES] + _cat([kpe] * nh), kvs[:, nh * LANES:]

    def rope_bwd(dq, dk, dv, tc, ts):
        c8, s8 = _cat([tc] * nh), _cat([ts] * nh)
        dks = dk[:, :LANES]
        for h in range(1, nh):
            dks = dks + dk[:, h * LANES:(h + 1) * LANES]
        return _cat([dq * c8, dq * s8]), _cat([dk, dv]), _cat([dks * tc, dks * ts])

    def merge_fwd(gs, gm, osb, omla):
        return osb / (1.0 + jnp.exp(-gs)) + omla / (1.0 + jnp.exp(-gm))

    def merge_bwd(dm, gs, gm, osb, omla):
        ss, sm = 1.0 / (1.0 + jnp.exp(-gs)), 1.0 / (1.0 + jnp.exp(-gm))
        return ss * dm, sm * dm, dm * osb * ss * (1.0 - ss), dm * omla * sm * (1.0 - sm)

    def gates_of(p):
        return [(p, o_gate), (p, o_gate + d)]

    xs = x[0]
    saved = []
    for l in range(depth):
        w = derive(unpack(ready_a, ids_a))
        sh1, sc1, g1, sh2, sc2, g2 = mods[l]
        h1 = _rowwise("norm1", _rmsmod_fwd, [xs], [g_mix_norm[l:l + 1], sc1, sh1], [(d, BF16)])[0]
        p = _matmul("in_proj", h1, w["w_in"], "nn")
        (o_sb, tot_sb), part_b = _sb_fwd("sb_fwd", p, nhp_sb, tq, tk, _Comm("gather_own", pick(l, ids_b)))
        y_sb = _matmul("sb_out", o_sb, w["w_sb"], "nn")
        qn = _rowwise("norm_q", _rmsmod_fwd, [(p, qr, o_qlat // qr, 0)],
                      [g_q_lat[l:l + 1], zero_vec(qr), zero_vec(qr)], [(qr, BF16)])[0]
        kvn = _rowwise("norm_kv", _rmsmod_fwd, [(p, kvr, o_kvlat // kvr, 0)],
                       [g_kv_lat[l:l + 1], zero_vec(kvr), zero_vec(kvr)], [(kvr, BF16)])[0]
        q2 = _matmul("q_up", qn, w["w_q"], "nn")
        kvs = _matmul("kv_up", kvn, w["w_kv"], "nn")
        qf, kf, vf = _rowwise("rope_fwd", rope_fwd, [q2, kvs, (p, 2 * LANES, o_rope // (2 * LANES), 0), rope_c, rope_s],
                              [], [(nh * LANES, BF16), (nh * LANES, BF16), (mlaw, BF16)])
        comms = [_Comm("gather_fwd", [], lands=part_b)]
        if l + 1 < depth:
            comms.append(_Comm("gather_own", pick(l + 1, ids_a)))
        group = _CommGroup(comms)
        (o_mla, lse), got = _mla_fwd("mla_fwd", qf, kf, vf, tq, tk, group)
        got = group.split(got)
        w["w_up"], w["w_down"] = unpack(got[0], ids_b)
        weights.append(w)
        y_mla, merged = _matmul("mla_out", o_mla, w["w_mla"], "nn", outs=[F32, BF16], rows=gates_of(p) + [y_sb],
                                epilogue=lambda acc, gs, gm, osb: (acc, merge_fwd(gs, gm, osb, acc)))
        resid = lambda acc, xv, g: (acc, xv + g * acc)
        y1, x_mid = _matmul("mix_out", merged, w["w_mix"], "nn", epilogue=resid, rows=[xs], vecs=[g1], outs=[F32, F32])
        h2 = _rowwise("norm2", _rmsmod_fwd, [x_mid], [g_mlp_norm[l:l + 1], sc2, sh2], [(d, BF16)])[0]
        relu2 = lambda acc: (acc, jnp.square(jnp.maximum(acc, 0.0)))
        if l + 1 < depth:
            (u, act), ready_a = _matmul("mlp_up", h2, w["w_up"], "nn", outs=[F32, BF16], epilogue=relu2,
                                        comm=_Comm("gather_fwd", [], lands=got[1]))
        else:
            u, act = _matmul("mlp_up", h2, w["w_up"], "nn", outs=[F32, BF16], epilogue=relu2)
        y2, x_out = _matmul("mlp_down", act, w["w_down"], "nn", epilogue=resid, rows=[x_mid], vecs=[g2], outs=[F32, F32])
        saved.append(dict(x=xs, h1=h1, p=p, o_sb=o_sb, tot_sb=tot_sb, y_sb=y_sb, qn=qn, kvn=kvn, qf=qf, kf=kf, vf=vf, o_mla=o_mla,
                          lse=lse, y_mla=y_mla, merged=merged, y1=y1, x_mid=x_mid, h2=h2, u=u, act=act, y2=y2))
        xs = x_out

    def final_fn(xv, tv, y2, g, gate):
        r, xh = _norm_parts(xv)
        diff = xh * g - tv
        dy = diff * (1.0 / d)
        dxh = dy * g
        dx = r * (dxh - xh * jnp.mean(dxh * xh, axis=-1, keepdims=True))
        return dx, dx * gate, diff * diff, dy * xh, dx * y2

    dx, dy2, sq, dg_final, dgate2 = _rowwise(
        "loss_head", final_fn, [xs, loss_target[0], saved[-1]["y2"]], [g_final.reshape(1, d), mods[-1][5]],
        [(d, F32), (d, BF16)], reds=[d, d, d])
    loss = lax.psum(0.5 * jnp.sum(sq) / d, ("x", "y", "c"))

    def norm2_bwd_fn(dh, xv, dres, y1, g, sc, gate):
        dx_mid, dsh, dsc, dg = _rmsmod_bwd(dh, xv, dres, g, sc)
        return dx_mid, dx_mid * gate, dsh, dsc, dg, dx_mid * y1

    def norm1_bwd_fn(dh, xv, dres, y2, g, sc, gate):
        dxv, dsh, dsc, dg = _rmsmod_bwd(dh, xv, dres, g, sc)
        return dxv, dxv * gate, dsh, dsc, dg, dxv * y2

    def chunk(gfull, wref, by_rows):
        rows, cols = wref.shape[1], wref.shape[2]
        if by_rows:
            return gfull.reshape(NDEV, rows, cols).astype(BF16)
        return jnp.transpose(gfull.reshape(rows, NDEV, cols), (1, 0, 2)).astype(BF16)

    dmods, small = [None] * depth, [None] * depth
    late, lands = None, [None] * len(big)
    for l in reversed(range(depth)):
        w, sv = weights[l], saved[l]
        sh1, sc1, g1, sh2, sc2, g2 = mods[l]
        gr = {}
        du = _matmul("mlp_down_dx", dy2, w["w_down"], "nt", outs=[BF16], rows=[sv["u"]],
                     epilogue=lambda acc, uv: (acc * 2.0 * jnp.maximum(uv, 0.0),))
        gr["w_down"] = _matmul("mlp_down_dw", sv["act"], dy2, "tn", BF16)
        dh2 = _matmul("mlp_up_dx", du, w["w_up"], "nt")
        gr["w_up"] = _matmul("mlp_up_dw", sv["h2"], du, "tn", BF16)
        dx_mid, dy1, dsh2, dsc2, dg_mlp, dgate1 = _rowwise(
            "norm2_bwd", norm2_bwd_fn, [dh2, sv["x_mid"], dx, sv["y1"]], [g_mlp_norm[l:l + 1], sc2, g1],
            [(d, F32), (d, BF16)], reds=[d, d, d, d])
        dy_sb, dy_mla, dgate_sb, dgate_mla = _matmul(
            "mix_out_dx", dy1, w["w_mix"], "nt", outs=[BF16] * 4, epilogue=merge_bwd,
            rows=gates_of(sv["p"]) + [sv["y_sb"], sv["y_mla"]])
        gr["w_mix"] = _matmul("mix_out_dw", sv["merged"], dy1, "tn", BF16)
        do_sb = _matmul("sb_out_dx", dy_sb, w["w_sb"], "nt")
        gr["w_sb"] = _matmul("sb_out_dw", sv["o_sb"], dy_sb, "tn", BF16)
        do_mla = _matmul("mla_out_dx", dy_mla, w["w_mla"], "nt")
        gr["w_mla"] = _matmul("mla_out_dw", sv["o_mla"], dy_mla, "tn", BF16)
        ready = {3: gr["w_sb"], 4: gr["w_mla"], 5: gr["w_mix"], 6: gr["w_up"], 7: gr["w_down"]}
        ready = {i: chunk(g, big[i], row_sharded[i]) for i, g in ready.items()}
        ids_a = [6] + ([0, 1, 2] if late is not None else [])
        comm_a = _Comm("scatter", [ready[6]] + (late or []), [lands[i] for i in ids_a], [l] + [l + 1] * 3, depth)
        comm_b = _Comm("scatter", [ready[7]], [lands[7]], [l], depth)
        (dq_sb, dk_sb, dv_sb), got_a = _sb_bwd("sb_bwd", sv["p"], sv["tot_sb"], do_sb, nhp_sb, tq, tk, comm_a)
        (dqf, dkf, dvf), got_b = _mla_bwd("mla_bwd", sv["qf"], sv["kf"], sv["vf"], sv["o_mla"], sv["lse"], do_mla,
                                          tq, tk, comm_b)
        for i, t in zip(ids_a + [7], list(got_a) + list(got_b)):
            lands[i] = t
        dq2, dkvs, drope = _rowwise("rope_bwd", rope_bwd, [dqf, dkf, dvf, rope_c, rope_s], [],
                                    [(2 * nh * LANES, BF16), (nh * LANES + mlaw, BF16), (2 * LANES, BF16)])
        dqn = _matmul("q_up_dx", dq2, w["w_q"], "nt")
        gr["w_q"] = _matmul("q_up_dw", sv["qn"], dq2, "tn")
        dkvn = _matmul("kv_up_dx", dkvs, w["w_kv"], "nt")
        gr["w_kv"] = _matmul("kv_up_dw", sv["kvn"], dkvs, "tn")
        dqlat, dg_q = _rowwise("norm_q_bwd", _rms_bwd_plain, [dqn, (sv["p"], qr, o_qlat // qr, 0)],
                               [g_q_lat[l:l + 1]], [(qr, BF16)], reds=[qr])
        dkvlat, dg_kv = _rowwise("norm_kv_bwd", _rms_bwd_plain, [dkvn, (sv["p"], kvr, o_kvlat // kvr, 0)],
                                 [g_kv_lat[l:l + 1]], [(kvr, BF16)], reds=[kvr])
        zb = lambda n: jnp.zeros((seq, n), BF16)
        dp = _cat([dq_sb.astype(BF16), dk_sb.astype(BF16), dv_sb.astype(BF16), zb(o_qlat - 3 * sbw), dqlat,
                   zb(o_kvlat - o_qlat - qr), dkvlat, zb(o_rope - o_kvlat - kvr), drope,
                   zb(o_gate - o_rope - 2 * LANES), dgate_sb, dgate_mla])
        dh1, got = _matmul("in_proj_dx", dp, w["w_in"], "nt", comm=_Comm("scatter", [ready[5]], [lands[5]], [l], depth))
        lands[5] = got[0]
        gr["w_in"], got = _matmul("in_proj_dw", sv["h1"], dp, "tn",
                                  comm=_Comm("scatter", [ready[3], ready[4]], [lands[3], lands[4]], [l, l], depth))
        lands[3], lands[4] = got
        dmods[l] = [None, None, dgate1, dsh2, dsc2, dgate2]
        if l > 0:
            dx, dy2, dsh1, dsc1, dg_mix, dgate2 = _rowwise(
                "norm1_bwd", norm1_bwd_fn, [dh1, sv["x"], dx_mid, saved[l - 1]["y2"]],
                [g_mix_norm[l:l + 1], sc1, mods[l - 1][5]], [(d, F32), (d, BF16)], reds=[d, d, d, d])
        else:
            dx, dsh1, dsc1, dg_mix = _rowwise("norm1_bwd", _rmsmod_bwd, [dh1, sv["x"], dx_mid],
                                              [g_mix_norm[l:l + 1], sc1], [(d, F32)], reds=[d, d, d])
        dmods[l] = _cat([dsh1, dsc1] + dmods[l][2:])
        small[l] = (dg_mix, dg_q, dg_kv, dg_mlp)
        g_in, g_q, g_kv = fold(gr)[:3]
        late = [g_in, chunk(g_q, big[1], False), chunk(g_kv, big[2], False)]

    small_parts = [jnp.concatenate(dmods, axis=0)]
    small_parts += [jnp.concatenate([small[l][i] for l in range(depth)], axis=0) for i in range(4)]
    small_parts.append(dg_final)
    small_all = _exchange("ag_small", _Comm("gather_all", small_parts))

    dmod_mine = lax.dynamic_slice_in_dim(small_all[0], me * ada_n, ada_n, axis=2)
    c_act_t = jnp.transpose(c_act)

    def outer_fn(ct, dm):
        acc = ct[:, 0:1] * dm[0:1, :]
        for b in range(1, NDEV):
            acc = acc + ct[:, b:b + 1] * dm[b:b + 1, :]
        return (acc,)

    g_w_ada = jnp.stack([_rowwise("ada_dw", outer_fn, [c_act_t], [dmod_mine[:, l, :]], [(ada_n, F32)])[0]
                         for l in range(depth)])

    landed = list(_exchange("a2a_last", _Comm("scatter", late, lands[:3], [0] * 3, depth))) + lands[3:]

    moments = dict(
        w_ada=(w_ada, m_w_ada, v_w_ada), b_ada=(b_ada, m_b_ada, v_b_ada),
        g_mix_norm=(g_mix_norm, m_g_mix_norm, v_g_mix_norm), w_in=(w_in, m_w_in, v_w_in),
        g_q_lat=(g_q_lat, m_g_q_lat, v_g_q_lat), w_q_up=(w_q_up, m_w_q_up, v_w_q_up),
        g_kv_lat=(g_kv_lat, m_g_kv_lat, v_g_kv_lat), w_kv_up=(w_kv_up, m_w_kv_up, v_w_kv_up),
        w_sb_out=(w_sb_out, m_w_sb_out, v_w_sb_out), w_mla_out=(w_mla_out, m_w_mla_out, v_w_mla_out),
        w_mix_out=(w_mix_out, m_w_mix_out, v_w_mix_out), g_mlp_norm=(g_mlp_norm, m_g_mlp_norm, v_g_mlp_norm),
        w_up=(w_up, m_w_up, v_w_up), w_down=(w_down, m_w_down, v_w_down),
        g_final=(g_final.reshape(1, d), m_g_final.reshape(1, d), v_g_final.reshape(1, d)))
    lands = dict(b_ada=small_all[0], g_mix_norm=small_all[1], g_q_lat=small_all[2], g_kv_lat=small_all[3],
                 g_mlp_norm=small_all[4], g_final=small_all[5], w_in=landed[0], w_q_up=landed[1],
                 w_kv_up=landed[2], w_sb_out=landed[3], w_mla_out=landed[4], w_mix_out=landed[5], w_up=landed[6],
                 w_down=landed[7])
    gs, deltas, new_ms, new_vs = [], [], [], []
    for name, (wt, mt, vt) in moments.items():
        if name == "w_ada":
            res = [g_w_ada] + _adamw("adamw_" + name, wt, g_w_ada, mt, vt)
        else:
            res = _sum_adamw("adamw_" + name, lands[name], wt, mt, vt)
        if name == "g_final":
            res = [t.reshape(d) for t in res]
        for lst, t in zip((gs, deltas, new_ms, new_vs), res):
            lst.append(t)

    return (loss, dx[None], *gs, *deltas, *new_ms, *new_vs)
```

```python
import functools

import jax
import jax.numpy as jnp
from jax import lax
from jax.experimental import pallas as pl
from jax.experimental.pallas import tpu as pltpu

F32 = jnp.float32
BF16 = jnp.bfloat16
NDEV = 8
LANES = 128
HEAD_DIM = 64
ROPE_DIM = 32
MLA_QK_DIM = HEAD_DIM + ROPE_DIM
ROPE_THETA = 10000.0
NORM_EPS = 1e-6
ADAM_LR = 0.001
ADAM_B1 = 0.9
ADAM_B2 = 0.999
ADAM_EPS = 1e-08
ADAM_WD = 0.01
ADAM_STEP = 10
VMEM_LIMIT = 48 * 1024 * 1024


def _pcall(body, **kw):
    return pl.pallas_call(body, **kw)


def _tile(n, pref):
    for t in (512, 384, 256, 128, 64, 32, 16, 8):
        if t <= pref and n % t == 0:
            return t
    return n


def _roundup(n, m):
    return (n + m - 1) // m * m


_CP = pltpu.CompilerParams(vmem_limit_bytes=VMEM_LIMIT)


def _rowwise(name, fn, rows, vecs, outs, reds=(), tb=256):
    rows = [r if isinstance(r, tuple) else (r, r.shape[1], 0, 0) for r in rows]
    nrows = None
    for arr, width, col, roff in rows:
        if roff == 0 and nrows is None:
            nrows = arr.shape[0]
    first_off = [r for r in rows if r[3] != 0]
    if first_off:
        nrows = min(nrows, first_off[0][3])
    tb = _tile(nrows, tb)
    nblk = nrows // tb
    n_in = len(rows) + len(vecs)
    n_out = len(outs)

    def body(*refs):
        vals = [r[...] for r in refs[:n_in]]
        res = fn(*vals)
        if not isinstance(res, (tuple, list)):
            res = (res,)
        for ref, val in zip(refs[n_in:n_in + n_out], res[:n_out]):
            ref[...] = val.astype(ref.dtype)
        for ref, val in zip(refs[n_in + n_out:], res[n_out:]):
            @pl.when(pl.program_id(0) == 0)
            def _(ref=ref):
                ref[...] = jnp.zeros_like(ref)
            ref[...] += jnp.sum(val.astype(F32), axis=0, keepdims=True)

    in_specs = []
    for arr, width, col, roff in rows:
        in_specs.append(pl.BlockSpec((tb, width), functools.partial(
            lambda i, col, rb: (rb + i, col), col=col, rb=roff // tb)))
    for v in vecs:
        in_specs.append(pl.BlockSpec(v.shape, lambda i, nd=v.ndim: (0,) * nd))
    out_specs = [pl.BlockSpec((tb, w), lambda i: (i, 0)) for w, _ in outs]
    out_specs += [pl.BlockSpec((1, w), lambda i: (0, 0)) for w in reds]
    out_shape = [jax.ShapeDtypeStruct((nrows, w), dt) for w, dt in outs]
    out_shape += [jax.ShapeDtypeStruct((1, w), F32) for w in reds]
    res = _pcall(body, name=name, grid=(nblk,), in_specs=in_specs, out_specs=out_specs,
                 out_shape=out_shape, compiler_params=_CP)(*[r[0] for r in rows], *vecs)
    return res


_DIMS = {"nn": (((1,), (0,)), ((), ())), "nt": (((1,), (1,)), ((), ())), "tn": (((0,), (0,)), ((), ()))}


def _matmul(name, a, b, mode, out_dtype=F32, epilogue=None, rows=(), vecs=(), outs=None, comm=None, owner_cols=None):
    if mode == "nn":
        (m, k), n = a.shape, b.shape[1]
    elif mode == "nt":
        (m, k), n = a.shape, b.shape[0]
    else:
        (k, m), n = a.shape, b.shape[1]
    tm = _tile(m, 512)
    tn = owner_cols or next((t for t in (1536, 1024) if n % t == 0 and n > t), _tile(n, 512))
    dims = _DIMS[mode]
    outs = [out_dtype] if outs is None else outs
    n_extra = len(rows) + len(vecs)
    rows = [r if isinstance(r, tuple) else (r, 0) for r in rows]

    def body(a_ref, b_ref, *refs):
        acc = lax.dot_general(a_ref[...].astype(BF16), b_ref[...].astype(BF16), dims, preferred_element_type=F32)
        res = (acc,) if epilogue is None else epilogue(acc, *[r[...] for r in refs[:n_extra]])
        for o_ref, val in zip(refs[n_extra:], res):
            o_ref[...] = val.astype(o_ref.dtype)

    a_spec = pl.BlockSpec((k, tm), lambda j, i: (0, i)) if mode == "tn" else pl.BlockSpec((tm, k), lambda j, i: (i, 0))
    b_spec = pl.BlockSpec((tn, k), lambda j, i: (j, 0)) if mode == "nt" else pl.BlockSpec((k, tn), lambda j, i: (0, j))
    blk = pl.BlockSpec((tm, tn), lambda j, i: (i, j))
    assert all(off % tn == 0 for _, off in rows), (name, tn)
    row_specs = [pl.BlockSpec((tm, tn), functools.partial(lambda j, i, first: (i, first + j), first=off // tn))
                 for _, off in rows]
    out_blk, out_dims = blk, (m, n)
    if owner_cols:
        out_blk, out_dims = pl.BlockSpec((None, tm, tn), lambda j, i: (j, i, 0)), (n // tn, m, tn)
    res, got = _hosted(name, body, (n // tn, m // tm), [a, b, *[r for r, _ in rows], *vecs],
                       [a_spec, b_spec] + row_specs + [pl.BlockSpec((1, tn), lambda j, i: (0, j))] * len(vecs),
                       [jax.ShapeDtypeStruct(out_dims, dt) for dt in outs], [out_blk] * len(outs), comm)
    res = res[0] if len(outs) == 1 else res
    return res if comm is None else (res, got)


class _Comm:
    KS = {"gather_all": (1, 2, 3, 4, 5, 6, 7), "gather_own": (1, 2, 4, 6), "gather_fwd": (2, 4, 6),
          "scatter": (1, 2, 3, 4, 5, 6, 7)}

    def __init__(self, kind, srcs, lands=None, layers=None, depth=None):
        self.kind, self.srcs, self.layers = kind, list(srcs), layers
        self.n = len(lands) if kind == "gather_fwd" else len(srcs)
        self.lands = list(lands) if lands is not None else [None] * self.n
        self.out_shapes = []
        for i, land in enumerate(self.lands):
            if land is not None:
                self.out_shapes.append(jax.ShapeDtypeStruct(land.shape, land.dtype))
            elif kind == "scatter":
                self.out_shapes.append(jax.ShapeDtypeStruct((NDEV, depth) + srcs[i].shape[1:], srcs[i].dtype))
            else:
                self.out_shapes.append(jax.ShapeDtypeStruct((NDEV,) + srcs[i].shape, srcs[i].dtype))
        self.operands = self.srcs + [t for t in self.lands if t is not None]
        self.scratch = [pltpu.SemaphoreType.DMA((NDEV - 1, self.n)), pltpu.SemaphoreType.DMA((NDEV - 1, self.n)),
                        pltpu.SemaphoreType.DMA((self.n,))]

    def aliases(self, first_in, first_out):
        given = [i for i, t in enumerate(self.lands) if t is not None]
        return {first_in + len(self.srcs) + pos: first_out + i for pos, i in enumerate(given)}

    def copies(self, in_refs, out_refs, send_sems, recv_sems, local_sems):
        x, y, c = lax.axis_index("x"), lax.axis_index("y"), lax.axis_index("c")
        me = 4 * x + 2 * y + c
        cps = []
        if self.kind != "gather_fwd":
            for i in range(self.n):
                src = in_refs[i].at[me] if self.kind == "scatter" else in_refs[i]
                dst = out_refs[i].at[me, self.layers[i]] if self.kind == "scatter" else out_refs[i].at[me]
                cps.append(pltpu.make_async_copy(src, dst, local_sems.at[i]))
        for k in self.KS[self.kind]:
            px = 1 - x if k & 4 else x
            py = 1 - y if k & 2 else y
            pc = 1 - c if k & 1 else c
            peer = 4 * px + 2 * py + pc
            for i in range(self.n):
                if self.kind == "gather_fwd":
                    src, dst, to = in_refs[i].at[peer], out_refs[i].at[peer], (x, y, 1 - c)
                elif self.kind == "scatter":
                    src, dst, to = in_refs[i].at[peer], out_refs[i].at[me, self.layers[i]], (px, py, pc)
                else:
                    src, dst, to = in_refs[i], out_refs[i].at[me], (px, py, pc)
                cps.append(pltpu.make_async_remote_copy(
                    src_ref=src, dst_ref=dst, send_sem=send_sems.at[k - 1, i], recv_sem=recv_sems.at[k - 1, i],
                    device_id=to, device_id_type=pl.DeviceIdType.MESH))
        return cps


class _CommGroup:
    def __init__(self, comms):
        self.comms = comms
        self.n = sum(cm.n for cm in comms)
        self.operands = [t for cm in comms for t in cm.operands]
        self.out_shapes = [t for cm in comms for t in cm.out_shapes]
        self.scratch = [t for cm in comms for t in cm.scratch]

    def aliases(self, first_in, first_out):
        out = {}
        for cm in self.comms:
            out.update(cm.aliases(first_in, first_out))
            first_in, first_out = first_in + len(cm.operands), first_out + cm.n
        return out

    def copies(self, in_refs, out_refs, *sems):
        cps, i, o = [], 0, 0
        for j, cm in enumerate(self.comms):
            cps += cm.copies(in_refs[i:i + len(cm.operands)], out_refs[o:o + cm.n], *sems[3 * j:3 * j + 3])
            i, o = i + len(cm.operands), o + cm.n
        return cps

    def split(self, outs):
        res, o = [], 0
        for cm in self.comms:
            res.append(list(outs[o:o + cm.n]))
            o += cm.n
        return res


_ANY = pl.BlockSpec(memory_space=pl.ANY)


def _exchange(name, comm):
    nci = len(comm.operands)

    def body(*refs):
        cps = comm.copies(refs[:nci], refs[nci:nci + comm.n], *refs[nci + comm.n:])
        for cp in cps:
            cp.start()
        for cp in cps:
            cp.wait()

    return _pcall(body, name=name, in_specs=[_ANY] * nci, out_specs=[_ANY] * comm.n, out_shape=comm.out_shapes,
                  scratch_shapes=comm.scratch, input_output_aliases=comm.aliases(0, 0))(*comm.operands)


def _hosted(name, body, grid, arrays, in_specs, out_shapes, out_specs, comm):
    if comm is None:
        return _pcall(body, name=name, grid=grid, in_specs=in_specs, out_specs=out_specs, out_shape=out_shapes,
                      compiler_params=_CP)(*arrays), []
    ni, no, nci = len(arrays), len(out_shapes), len(comm.operands)

    def full(*refs):
        ins, cin = refs[:ni], refs[ni:ni + nci]
        outs = refs[ni + nci:ni + nci + no]
        cout = refs[ni + nci + no:ni + nci + no + comm.n]
        sems = refs[ni + nci + no + comm.n:]
        first = functools.reduce(jnp.logical_and, [pl.program_id(a) == 0 for a in range(len(grid))])
        last = functools.reduce(jnp.logical_and, [pl.program_id(a) == grid[a] - 1 for a in range(len(grid))])

        @pl.when(first)
        def _():
            for cp in comm.copies(cin, cout, *sems):
                cp.start()

        body(*ins, *outs)

        @pl.when(last)
        def _():
            for cp in comm.copies(cin, cout, *sems):
                cp.wait()

    res = _pcall(full, name=name, grid=grid, in_specs=list(in_specs) + [_ANY] * nci,
                 out_specs=list(out_specs) + [_ANY] * comm.n, out_shape=list(out_shapes) + comm.out_shapes,
                 scratch_shapes=comm.scratch, input_output_aliases=comm.aliases(ni, no),
                 compiler_params=_CP)(*arrays, *comm.operands)
    return res[:no], res[no:]


def _dot_nt(a, b):
    return lax.dot_general(a, b, _DIMS["nt"], preferred_element_type=F32)


def _dot_tn(a, b):
    return lax.dot_general(a, b, _DIMS["tn"], preferred_element_type=F32)


def _dot_nn(a, b):
    return jnp.dot(a, b, preferred_element_type=F32)


def _tri(tk, rel):
    j = lax.broadcasted_iota(jnp.int32, (tk, tk), 0)
    s = lax.broadcasted_iota(jnp.int32, (tk, tk), 1)
    return {"after": j > s, "upto": j <= s, "before": j < s}[rel].astype(BF16)


def _pairs_per_step(nhp):
    return 2 if nhp % 2 == 0 else 1


def _pair(a, pr):
    return a[:, pr * LANES:(pr + 1) * LANES]


def _head_masks():
    lane = lax.broadcasted_iota(jnp.int32, (1, LANES), 1)
    return [(lane // HEAD_DIM) == h for h in range(2)]


ROW_CHUNK = 32


def _by_rows(fn, n_out, *arrays):
    rows = arrays[0].shape[0]
    step = min(ROW_CHUNK, rows)
    outs = [[] for _ in range(n_out)]
    for r in range(0, rows, step):
        for o, val in zip(outs, fn(r, *[a[r:r + step] for a in arrays])):
            o.append(val)
    return [jnp.concatenate(o, axis=0) for o in outs]


def _causal(r0, k0, rows, tk, strict):
    row = lax.broadcasted_iota(jnp.int32, (rows, tk), 0) + r0
    col = lax.broadcasted_iota(jnp.int32, (rows, tk), 1) + k0
    return col < row if strict else col <= row


def _wide(stat, width):
    return stat if width == LANES else jnp.concatenate([stat] * (width // LANES), axis=1)


def _row_sum(v):
    return jnp.broadcast_to(jnp.sum(v, axis=1, keepdims=True), (v.shape[0], LANES))


def _split_bf16(v):
    hi = v.astype(BF16)
    return hi, (v - hi.astype(F32)).astype(BF16)


def _sb_logs(z, scale, mask):
    z = z * scale
    e = jnp.exp(-jnp.abs(z))
    log_sig = jnp.minimum(z, 0.0) - jnp.log(1.0 + e)
    log_fail = log_sig - z
    return z, log_sig, (log_fail if mask is None else jnp.where(mask, log_fail, 0.0))


def _two_loops(n_full, nkb, near_first, step, carry):
    if near_first:
        carry = lax.fori_loop(0, nkb - n_full, lambda j, c: step(nkb - 1 - j, True, c), carry)
        return lax.fori_loop(0, n_full, lambda j, c: step(n_full - 1 - j, False, c), carry)
    carry = lax.fori_loop(0, n_full, lambda j, c: step(j, False, c), carry)
    return lax.fori_loop(n_full, nkb, lambda j, c: step(j, True, c), carry)


def _sb_fwd(name, p, nhp, tq, tk, comm=None):
    s = p.shape[0]
    scale = HEAD_DIM ** -0.5
    nq = s // tq
    pp = _pairs_per_step(nhp)
    wide = pp * LANES

    def body(q_ref, k_ref, v_ref, o_ref, tot_ref):
        qi = pl.program_id(1)
        masks = _head_masks()
        after = _tri(tk, "after")
        nkb = ((qi + 1) * tq + tk - 1) // tk
        q = q_ref[...]
        qhs = [jnp.where(hm, _pair(q, pr), 0.0).astype(BF16) for pr in range(pp) for hm in masks]

        def step(kb, masked, carry):
            ks = pl.multiple_of(kb * tk, tk)
            ks_all = k_ref[pl.ds(ks, tk), :].astype(BF16)
            vs_all = v_ref[pl.ds(ks, tk), :].astype(BF16)
            mask_of = lambda r, n: _causal(qi * tq + r, ks, n, tk, True) if masked else None
            heads = range(len(qhs))

            def logs(r, zc):
                _, log_sig, log_fail = _sb_logs(zc, scale, mask_of(r, zc.shape[0]))
                return (log_sig,) + _split_bf16(log_fail) + (_row_sum(log_fail),)

            def weights(r, lsc, runc, laterc):
                w = jnp.exp(lsc + runc + _wide(laterc, tk))
                return ((jnp.where(mask_of(r, w.shape[0]), w, 0.0) if masked else w).astype(BF16),)

            zs = [_dot_nt(qhs[i], _pair(ks_all, i // 2)) for i in heads]
            first = [_by_rows(logs, 4, zs[i]) for i in heads]
            runs = [_dot_nn(first[i][1], after) + _dot_nn(first[i][2], after) for i in heads]
            ws = [_by_rows(weights, 1, first[i][0], runs[i], carry[i][0])[0] for i in heads]
            pvs = [_dot_nn(ws[i], _pair(vs_all, i // 2)) for i in heads]
            return tuple((carry[i][0] + first[i][3], carry[i][1] + pvs[i]) for i in heads)

        init = (jnp.zeros((tq, LANES), F32), jnp.zeros((tq, LANES), F32))
        res = _two_loops((qi * tq) // tk, nkb, True, step, (init,) * (2 * pp))
        for pr in range(pp):
            (tot0, acc0), (tot1, acc1) = res[2 * pr], res[2 * pr + 1]
            o_ref[:, pr * LANES:(pr + 1) * LANES] = jnp.where(masks[0], acc0, acc1)
            tot_ref[:, pr * LANES:(pr + 1) * LANES] = jnp.where(masks[0], tot0, tot1)

    ng = nhp // pp
    blk = pl.BlockSpec((tq, wide), lambda h, i: (i, h))
    shape = jax.ShapeDtypeStruct((s, nhp * LANES), F32)
    return _hosted(name, body, (ng, nq), [p, p, p],
                   [blk, pl.BlockSpec((s, wide), lambda h, i: (0, ng + h)),
                    pl.BlockSpec((s, wide), lambda h, i: (0, 2 * ng + h))], [shape, shape], [blk, blk], comm)


def _sb_bwd(name, p, tot, do, nhp, tq, tk, comm=None):
    s = p.shape[0]
    scale = HEAD_DIM ** -0.5
    nq = s // tq
    pp = _pairs_per_step(nhp)
    wide = pp * LANES

    def body(q_ref, k_ref, v_ref, tot_ref, do_ref, dq_ref, dk_ref, dv_ref):
        qi = pl.program_id(1)

        @pl.when(qi == 0)
        def _():
            dk_ref[...] = jnp.zeros_like(dk_ref)
            dv_ref[...] = jnp.zeros_like(dv_ref)

        masks = _head_masks()
        upto, before = _tri(tk, "upto"), _tri(tk, "before")
        nkb = ((qi + 1) * tq + tk - 1) // tk
        q = q_ref[...]
        qbs = q.astype(BF16)
        dout = do_ref[...]
        doutbs = dout.astype(BF16)
        qhs = [jnp.where(hm, _pair(q, pr), 0.0).astype(BF16) for pr in range(pp) for hm in masks]
        dohs = [jnp.where(hm, _pair(dout, pr), 0.0).astype(BF16) for pr in range(pp) for hm in masks]
        tot = tot_ref[...]
        totals = [jnp.broadcast_to(tot[:, h * HEAD_DIM:h * HEAD_DIM + 1], (tq, LANES)) for h in range(2 * pp)]

        def step(kb, masked, carry):
            ks = pl.multiple_of(kb * tk, tk)
            ks_all = k_ref[pl.ds(ks, tk), :].astype(BF16)
            vs_all = v_ref[pl.ds(ks, tk), :].astype(BF16)
            mask_of = lambda r, n: _causal(qi * tq + r, ks, n, tk, True) if masked else None
            heads = range(len(qhs))

            def logs(r, zc):
                _, log_sig, log_fail = _sb_logs(zc, scale, mask_of(r, zc.shape[0]))
                return (log_sig,) + _split_bf16(log_fail) + (_row_sum(log_fail),)

            def weights(r, lsc, runc, basec, dwc):
                w = jnp.exp(lsc + (_wide(basec, tk) - runc))
                if masked:
                    w = jnp.where(mask_of(r, w.shape[0]), w, 0.0)
                g = w * dwc
                return (w.astype(BF16), g) + _split_bf16(g) + (_row_sum(g),)

            def dscore(r, gc, lsc, zc, grc, gbc):
                dz = gc * jnp.exp(lsc - zc * scale) - jnp.exp(lsc) * (_wide(gbc, tk) + grc)
                if masked:
                    dz = jnp.where(mask_of(r, dz.shape[0]), dz, 0.0)
                return ((dz * scale).astype(BF16),)

            zs = [_dot_nt(qhs[i], _pair(ks_all, i // 2)) for i in heads]
            dws = [_dot_nt(dohs[i], _pair(vs_all, i // 2)) for i in heads]
            first = [_by_rows(logs, 4, zs[i]) for i in heads]
            runs = [_dot_nn(first[i][1], upto) + _dot_nn(first[i][2], upto) for i in heads]
            second = [_by_rows(weights, 5, first[i][0], runs[i], totals[i] - carry[i][0], dws[i])
                      for i in heads]
            g_runs = [_dot_nn(second[i][2], before) + _dot_nn(second[i][3], before) for i in heads]
            dzs = [_by_rows(dscore, 1, second[i][1], first[i][0], zs[i], g_runs[i], carry[i][1])[0] for i in heads]
            dks = [_dot_tn(dzs[i], _pair(qbs, i // 2)) for i in heads]
            dvs = [_dot_tn(second[i][0], _pair(doutbs, i // 2)) for i in heads]
            dqs = [_dot_nn(dzs[i], _pair(ks_all, i // 2)) for i in heads]
            for pr in range(pp):
                cols = slice(pr * LANES, (pr + 1) * LANES)
                dk_ref[pl.ds(ks, tk), cols] += jnp.where(masks[0], dks[2 * pr], dks[2 * pr + 1])
                dv_ref[pl.ds(ks, tk), cols] += jnp.where(masks[0], dvs[2 * pr], dvs[2 * pr + 1])
            return tuple((carry[i][0] + first[i][3], carry[i][1] + second[i][4], carry[i][2] + dqs[i]) for i in heads)

        zero = jnp.zeros((tq, LANES), F32)
        res = _two_loops((qi * tq) // tk, nkb, False, step, ((zero, zero, zero),) * (2 * pp))
        for pr in range(pp):
            dq_ref[:, pr * LANES:(pr + 1) * LANES] = jnp.where(masks[0], res[2 * pr][2], res[2 * pr + 1][2])

    ng = nhp // pp
    blk = pl.BlockSpec((tq, wide), lambda h, i: (i, h))
    full = pl.BlockSpec((s, wide), lambda h, i: (0, h))
    shape = jax.ShapeDtypeStruct((s, nhp * LANES), F32)
    return _hosted(name, body, (ng, nq), [p, p, p, tot, do],
                   [blk, pl.BlockSpec((s, wide), lambda h, i: (0, ng + h)),
                    pl.BlockSpec((s, wide), lambda h, i: (0, 2 * ng + h)), blk, blk],
                   [shape, shape, shape], [blk, full, full], comm)


def _mla_fwd(name, q, k, v, tq, tk, comm=None):
    s = q.shape[0]
    nhp = v.shape[1] // LANES
    scale = MLA_QK_DIM ** -0.5
    nq = s // tq
    pp = _pairs_per_step(nhp)

    def body(q_ref, k_ref, v_ref, o_ref, lse_ref):
        qi = pl.program_id(1)
        masks = _head_masks()
        nkb = ((qi + 1) * tq + tk - 1) // tk
        qhs = [q_ref[:, h * LANES:(h + 1) * LANES] for h in range(2 * pp)]

        def step(kb, masked, carry):
            ks = pl.multiple_of(kb * tk, tk)
            heads = range(len(qhs))

            def soft(r, zc, mc, lc):
                zc = zc * scale
                if masked:
                    zc = jnp.where(_causal(qi * tq + r, ks, zc.shape[0], tk, False), zc, -1e30)
                m_new = jnp.maximum(mc, jnp.max(zc, axis=1, keepdims=True))
                a = jnp.exp(mc - m_new)
                pr = jnp.exp(zc - _wide(m_new, tk))
                return pr.astype(BF16), m_new, a * lc + _row_sum(pr), a

            zs = [_dot_nt(qhs[h], k_ref[pl.ds(ks, tk), h * LANES:(h + 1) * LANES]) for h in heads]
            first = [_by_rows(soft, 4, zs[h], carry[h][0], carry[h][1]) for h in heads]
            pvs = [_dot_nn(first[h][0], v_ref[pl.ds(ks, tk), (h // 2) * LANES:(h // 2 + 1) * LANES]) for h in heads]
            accs = [_by_rows(lambda r, ac, aa, pc: (aa * ac + pc,), 1, carry[h][2], first[h][3], pvs[h])[0]
                    for h in heads]
            return tuple((first[h][1], first[h][2], accs[h]) for h in heads)

        init = (jnp.full((tq, LANES), -1e30, F32), jnp.zeros((tq, LANES), F32), jnp.zeros((tq, LANES), F32))
        res = _two_loops((qi * tq) // tk, nkb, False, step, (init,) * (2 * pp))
        for pr in range(pp):
            (m0, l0, acc0), (m1, l1, acc1) = res[2 * pr], res[2 * pr + 1]
            o_ref[:, pr * LANES:(pr + 1) * LANES] = jnp.where(masks[0], acc0 / l0, acc1 / l1)
            lse_ref[:, pr * LANES:(pr + 1) * LANES] = jnp.where(masks[0], m0 + jnp.log(l0), m1 + jnp.log(l1))

    shape = jax.ShapeDtypeStruct((s, nhp * LANES), F32)
    blk = pl.BlockSpec((tq, pp * LANES), lambda h, i: (i, h))
    return _hosted(name, body, (nhp // pp, nq), [q, k, v],
                   [pl.BlockSpec((tq, 2 * pp * LANES), lambda h, i: (i, h)),
                    pl.BlockSpec((s, 2 * pp * LANES), lambda h, i: (0, h)),
                    pl.BlockSpec((s, pp * LANES), lambda h, i: (0, h))], [shape, shape], [blk, blk], comm)


def _mla_bwd(name, q, k, v, o, lse, do, tq, tk, comm=None):
    s = q.shape[0]
    nhp = v.shape[1] // LANES
    scale = MLA_QK_DIM ** -0.5
    nq = s // tq
    pp = _pairs_per_step(nhp)

    def body(q_ref, k_ref, v_ref, o_ref, lse_ref, do_ref, dq_ref, dk_ref, dv_ref):
        qi = pl.program_id(1)

        @pl.when(qi == 0)
        def _():
            dk_ref[...] = jnp.zeros_like(dk_ref)
            dv_ref[...] = jnp.zeros_like(dv_ref)

        masks = _head_masks()
        nkb = ((qi + 1) * tq + tk - 1) // tk
        dout = do_ref[...]
        doutbs = dout.astype(BF16)
        prod = dout * o_ref[...]
        qhs = [q_ref[:, h * LANES:(h + 1) * LANES] for h in range(2 * pp)]
        dohs = [jnp.where(hm, _pair(dout, pr), 0.0).astype(BF16) for pr in range(pp) for hm in masks]
        totals = [_row_sum(jnp.where(hm, _pair(prod, pr), 0.0)) for pr in range(pp) for hm in masks]
        lse = lse_ref[...]
        lses = [jnp.broadcast_to(lse[:, h * HEAD_DIM:h * HEAD_DIM + 1], (tq, LANES)) for h in range(2 * pp)]

        def step(kb, masked, carry):
            ks = pl.multiple_of(kb * tk, tk)
            heads = range(len(qhs))

            def probs(r, zc, dpc, lsec, totc):
                pr = jnp.exp(zc * scale - _wide(lsec, tk))
                if masked:
                    pr = jnp.where(_causal(qi * tq + r, ks, pr.shape[0], tk, False), pr, 0.0)
                return pr.astype(BF16), (pr * (dpc - _wide(totc, tk)) * scale).astype(BF16)

            khs = [k_ref[pl.ds(ks, tk), h * LANES:(h + 1) * LANES] for h in heads]
            vvs = [v_ref[pl.ds(ks, tk), (h // 2) * LANES:(h // 2 + 1) * LANES] for h in heads]
            zs = [_dot_nt(qhs[h], khs[h]) for h in heads]
            dps = [_dot_nt(dohs[h], vvs[h]) for h in heads]
            both = [_by_rows(probs, 2, zs[h], dps[h], lses[h], totals[h]) for h in heads]
            dks = [_dot_tn(both[h][1], qhs[h]) for h in heads]
            dvs = [_dot_tn(both[h][0], _pair(doutbs, h // 2)) for h in heads]
            dqs = [_dot_nn(both[h][1], khs[h]) for h in heads]
            for h in heads:
                dk_ref[pl.ds(ks, tk), h * LANES:(h + 1) * LANES] += dks[h]
            for pr in range(pp):
                dv_ref[pl.ds(ks, tk), pr * LANES:(pr + 1) * LANES] += jnp.where(masks[0], dvs[2 * pr], dvs[2 * pr + 1])
            return tuple(carry[h] + dqs[h] for h in heads)

        zero = jnp.zeros((tq, LANES), F32)
        dqs = _two_loops((qi * tq) // tk, nkb, False, step, (zero,) * (2 * pp))
        for h in range(2 * pp):
            dq_ref[:, h * LANES:(h + 1) * LANES] = dqs[h]

    blk = pl.BlockSpec((tq, pp * LANES), lambda h, i: (i, h))
    blk2 = pl.BlockSpec((tq, 2 * pp * LANES), lambda h, i: (i, h))
    full = pl.BlockSpec((s, pp * LANES), lambda h, i: (0, h))
    full2 = pl.BlockSpec((s, 2 * pp * LANES), lambda h, i: (0, h))
    return _hosted(name, body, (nhp // pp, nq), [q, k, v, o, lse, do], [blk2, full2, full, blk, blk, blk],
                   [jax.ShapeDtypeStruct(q.shape, F32), jax.ShapeDtypeStruct(k.shape, F32),
                    jax.ShapeDtypeStruct(v.shape, F32)], [blk2, full2, full], comm)


def _norm_parts(x):
    r = lax.rsqrt(jnp.mean(x * x, axis=-1, keepdims=True) + NORM_EPS)
    return r, x * r


def _rmsmod_fwd(x, g, sc, sh):
    _, xh = _norm_parts(x)
    return ((xh * g) * (1.0 + sc) + sh,)


def _rmsmod_bwd(dh, x, dres, g, sc):
    r, xh = _norm_parts(x)
    dy = dh * (1.0 + sc)
    dxh = dy * g
    dx = r * (dxh - xh * jnp.mean(dxh * xh, axis=-1, keepdims=True)) + dres
    return dx, dh, dh * (xh * g), dy * xh


def _rms_bwd_plain(dh, x, g):
    r, xh = _norm_parts(x)
    dxh = dh * g
    return r * (dxh - xh * jnp.mean(dxh * xh, axis=-1, keepdims=True)), dh * xh


def _cat(parts):
    return jnp.concatenate(parts, axis=1)


def _swap_halves(a):
    half = a.shape[-1] // 2
    return jnp.concatenate([a[..., half:], a[..., :half]], axis=-1)


def _adamw_fn(w, g, m, v):
    m = ADAM_B1 * m + (1.0 - ADAM_B1) * g
    v = ADAM_B2 * v + (1.0 - ADAM_B2) * jnp.square(g)
    m_hat = m / (1.0 - ADAM_B1 ** ADAM_STEP)
    v_hat = v / (1.0 - ADAM_B2 ** ADAM_STEP)
    delta = -ADAM_LR * (m_hat / (jnp.sqrt(v_hat) + ADAM_EPS) + ADAM_WD * w)
    return delta, m, v


def _adamw(name, w, g, m, v):
    shape = w.shape
    width = shape[-1]
    flat = [t.reshape(-1, width) for t in (w, g, m, v)]
    res = _rowwise(name, _adamw_fn, flat, [], [(width, F32)] * 3)
    return [t.reshape(shape) for t in res]


def _sum_adamw(name, land, w, m, v):
    shape = w.shape
    width = shape[-1]
    rows = w.size // width

    def fn(*blocks):
        g = blocks[0].astype(F32)
        for b in blocks[1:NDEV]:
            g = g + b.astype(F32)
        return (g,) + _adamw_fn(blocks[NDEV], g, blocks[NDEV + 1], blocks[NDEV + 2])

    def fn_whole(wb, mb, vb, lb):
        return fn(*[lb[i] for i in range(NDEV)], wb, mb, vb)

    flat = [t.reshape(rows, width) for t in (w, m, v)]
    if rows % 16 == 0:
        views = [(land.reshape(NDEV * rows, width), width, 0, i * rows) for i in range(NDEV)]
        res = _rowwise(name, fn, views + flat, [], [(width, F32)] * 4)
    else:
        res = _rowwise(name, fn_whole, flat, [land.reshape(NDEV, rows, width)], [(width, F32)] * 4)
    return [t.reshape(shape) for t in res]


def kernel(x, c, positions, w_ada, b_ada, g_mix_norm, w_in, g_q_lat, w_q_up, g_kv_lat, w_kv_up, w_sb_out, w_mla_out, w_mix_out, g_mlp_norm, w_up, w_down, g_final, loss_target, m_w_ada, m_b_ada, m_g_mix_norm, m_w_in, m_g_q_lat, m_w_q_up, m_g_kv_lat, m_w_kv_up, m_w_sb_out, m_w_mla_out, m_w_mix_out, m_g_mlp_norm, m_w_up, m_w_down, m_g_final, v_w_ada, v_b_ada, v_g_mix_norm, v_w_in, v_g_q_lat, v_w_q_up, v_g_kv_lat, v_w_kv_up, v_w_sb_out, v_w_mla_out, v_w_mix_out, v_g_mlp_norm, v_w_up, v_w_down, v_g_final):
    seq, d = x.shape[1], x.shape[2]
    depth = w_ada.shape[0]
    qr, kvr = g_q_lat.shape[1], g_kv_lat.shape[1]
    sbw, mlaw = w_sb_out.shape[1], w_mla_out.shape[1]
    nh = mlaw // HEAD_DIM
    nhp_sb = sbw // LANES
    dff = w_up.shape[2] * NDEV
    ada_n = w_ada.shape[2]
    gb = min(512, d)
    tq, tk = min(256, seq), min(256, seq)
    me = 4 * lax.axis_index("x") + 2 * lax.axis_index("y") + lax.axis_index("c")

    o_qlat = _roundup(3 * sbw, qr)
    o_kvlat = _roundup(o_qlat + qr, kvr)
    o_rope = _roundup(o_kvlat + kvr, 2 * LANES)
    o_gate = _roundup(o_rope + 2 * LANES, gb)
    wp = o_gate + 2 * d

    c_all = _exchange("ag_c", _Comm("gather_all", [c.reshape(d // LANES, LANES)]))[0].reshape(NDEV, d)
    c_act = _rowwise("silu_c", lambda t: (t * (1.0 / (1.0 + jnp.exp(-t))),), [c_all], [], [(d, F32)])[0]
    parts = jnp.stack([_matmul("ada_fwd", c_act, w_ada[l], "nn") for l in range(depth)])
    parts_all = _exchange("ag_mod", _Comm("gather_all", [parts]))[0]
    mine = jnp.transpose(lax.dynamic_index_in_dim(parts_all, me, axis=2, keepdims=False), (1, 0, 2))
    mod = _rowwise("mod_bias", lambda a, b: (a + b,), [mine.reshape(depth, NDEV * ada_n), b_ada], [],
                   [(6 * d, F32)])[0]
    mods = [[mod[l:l + 1, i * d:(i + 1) * d] for i in range(6)] for l in range(depth)]

    big = [w_in, w_q_up, w_kv_up, w_sb_out, w_mla_out, w_mix_out, w_up, w_down]
    row_sharded = [False, False, False, False, False, True, False, True]
    shards = [[w[l].astype(BF16) for w in big] for l in range(depth)]

    ids_a, ids_b = [0, 1, 2, 3, 4, 5], [6, 7]
    pick = lambda l, ids: [shards[l][i] for i in ids]

    def unpack(gathered, ids):
        out = []
        for g, i in zip(gathered, ids):
            _, rows, cols = g.shape
            if i == 0:
                out.append(g)
            elif row_sharded[i]:
                out.append(g.reshape(NDEV * rows, cols))
            else:
                out.append(jnp.transpose(g, (1, 0, 2)).reshape(rows, NDEV * cols))
        return out

    n_in = w_in.shape[2]
    r0 = 3 * sbw + qr + kvr
    g0 = r0 + ROPE_DIM
    runs = [(0, 3 * sbw, 0), (3 * sbw, 3 * sbw + qr, o_qlat), (3 * sbw + qr, r0, o_kvlat), (g0, g0 + 2 * d, o_gate)]

    def shard_cols(g, a, b):
        return [g[j][:, max(a, n_in * j) - n_in * j:min(b, n_in * (j + 1)) - n_in * j]
                for j in range(a // n_in, (b - 1) // n_in + 1)]

    def derive(full):
        wi, wq, wkv, wsb, wmla, wmix = full
        dt = wi.dtype
        z = lambda r, n: jnp.zeros((r, n), dt)
        kr = _cat(shard_cols(wi, r0, g0))
        pieces, at = [], 0
        for a, b, start in runs[:3]:
            pieces += [z(d, start - at)] + shard_cols(wi, a, b)
            at = start + b - a
        pieces += [z(d, o_rope - at), z(d, HEAD_DIM), kr, z(d, LANES - MLA_QK_DIM),
                   z(d, HEAD_DIM), _swap_halves(kr), z(d, LANES - MLA_QK_DIM),
                   z(d, o_gate - o_rope - 2 * LANES)] + shard_cols(wi, g0, g0 + 2 * d)
        w_in_pad = _cat([t for t in pieces if t.shape[1]])
        wq3 = wq.reshape(qr, nh, MLA_QK_DIM)
        z3 = lambda n: jnp.zeros((qr, nh, n), dt)
        rope_w = wq3[:, :, HEAD_DIM:]
        wq_a = jnp.concatenate([wq3[:, :, :HEAD_DIM], rope_w, z3(LANES - MLA_QK_DIM)], axis=2).reshape(qr, nh * LANES)
        wq_b = jnp.concatenate([z3(HEAD_DIM), _swap_halves(rope_w), z3(LANES - MLA_QK_DIM)], axis=2).reshape(qr, nh * LANES)
        wkv3 = wkv.reshape(kvr, nh, 2 * HEAD_DIM)
        wk = jnp.concatenate([wkv3[:, :, :HEAD_DIM], jnp.zeros((kvr, nh, HEAD_DIM), dt)], axis=2).reshape(kvr, nh * LANES)
        wv = wkv3[:, :, HEAD_DIM:].reshape(kvr, nh * HEAD_DIM)
        return dict(w_in=w_in_pad, w_q=_cat([wq_a, wq_b]), w_kv=_cat([wk, wv]), w_sb=wsb, w_mla=wmla, w_mix=wmix)

    def fold(gr):
        gi, gq, gkv = gr["w_in"], gr["w_q"], gr["w_kv"]
        ra = gi[:, o_rope + HEAD_DIM:o_rope + MLA_QK_DIM]
        rb = gi[:, o_rope + LANES + HEAD_DIM:o_rope + LANES + MLA_QK_DIM]
        rope = (ra.astype(F32) + _swap_halves(rb).astype(F32)).astype(gi.dtype)

        def cols(a, b):
            out = []
            for s0, s1, start in runs[:3] + [(r0, g0, None)] + runs[3:]:
                lo, hi = max(a, s0), min(b, s1)
                if lo < hi:
                    out.append(rope[:, lo - r0:hi - r0] if start is None else gi[:, start + lo - s0:start + hi - s0])
            return out

        g_in = jnp.stack([_cat(cols(n_in * j, n_in * (j + 1))) for j in range(NDEV)]).astype(BF16)
        ga = gq[:, :nh * LANES].reshape(qr, nh, LANES)
        gb_ = gq[:, nh * LANES:].reshape(qr, nh, LANES)
        g_q = jnp.concatenate([ga[:, :, :HEAD_DIM], ga[:, :, HEAD_DIM:MLA_QK_DIM]
                               + _swap_halves(gb_[:, :, HEAD_DIM:MLA_QK_DIM])], axis=2).reshape(qr, nh * MLA_QK_DIM)
        gk = gkv[:, :nh * LANES].reshape(kvr, nh, LANES)[:, :, :HEAD_DIM]
        gv = gkv[:, nh * LANES:].reshape(kvr, nh, HEAD_DIM)
        g_kv = jnp.concatenate([gk, gv], axis=2).reshape(kvr, nh * 2 * HEAD_DIM)
        return [g_in, g_q, g_kv, gr["w_sb"], gr["w_mla"], gr["w_mix"], gr["w_up"], gr["w_down"]]

    part_a = _exchange("ag_w0_own", _Comm("gather_own", pick(0, ids_a)))
    ready_a = _exchange("ag_w0_fwd", _Comm("gather_fwd", [], lands=part_a))
    part_b, weights = None, []

    inv_freq = 1.0 / (ROPE_THETA ** (jnp.arange(0, ROPE_DIM, 2, dtype=F32) / ROPE_DIM))
    ang = positions[0].astype(F32)[:, None] * inv_freq
    cos, sin = jnp.cos(ang), jnp.sin(ang)
    tail = jnp.zeros((seq, LANES - MLA_QK_DIM), F32)
    rope_c = _cat([jnp.ones((seq, HEAD_DIM), F32), cos, cos, tail])
    rope_s = _cat([jnp.zeros((seq, HEAD_DIM), F32), -sin, sin, tail])
    zero_vec = lambda n: jnp.zeros((1, n), F32)

    def rope_fwd(q2, kvs, pd, tc, ts):
        c8, s8 = _cat([tc] * nh), _cat([ts] * nh)
        qf = q2[:, :nh * LANES] * c8 + q2[:, nh * LANES:] * s8
        kpe = pd[:, :LANES] * tc + pd[:, LANES:] * ts
        return qf, kvs[:, :nh * LANES] + _cat([kpe] * nh), kvs[:, nh * LANES:]

    def rope_bwd(dq, dk, dv, tc, ts):
        c8, s8 = _cat([tc] * nh), _cat([ts] * nh)
        dks = dk[:, :LANES]
        for h in range(1, nh):
            dks = dks + dk[:, h * LANES:(h + 1) * LANES]
        return _cat([dq * c8, dq * s8]), _cat([dk, dv]), _cat([dks * tc, dks * ts])

    def merge_fwd(gs, gm, osb, omla):
        return osb / (1.0 + jnp.exp(-gs)) + omla / (1.0 + jnp.exp(-gm))

    def merge_bwd(dm, gs, gm, osb, omla):
        ss, sm = 1.0 / (1.0 + jnp.exp(-gs)), 1.0 / (1.0 + jnp.exp(-gm))
        return ss * dm, sm * dm, dm * osb * ss * (1.0 - ss), dm * omla * sm * (1.0 - sm)

    def gates_of(p):
        return [(p, o_gate), (p, o_gate + d)]

    xs = x[0]
    saved = []
    for l in range(depth):
        w = derive(unpack(ready_a, ids_a))
        sh1, sc1, g1, sh2, sc2, g2 = mods[l]
        h1 = _rowwise("norm1", _rmsmod_fwd, [xs], [g_mix_norm[l:l + 1], sc1, sh1], [(d, BF16)])[0]
        p = _matmul("in_proj", h1, w["w_in"], "nn")
        (o_sb, tot_sb), part_b = _sb_fwd("sb_fwd", p, nhp_sb, tq, tk, _Comm("gather_own", pick(l, ids_b)))
        y_sb = _matmul("sb_out", o_sb, w["w_sb"], "nn")
        qn = _rowwise("norm_q", _rmsmod_fwd, [(p, qr, o_qlat // qr, 0)],
                      [g_q_lat[l:l + 1], zero_vec(qr), zero_vec(qr)], [(qr, BF16)])[0]
        kvn = _rowwise("norm_kv", _rmsmod_fwd, [(p, kvr, o_kvlat // kvr, 0)],
                       [g_kv_lat[l:l + 1], zero_vec(kvr), zero_vec(kvr)], [(kvr, BF16)])[0]
        q2 = _matmul("q_up", qn, w["w_q"], "nn")
        kvs = _matmul("kv_up", kvn, w["w_kv"], "nn")
        qf, kf, vf = _rowwise("rope_fwd", rope_fwd, [q2, kvs, (p, 2 * LANES, o_rope // (2 * LANES), 0), rope_c, rope_s],
                              [], [(nh * LANES, BF16), (nh * LANES, BF16), (mlaw, BF16)])
        comms = [_Comm("gather_fwd", [], lands=part_b)]
        if l + 1 < depth:
            comms.append(_Comm("gather_own", pick(l + 1, ids_a)))
        group = _CommGroup(comms)
        (o_mla, lse), got = _mla_fwd("mla_fwd", qf, kf, vf, tq, tk, group)
        got = group.split(got)
        w["w_up"], w["w_down"] = unpack(got[0], ids_b)
        weights.append(w)
        y_mla, merged = _matmul("mla_out", o_mla, w["w_mla"], "nn", outs=[F32, BF16], rows=gates_of(p) + [y_sb],
                                epilogue=lambda acc, gs, gm, osb: (acc, merge_fwd(gs, gm, osb, acc)))
        resid = lambda acc, xv, g: (acc, xv + g * acc)
        y1, x_mid = _matmul("mix_out", merged, w["w_mix"], "nn", epilogue=resid, rows=[xs], vecs=[g1], outs=[F32, F32])
        h2 = _rowwise("norm2", _rmsmod_fwd, [x_mid], [g_mlp_norm[l:l + 1], sc2, sh2], [(d, BF16)])[0]
        relu2 = lambda acc: (acc, jnp.square(jnp.maximum(acc, 0.0)))
        if l + 1 < depth:
            (u, act), ready_a = _matmul("mlp_up", h2, w["w_up"], "nn", outs=[F32, BF16], epilogue=relu2,
                                        comm=_Comm("gather_fwd", [], lands=got[1]))
        else:
            u, act = _matmul("mlp_up", h2, w["w_up"], "nn", outs=[F32, BF16], epilogue=relu2)
        y2, x_out = _matmul("mlp_down", act, w["w_down"], "nn", epilogue=resid, rows=[x_mid], vecs=[g2], outs=[F32, F32])
        saved.append(dict(x=xs, h1=h1, p=p, o_sb=o_sb, tot_sb=tot_sb, y_sb=y_sb, qn=qn, kvn=kvn, qf=qf, kf=kf, vf=vf, o_mla=o_mla,
                          lse=lse, y_mla=y_mla, merged=merged, y1=y1, x_mid=x_mid, h2=h2, u=u, act=act, y2=y2))
        xs = x_out

    def final_fn(xv, tv, y2, g, gate):
        r, xh = _norm_parts(xv)
        diff = xh * g - tv
        dy = diff * (1.0 / d)
        dxh = dy * g
        dx = r * (dxh - xh * jnp.mean(dxh * xh, axis=-1, keepdims=True))
        return dx, dx * gate, diff * diff, dy * xh, dx * y2

    dx, dy2, sq, dg_final, dgate2 = _rowwise(
        "loss_head", final_fn, [xs, loss_target[0], saved[-1]["y2"]], [g_final.reshape(1, d), mods[-1][5]],
        [(d, F32), (d, BF16)], reds=[d, d, d])
    loss = lax.psum(0.5 * jnp.sum(sq) / d, ("x", "y", "c"))

    def norm2_bwd_fn(dh, xv, dres, y1, g, sc, gate):
        dx_mid, dsh, dsc, dg = _rmsmod_bwd(dh, xv, dres, g, sc)
        return dx_mid, dx_mid * gate, dsh, dsc, dg, dx_mid * y1

    def norm1_bwd_fn(dh, xv, dres, y2, g, sc, gate):
        dxv, dsh, dsc, dg = _rmsmod_bwd(dh, xv, dres, g, sc)
        return dxv, dxv * gate, dsh, dsc, dg, dxv * y2

    def chunk(gfull, wref, by_rows):
        rows, cols = wref.shape[1], wref.shape[2]
        if by_rows:
            return gfull.reshape(NDEV, rows, cols).astype(BF16)
        return jnp.transpose(gfull.reshape(rows, NDEV, cols), (1, 0, 2)).astype(BF16)

    dmods, small = [None] * depth, [None] * depth
    late, lands = None, [None] * len(big)
    for l in reversed(range(depth)):
        w, sv = weights[l], saved[l]
        sh1, sc1, g1, sh2, sc2, g2 = mods[l]
        gr = {}
        du = _matmul("mlp_down_dx", dy2, w["w_down"], "nt", outs=[BF16], rows=[sv["u"]],
                     epilogue=lambda acc, uv: (acc * 2.0 * jnp.maximum(uv, 0.0),))
        gr["w_down"] = _matmul("mlp_down_dw", sv["act"], dy2, "tn", BF16)
        dh2 = _matmul("mlp_up_dx", du, w["w_up"], "nt")
        gr["w_up"] = _matmul("mlp_up_dw", sv["h2"], du, "tn", BF16, owner_cols=w_up.shape[2])
        dx_mid, dy1, dsh2, dsc2, dg_mlp, dgate1 = _rowwise(
            "norm2_bwd", norm2_bwd_fn, [dh2, sv["x_mid"], dx, sv["y1"]], [g_mlp_norm[l:l + 1], sc2, g1],
            [(d, F32), (d, BF16)], reds=[d, d, d, d])
        dy_sb, dy_mla, dgate_sb, dgate_mla = _matmul(
            "mix_out_dx", dy1, w["w_mix"], "nt", outs=[BF16] * 4, epilogue=merge_bwd,
            rows=gates_of(sv["p"]) + [sv["y_sb"], sv["y_mla"]])
        gr["w_mix"] = _matmul("mix_out_dw", sv["merged"], dy1, "tn", BF16)
        do_sb = _matmul("sb_out_dx", dy_sb, w["w_sb"], "nt")
        gr["w_sb"] = _matmul("sb_out_dw", sv["o_sb"], dy_sb, "tn", BF16)
        do_mla = _matmul("mla_out_dx", dy_mla, w["w_mla"], "nt")
        gr["w_mla"] = _matmul("mla_out_dw", sv["o_mla"], dy_mla, "tn", BF16)
        ready = {3: gr["w_sb"], 4: gr["w_mla"], 5: gr["w_mix"], 7: gr["w_down"]}
        ready = {i: chunk(g, big[i], row_sharded[i]) for i, g in ready.items()}
        ready[6] = gr["w_up"]
        ids_a = [6] + ([0, 1, 2] if late is not None else [])
        comm_a = _Comm("scatter", [ready[6]] + (late or []), [lands[i] for i in ids_a], [l] + [l + 1] * 3, depth)
        comm_b = _Comm("scatter", [ready[7]], [lands[7]], [l], depth)
        (dq_sb, dk_sb, dv_sb), got_a = _sb_bwd("sb_bwd", sv["p"], sv["tot_sb"], do_sb, nhp_sb, tq, tk, comm_a)
        (dqf, dkf, dvf), got_b = _mla_bwd("mla_bwd", sv["qf"], sv["kf"], sv["vf"], sv["o_mla"], sv["lse"], do_mla,
                                          tq, tk, comm_b)
        for i, t in zip(ids_a + [7], list(got_a) + list(got_b)):
            lands[i] = t
        dq2, dkvs, drope = _rowwise("rope_bwd", rope_bwd, [dqf, dkf, dvf, rope_c, rope_s], [],
                                    [(2 * nh * LANES, BF16), (nh * LANES + mlaw, BF16), (2 * LANES, BF16)])
        dqn = _matmul("q_up_dx", dq2, w["w_q"], "nt")
        gr["w_q"] = _matmul("q_up_dw", sv["qn"], dq2, "tn")
        dkvn = _matmul("kv_up_dx", dkvs, w["w_kv"], "nt")
        gr["w_kv"] = _matmul("kv_up_dw", sv["kvn"], dkvs, "tn")
        dqlat, dg_q = _rowwise("norm_q_bwd", _rms_bwd_plain, [dqn, (sv["p"], qr, o_qlat // qr, 0)],
                               [g_q_lat[l:l + 1]], [(qr, BF16)], reds=[qr])
        dkvlat, dg_kv = _rowwise("norm_kv_bwd", _rms_bwd_plain, [dkvn, (sv["p"], kvr, o_kvlat // kvr, 0)],
                                 [g_kv_lat[l:l + 1]], [(kvr, BF16)], reds=[kvr])
        zb = lambda n: jnp.zeros((seq, n), BF16)
        dp = _cat([dq_sb.astype(BF16), dk_sb.astype(BF16), dv_sb.astype(BF16), zb(o_qlat - 3 * sbw), dqlat,
                   zb(o_kvlat - o_qlat - qr), dkvlat, zb(o_rope - o_kvlat - kvr), drope,
                   zb(o_gate - o_rope - 2 * LANES), dgate_sb, dgate_mla])
        dh1, got = _matmul("in_proj_dx", dp, w["w_in"], "nt", comm=_Comm("scatter", [ready[5]], [lands[5]], [l], depth))
        lands[5] = got[0]
        gr["w_in"], got = _matmul("in_proj_dw", sv["h1"], dp, "tn", BF16,
                                  comm=_Comm("scatter", [ready[3], ready[4]], [lands[3], lands[4]], [l, l], depth))
        lands[3], lands[4] = got
        dmods[l] = [None, None, dgate1, dsh2, dsc2, dgate2]
        if l > 0:
            dx, dy2, dsh1, dsc1, dg_mix, dgate2 = _rowwise(
                "norm1_bwd", norm1_bwd_fn, [dh1, sv["x"], dx_mid, saved[l - 1]["y2"]],
                [g_mix_norm[l:l + 1], sc1, mods[l - 1][5]], [(d, F32), (d, BF16)], reds=[d, d, d, d])
        else:
            dx, dsh1, dsc1, dg_mix = _rowwise("norm1_bwd", _rmsmod_bwd, [dh1, sv["x"], dx_mid],
                                              [g_mix_norm[l:l + 1], sc1], [(d, F32)], reds=[d, d, d])
        dmods[l] = _cat([dsh1, dsc1] + dmods[l][2:])
        small[l] = (dg_mix, dg_q, dg_kv, dg_mlp)
        g_in, g_q, g_kv = fold(gr)[:3]
        late = [g_in, chunk(g_q, big[1], False), chunk(g_kv, big[2], False)]

    small_parts = [jnp.concatenate(dmods, axis=0)]
    small_parts += [jnp.concatenate([small[l][i] for l in range(depth)], axis=0) for i in range(4)]
    small_parts.append(dg_final)
    small_all = _exchange("ag_small", _Comm("gather_all", small_parts))

    dmod_mine = lax.dynamic_slice_in_dim(small_all[0], me * ada_n, ada_n, axis=2)
    c_act_t = jnp.transpose(c_act)

    def outer_fn(ct, dm):
        acc = ct[:, 0:1] * dm[0:1, :]
        for b in range(1, NDEV):
            acc = acc + ct[:, b:b + 1] * dm[b:b + 1, :]
        return (acc,)

    g_w_ada = jnp.stack([_rowwise("ada_dw", outer_fn, [c_act_t], [dmod_mine[:, l, :]], [(ada_n, F32)])[0]
                         for l in range(depth)])

    landed = list(_exchange("a2a_last", _Comm("scatter", late, lands[:3], [0] * 3, depth))) + lands[3:]

    moments = dict(
        w_ada=(w_ada, m_w_ada, v_w_ada), b_ada=(b_ada, m_b_ada, v_b_ada),
        g_mix_norm=(g_mix_norm, m_g_mix_norm, v_g_mix_norm), w_in=(w_in, m_w_in, v_w_in),
        g_q_lat=(g_q_lat, m_g_q_lat, v_g_q_lat), w_q_up=(w_q_up, m_w_q_up, v_w_q_up),
        g_kv_lat=(g_kv_lat, m_g_kv_lat, v_g_kv_lat), w_kv_up=(w_kv_up, m_w_kv_up, v_w_kv_up),
        w_sb_out=(w_sb_out, m_w_sb_out, v_w_sb_out), w_mla_out=(w_mla_out, m_w_mla_out, v_w_mla_out),
        w_mix_out=(w_mix_out, m_w_mix_out, v_w_mix_out), g_mlp_norm=(g_mlp_norm, m_g_mlp_norm, v_g_mlp_norm),
        w_up=(w_up, m_w_up, v_w_up), w_down=(w_down, m_w_down, v_w_down),
        g_final=(g_final.reshape(1, d), m_g_final.reshape(1, d), v_g_final.reshape(1, d)))
    lands = dict(b_ada=small_all[0], g_mix_norm=small_all[1], g_q_lat=small_all[2], g_kv_lat=small_all[3],
                 g_mlp_norm=small_all[4], g_final=small_all[5], w_in=landed[0], w_q_up=landed[1],
                 w_kv_up=landed[2], w_sb_out=landed[3], w_mla_out=landed[4], w_mix_out=landed[5], w_up=landed[6],
                 w_down=landed[7])
    gs, deltas, new_ms, new_vs = [], [], [], []
    for name, (wt, mt, vt) in moments.items():
        if name == "w_ada":
            res = [g_w_ada] + _adamw("adamw_" + name, wt, g_w_ada, mt, vt)
        else:
            res = _sum_adamw("adamw_" + name, lands[name], wt, mt, vt)
        if name == "g_final":
            res = [t.reshape(d) for t in res]
        for lst, t in zip((gs, deltas, new_ms, new_vs), res):
            lst.append(t)

    return (loss, dx[None], *gs, *deltas, *new_ms, *new_vs)
```

```python
import functools

import jax
import jax.numpy as jnp
from jax import lax
from jax.experimental import pallas as pl
from jax.experimental.pallas import tpu as pltpu

F32 = jnp.float32
BF16 = jnp.bfloat16
NDEV = 8
LANES = 128
HEAD_DIM = 64
ROPE_DIM = 32
MLA_QK_DIM = HEAD_DIM + ROPE_DIM
ROPE_THETA = 10000.0
NORM_EPS = 1e-6
ADAM_LR = 0.001
ADAM_B1 = 0.9
ADAM_B2 = 0.999
ADAM_EPS = 1e-08
ADAM_WD = 0.01
ADAM_STEP = 10
VMEM_LIMIT = 48 * 1024 * 1024


def _pcall(body, **kw):
    return pl.pallas_call(body, **kw)


def _tile(n, pref):
    for t in (512, 384, 256, 128, 64, 32, 16, 8):
        if t <= pref and n % t == 0:
            return t
    return n


def _roundup(n, m):
    return (n + m - 1) // m * m


_CP = pltpu.CompilerParams(vmem_limit_bytes=VMEM_LIMIT)


def _rowwise(name, fn, rows, vecs, outs, reds=(), tb=256, comm=None):
    rows = [r if isinstance(r, tuple) else (r, r.shape[1], 0, 0) for r in rows]
    nrows = None
    for arr, width, col, roff in rows:
        if roff == 0 and nrows is None:
            nrows = arr.shape[0]
    first_off = [r for r in rows if r[3] != 0]
    if first_off:
        nrows = min(nrows, first_off[0][3])
    tb = _tile(nrows, tb)
    nblk = nrows // tb
    n_in = len(rows) + len(vecs)
    n_out = len(outs)

    def body(*refs):
        vals = [r[...] for r in refs[:n_in]]
        res = fn(*vals)
        if not isinstance(res, (tuple, list)):
            res = (res,)
        for ref, val in zip(refs[n_in:n_in + n_out], res[:n_out]):
            ref[...] = val.astype(ref.dtype)
        for ref, val in zip(refs[n_in + n_out:], res[n_out:]):
            @pl.when(pl.program_id(0) == 0)
            def _(ref=ref):
                ref[...] = jnp.zeros_like(ref)
            ref[...] += jnp.sum(val.astype(F32), axis=0, keepdims=True)

    in_specs = []
    for arr, width, col, roff in rows:
        in_specs.append(pl.BlockSpec((tb, width), functools.partial(
            lambda i, col, rb: (rb + i, col), col=col, rb=roff // tb)))
    for v in vecs:
        in_specs.append(pl.BlockSpec(v.shape, lambda i, nd=v.ndim: (0,) * nd))
    out_specs = [pl.BlockSpec((tb, w), lambda i: (i, 0)) for w, _ in outs]
    out_specs += [pl.BlockSpec((1, w), lambda i: (0, 0)) for w in reds]
    out_shape = [jax.ShapeDtypeStruct((nrows, w), dt) for w, dt in outs]
    out_shape += [jax.ShapeDtypeStruct((1, w), F32) for w in reds]
    res, got = _hosted(name, body, (nblk,), [r[0] for r in rows] + list(vecs), in_specs, out_shape, out_specs, comm)
    return res if comm is None else (res, got)


_DIMS = {"nn": (((1,), (0,)), ((), ())), "nt": (((1,), (1,)), ((), ())), "tn": (((0,), (0,)), ((), ()))}


def _matmul(name, a, b, mode, out_dtype=F32, epilogue=None, rows=(), vecs=(), outs=None, comm=None, owner_cols=None):
    if mode == "nn":
        (m, k), n = a.shape, b.shape[1]
    elif mode == "nt":
        (m, k), n = a.shape, b.shape[0]
    else:
        (k, m), n = a.shape, b.shape[1]
    tm = _tile(m, 512)
    tn = owner_cols or next((t for t in (1536, 1024) if n % t == 0 and n > t), _tile(n, 512))
    dims = _DIMS[mode]
    outs = [out_dtype] if outs is None else outs
    n_extra = len(rows) + len(vecs)
    rows = [r if isinstance(r, tuple) else (r, 0) for r in rows]

    def body(a_ref, b_ref, *refs):
        acc = lax.dot_general(a_ref[...].astype(BF16), b_ref[...].astype(BF16), dims, preferred_element_type=F32)
        res = (acc,) if epilogue is None else epilogue(acc, *[r[...] for r in refs[:n_extra]])
        for o_ref, val in zip(refs[n_extra:], res):
            o_ref[...] = val.astype(o_ref.dtype)

    a_spec = pl.BlockSpec((k, tm), lambda j, i: (0, i)) if mode == "tn" else pl.BlockSpec((tm, k), lambda j, i: (i, 0))
    b_spec = pl.BlockSpec((tn, k), lambda j, i: (j, 0)) if mode == "nt" else pl.BlockSpec((k, tn), lambda j, i: (0, j))
    blk = pl.BlockSpec((tm, tn), lambda j, i: (i, j))
    assert all(off % tn == 0 for _, off in rows), (name, tn)
    row_specs = [pl.BlockSpec((tm, tn), functools.partial(lambda j, i, first: (i, first + j), first=off // tn))
                 for _, off in rows]
    out_blk, out_dims = blk, (m, n)
    if owner_cols:
        out_blk, out_dims = pl.BlockSpec((None, tm, tn), lambda j, i: (j, i, 0)), (n // tn, m, tn)
    res, got = _hosted(name, body, (n // tn, m // tm), [a, b, *[r for r, _ in rows], *vecs],
                       [a_spec, b_spec] + row_specs + [pl.BlockSpec((1, tn), lambda j, i: (0, j))] * len(vecs),
                       [jax.ShapeDtypeStruct(out_dims, dt) for dt in outs], [out_blk] * len(outs), comm)
    res = res[0] if len(outs) == 1 else res
    return res if comm is None else (res, got)


class _Comm:
    KS = {"gather_all": (1, 2, 3, 4, 5, 6, 7), "gather_own": (1, 2, 4, 6), "gather_fwd": (2, 4, 6),
          "scatter": (1, 2, 3, 4, 5, 6, 7)}

    def __init__(self, kind, srcs, lands=None, layers=None, depth=None, row0=None):
        self.kind, self.srcs, self.layers, self.row0 = kind, list(srcs), layers, row0
        self.n = len(lands) if kind == "gather_fwd" else len(srcs)
        self.lands = list(lands) if lands is not None else [None] * self.n
        self.out_shapes = []
        for i, land in enumerate(self.lands):
            if land is not None:
                self.out_shapes.append(jax.ShapeDtypeStruct(land.shape, land.dtype))
            elif kind == "scatter":
                self.out_shapes.append(jax.ShapeDtypeStruct((NDEV, depth) + srcs[i].shape[1:], srcs[i].dtype))
            else:
                self.out_shapes.append(jax.ShapeDtypeStruct((NDEV,) + srcs[i].shape, srcs[i].dtype))
        self.operands = self.srcs + [t for t in self.lands if t is not None]
        self.scratch = [pltpu.SemaphoreType.DMA((NDEV - 1, self.n)), pltpu.SemaphoreType.DMA((NDEV - 1, self.n)),
                        pltpu.SemaphoreType.DMA((self.n,))]

    def aliases(self, first_in, first_out):
        given = [i for i, t in enumerate(self.lands) if t is not None]
        return {first_in + len(self.srcs) + pos: first_out + i for pos, i in enumerate(given)}

    def copies(self, in_refs, out_refs, send_sems, recv_sems, local_sems):
        x, y, c = lax.axis_index("x"), lax.axis_index("y"), lax.axis_index("c")
        me = 4 * x + 2 * y + c
        def landing(i):
            if self.row0 is None or self.row0[i] is None:
                return out_refs[i].at[me, self.layers[i]]
            return out_refs[i].at[me, self.layers[i], pl.ds(self.row0[i], self.srcs[i].shape[1])]

        cps = []
        if self.kind != "gather_fwd":
            for i in range(self.n):
                src = in_refs[i].at[me] if self.kind == "scatter" else in_refs[i]
                dst = landing(i) if self.kind == "scatter" else out_refs[i].at[me]
                cps.append(pltpu.make_async_copy(src, dst, local_sems.at[i]))
        for k in self.KS[self.kind]:
            px = 1 - x if k & 4 else x
            py = 1 - y if k & 2 else y
            pc = 1 - c if k & 1 else c
            peer = 4 * px + 2 * py + pc
            for i in range(self.n):
                if self.kind == "gather_fwd":
                    src, dst, to = in_refs[i].at[peer], out_refs[i].at[peer], (x, y, 1 - c)
                elif self.kind == "scatter":
                    src, dst, to = in_refs[i].at[peer], landing(i), (px, py, pc)
                else:
                    src, dst, to = in_refs[i], out_refs[i].at[me], (px, py, pc)
                cps.append(pltpu.make_async_remote_copy(
                    src_ref=src, dst_ref=dst, send_sem=send_sems.at[k - 1, i], recv_sem=recv_sems.at[k - 1, i],
                    device_id=to, device_id_type=pl.DeviceIdType.MESH))
        return cps


class _CommGroup:
    def __init__(self, comms):
        self.comms = comms
        self.n = sum(cm.n for cm in comms)
        self.operands = [t for cm in comms for t in cm.operands]
        self.out_shapes = [t for cm in comms for t in cm.out_shapes]
        self.scratch = [t for cm in comms for t in cm.scratch]

    def aliases(self, first_in, first_out):
        out = {}
        for cm in self.comms:
            out.update(cm.aliases(first_in, first_out))
            first_in, first_out = first_in + len(cm.operands), first_out + cm.n
        return out

    def copies(self, in_refs, out_refs, *sems):
        cps, i, o = [], 0, 0
        for j, cm in enumerate(self.comms):
            cps += cm.copies(in_refs[i:i + len(cm.operands)], out_refs[o:o + cm.n], *sems[3 * j:3 * j + 3])
            i, o = i + len(cm.operands), o + cm.n
        return cps

    def split(self, outs):
        res, o = [], 0
        for cm in self.comms:
            res.append(list(outs[o:o + cm.n]))
            o += cm.n
        return res


_ANY = pl.BlockSpec(memory_space=pl.ANY)


def _exchange(name, comm):
    nci = len(comm.operands)

    def body(*refs):
        cps = comm.copies(refs[:nci], refs[nci:nci + comm.n], *refs[nci + comm.n:])
        for cp in cps:
            cp.start()
        for cp in cps:
            cp.wait()

    return _pcall(body, name=name, in_specs=[_ANY] * nci, out_specs=[_ANY] * comm.n, out_shape=comm.out_shapes,
                  scratch_shapes=comm.scratch, input_output_aliases=comm.aliases(0, 0))(*comm.operands)


def _hosted(name, body, grid, arrays, in_specs, out_shapes, out_specs, comm):
    if comm is None:
        return _pcall(body, name=name, grid=grid, in_specs=in_specs, out_specs=out_specs, out_shape=out_shapes,
                      compiler_params=_CP)(*arrays), []
    ni, no, nci = len(arrays), len(out_shapes), len(comm.operands)

    def full(*refs):
        ins, cin = refs[:ni], refs[ni:ni + nci]
        outs = refs[ni + nci:ni + nci + no]
        cout = refs[ni + nci + no:ni + nci + no + comm.n]
        sems = refs[ni + nci + no + comm.n:]
        first = functools.reduce(jnp.logical_and, [pl.program_id(a) == 0 for a in range(len(grid))])
        last = functools.reduce(jnp.logical_and, [pl.program_id(a) == grid[a] - 1 for a in range(len(grid))])

        @pl.when(first)
        def _():
            for cp in comm.copies(cin, cout, *sems):
                cp.start()

        body(*ins, *outs)

        @pl.when(last)
        def _():
            for cp in comm.copies(cin, cout, *sems):
                cp.wait()

    res = _pcall(full, name=name, grid=grid, in_specs=list(in_specs) + [_ANY] * nci,
                 out_specs=list(out_specs) + [_ANY] * comm.n, out_shape=list(out_shapes) + comm.out_shapes,
                 scratch_shapes=comm.scratch, input_output_aliases=comm.aliases(ni, no),
                 compiler_params=_CP)(*arrays, *comm.operands)
    return res[:no], res[no:]


def _dot_nt(a, b):
    return lax.dot_general(a, b, _DIMS["nt"], preferred_element_type=F32)


def _dot_tn(a, b):
    return lax.dot_general(a, b, _DIMS["tn"], preferred_element_type=F32)


def _dot_nn(a, b):
    return jnp.dot(a, b, preferred_element_type=F32)


def _tri(tk, rel):
    j = lax.broadcasted_iota(jnp.int32, (tk, tk), 0)
    s = lax.broadcasted_iota(jnp.int32, (tk, tk), 1)
    return {"after": j > s, "upto": j <= s, "before": j < s}[rel].astype(BF16)


def _pairs_per_step(nhp):
    return 2 if nhp % 2 == 0 else 1


def _pair(a, pr):
    return a[:, pr * LANES:(pr + 1) * LANES]


def _head_masks():
    lane = lax.broadcasted_iota(jnp.int32, (1, LANES), 1)
    return [(lane // HEAD_DIM) == h for h in range(2)]


ROW_CHUNK = 32


def _by_rows(fn, n_out, *arrays):
    rows = arrays[0].shape[0]
    step = min(ROW_CHUNK, rows)
    outs = [[] for _ in range(n_out)]
    for r in range(0, rows, step):
        for o, val in zip(outs, fn(r, *[a[r:r + step] for a in arrays])):
            o.append(val)
    return [jnp.concatenate(o, axis=0) for o in outs]


def _causal(r0, k0, rows, tk, strict):
    row = lax.broadcasted_iota(jnp.int32, (rows, tk), 0) + r0
    col = lax.broadcasted_iota(jnp.int32, (rows, tk), 1) + k0
    return col < row if strict else col <= row


def _wide(stat, width):
    return stat if width == LANES else jnp.concatenate([stat] * (width // LANES), axis=1)


def _row_sum(v):
    return jnp.broadcast_to(jnp.sum(v, axis=1, keepdims=True), (v.shape[0], LANES))


def _split_bf16(v):
    hi = v.astype(BF16)
    return hi, (v - hi.astype(F32)).astype(BF16)


def _sb_logs(z, scale, mask):
    z = z * scale
    e = jnp.exp(-jnp.abs(z))
    log_sig = jnp.minimum(z, 0.0) - jnp.log(1.0 + e)
    log_fail = log_sig - z
    return z, log_sig, (log_fail if mask is None else jnp.where(mask, log_fail, 0.0))


def _two_loops(n_full, nkb, near_first, step, carry):
    if near_first:
        carry = lax.fori_loop(0, nkb - n_full, lambda j, c: step(nkb - 1 - j, True, c), carry)
        return lax.fori_loop(0, n_full, lambda j, c: step(n_full - 1 - j, False, c), carry)
    carry = lax.fori_loop(0, n_full, lambda j, c: step(j, False, c), carry)
    return lax.fori_loop(n_full, nkb, lambda j, c: step(j, True, c), carry)


def _sb_fwd(name, p, nhp, tq, tk, comm=None):
    s = p.shape[0]
    scale = HEAD_DIM ** -0.5
    nq = s // tq
    pp = _pairs_per_step(nhp)
    wide = pp * LANES

    def body(q_ref, k_ref, v_ref, o_ref, tot_ref):
        qi = pl.program_id(1)
        masks = _head_masks()
        after = _tri(tk, "after")
        nkb = ((qi + 1) * tq + tk - 1) // tk
        q = q_ref[...]
        qhs = [jnp.where(hm, _pair(q, pr), 0.0).astype(BF16) for pr in range(pp) for hm in masks]

        def step(kb, masked, carry):
            ks = pl.multiple_of(kb * tk, tk)
            ks_all = k_ref[pl.ds(ks, tk), :].astype(BF16)
            vs_all = v_ref[pl.ds(ks, tk), :].astype(BF16)
            mask_of = lambda r, n: _causal(qi * tq + r, ks, n, tk, True) if masked else None
            heads = range(len(qhs))

            def logs(r, zc):
                _, log_sig, log_fail = _sb_logs(zc, scale, mask_of(r, zc.shape[0]))
                return (log_sig,) + _split_bf16(log_fail) + (_row_sum(log_fail),)

            def weights(r, lsc, runc, laterc):
                w = jnp.exp(lsc + runc + _wide(laterc, tk))
                return ((jnp.where(mask_of(r, w.shape[0]), w, 0.0) if masked else w).astype(BF16),)

            zs = [_dot_nt(qhs[i], _pair(ks_all, i // 2)) for i in heads]
            first = [_by_rows(logs, 4, zs[i]) for i in heads]
            runs = [_dot_nn(first[i][1], after) + _dot_nn(first[i][2], after) for i in heads]
            ws = [_by_rows(weights, 1, first[i][0], runs[i], carry[i][0])[0] for i in heads]
            pvs = [_dot_nn(ws[i], _pair(vs_all, i // 2)) for i in heads]
            return tuple((carry[i][0] + first[i][3], carry[i][1] + pvs[i]) for i in heads)

        init = (jnp.zeros((tq, LANES), F32), jnp.zeros((tq, LANES), F32))
        res = _two_loops((qi * tq) // tk, nkb, True, step, (init,) * (2 * pp))
        for pr in range(pp):
            (tot0, acc0), (tot1, acc1) = res[2 * pr], res[2 * pr + 1]
            o_ref[:, pr * LANES:(pr + 1) * LANES] = jnp.where(masks[0], acc0, acc1)
            tot_ref[:, pr * LANES:(pr + 1) * LANES] = jnp.where(masks[0], tot0, tot1)

    ng = nhp // pp
    blk = pl.BlockSpec((tq, wide), lambda h, i: (i, h))
    shape = jax.ShapeDtypeStruct((s, nhp * LANES), F32)
    return _hosted(name, body, (ng, nq), [p, p, p],
                   [blk, pl.BlockSpec((s, wide), lambda h, i: (0, ng + h)),
                    pl.BlockSpec((s, wide), lambda h, i: (0, 2 * ng + h))], [shape, shape], [blk, blk], comm)


def _sb_bwd(name, p, tot, do, nhp, tq, tk, comm=None):
    s = p.shape[0]
    scale = HEAD_DIM ** -0.5
    nq = s // tq
    pp = _pairs_per_step(nhp)
    wide = pp * LANES

    def body(q_ref, k_ref, v_ref, tot_ref, do_ref, dq_ref, dk_ref, dv_ref):
        qi = pl.program_id(1)

        @pl.when(qi == 0)
        def _():
            dk_ref[...] = jnp.zeros_like(dk_ref)
            dv_ref[...] = jnp.zeros_like(dv_ref)

        masks = _head_masks()
        upto, before = _tri(tk, "upto"), _tri(tk, "before")
        nkb = ((qi + 1) * tq + tk - 1) // tk
        q = q_ref[...]
        qbs = q.astype(BF16)
        dout = do_ref[...]
        doutbs = dout.astype(BF16)
        qhs = [jnp.where(hm, _pair(q, pr), 0.0).astype(BF16) for pr in range(pp) for hm in masks]
        dohs = [jnp.where(hm, _pair(dout, pr), 0.0).astype(BF16) for pr in range(pp) for hm in masks]
        tot = tot_ref[...]
        totals = [jnp.broadcast_to(tot[:, h * HEAD_DIM:h * HEAD_DIM + 1], (tq, LANES)) for h in range(2 * pp)]

        def step(kb, masked, carry):
            ks = pl.multiple_of(kb * tk, tk)
            ks_all = k_ref[pl.ds(ks, tk), :].astype(BF16)
            vs_all = v_ref[pl.ds(ks, tk), :].astype(BF16)
            mask_of = lambda r, n: _causal(qi * tq + r, ks, n, tk, True) if masked else None
            heads = range(len(qhs))

            def logs(r, zc):
                _, log_sig, log_fail = _sb_logs(zc, scale, mask_of(r, zc.shape[0]))
                return (log_sig,) + _split_bf16(log_fail) + (_row_sum(log_fail),)

            def weights(r, lsc, runc, basec, dwc):
                w = jnp.exp(lsc + (_wide(basec, tk) - runc))
                if masked:
                    w = jnp.where(mask_of(r, w.shape[0]), w, 0.0)
                g = w * dwc
                return (w.astype(BF16), g) + _split_bf16(g) + (_row_sum(g),)

            def dscore(r, gc, lsc, zc, grc, gbc):
                dz = gc * jnp.exp(lsc - zc * scale) - jnp.exp(lsc) * (_wide(gbc, tk) + grc)
                if masked:
                    dz = jnp.where(mask_of(r, dz.shape[0]), dz, 0.0)
                return ((dz * scale).astype(BF16),)

            zs = [_dot_nt(qhs[i], _pair(ks_all, i // 2)) for i in heads]
            dws = [_dot_nt(dohs[i], _pair(vs_all, i // 2)) for i in heads]
            first = [_by_rows(logs, 4, zs[i]) for i in heads]
            runs = [_dot_nn(first[i][1], upto) + _dot_nn(first[i][2], upto) for i in heads]
            second = [_by_rows(weights, 5, first[i][0], runs[i], totals[i] - carry[i][0], dws[i])
                      for i in heads]
            g_runs = [_dot_nn(second[i][2], before) + _dot_nn(second[i][3], before) for i in heads]
            dzs = [_by_rows(dscore, 1, second[i][1], first[i][0], zs[i], g_runs[i], carry[i][1])[0] for i in heads]
            dks = [_dot_tn(dzs[i], _pair(qbs, i // 2)) for i in heads]
            dvs = [_dot_tn(second[i][0], _pair(doutbs, i // 2)) for i in heads]
            dqs = [_dot_nn(dzs[i], _pair(ks_all, i // 2)) for i in heads]
            for pr in range(pp):
                cols = slice(pr * LANES, (pr + 1) * LANES)
                dk_ref[pl.ds(ks, tk), cols] += jnp.where(masks[0], dks[2 * pr], dks[2 * pr + 1])
                dv_ref[pl.ds(ks, tk), cols] += jnp.where(masks[0], dvs[2 * pr], dvs[2 * pr + 1])
            return tuple((carry[i][0] + first[i][3], carry[i][1] + second[i][4], carry[i][2] + dqs[i]) for i in heads)

        zero = jnp.zeros((tq, LANES), F32)
        res = _two_loops((qi * tq) // tk, nkb, False, step, ((zero, zero, zero),) * (2 * pp))
        for pr in range(pp):
            dq_ref[:, pr * LANES:(pr + 1) * LANES] = jnp.where(masks[0], res[2 * pr][2], res[2 * pr + 1][2])

    ng = nhp // pp
    blk = pl.BlockSpec((tq, wide), lambda h, i: (i, h))
    full = pl.BlockSpec((s, wide), lambda h, i: (0, h))
    shape = jax.ShapeDtypeStruct((s, nhp * LANES), F32)
    return _hosted(name, body, (ng, nq), [p, p, p, tot, do],
                   [blk, pl.BlockSpec((s, wide), lambda h, i: (0, ng + h)),
                    pl.BlockSpec((s, wide), lambda h, i: (0, 2 * ng + h)), blk, blk],
                   [shape, shape, shape], [blk, full, full], comm)


def _mla_fwd(name, q, k, v, tq, tk, comm=None):
    s = q.shape[0]
    nhp = v.shape[1] // LANES
    scale = MLA_QK_DIM ** -0.5
    nq = s // tq
    pp = _pairs_per_step(nhp)

    def body(q_ref, k_ref, v_ref, o_ref, lse_ref):
        qi = pl.program_id(1)
        masks = _head_masks()
        nkb = ((qi + 1) * tq + tk - 1) // tk
        qhs = [q_ref[:, h * LANES:(h + 1) * LANES] for h in range(2 * pp)]

        def step(kb, masked, carry):
            ks = pl.multiple_of(kb * tk, tk)
            heads = range(len(qhs))

            def soft(r, zc, mc, lc):
                zc = zc * scale
                if masked:
                    zc = jnp.where(_causal(qi * tq + r, ks, zc.shape[0], tk, False), zc, -1e30)
                m_new = jnp.maximum(mc, jnp.max(zc, axis=1, keepdims=True))
                a = jnp.exp(mc - m_new)
                pr = jnp.exp(zc - _wide(m_new, tk))
                return pr.astype(BF16), m_new, a * lc + _row_sum(pr), a

            zs = [_dot_nt(qhs[h], k_ref[pl.ds(ks, tk), h * LANES:(h + 1) * LANES]) for h in heads]
            first = [_by_rows(soft, 4, zs[h], carry[h][0], carry[h][1]) for h in heads]
            pvs = [_dot_nn(first[h][0], v_ref[pl.ds(ks, tk), (h // 2) * LANES:(h // 2 + 1) * LANES]) for h in heads]
            accs = [_by_rows(lambda r, ac, aa, pc: (aa * ac + pc,), 1, carry[h][2], first[h][3], pvs[h])[0]
                    for h in heads]
            return tuple((first[h][1], first[h][2], accs[h]) for h in heads)

        init = (jnp.full((tq, LANES), -1e30, F32), jnp.zeros((tq, LANES), F32), jnp.zeros((tq, LANES), F32))
        res = _two_loops((qi * tq) // tk, nkb, False, step, (init,) * (2 * pp))
        for pr in range(pp):
            (m0, l0, acc0), (m1, l1, acc1) = res[2 * pr], res[2 * pr + 1]
            o_ref[:, pr * LANES:(pr + 1) * LANES] = jnp.where(masks[0], acc0 / l0, acc1 / l1)
            lse_ref[:, pr * LANES:(pr + 1) * LANES] = jnp.where(masks[0], m0 + jnp.log(l0), m1 + jnp.log(l1))

    shape = jax.ShapeDtypeStruct((s, nhp * LANES), F32)
    blk = pl.BlockSpec((tq, pp * LANES), lambda h, i: (i, h))
    return _hosted(name, body, (nhp // pp, nq), [q, k, v],
                   [pl.BlockSpec((tq, 2 * pp * LANES), lambda h, i: (i, h)),
                    pl.BlockSpec((s, 2 * pp * LANES), lambda h, i: (0, h)),
                    pl.BlockSpec((s, pp * LANES), lambda h, i: (0, h))], [shape, shape], [blk, blk], comm)


def _mla_bwd(name, q, k, v, o, lse, do, tq, tk, comm=None):
    s = q.shape[0]
    nhp = v.shape[1] // LANES
    scale = MLA_QK_DIM ** -0.5
    nq = s // tq
    pp = _pairs_per_step(nhp)

    def body(q_ref, k_ref, v_ref, o_ref, lse_ref, do_ref, dq_ref, dk_ref, dv_ref):
        qi = pl.program_id(1)

        @pl.when(qi == 0)
        def _():
            dk_ref[...] = jnp.zeros_like(dk_ref)
            dv_ref[...] = jnp.zeros_like(dv_ref)

        masks = _head_masks()
        nkb = ((qi + 1) * tq + tk - 1) // tk
        dout = do_ref[...]
        doutbs = dout.astype(BF16)
        prod = dout * o_ref[...]
        qhs = [q_ref[:, h * LANES:(h + 1) * LANES] for h in range(2 * pp)]
        dohs = [jnp.where(hm, _pair(dout, pr), 0.0).astype(BF16) for pr in range(pp) for hm in masks]
        totals = [_row_sum(jnp.where(hm, _pair(prod, pr), 0.0)) for pr in range(pp) for hm in masks]
        lse = lse_ref[...]
        lses = [jnp.broadcast_to(lse[:, h * HEAD_DIM:h * HEAD_DIM + 1], (tq, LANES)) for h in range(2 * pp)]

        def step(kb, masked, carry):
            ks = pl.multiple_of(kb * tk, tk)
            heads = range(len(qhs))

            def probs(r, zc, dpc, lsec, totc):
                pr = jnp.exp(zc * scale - _wide(lsec, tk))
                if masked:
                    pr = jnp.where(_causal(qi * tq + r, ks, pr.shape[0], tk, False), pr, 0.0)
                return pr.astype(BF16), (pr * (dpc - _wide(totc, tk)) * scale).astype(BF16)

            khs = [k_ref[pl.ds(ks, tk), h * LANES:(h + 1) * LANES] for h in heads]
            vvs = [v_ref[pl.ds(ks, tk), (h // 2) * LANES:(h // 2 + 1) * LANES] for h in heads]
            zs = [_dot_nt(qhs[h], khs[h]) for h in heads]
            dps = [_dot_nt(dohs[h], vvs[h]) for h in heads]
            both = [_by_rows(probs, 2, zs[h], dps[h], lses[h], totals[h]) for h in heads]
            dks = [_dot_tn(both[h][1], qhs[h]) for h in heads]
            dvs = [_dot_tn(both[h][0], _pair(doutbs, h // 2)) for h in heads]
            dqs = [_dot_nn(both[h][1], khs[h]) for h in heads]
            for h in heads:
                dk_ref[pl.ds(ks, tk), h * LANES:(h + 1) * LANES] += dks[h]
            for pr in range(pp):
                dv_ref[pl.ds(ks, tk), pr * LANES:(pr + 1) * LANES] += jnp.where(masks[0], dvs[2 * pr], dvs[2 * pr + 1])
            return tuple(carry[h] + dqs[h] for h in heads)

        zero = jnp.zeros((tq, LANES), F32)
        dqs = _two_loops((qi * tq) // tk, nkb, False, step, (zero,) * (2 * pp))
        for h in range(2 * pp):
            dq_ref[:, h * LANES:(h + 1) * LANES] = dqs[h]

    blk = pl.BlockSpec((tq, pp * LANES), lambda h, i: (i, h))
    blk2 = pl.BlockSpec((tq, 2 * pp * LANES), lambda h, i: (i, h))
    full = pl.BlockSpec((s, pp * LANES), lambda h, i: (0, h))
    full2 = pl.BlockSpec((s, 2 * pp * LANES), lambda h, i: (0, h))
    return _hosted(name, body, (nhp // pp, nq), [q, k, v, o, lse, do], [blk2, full2, full, blk, blk, blk],
                   [jax.ShapeDtypeStruct(q.shape, F32), jax.ShapeDtypeStruct(k.shape, F32),
                    jax.ShapeDtypeStruct(v.shape, F32)], [blk2, full2, full], comm)


def _norm_parts(x):
    r = lax.rsqrt(jnp.mean(x * x, axis=-1, keepdims=True) + NORM_EPS)
    return r, x * r


def _rmsmod_fwd(x, g, sc, sh):
    _, xh = _norm_parts(x)
    return ((xh * g) * (1.0 + sc) + sh,)


def _rmsmod_bwd(dh, x, dres, g, sc):
    r, xh = _norm_parts(x)
    dy = dh * (1.0 + sc)
    dxh = dy * g
    dx = r * (dxh - xh * jnp.mean(dxh * xh, axis=-1, keepdims=True)) + dres
    return dx, dh, dh * (xh * g), dy * xh


def _rms_bwd_plain(dh, x, g):
    r, xh = _norm_parts(x)
    dxh = dh * g
    return r * (dxh - xh * jnp.mean(dxh * xh, axis=-1, keepdims=True)), dh * xh


def _cat(parts):
    return jnp.concatenate(parts, axis=1)


def _swap_halves(a):
    half = a.shape[-1] // 2
    return jnp.concatenate([a[..., half:], a[..., :half]], axis=-1)


def _adamw_fn(w, g, m, v):
    m = ADAM_B1 * m + (1.0 - ADAM_B1) * g
    v = ADAM_B2 * v + (1.0 - ADAM_B2) * jnp.square(g)
    m_hat = m / (1.0 - ADAM_B1 ** ADAM_STEP)
    v_hat = v / (1.0 - ADAM_B2 ** ADAM_STEP)
    delta = -ADAM_LR * (m_hat / (jnp.sqrt(v_hat) + ADAM_EPS) + ADAM_WD * w)
    return delta, m, v


def _adamw(name, w, g, m, v, comm=None):
    shape = w.shape
    width = shape[-1]
    flat = [t.reshape(-1, width) for t in (w, g, m, v)]
    res = _rowwise(name, _adamw_fn, flat, [], [(width, F32)] * 3, comm=comm)
    res, got = res if comm is not None else (res, None)
    res = [t.reshape(shape) for t in res]
    return res if comm is None else (res, got)


def _sum_adamw(name, land, w, m, v, comm=None):
    shape = w.shape
    width = shape[-1]
    rows = w.size // width

    def fn(*blocks):
        g = blocks[0].astype(F32)
        for b in blocks[1:NDEV]:
            g = g + b.astype(F32)
        return (g,) + _adamw_fn(blocks[NDEV], g, blocks[NDEV + 1], blocks[NDEV + 2])

    def fn_whole(wb, mb, vb, lb):
        return fn(*[lb[i] for i in range(NDEV)], wb, mb, vb)

    flat = [t.reshape(rows, width) for t in (w, m, v)]
    if rows % 16 == 0:
        views = [(land.reshape(NDEV * rows, width), width, 0, i * rows) for i in range(NDEV)]
        res = _rowwise(name, fn, views + flat, [], [(width, F32)] * 4, comm=comm)
    else:
        res = _rowwise(name, fn_whole, flat, [land.reshape(NDEV, rows, width)], [(width, F32)] * 4, comm=comm)
    res, got = res if comm is not None else (res, None)
    res = [t.reshape(shape) for t in res]
    return res if comm is None else (res, got)


def kernel(x, c, positions, w_ada, b_ada, g_mix_norm, w_in, g_q_lat, w_q_up, g_kv_lat, w_kv_up, w_sb_out, w_mla_out, w_mix_out, g_mlp_norm, w_up, w_down, g_final, loss_target, m_w_ada, m_b_ada, m_g_mix_norm, m_w_in, m_g_q_lat, m_w_q_up, m_g_kv_lat, m_w_kv_up, m_w_sb_out, m_w_mla_out, m_w_mix_out, m_g_mlp_norm, m_w_up, m_w_down, m_g_final, v_w_ada, v_b_ada, v_g_mix_norm, v_w_in, v_g_q_lat, v_w_q_up, v_g_kv_lat, v_w_kv_up, v_w_sb_out, v_w_mla_out, v_w_mix_out, v_g_mlp_norm, v_w_up, v_w_down, v_g_final):
    seq, d = x.shape[1], x.shape[2]
    depth = w_ada.shape[0]
    qr, kvr = g_q_lat.shape[1], g_kv_lat.shape[1]
    sbw, mlaw = w_sb_out.shape[1], w_mla_out.shape[1]
    nh = mlaw // HEAD_DIM
    nhp_sb = sbw // LANES
    dff = w_up.shape[2] * NDEV
    ada_n = w_ada.shape[2]
    gb = min(512, d)
    tq, tk = min(256, seq), min(256, seq)
    me = 4 * lax.axis_index("x") + 2 * lax.axis_index("y") + lax.axis_index("c")

    o_qlat = _roundup(3 * sbw, qr)
    o_kvlat = _roundup(o_qlat + qr, kvr)
    o_rope = _roundup(o_kvlat + kvr, 2 * LANES)
    o_gate = _roundup(o_rope + 2 * LANES, gb)
    wp = o_gate + 2 * d

    c_all = _exchange("ag_c", _Comm("gather_all", [c.reshape(d // LANES, LANES)]))[0].reshape(NDEV, d)
    c_act = _rowwise("silu_c", lambda t: (t * (1.0 / (1.0 + jnp.exp(-t))),), [c_all], [], [(d, F32)])[0]
    parts = jnp.stack([_matmul("ada_fwd", c_act, w_ada[l], "nn") for l in range(depth)])
    parts_all = _exchange("ag_mod", _Comm("gather_all", [parts]))[0]
    mine = jnp.transpose(lax.dynamic_index_in_dim(parts_all, me, axis=2, keepdims=False), (1, 0, 2))
    mod = _rowwise("mod_bias", lambda a, b: (a + b,), [mine.reshape(depth, NDEV * ada_n), b_ada], [],
                   [(6 * d, F32)])[0]
    mods = [[mod[l:l + 1, i * d:(i + 1) * d] for i in range(6)] for l in range(depth)]

    big = [w_in, w_q_up, w_kv_up, w_sb_out, w_mla_out, w_mix_out, w_up, w_down]
    row_sharded = [False, False, False, False, False, True, False, True]
    shards = [[w[l].astype(BF16) for w in big] for l in range(depth)]

    ids_a, ids_b = [0, 1, 2, 3, 4, 5], [6, 7]
    pick = lambda l, ids: [shards[l][i] for i in ids]

    def unpack(gathered, ids):
        out = []
        for g, i in zip(gathered, ids):
            _, rows, cols = g.shape
            if i == 0:
                out.append(g)
            elif row_sharded[i]:
                out.append(g.reshape(NDEV * rows, cols))
            else:
                out.append(jnp.transpose(g, (1, 0, 2)).reshape(rows, NDEV * cols))
        return out

    n_in = w_in.shape[2]
    r0 = 3 * sbw + qr + kvr
    g0 = r0 + ROPE_DIM
    runs = [(0, 3 * sbw, 0), (3 * sbw, 3 * sbw + qr, o_qlat), (3 * sbw + qr, r0, o_kvlat), (g0, g0 + 2 * d, o_gate)]

    def shard_cols(g, a, b):
        return [g[j][:, max(a, n_in * j) - n_in * j:min(b, n_in * (j + 1)) - n_in * j]
                for j in range(a // n_in, (b - 1) // n_in + 1)]

    def derive(full):
        wi, wq, wkv, wsb, wmla, wmix = full
        dt = wi.dtype
        z = lambda r, n: jnp.zeros((r, n), dt)
        kr = _cat(shard_cols(wi, r0, g0))
        pieces, at = [], 0
        for a, b, start in runs[:3]:
            pieces += [z(d, start - at)] + shard_cols(wi, a, b)
            at = start + b - a
        pieces += [z(d, o_rope - at), z(d, HEAD_DIM), kr, z(d, LANES - MLA_QK_DIM),
                   z(d, HEAD_DIM), _swap_halves(kr), z(d, LANES - MLA_QK_DIM),
                   z(d, o_gate - o_rope - 2 * LANES)] + shard_cols(wi, g0, g0 + 2 * d)
        w_in_pad = _cat([t for t in pieces if t.shape[1]])
        wq3 = wq.reshape(qr, nh, MLA_QK_DIM)
        z3 = lambda n: jnp.zeros((qr, nh, n), dt)
        rope_w = wq3[:, :, HEAD_DIM:]
        wq_a = jnp.concatenate([wq3[:, :, :HEAD_DIM], rope_w, z3(LANES - MLA_QK_DIM)], axis=2).reshape(qr, nh * LANES)
        wq_b = jnp.concatenate([z3(HEAD_DIM), _swap_halves(rope_w), z3(LANES - MLA_QK_DIM)], axis=2).reshape(qr, nh * LANES)
        wkv3 = wkv.reshape(kvr, nh, 2 * HEAD_DIM)
        wk = jnp.concatenate([wkv3[:, :, :HEAD_DIM], jnp.zeros((kvr, nh, HEAD_DIM), dt)], axis=2).reshape(kvr, nh * LANES)
        wv = wkv3[:, :, HEAD_DIM:].reshape(kvr, nh * HEAD_DIM)
        return dict(w_in=w_in_pad, w_q=_cat([wq_a, wq_b]), w_kv=_cat([wk, wv]), w_sb=wsb, w_mla=wmla, w_mix=wmix)

    def fold(gr):
        gi, gq, gkv = gr["w_in"], gr["w_q"], gr["w_kv"]
        ra = gi[:, o_rope + HEAD_DIM:o_rope + MLA_QK_DIM]
        rb = gi[:, o_rope + LANES + HEAD_DIM:o_rope + LANES + MLA_QK_DIM]
        rope = (ra.astype(F32) + _swap_halves(rb).astype(F32)).astype(gi.dtype)

        def cols(a, b):
            out = []
            for s0, s1, start in runs[:3] + [(r0, g0, None)] + runs[3:]:
                lo, hi = max(a, s0), min(b, s1)
                if lo < hi:
                    out.append(rope[:, lo - r0:hi - r0] if start is None else gi[:, start + lo - s0:start + hi - s0])
            return out

        g_in = jnp.stack([_cat(cols(n_in * j, n_in * (j + 1))) for j in range(NDEV)]).astype(BF16)
        ga = gq[:, :nh * LANES].reshape(qr, nh, LANES)
        gb_ = gq[:, nh * LANES:].reshape(qr, nh, LANES)
        g_q = jnp.concatenate([ga[:, :, :HEAD_DIM], ga[:, :, HEAD_DIM:MLA_QK_DIM]
                               + _swap_halves(gb_[:, :, HEAD_DIM:MLA_QK_DIM])], axis=2).reshape(qr, nh * MLA_QK_DIM)
        gk = gkv[:, :nh * LANES].reshape(kvr, nh, LANES)[:, :, :HEAD_DIM]
        gv = gkv[:, nh * LANES:].reshape(kvr, nh, HEAD_DIM)
        g_kv = jnp.concatenate([gk, gv], axis=2).reshape(kvr, nh * 2 * HEAD_DIM)
        return [g_in, g_q, g_kv, gr["w_sb"], gr["w_mla"], gr["w_mix"], gr["w_up"], gr["w_down"]]

    part_a = _exchange("ag_w0_own", _Comm("gather_own", pick(0, ids_a)))
    ready_a = _exchange("ag_w0_fwd", _Comm("gather_fwd", [], lands=part_a))
    part_b, weights = None, []

    inv_freq = 1.0 / (ROPE_THETA ** (jnp.arange(0, ROPE_DIM, 2, dtype=F32) / ROPE_DIM))
    ang = positions[0].astype(F32)[:, None] * inv_freq
    cos, sin = jnp.cos(ang), jnp.sin(ang)
    tail = jnp.zeros((seq, LANES - MLA_QK_DIM), F32)
    rope_c = _cat([jnp.ones((seq, HEAD_DIM), F32), cos, cos, tail])
    rope_s = _cat([jnp.zeros((seq, HEAD_DIM), F32), -sin, sin, tail])
    zero_vec = lambda n: jnp.zeros((1, n), F32)

    def rope_fwd(q2, kvs, pd, tc, ts):
        c8, s8 = _cat([tc] * nh), _cat([ts] * nh)
        qf = q2[:, :nh * LANES] * c8 + q2[:, nh * LANES:] * s8
        kpe = pd[:, :LANES] * tc + pd[:, LANES:] * ts
        return qf, kvs[:, :nh * LANES] + _cat([kpe] * nh), kvs[:, nh * LANES:]

    def rope_bwd(dq, dk, dv, tc, ts):
        c8, s8 = _cat([tc] * nh), _cat([ts] * nh)
        dks = dk[:, :LANES]
        for h in range(1, nh):
            dks = dks + dk[:, h * LANES:(h + 1) * LANES]
        return _cat([dq * c8, dq * s8]), _cat([dk, dv]), _cat([dks * tc, dks * ts])

    def merge_fwd(gs, gm, osb, omla):
        return osb / (1.0 + jnp.exp(-gs)) + omla / (1.0 + jnp.exp(-gm))

    def merge_bwd(dm, gs, gm, osb, omla):
        ss, sm = 1.0 / (1.0 + jnp.exp(-gs)), 1.0 / (1.0 + jnp.exp(-gm))
        return ss * dm, sm * dm, dm * osb * ss * (1.0 - ss), dm * omla * sm * (1.0 - sm)

    def gates_of(p):
        return [(p, o_gate), (p, o_gate + d)]

    xs = x[0]
    saved = []
    for l in range(depth):
        w = derive(unpack(ready_a, ids_a))
        sh1, sc1, g1, sh2, sc2, g2 = mods[l]
        h1 = _rowwise("norm1", _rmsmod_fwd, [xs], [g_mix_norm[l:l + 1], sc1, sh1], [(d, BF16)])[0]
        p = _matmul("in_proj", h1, w["w_in"], "nn")
        (o_sb, tot_sb), part_b = _sb_fwd("sb_fwd", p, nhp_sb, tq, tk, _Comm("gather_own", pick(l, ids_b)))
        y_sb = _matmul("sb_out", o_sb, w["w_sb"], "nn")
        qn = _rowwise("norm_q", _rmsmod_fwd, [(p, qr, o_qlat // qr, 0)],
                      [g_q_lat[l:l + 1], zero_vec(qr), zero_vec(qr)], [(qr, BF16)])[0]
        kvn = _rowwise("norm_kv", _rmsmod_fwd, [(p, kvr, o_kvlat // kvr, 0)],
                       [g_kv_lat[l:l + 1], zero_vec(kvr), zero_vec(kvr)], [(kvr, BF16)])[0]
        q2 = _matmul("q_up", qn, w["w_q"], "nn")
        kvs = _matmul("kv_up", kvn, w["w_kv"], "nn")
        qf, kf, vf = _rowwise("rope_fwd", rope_fwd, [q2, kvs, (p, 2 * LANES, o_rope // (2 * LANES), 0), rope_c, rope_s],
                              [], [(nh * LANES, BF16), (nh * LANES, BF16), (mlaw, BF16)])
        comms = [_Comm("gather_fwd", [], lands=part_b)]
        if l + 1 < depth:
            comms.append(_Comm("gather_own", pick(l + 1, ids_a)))
        group = _CommGroup(comms)
        (o_mla, lse), got = _mla_fwd("mla_fwd", qf, kf, vf, tq, tk, group)
        got = group.split(got)
        w["w_up"], w["w_down"] = unpack(got[0], ids_b)
        weights.append(w)
        y_mla, merged = _matmul("mla_out", o_mla, w["w_mla"], "nn", outs=[F32, BF16], rows=gates_of(p) + [y_sb],
                                epilogue=lambda acc, gs, gm, osb: (acc, merge_fwd(gs, gm, osb, acc)))
        resid = lambda acc, xv, g: (acc, xv + g * acc)
        y1, x_mid = _matmul("mix_out", merged, w["w_mix"], "nn", epilogue=resid, rows=[xs], vecs=[g1], outs=[F32, F32])
        h2 = _rowwise("norm2", _rmsmod_fwd, [x_mid], [g_mlp_norm[l:l + 1], sc2, sh2], [(d, BF16)])[0]
        relu2 = lambda acc: (acc, jnp.square(jnp.maximum(acc, 0.0)))
        if l + 1 < depth:
            (u, act), ready_a = _matmul("mlp_up", h2, w["w_up"], "nn", outs=[F32, BF16], epilogue=relu2,
                                        comm=_Comm("gather_fwd", [], lands=got[1]))
        else:
            u, act = _matmul("mlp_up", h2, w["w_up"], "nn", outs=[F32, BF16], epilogue=relu2)
        y2, x_out = _matmul("mlp_down", act, w["w_down"], "nn", epilogue=resid, rows=[x_mid], vecs=[g2], outs=[F32, F32])
        saved.append(dict(x=xs, h1=h1, p=p, o_sb=o_sb, tot_sb=tot_sb, y_sb=y_sb, qn=qn, kvn=kvn, qf=qf, kf=kf, vf=vf, o_mla=o_mla,
                          lse=lse, y_mla=y_mla, merged=merged, y1=y1, x_mid=x_mid, h2=h2, u=u, act=act, y2=y2))
        xs = x_out

    def final_fn(xv, tv, y2, g, gate):
        r, xh = _norm_parts(xv)
        diff = xh * g - tv
        dy = diff * (1.0 / d)
        dxh = dy * g
        dx = r * (dxh - xh * jnp.mean(dxh * xh, axis=-1, keepdims=True))
        return dx, dx * gate, diff * diff, dy * xh, dx * y2

    dx, dy2, sq, dg_final, dgate2 = _rowwise(
        "loss_head", final_fn, [xs, loss_target[0], saved[-1]["y2"]], [g_final.reshape(1, d), mods[-1][5]],
        [(d, F32), (d, BF16)], reds=[d, d, d])
    loss = lax.psum(0.5 * jnp.sum(sq) / d, ("x", "y", "c"))

    def norm2_bwd_fn(dh, xv, dres, y1, g, sc, gate):
        dx_mid, dsh, dsc, dg = _rmsmod_bwd(dh, xv, dres, g, sc)
        return dx_mid, dx_mid * gate, dsh, dsc, dg, dx_mid * y1

    def norm1_bwd_fn(dh, xv, dres, y2, g, sc, gate):
        dxv, dsh, dsc, dg = _rmsmod_bwd(dh, xv, dres, g, sc)
        return dxv, dxv * gate, dsh, dsc, dg, dxv * y2

    def chunk(gfull, wref, by_rows):
        rows, cols = wref.shape[1], wref.shape[2]
        if by_rows:
            return gfull.reshape(NDEV, rows, cols).astype(BF16)
        return jnp.transpose(gfull.reshape(rows, NDEV, cols), (1, 0, 2)).astype(BF16)

    dmods, small = [None] * depth, [None] * depth
    late, lands = None, [None] * len(big)
    for l in reversed(range(depth)):
        w, sv = weights[l], saved[l]
        sh1, sc1, g1, sh2, sc2, g2 = mods[l]
        gr = {}
        du = _matmul("mlp_down_dx", dy2, w["w_down"], "nt", outs=[BF16], rows=[sv["u"]],
                     epilogue=lambda acc, uv: (acc * 2.0 * jnp.maximum(uv, 0.0),))
        gr["w_down"] = _matmul("mlp_down_dw", sv["act"], dy2, "tn", BF16)
        dh2 = _matmul("mlp_up_dx", du, w["w_up"], "nt")
        gr["w_up"] = _matmul("mlp_up_dw", sv["h2"], du, "tn", BF16, owner_cols=w_up.shape[2])
        dx_mid, dy1, dsh2, dsc2, dg_mlp, dgate1 = _rowwise(
            "norm2_bwd", norm2_bwd_fn, [dh2, sv["x_mid"], dx, sv["y1"]], [g_mlp_norm[l:l + 1], sc2, g1],
            [(d, F32), (d, BF16)], reds=[d, d, d, d])
        dy_sb, dy_mla, dgate_sb, dgate_mla = _matmul(
            "mix_out_dx", dy1, w["w_mix"], "nt", outs=[BF16] * 4, epilogue=merge_bwd,
            rows=gates_of(sv["p"]) + [sv["y_sb"], sv["y_mla"]])
        gr["w_mix"] = _matmul("mix_out_dw", sv["merged"], dy1, "tn", BF16)
        do_sb = _matmul("sb_out_dx", dy_sb, w["w_sb"], "nt")
        gr["w_sb"] = _matmul("sb_out_dw", sv["o_sb"], dy_sb, "tn", BF16)
        do_mla = _matmul("mla_out_dx", dy_mla, w["w_mla"], "nt")
        gr["w_mla"] = _matmul("mla_out_dw", sv["o_mla"], dy_mla, "tn", BF16)
        ready = {3: gr["w_sb"], 4: gr["w_mla"], 5: gr["w_mix"], 7: gr["w_down"]}
        ready = {i: chunk(g, big[i], row_sharded[i]) for i, g in ready.items()}
        ready[6] = gr["w_up"]
        ids_a = [6] + ([0, 1, 2] if late is not None else [])
        comm_a = _Comm("scatter", [ready[6]] + (late or []), [lands[i] for i in ids_a], [l] + [l + 1] * 3, depth)
        comm_b = _Comm("scatter", [ready[7]], [lands[7]], [l], depth)
        (dq_sb, dk_sb, dv_sb), got_a = _sb_bwd("sb_bwd", sv["p"], sv["tot_sb"], do_sb, nhp_sb, tq, tk, comm_a)
        (dqf, dkf, dvf), got_b = _mla_bwd("mla_bwd", sv["qf"], sv["kf"], sv["vf"], sv["o_mla"], sv["lse"], do_mla,
                                          tq, tk, comm_b)
        for i, t in zip(ids_a + [7], list(got_a) + list(got_b)):
            lands[i] = t
        dq2, dkvs, drope = _rowwise("rope_bwd", rope_bwd, [dqf, dkf, dvf, rope_c, rope_s], [],
                                    [(2 * nh * LANES, BF16), (nh * LANES + mlaw, BF16), (2 * LANES, BF16)])
        dqn = _matmul("q_up_dx", dq2, w["w_q"], "nt")
        gr["w_q"] = _matmul("q_up_dw", sv["qn"], dq2, "tn")
        dkvn = _matmul("kv_up_dx", dkvs, w["w_kv"], "nt")
        gr["w_kv"] = _matmul("kv_up_dw", sv["kvn"], dkvs, "tn")
        dqlat, dg_q = _rowwise("norm_q_bwd", _rms_bwd_plain, [dqn, (sv["p"], qr, o_qlat // qr, 0)],
                               [g_q_lat[l:l + 1]], [(qr, BF16)], reds=[qr])
        dkvlat, dg_kv = _rowwise("norm_kv_bwd", _rms_bwd_plain, [dkvn, (sv["p"], kvr, o_kvlat // kvr, 0)],
                                 [g_kv_lat[l:l + 1]], [(kvr, BF16)], reds=[kvr])
        zb = lambda n: jnp.zeros((seq, n), BF16)
        dp = _cat([dq_sb.astype(BF16), dk_sb.astype(BF16), dv_sb.astype(BF16), zb(o_qlat - 3 * sbw), dqlat,
                   zb(o_kvlat - o_qlat - qr), dkvlat, zb(o_rope - o_kvlat - kvr), drope,
                   zb(o_gate - o_rope - 2 * LANES), dgate_sb, dgate_mla])
        dh1, got = _matmul("in_proj_dx", dp, w["w_in"], "nt", comm=_Comm("scatter", [ready[5]], [lands[5]], [l], depth))
        lands[5] = got[0]
        gr["w_in"], got = _matmul("in_proj_dw", sv["h1"], dp, "tn", BF16,
                                  comm=_Comm("scatter", [ready[3], ready[4]], [lands[3], lands[4]], [l, l], depth))
        lands[3], lands[4] = got
        dmods[l] = [None, None, dgate1, dsh2, dsc2, dgate2]
        if l > 0:
            dx, dy2, dsh1, dsc1, dg_mix, dgate2 = _rowwise(
                "norm1_bwd", norm1_bwd_fn, [dh1, sv["x"], dx_mid, saved[l - 1]["y2"]],
                [g_mix_norm[l:l + 1], sc1, mods[l - 1][5]], [(d, F32), (d, BF16)], reds=[d, d, d, d])
        else:
            dx, dsh1, dsc1, dg_mix = _rowwise("norm1_bwd", _rmsmod_bwd, [dh1, sv["x"], dx_mid],
                                              [g_mix_norm[l:l + 1], sc1], [(d, F32)], reds=[d, d, d])
        dmods[l] = _cat([dsh1, dsc1] + dmods[l][2:])
        small[l] = (dg_mix, dg_q, dg_kv, dg_mlp)
        g_in, g_q, g_kv = fold(gr)[:3]
        late = [g_in, chunk(g_q, big[1], False), chunk(g_kv, big[2], False)]

    small_parts = [jnp.concatenate(dmods, axis=0)]
    small_parts += [jnp.concatenate([small[l][i] for l in range(depth)], axis=0) for i in range(4)]
    small_parts.append(dg_final)
    small_all = _exchange("ag_small", _Comm("gather_all", small_parts))

    dmod_mine = lax.dynamic_slice_in_dim(small_all[0], me * ada_n, ada_n, axis=2)
    c_act_t = jnp.transpose(c_act)

    def outer_fn(ct, dm):
        acc = ct[:, 0:1] * dm[0:1, :]
        for b in range(1, NDEV):
            acc = acc + ct[:, b:b + 1] * dm[b:b + 1, :]
        return (acc,)

    g_w_ada = jnp.stack([_rowwise("ada_dw", outer_fn, [c_act_t], [dmod_mine[:, l, :]], [(ada_n, F32)])[0]
                         for l in range(depth)])

    hosts = ["w_ada", "w_up", "w_down"]
    if lands[0] is not None:
        step = _roundup(d // len(hosts), 16)
        bounds = [min(d, i * step) for i in range(len(hosts) + 1)]
        riders = [dict(srcs=[late[0][:, a:b]], lands=[0], row0=[a]) for a, b in zip(bounds[:-1], bounds[1:])]
        riders[-1] = dict(srcs=riders[-1]["srcs"] + late[1:], lands=[0, 1, 2], row0=riders[-1]["row0"] + [None, None])
    else:
        riders, hosts = [], []
        lands[:3] = _exchange("a2a_last", _Comm("scatter", late, lands[:3], [0] * 3, depth))

    def rider_of(name):
        if name not in hosts:
            return None, None
        r = riders[hosts.index(name)]
        return _Comm("scatter", r["srcs"], [lands[i] for i in r["lands"]], [0] * len(r["srcs"]), depth, r["row0"]), r

    moments = dict(
        w_ada=(w_ada, m_w_ada, v_w_ada), b_ada=(b_ada, m_b_ada, v_b_ada),
        g_mix_norm=(g_mix_norm, m_g_mix_norm, v_g_mix_norm), w_in=(w_in, m_w_in, v_w_in),
        g_q_lat=(g_q_lat, m_g_q_lat, v_g_q_lat), w_q_up=(w_q_up, m_w_q_up, v_w_q_up),
        g_kv_lat=(g_kv_lat, m_g_kv_lat, v_g_kv_lat), w_kv_up=(w_kv_up, m_w_kv_up, v_w_kv_up),
        w_sb_out=(w_sb_out, m_w_sb_out, v_w_sb_out), w_mla_out=(w_mla_out, m_w_mla_out, v_w_mla_out),
        w_mix_out=(w_mix_out, m_w_mix_out, v_w_mix_out), g_mlp_norm=(g_mlp_norm, m_g_mlp_norm, v_g_mlp_norm),
        w_up=(w_up, m_w_up, v_w_up), w_down=(w_down, m_w_down, v_w_down),
        g_final=(g_final.reshape(1, d), m_g_final.reshape(1, d), v_g_final.reshape(1, d)))
    small_lands = dict(b_ada=small_all[0], g_mix_norm=small_all[1], g_q_lat=small_all[2], g_kv_lat=small_all[3],
                       g_mlp_norm=small_all[4], g_final=small_all[5])
    big_index = dict(w_in=0, w_q_up=1, w_kv_up=2, w_sb_out=3, w_mla_out=4, w_mix_out=5, w_up=6, w_down=7)
    results = {}
    for name in hosts + [n for n in moments if n not in hosts]:
        wt, mt, vt = moments[name]
        comm, rider = rider_of(name)
        if name == "w_ada":
            res = _adamw("adamw_" + name, wt, g_w_ada, mt, vt, comm)
        else:
            land = small_lands[name] if name in small_lands else lands[big_index[name]]
            res = _sum_adamw("adamw_" + name, land, wt, mt, vt, comm)
        if comm is not None:
            res, got = res
            for i, t in zip(rider["lands"], got):
                lands[i] = t
        res = ([g_w_ada] if name == "w_ada" else []) + list(res)
        results[name] = [t.reshape(d) for t in res] if name == "g_final" else res
    gs, deltas, new_ms, new_vs = ([results[n][k] for n in moments] for k in range(4))

    return (loss, dx[None], *gs, *deltas, *new_ms, *new_vs)
```

```python
import functools

import jax
import jax.numpy as jnp
from jax import lax
from jax.experimental import pallas as pl
from jax.experimental.pallas import tpu as pltpu

F32 = jnp.float32
BF16 = jnp.bfloat16
NDEV = 8
LANES = 128
HEAD_DIM = 64
ROPE_DIM = 32
MLA_QK_DIM = HEAD_DIM + ROPE_DIM
ROPE_THETA = 10000.0
NORM_EPS = 1e-6
ADAM_LR = 0.001
ADAM_B1 = 0.9
ADAM_B2 = 0.999
ADAM_EPS = 1e-08
ADAM_WD = 0.01
ADAM_STEP = 10
VMEM_LIMIT = 48 * 1024 * 1024


def _pcall(body, **kw):
    return pl.pallas_call(body, **kw)


def _tile(n, pref):
    for t in (512, 384, 256, 128, 64, 32, 16, 8):
        if t <= pref and n % t == 0:
            return t
    return n


def _roundup(n, m):
    return (n + m - 1) // m * m


_CP = pltpu.CompilerParams(vmem_limit_bytes=VMEM_LIMIT)


def _rowwise(name, fn, rows, vecs, outs, reds=(), tb=256, comm=None):
    rows = [r if isinstance(r, tuple) else (r, r.shape[1], 0, 0) for r in rows]
    nrows = None
    for arr, width, col, roff in rows:
        if roff == 0 and nrows is None:
            nrows = arr.shape[0]
    first_off = [r for r in rows if r[3] != 0]
    if first_off:
        nrows = min(nrows, first_off[0][3])
    tb = _tile(nrows, tb)
    nblk = nrows // tb
    n_in = len(rows) + len(vecs)
    n_out = len(outs)

    def body(*refs):
        vals = [r[...] for r in refs[:n_in]]
        res = fn(*vals)
        if not isinstance(res, (tuple, list)):
            res = (res,)
        for ref, val in zip(refs[n_in:n_in + n_out], res[:n_out]):
            ref[...] = val.astype(ref.dtype)
        for ref, val in zip(refs[n_in + n_out:], res[n_out:]):
            @pl.when(pl.program_id(0) == 0)
            def _(ref=ref):
                ref[...] = jnp.zeros_like(ref)
            ref[...] += jnp.sum(val.astype(F32), axis=0, keepdims=True)

    in_specs = []
    for arr, width, col, roff in rows:
        in_specs.append(pl.BlockSpec((tb, width), functools.partial(
            lambda i, col, rb: (rb + i, col), col=col, rb=roff // tb)))
    for v in vecs:
        in_specs.append(pl.BlockSpec(v.shape, lambda i, nd=v.ndim: (0,) * nd))
    out_specs = [pl.BlockSpec((tb, w), lambda i: (i, 0)) for w, _ in outs]
    out_specs += [pl.BlockSpec((1, w), lambda i: (0, 0)) for w in reds]
    out_shape = [jax.ShapeDtypeStruct((nrows, w), dt) for w, dt in outs]
    out_shape += [jax.ShapeDtypeStruct((1, w), F32) for w in reds]
    res, got = _hosted(name, body, (nblk,), [r[0] for r in rows] + list(vecs), in_specs, out_shape, out_specs, comm)
    return res if comm is None else (res, got)


_DIMS = {"nn": (((1,), (0,)), ((), ())), "nt": (((1,), (1,)), ((), ())), "tn": (((0,), (0,)), ((), ()))}


def _matmul(name, a, b, mode, out_dtype=F32, epilogue=None, rows=(), vecs=(), outs=None, comm=None, owner_cols=None):
    if mode == "nn":
        (m, k), n = a.shape, b.shape[1]
    elif mode == "nt":
        (m, k), n = a.shape, b.shape[0]
    else:
        (k, m), n = a.shape, b.shape[1]
    tm = _tile(m, 512)
    tn = owner_cols or next((t for t in (1536, 1024) if n % t == 0 and n > t), _tile(n, 512))
    dims = _DIMS[mode]
    outs = [out_dtype] if outs is None else outs
    n_extra = len(rows) + len(vecs)
    rows = [r if isinstance(r, tuple) else (r, 0) for r in rows]

    def body(a_ref, b_ref, *refs):
        acc = lax.dot_general(a_ref[...].astype(BF16), b_ref[...].astype(BF16), dims, preferred_element_type=F32)
        res = (acc,) if epilogue is None else epilogue(acc, *[r[...] for r in refs[:n_extra]])
        for o_ref, val in zip(refs[n_extra:], res):
            o_ref[...] = val.astype(o_ref.dtype)

    a_spec = pl.BlockSpec((k, tm), lambda j, i: (0, i)) if mode == "tn" else pl.BlockSpec((tm, k), lambda j, i: (i, 0))
    b_spec = pl.BlockSpec((tn, k), lambda j, i: (j, 0)) if mode == "nt" else pl.BlockSpec((k, tn), lambda j, i: (0, j))
    blk = pl.BlockSpec((tm, tn), lambda j, i: (i, j))
    assert all(off % tn == 0 for _, off in rows), (name, tn)
    row_specs = [pl.BlockSpec((tm, tn), functools.partial(lambda j, i, first: (i, first + j), first=off // tn))
                 for _, off in rows]
    out_blk, out_dims = blk, (m, n)
    if owner_cols:
        out_blk, out_dims = pl.BlockSpec((None, tm, tn), lambda j, i: (j, i, 0)), (n // tn, m, tn)
    res, got = _hosted(name, body, (n // tn, m // tm), [a, b, *[r for r, _ in rows], *vecs],
                       [a_spec, b_spec] + row_specs + [pl.BlockSpec((1, tn), lambda j, i: (0, j))] * len(vecs),
                       [jax.ShapeDtypeStruct(out_dims, dt) for dt in outs], [out_blk] * len(outs), comm)
    res = res[0] if len(outs) == 1 else res
    return res if comm is None else (res, got)


class _Comm:
    KS = {"gather_all": (1, 2, 3, 4, 5, 6, 7), "gather_own": (1, 2, 4, 6), "gather_fwd": (2, 4, 6),
          "scatter": (1, 2, 3, 4, 5, 6, 7)}

    def __init__(self, kind, srcs, lands=None, layers=None, depth=None, row0=None):
        self.kind, self.srcs, self.layers, self.row0 = kind, list(srcs), layers, row0
        self.n = len(lands) if kind == "gather_fwd" else len(srcs)
        self.lands = list(lands) if lands is not None else [None] * self.n
        self.out_shapes = []
        for i, land in enumerate(self.lands):
            if land is not None:
                self.out_shapes.append(jax.ShapeDtypeStruct(land.shape, land.dtype))
            elif kind == "scatter":
                self.out_shapes.append(jax.ShapeDtypeStruct((NDEV, depth) + srcs[i].shape[1:], srcs[i].dtype))
            else:
                self.out_shapes.append(jax.ShapeDtypeStruct((NDEV,) + srcs[i].shape, srcs[i].dtype))
        self.operands = self.srcs + [t for t in self.lands if t is not None]
        self.scratch = [pltpu.SemaphoreType.DMA((NDEV - 1, self.n)), pltpu.SemaphoreType.DMA((NDEV - 1, self.n)),
                        pltpu.SemaphoreType.DMA((self.n,))]

    def aliases(self, first_in, first_out):
        given = [i for i, t in enumerate(self.lands) if t is not None]
        return {first_in + len(self.srcs) + pos: first_out + i for pos, i in enumerate(given)}

    def copies(self, in_refs, out_refs, send_sems, recv_sems, local_sems):
        x, y, c = lax.axis_index("x"), lax.axis_index("y"), lax.axis_index("c")
        me = 4 * x + 2 * y + c
        def landing(i):
            if self.row0 is None or self.row0[i] is None:
                return out_refs[i].at[me, self.layers[i]]
            return out_refs[i].at[me, self.layers[i], pl.ds(self.row0[i], self.srcs[i].shape[1])]

        cps = []
        if self.kind != "gather_fwd":
            for i in range(self.n):
                src = in_refs[i].at[me] if self.kind == "scatter" else in_refs[i]
                dst = landing(i) if self.kind == "scatter" else out_refs[i].at[me]
                cps.append(pltpu.make_async_copy(src, dst, local_sems.at[i]))
        for k in self.KS[self.kind]:
            px = 1 - x if k & 4 else x
            py = 1 - y if k & 2 else y
            pc = 1 - c if k & 1 else c
            peer = 4 * px + 2 * py + pc
            for i in range(self.n):
                if self.kind == "gather_fwd":
                    src, dst, to = in_refs[i].at[peer], out_refs[i].at[peer], (x, y, 1 - c)
                elif self.kind == "scatter":
                    src, dst, to = in_refs[i].at[peer], landing(i), (px, py, pc)
                else:
                    src, dst, to = in_refs[i], out_refs[i].at[me], (px, py, pc)
                cps.append(pltpu.make_async_remote_copy(
                    src_ref=src, dst_ref=dst, send_sem=send_sems.at[k - 1, i], recv_sem=recv_sems.at[k - 1, i],
                    device_id=to, device_id_type=pl.DeviceIdType.MESH))
        return cps


class _CommGroup:
    def __init__(self, comms):
        self.comms = comms
        self.n = sum(cm.n for cm in comms)
        self.operands = [t for cm in comms for t in cm.operands]
        self.out_shapes = [t for cm in comms for t in cm.out_shapes]
        self.scratch = [t for cm in comms for t in cm.scratch]

    def aliases(self, first_in, first_out):
        out = {}
        for cm in self.comms:
            out.update(cm.aliases(first_in, first_out))
            first_in, first_out = first_in + len(cm.operands), first_out + cm.n
        return out

    def copies(self, in_refs, out_refs, *sems):
        cps, i, o = [], 0, 0
        for j, cm in enumerate(self.comms):
            cps += cm.copies(in_refs[i:i + len(cm.operands)], out_refs[o:o + cm.n], *sems[3 * j:3 * j + 3])
            i, o = i + len(cm.operands), o + cm.n
        return cps

    def split(self, outs):
        res, o = [], 0
        for cm in self.comms:
            res.append(list(outs[o:o + cm.n]))
            o += cm.n
        return res


_ANY = pl.BlockSpec(memory_space=pl.ANY)


def _exchange(name, comm):
    nci = len(comm.operands)

    def body(*refs):
        cps = comm.copies(refs[:nci], refs[nci:nci + comm.n], *refs[nci + comm.n:])
        for cp in cps:
            cp.start()
        for cp in cps:
            cp.wait()

    return _pcall(body, name=name, in_specs=[_ANY] * nci, out_specs=[_ANY] * comm.n, out_shape=comm.out_shapes,
                  scratch_shapes=comm.scratch, input_output_aliases=comm.aliases(0, 0))(*comm.operands)


def _hosted(name, body, grid, arrays, in_specs, out_shapes, out_specs, comm):
    if comm is None:
        return _pcall(body, name=name, grid=grid, in_specs=in_specs, out_specs=out_specs, out_shape=out_shapes,
                      compiler_params=_CP)(*arrays), []
    ni, no, nci = len(arrays), len(out_shapes), len(comm.operands)

    def full(*refs):
        ins, cin = refs[:ni], refs[ni:ni + nci]
        outs = refs[ni + nci:ni + nci + no]
        cout = refs[ni + nci + no:ni + nci + no + comm.n]
        sems = refs[ni + nci + no + comm.n:]
        first = functools.reduce(jnp.logical_and, [pl.program_id(a) == 0 for a in range(len(grid))])
        last = functools.reduce(jnp.logical_and, [pl.program_id(a) == grid[a] - 1 for a in range(len(grid))])

        @pl.when(first)
        def _():
            for cp in comm.copies(cin, cout, *sems):
                cp.start()

        body(*ins, *outs)

        @pl.when(last)
        def _():
            for cp in comm.copies(cin, cout, *sems):
                cp.wait()

    res = _pcall(full, name=name, grid=grid, in_specs=list(in_specs) + [_ANY] * nci,
                 out_specs=list(out_specs) + [_ANY] * comm.n, out_shape=list(out_shapes) + comm.out_shapes,
                 scratch_shapes=comm.scratch, input_output_aliases=comm.aliases(ni, no),
                 compiler_params=_CP)(*arrays, *comm.operands)
    return res[:no], res[no:]


def _dot_nt(a, b):
    return lax.dot_general(a, b, _DIMS["nt"], preferred_element_type=F32)


def _dot_tn(a, b):
    return lax.dot_general(a, b, _DIMS["tn"], preferred_element_type=F32)


def _dot_nn(a, b):
    return jnp.dot(a, b, preferred_element_type=F32)


def _tri(tk, rel):
    j = lax.broadcasted_iota(jnp.int32, (tk, tk), 0)
    s = lax.broadcasted_iota(jnp.int32, (tk, tk), 1)
    return {"after": j > s, "upto": j <= s, "before": j < s}[rel].astype(BF16)


def _pairs_per_step(nhp):
    return 2 if nhp % 2 == 0 else 1


def _pair(a, pr):
    return a[:, pr * LANES:(pr + 1) * LANES]


def _head_masks():
    lane = lax.broadcasted_iota(jnp.int32, (1, LANES), 1)
    return [(lane // HEAD_DIM) == h for h in range(2)]


ROW_CHUNK = 32


def _by_rows(fn, n_out, *arrays):
    rows = arrays[0].shape[0]
    step = min(ROW_CHUNK, rows)
    outs = [[] for _ in range(n_out)]
    for r in range(0, rows, step):
        for o, val in zip(outs, fn(r, *[a[r:r + step] for a in arrays])):
            o.append(val)
    return [jnp.concatenate(o, axis=0) for o in outs]


def _causal(r0, k0, rows, tk, strict):
    row = lax.broadcasted_iota(jnp.int32, (rows, tk), 0) + r0
    col = lax.broadcasted_iota(jnp.int32, (rows, tk), 1) + k0
    return col < row if strict else col <= row


def _wide(stat, width):
    return stat if width == LANES else jnp.concatenate([stat] * (width // LANES), axis=1)


def _row_sum(v):
    return jnp.broadcast_to(jnp.sum(v, axis=1, keepdims=True), (v.shape[0], LANES))


def _split_bf16(v):
    hi = v.astype(BF16)
    return hi, (v - hi.astype(F32)).astype(BF16)


def _sb_logs(z, scale, mask):
    z = z * scale
    e = jnp.exp(-jnp.abs(z))
    log_sig = jnp.minimum(z, 0.0) - jnp.log(1.0 + e)
    log_fail = log_sig - z
    return z, log_sig, (log_fail if mask is None else jnp.where(mask, log_fail, 0.0))


def _two_loops(n_full, nkb, near_first, step, carry):
    if near_first:
        carry = lax.fori_loop(0, nkb - n_full, lambda j, c: step(nkb - 1 - j, True, c), carry)
        return lax.fori_loop(0, n_full, lambda j, c: step(n_full - 1 - j, False, c), carry)
    carry = lax.fori_loop(0, n_full, lambda j, c: step(j, False, c), carry)
    return lax.fori_loop(n_full, nkb, lambda j, c: step(j, True, c), carry)


def _sb_fwd(name, p, nhp, tq, tk, comm=None):
    s = p.shape[0]
    scale = HEAD_DIM ** -0.5
    nq = s // tq
    pp = _pairs_per_step(nhp)
    wide = pp * LANES

    def body(q_ref, k_ref, v_ref, o_ref, tot_ref):
        qi = pl.program_id(1)
        masks = _head_masks()
        after = _tri(tk, "after")
        nkb = ((qi + 1) * tq + tk - 1) // tk
        q = q_ref[...]
        qhs = [jnp.where(hm, _pair(q, pr), 0.0).astype(BF16) for pr in range(pp) for hm in masks]

        def step(kb, masked, carry):
            ks = pl.multiple_of(kb * tk, tk)
            ks_all = k_ref[pl.ds(ks, tk), :].astype(BF16)
            vs_all = v_ref[pl.ds(ks, tk), :].astype(BF16)
            mask_of = lambda r, n: _causal(qi * tq + r, ks, n, tk, True) if masked else None
            heads = range(len(qhs))

            def logs(r, zc):
                _, log_sig, log_fail = _sb_logs(zc, scale, mask_of(r, zc.shape[0]))
                return (log_sig,) + _split_bf16(log_fail) + (_row_sum(log_fail),)

            def weights(r, lsc, runc, laterc):
                w = jnp.exp(lsc + runc + _wide(laterc, tk))
                return ((jnp.where(mask_of(r, w.shape[0]), w, 0.0) if masked else w).astype(BF16),)

            zs = [_dot_nt(qhs[i], _pair(ks_all, i // 2)) for i in heads]
            first = [_by_rows(logs, 4, zs[i]) for i in heads]
            runs = [_dot_nn(first[i][1], after) + _dot_nn(first[i][2], after) for i in heads]
            ws = [_by_rows(weights, 1, first[i][0], runs[i], carry[i][0])[0] for i in heads]
            pvs = [_dot_nn(ws[i], _pair(vs_all, i // 2)) for i in heads]
            return tuple((carry[i][0] + first[i][3], carry[i][1] + pvs[i]) for i in heads)

        init = (jnp.zeros((tq, LANES), F32), jnp.zeros((tq, LANES), F32))
        res = _two_loops((qi * tq) // tk, nkb, True, step, (init,) * (2 * pp))
        for pr in range(pp):
            (tot0, acc0), (tot1, acc1) = res[2 * pr], res[2 * pr + 1]
            o_ref[:, pr * LANES:(pr + 1) * LANES] = jnp.where(masks[0], acc0, acc1)
            tot_ref[:, pr * LANES:(pr + 1) * LANES] = jnp.where(masks[0], tot0, tot1)

    ng = nhp // pp
    blk = pl.BlockSpec((tq, wide), lambda h, i: (i, h))
    shape = jax.ShapeDtypeStruct((s, nhp * LANES), F32)
    return _hosted(name, body, (ng, nq), [p, p, p],
                   [blk, pl.BlockSpec((s, wide), lambda h, i: (0, ng + h)),
                    pl.BlockSpec((s, wide), lambda h, i: (0, 2 * ng + h))], [shape, shape], [blk, blk], comm)


def _sb_bwd(name, p, tot, do, nhp, tq, tk, comm=None):
    s = p.shape[0]
    scale = HEAD_DIM ** -0.5
    nq = s // tq
    pp = _pairs_per_step(nhp)
    wide = pp * LANES

    def body(q_ref, k_ref, v_ref, tot_ref, do_ref, dq_ref, dk_ref, dv_ref):
        qi = pl.program_id(1)

        @pl.when(qi == 0)
        def _():
            dk_ref[...] = jnp.zeros_like(dk_ref)
            dv_ref[...] = jnp.zeros_like(dv_ref)

        masks = _head_masks()
        upto, before = _tri(tk, "upto"), _tri(tk, "before")
        nkb = ((qi + 1) * tq + tk - 1) // tk
        q = q_ref[...]
        qbs = q.astype(BF16)
        dout = do_ref[...]
        doutbs = dout.astype(BF16)
        qhs = [jnp.where(hm, _pair(q, pr), 0.0).astype(BF16) for pr in range(pp) for hm in masks]
        dohs = [jnp.where(hm, _pair(dout, pr), 0.0).astype(BF16) for pr in range(pp) for hm in masks]
        tot = tot_ref[...]
        totals = [jnp.broadcast_to(tot[:, h * HEAD_DIM:h * HEAD_DIM + 1], (tq, LANES)) for h in range(2 * pp)]

        def step(kb, masked, carry):
            ks = pl.multiple_of(kb * tk, tk)
            ks_all = k_ref[pl.ds(ks, tk), :].astype(BF16)
            vs_all = v_ref[pl.ds(ks, tk), :].astype(BF16)
            mask_of = lambda r, n: _causal(qi * tq + r, ks, n, tk, True) if masked else None
            heads = range(len(qhs))

            def logs(r, zc):
                _, log_sig, log_fail = _sb_logs(zc, scale, mask_of(r, zc.shape[0]))
                return (log_sig,) + _split_bf16(log_fail) + (_row_sum(log_fail),)

            def weights(r, lsc, runc, basec, dwc):
                w = jnp.exp(lsc + (_wide(basec, tk) - runc))
                if masked:
                    w = jnp.where(mask_of(r, w.shape[0]), w, 0.0)
                g = w * dwc
                return (w.astype(BF16), g) + _split_bf16(g) + (_row_sum(g),)

            def dscore(r, gc, lsc, zc, grc, gbc):
                dz = gc * jnp.exp(lsc - zc * scale) - jnp.exp(lsc) * (_wide(gbc, tk) + grc)
                if masked:
                    dz = jnp.where(mask_of(r, dz.shape[0]), dz, 0.0)
                return ((dz * scale).astype(BF16),)

            zs = [_dot_nt(qhs[i], _pair(ks_all, i // 2)) for i in heads]
            dws = [_dot_nt(dohs[i], _pair(vs_all, i // 2)) for i in heads]
            first = [_by_rows(logs, 4, zs[i]) for i in heads]
            runs = [_dot_nn(first[i][1], upto) + _dot_nn(first[i][2], upto) for i in heads]
            second = [_by_rows(weights, 5, first[i][0], runs[i], totals[i] - carry[i][0], dws[i])
                      for i in heads]
            g_runs = [_dot_nn(second[i][2], before) + _dot_nn(second[i][3], before) for i in heads]
            dzs = [_by_rows(dscore, 1, second[i][1], first[i][0], zs[i], g_runs[i], carry[i][1])[0] for i in heads]
            dks = [_dot_tn(dzs[i], _pair(qbs, i // 2)) for i in heads]
            dvs = [_dot_tn(second[i][0], _pair(doutbs, i // 2)) for i in heads]
            dqs = [_dot_nn(dzs[i], _pair(ks_all, i // 2)) for i in heads]
            for pr in range(pp):
                cols = slice(pr * LANES, (pr + 1) * LANES)
                dk_ref[pl.ds(ks, tk), cols] += jnp.where(masks[0], dks[2 * pr], dks[2 * pr + 1])
                dv_ref[pl.ds(ks, tk), cols] += jnp.where(masks[0], dvs[2 * pr], dvs[2 * pr + 1])
            return tuple((carry[i][0] + first[i][3], carry[i][1] + second[i][4], carry[i][2] + dqs[i]) for i in heads)

        zero = jnp.zeros((tq, LANES), F32)
        res = _two_loops((qi * tq) // tk, nkb, False, step, ((zero, zero, zero),) * (2 * pp))
        for pr in range(pp):
            dq_ref[:, pr * LANES:(pr + 1) * LANES] = jnp.where(masks[0], res[2 * pr][2], res[2 * pr + 1][2])

    ng = nhp // pp
    blk = pl.BlockSpec((tq, wide), lambda h, i: (i, h))
    full = pl.BlockSpec((s, wide), lambda h, i: (0, h))
    shape = jax.ShapeDtypeStruct((s, nhp * LANES), F32)
    return _hosted(name, body, (ng, nq), [p, p, p, tot, do],
                   [blk, pl.BlockSpec((s, wide), lambda h, i: (0, ng + h)),
                    pl.BlockSpec((s, wide), lambda h, i: (0, 2 * ng + h)), blk, blk],
                   [shape, shape, shape], [blk, full, full], comm)


def _mla_fwd(name, q, k, v, tq, tk, comm=None):
    s = q.shape[0]
    nhp = v.shape[1] // LANES
    scale = MLA_QK_DIM ** -0.5
    nq = s // tq
    pp = _pairs_per_step(nhp)

    def body(q_ref, k_ref, v_ref, o_ref, lse_ref):
        qi = pl.program_id(1)
        masks = _head_masks()
        nkb = ((qi + 1) * tq + tk - 1) // tk
        qhs = [q_ref[:, h * LANES:(h + 1) * LANES] for h in range(2 * pp)]

        def step(kb, masked, carry):
            ks = pl.multiple_of(kb * tk, tk)
            heads = range(len(qhs))

            def soft(r, zc, mc, lc):
                zc = zc * scale
                if masked:
                    zc = jnp.where(_causal(qi * tq + r, ks, zc.shape[0], tk, False), zc, -1e30)
                m_new = jnp.maximum(mc, jnp.max(zc, axis=1, keepdims=True))
                a = jnp.exp(mc - m_new)
                pr = jnp.exp(zc - _wide(m_new, tk))
                return pr.astype(BF16), m_new, a * lc + _row_sum(pr), a

            zs = [_dot_nt(qhs[h], k_ref[pl.ds(ks, tk), h * LANES:(h + 1) * LANES]) for h in heads]
            first = [_by_rows(soft, 4, zs[h], carry[h][0], carry[h][1]) for h in heads]
            pvs = [_dot_nn(first[h][0], v_ref[pl.ds(ks, tk), (h // 2) * LANES:(h // 2 + 1) * LANES]) for h in heads]
            accs = [_by_rows(lambda r, ac, aa, pc: (aa * ac + pc,), 1, carry[h][2], first[h][3], pvs[h])[0]
                    for h in heads]
            return tuple((first[h][1], first[h][2], accs[h]) for h in heads)

        init = (jnp.full((tq, LANES), -1e30, F32), jnp.zeros((tq, LANES), F32), jnp.zeros((tq, LANES), F32))
        res = _two_loops((qi * tq) // tk, nkb, False, step, (init,) * (2 * pp))
        for pr in range(pp):
            (m0, l0, acc0), (m1, l1, acc1) = res[2 * pr], res[2 * pr + 1]
            o_ref[:, pr * LANES:(pr + 1) * LANES] = jnp.where(masks[0], acc0 / l0, acc1 / l1)
            lse_ref[:, pr * LANES:(pr + 1) * LANES] = jnp.where(masks[0], m0 + jnp.log(l0), m1 + jnp.log(l1))

    shape = jax.ShapeDtypeStruct((s, nhp * LANES), F32)
    blk = pl.BlockSpec((tq, pp * LANES), lambda h, i: (i, h))
    return _hosted(name, body, (nhp // pp, nq), [q, k, v],
                   [pl.BlockSpec((tq, 2 * pp * LANES), lambda h, i: (i, h)),
                    pl.BlockSpec((s, 2 * pp * LANES), lambda h, i: (0, h)),
                    pl.BlockSpec((s, pp * LANES), lambda h, i: (0, h))], [shape, shape], [blk, blk], comm)


def _mla_bwd(name, q, k, v, o, lse, do, tq, tk, comm=None):
    s = q.shape[0]
    nhp = v.shape[1] // LANES
    scale = MLA_QK_DIM ** -0.5
    nq = s // tq
    pp = _pairs_per_step(nhp)

    def body(q_ref, k_ref, v_ref, o_ref, lse_ref, do_ref, dq_ref, dk_ref, dv_ref):
        qi = pl.program_id(1)

        @pl.when(qi == 0)
        def _():
            dk_ref[...] = jnp.zeros_like(dk_ref)
            dv_ref[...] = jnp.zeros_like(dv_ref)

        masks = _head_masks()
        nkb = ((qi + 1) * tq + tk - 1) // tk
        dout = do_ref[...]
        doutbs = dout.astype(BF16)
        prod = dout * o_ref[...]
        qhs = [q_ref[:, h * LANES:(h + 1) * LANES] for h in range(2 * pp)]
        dohs = [jnp.where(hm, _pair(dout, pr), 0.0).astype(BF16) for pr in range(pp) for hm in masks]
        totals = [_row_sum(jnp.where(hm, _pair(prod, pr), 0.0)) for pr in range(pp) for hm in masks]
        lse = lse_ref[...]
        lses = [jnp.broadcast_to(lse[:, h * HEAD_DIM:h * HEAD_DIM + 1], (tq, LANES)) for h in range(2 * pp)]

        def step(kb, masked, carry):
            ks = pl.multiple_of(kb * tk, tk)
            heads = range(len(qhs))

            def probs(r, zc, dpc, lsec, totc):
                pr = jnp.exp(zc * scale - _wide(lsec, tk))
                if masked:
                    pr = jnp.where(_causal(qi * tq + r, ks, pr.shape[0], tk, False), pr, 0.0)
                return pr.astype(BF16), (pr * (dpc - _wide(totc, tk)) * scale).astype(BF16)

            khs = [k_ref[pl.ds(ks, tk), h * LANES:(h + 1) * LANES] for h in heads]
            vvs = [v_ref[pl.ds(ks, tk), (h // 2) * LANES:(h // 2 + 1) * LANES] for h in heads]
            zs = [_dot_nt(qhs[h], khs[h]) for h in heads]
            dps = [_dot_nt(dohs[h], vvs[h]) for h in heads]
            both = [_by_rows(probs, 2, zs[h], dps[h], lses[h], totals[h]) for h in heads]
            dks = [_dot_tn(both[h][1], qhs[h]) for h in heads]
            dvs = [_dot_tn(both[h][0], _pair(doutbs, h // 2)) for h in heads]
            dqs = [_dot_nn(both[h][1], khs[h]) for h in heads]
            for h in heads:
                dk_ref[pl.ds(ks, tk), h * LANES:(h + 1) * LANES] += dks[h]
            for pr in range(pp):
                dv_ref[pl.ds(ks, tk), pr * LANES:(pr + 1) * LANES] += jnp.where(masks[0], dvs[2 * pr], dvs[2 * pr + 1])
            return tuple(carry[h] + dqs[h] for h in heads)

        zero = jnp.zeros((tq, LANES), F32)
        dqs = _two_loops((qi * tq) // tk, nkb, False, step, (zero,) * (2 * pp))
        for h in range(2 * pp):
            dq_ref[:, h * LANES:(h + 1) * LANES] = dqs[h]

    blk = pl.BlockSpec((tq, pp * LANES), lambda h, i: (i, h))
    blk2 = pl.BlockSpec((tq, 2 * pp * LANES), lambda h, i: (i, h))
    full = pl.BlockSpec((s, pp * LANES), lambda h, i: (0, h))
    full2 = pl.BlockSpec((s, 2 * pp * LANES), lambda h, i: (0, h))
    return _hosted(name, body, (nhp // pp, nq), [q, k, v, o, lse, do], [blk2, full2, full, blk, blk, blk],
                   [jax.ShapeDtypeStruct(q.shape, F32), jax.ShapeDtypeStruct(k.shape, F32),
                    jax.ShapeDtypeStruct(v.shape, F32)], [blk2, full2, full], comm)


def _norm_parts(x):
    r = lax.rsqrt(jnp.mean(x * x, axis=-1, keepdims=True) + NORM_EPS)
    return r, x * r


def _rmsmod_fwd(x, g, sc, sh):
    _, xh = _norm_parts(x)
    return ((xh * g) * (1.0 + sc) + sh,)


def _rmsmod_bwd(dh, x, dres, g, sc):
    r, xh = _norm_parts(x)
    dy = dh * (1.0 + sc)
    dxh = dy * g
    dx = r * (dxh - xh * jnp.mean(dxh * xh, axis=-1, keepdims=True)) + dres
    return dx, dh, dh * (xh * g), dy * xh


def _rms_bwd_plain(dh, x, g):
    r, xh = _norm_parts(x)
    dxh = dh * g
    return r * (dxh - xh * jnp.mean(dxh * xh, axis=-1, keepdims=True)), dh * xh


def _cat(parts):
    return jnp.concatenate(parts, axis=1)


def _swap_halves(a):
    half = a.shape[-1] // 2
    return jnp.concatenate([a[..., half:], a[..., :half]], axis=-1)


def _adamw_fn(w, g, m, v):
    m = ADAM_B1 * m + (1.0 - ADAM_B1) * g
    v = ADAM_B2 * v + (1.0 - ADAM_B2) * jnp.square(g)
    m_hat = m / (1.0 - ADAM_B1 ** ADAM_STEP)
    v_hat = v / (1.0 - ADAM_B2 ** ADAM_STEP)
    delta = -ADAM_LR * (m_hat / (jnp.sqrt(v_hat) + ADAM_EPS) + ADAM_WD * w)
    return delta, m, v


def _adamw(name, w, g, m, v, comm=None):
    shape = w.shape
    width = shape[-1]
    flat = [t.reshape(-1, width) for t in (w, g, m, v)]
    res = _rowwise(name, _adamw_fn, flat, [], [(width, F32)] * 3, comm=comm)
    res, got = res if comm is not None else (res, None)
    res = [t.reshape(shape) for t in res]
    return res if comm is None else (res, got)


def _sum_adamw(name, land, w, m, v, comm=None):
    shape = w.shape
    width = shape[-1]
    rows = w.size // width

    def fn(*blocks):
        g = blocks[0].astype(F32)
        for b in blocks[1:NDEV]:
            g = g + b.astype(F32)
        return (g,) + _adamw_fn(blocks[NDEV], g, blocks[NDEV + 1], blocks[NDEV + 2])

    def fn_whole(wb, mb, vb, lb):
        return fn(*[lb[i] for i in range(NDEV)], wb, mb, vb)

    flat = [t.reshape(rows, width) for t in (w, m, v)]
    if rows % 16 == 0:
        views = [(land.reshape(NDEV * rows, width), width, 0, i * rows) for i in range(NDEV)]
        res = _rowwise(name, fn, views + flat, [], [(width, F32)] * 4, comm=comm)
    else:
        res = _rowwise(name, fn_whole, flat, [land.reshape(NDEV, rows, width)], [(width, F32)] * 4, comm=comm)
    res, got = res if comm is not None else (res, None)
    res = [t.reshape(shape) for t in res]
    return res if comm is None else (res, got)


def kernel(x, c, positions, w_ada, b_ada, g_mix_norm, w_in, g_q_lat, w_q_up, g_kv_lat, w_kv_up, w_sb_out, w_mla_out, w_mix_out, g_mlp_norm, w_up, w_down, g_final, loss_target, m_w_ada, m_b_ada, m_g_mix_norm, m_w_in, m_g_q_lat, m_w_q_up, m_g_kv_lat, m_w_kv_up, m_w_sb_out, m_w_mla_out, m_w_mix_out, m_g_mlp_norm, m_w_up, m_w_down, m_g_final, v_w_ada, v_b_ada, v_g_mix_norm, v_w_in, v_g_q_lat, v_w_q_up, v_g_kv_lat, v_w_kv_up, v_w_sb_out, v_w_mla_out, v_w_mix_out, v_g_mlp_norm, v_w_up, v_w_down, v_g_final):
    seq, d = x.shape[1], x.shape[2]
    depth = w_ada.shape[0]
    qr, kvr = g_q_lat.shape[1], g_kv_lat.shape[1]
    sbw, mlaw = w_sb_out.shape[1], w_mla_out.shape[1]
    nh = mlaw // HEAD_DIM
    nhp_sb = sbw // LANES
    dff = w_up.shape[2] * NDEV
    ada_n = w_ada.shape[2]
    gb = min(512, d)
    tq, tk = min(256, seq), min(256, seq)
    me = 4 * lax.axis_index("x") + 2 * lax.axis_index("y") + lax.axis_index("c")

    o_qlat = _roundup(3 * sbw, qr)
    o_kvlat = _roundup(o_qlat + qr, kvr)
    o_rope = _roundup(o_kvlat + kvr, 2 * LANES)
    o_gate = _roundup(o_rope + 2 * LANES, gb)
    wp = o_gate + 2 * d

    c_all = _exchange("ag_c", _Comm("gather_all", [c.reshape(d // LANES, LANES)]))[0].reshape(NDEV, d)
    c_act = _rowwise("silu_c", lambda t: (t * (1.0 / (1.0 + jnp.exp(-t))),), [c_all], [], [(d, F32)])[0]
    parts = jnp.stack([_matmul("ada_fwd", c_act, w_ada[l], "nn") for l in range(depth)])
    parts_all = _exchange("ag_mod", _Comm("gather_all", [parts]))[0]
    mine = jnp.transpose(lax.dynamic_index_in_dim(parts_all, me, axis=2, keepdims=False), (1, 0, 2))
    mod = _rowwise("mod_bias", lambda a, b: (a + b,), [mine.reshape(depth, NDEV * ada_n), b_ada], [],
                   [(6 * d, F32)])[0]
    mods = [[mod[l:l + 1, i * d:(i + 1) * d] for i in range(6)] for l in range(depth)]

    big = [w_in, w_q_up, w_kv_up, w_sb_out, w_mla_out, w_mix_out, w_up, w_down]
    row_sharded = [False, False, False, False, False, True, False, True]
    shards = [[w[l].astype(BF16) for w in big] for l in range(depth)]

    ids_a, ids_b = [0, 1, 2, 3, 4, 5], [6, 7]
    pick = lambda l, ids: [shards[l][i] for i in ids]

    def unpack(gathered, ids):
        out = []
        for g, i in zip(gathered, ids):
            _, rows, cols = g.shape
            if i == 0:
                out.append(g)
            elif row_sharded[i]:
                out.append(g.reshape(NDEV * rows, cols))
            else:
                out.append(jnp.transpose(g, (1, 0, 2)).reshape(rows, NDEV * cols))
        return out

    n_in = w_in.shape[2]
    r0 = 3 * sbw + qr + kvr
    g0 = r0 + ROPE_DIM
    runs = [(0, 3 * sbw, 0), (3 * sbw, 3 * sbw + qr, o_qlat), (3 * sbw + qr, r0, o_kvlat), (g0, g0 + 2 * d, o_gate)]

    def shard_cols(g, a, b):
        return [g[j][:, max(a, n_in * j) - n_in * j:min(b, n_in * (j + 1)) - n_in * j]
                for j in range(a // n_in, (b - 1) // n_in + 1)]

    def derive(full):
        wi, wq, wkv, wsb, wmla, wmix = full
        dt = wi.dtype
        z = lambda r, n: jnp.zeros((r, n), dt)
        kr = _cat(shard_cols(wi, r0, g0))
        pieces, at = [], 0
        for a, b, start in runs[:3]:
            pieces += [z(d, start - at)] + shard_cols(wi, a, b)
            at = start + b - a
        pieces += [z(d, o_rope - at), z(d, HEAD_DIM), kr, z(d, LANES - MLA_QK_DIM),
                   z(d, HEAD_DIM), _swap_halves(kr), z(d, LANES - MLA_QK_DIM),
                   z(d, o_gate - o_rope - 2 * LANES)] + shard_cols(wi, g0, g0 + 2 * d)
        w_in_pad = _cat([t for t in pieces if t.shape[1]])
        wq3 = wq.reshape(qr, nh, MLA_QK_DIM)
        z3 = lambda n: jnp.zeros((qr, nh, n), dt)
        rope_w = wq3[:, :, HEAD_DIM:]
        wq_a = jnp.concatenate([wq3[:, :, :HEAD_DIM], rope_w, z3(LANES - MLA_QK_DIM)], axis=2).reshape(qr, nh * LANES)
        wq_b = jnp.concatenate([z3(HEAD_DIM), _swap_halves(rope_w), z3(LANES - MLA_QK_DIM)], axis=2).reshape(qr, nh * LANES)
        wkv3 = wkv.reshape(kvr, nh, 2 * HEAD_DIM)
        wk = jnp.concatenate([wkv3[:, :, :HEAD_DIM], jnp.zeros((kvr, nh, HEAD_DIM), dt)], axis=2).reshape(kvr, nh * LANES)
        wv = wkv3[:, :, HEAD_DIM:].reshape(kvr, nh * HEAD_DIM)
        return dict(w_in=w_in_pad, w_q=_cat([wq_a, wq_b]), w_kv=_cat([wk, wv]), w_sb=wsb, w_mla=wmla, w_mix=wmix)

    def fold(gr):
        gi, gq, gkv = gr["w_in"], gr["w_q"], gr["w_kv"]
        ra = gi[:, o_rope + HEAD_DIM:o_rope + MLA_QK_DIM]
        rb = gi[:, o_rope + LANES + HEAD_DIM:o_rope + LANES + MLA_QK_DIM]
        rope = (ra.astype(F32) + _swap_halves(rb).astype(F32)).astype(gi.dtype)

        def cols(a, b):
            out = []
            for s0, s1, start in runs[:3] + [(r0, g0, None)] + runs[3:]:
                lo, hi = max(a, s0), min(b, s1)
                if lo < hi:
                    out.append(rope[:, lo - r0:hi - r0] if start is None else gi[:, start + lo - s0:start + hi - s0])
            return out

        g_in = jnp.stack([_cat(cols(n_in * j, n_in * (j + 1))) for j in range(NDEV)]).astype(BF16)
        ga = gq[:, :nh * LANES].reshape(qr, nh, LANES)
        gb_ = gq[:, nh * LANES:].reshape(qr, nh, LANES)
        g_q = jnp.concatenate([ga[:, :, :HEAD_DIM], ga[:, :, HEAD_DIM:MLA_QK_DIM]
                               + _swap_halves(gb_[:, :, HEAD_DIM:MLA_QK_DIM])], axis=2).reshape(qr, nh * MLA_QK_DIM)
        gk = gkv[:, :nh * LANES].reshape(kvr, nh, LANES)[:, :, :HEAD_DIM]
        gv = gkv[:, nh * LANES:].reshape(kvr, nh, HEAD_DIM)
        g_kv = jnp.concatenate([gk, gv], axis=2).reshape(kvr, nh * 2 * HEAD_DIM)
        return [g_in, g_q, g_kv, gr["w_sb"], gr["w_mla"], gr["w_mix"], gr["w_up"], gr["w_down"]]

    part_a = _exchange("ag_w0_own", _Comm("gather_own", pick(0, ids_a)))
    ready_a = _exchange("ag_w0_fwd", _Comm("gather_fwd", [], lands=part_a))
    part_b, weights = None, []

    inv_freq = 1.0 / (ROPE_THETA ** (jnp.arange(0, ROPE_DIM, 2, dtype=F32) / ROPE_DIM))
    ang = positions[0].astype(F32)[:, None] * inv_freq
    cos, sin = jnp.cos(ang), jnp.sin(ang)
    tail = jnp.zeros((seq, LANES - MLA_QK_DIM), F32)
    rope_c = _cat([jnp.ones((seq, HEAD_DIM), F32), cos, cos, tail])
    rope_s = _cat([jnp.zeros((seq, HEAD_DIM), F32), -sin, sin, tail])
    zero_vec = lambda n: jnp.zeros((1, n), F32)

    def rope_fwd(q2, kvs, pd, tc, ts):
        c8, s8 = _cat([tc] * nh), _cat([ts] * nh)
        qf = q2[:, :nh * LANES] * c8 + q2[:, nh * LANES:] * s8
        kpe = pd[:, :LANES] * tc + pd[:, LANES:] * ts
        return qf, kvs[:, :nh * LANES] + _cat([kpe] * nh), kvs[:, nh * LANES:]

    def rope_bwd(dq, dk, dv, tc, ts):
        c8, s8 = _cat([tc] * nh), _cat([ts] * nh)
        dks = dk[:, :LANES]
        for h in range(1, nh):
            dks = dks + dk[:, h * LANES:(h + 1) * LANES]
        return _cat([dq * c8, dq * s8]), _cat([dk, dv]), _cat([dks * tc, dks * ts])

    def merge_fwd(gs, gm, osb, omla):
        return osb / (1.0 + jnp.exp(-gs)) + omla / (1.0 + jnp.exp(-gm))

    def merge_bwd(dm, gs, gm, osb, omla):
        ss, sm = 1.0 / (1.0 + jnp.exp(-gs)), 1.0 / (1.0 + jnp.exp(-gm))
        return ss * dm, sm * dm, dm * osb * ss * (1.0 - ss), dm * omla * sm * (1.0 - sm)

    def gates_of(p):
        return [(p, o_gate), (p, o_gate + d)]

    xs = x[0]
    saved = []
    for l in range(depth):
        w = derive(unpack(ready_a, ids_a))
        sh1, sc1, g1, sh2, sc2, g2 = mods[l]
        h1 = _rowwise("norm1", _rmsmod_fwd, [xs], [g_mix_norm[l:l + 1], sc1, sh1], [(d, BF16)])[0]
        p = _matmul("in_proj", h1, w["w_in"], "nn")
        (o_sb, tot_sb), part_b = _sb_fwd("sb_fwd", p, nhp_sb, tq, tk, _Comm("gather_own", pick(l, ids_b)))
        y_sb = _matmul("sb_out", o_sb, w["w_sb"], "nn")
        qn = _rowwise("norm_q", _rmsmod_fwd, [(p, qr, o_qlat // qr, 0)],
                      [g_q_lat[l:l + 1], zero_vec(qr), zero_vec(qr)], [(qr, BF16)])[0]
        kvn = _rowwise("norm_kv", _rmsmod_fwd, [(p, kvr, o_kvlat // kvr, 0)],
                       [g_kv_lat[l:l + 1], zero_vec(kvr), zero_vec(kvr)], [(kvr, BF16)])[0]
        q2 = _matmul("q_up", qn, w["w_q"], "nn")
        kvs = _matmul("kv_up", kvn, w["w_kv"], "nn")
        qf, kf, vf = _rowwise("rope_fwd", rope_fwd, [q2, kvs, (p, 2 * LANES, o_rope // (2 * LANES), 0), rope_c, rope_s],
                              [], [(nh * LANES, BF16), (nh * LANES, BF16), (mlaw, BF16)])
        comms = [_Comm("gather_fwd", [], lands=part_b)]
        if l + 1 < depth:
            comms.append(_Comm("gather_own", pick(l + 1, ids_a)))
        group = _CommGroup(comms)
        (o_mla, lse), got = _mla_fwd("mla_fwd", qf, kf, vf, tq, tk, group)
        got = group.split(got)
        w["w_up"], w["w_down"] = unpack(got[0], ids_b)
        weights.append(w)
        y_mla, merged = _matmul("mla_out", o_mla, w["w_mla"], "nn", outs=[F32, BF16], rows=gates_of(p) + [y_sb],
                                epilogue=lambda acc, gs, gm, osb: (acc, merge_fwd(gs, gm, osb, acc)))
        resid = lambda acc, xv, g: (acc, xv + g * acc)
        y1, x_mid = _matmul("mix_out", merged, w["w_mix"], "nn", epilogue=resid, rows=[xs], vecs=[g1], outs=[F32, F32])
        h2 = _rowwise("norm2", _rmsmod_fwd, [x_mid], [g_mlp_norm[l:l + 1], sc2, sh2], [(d, BF16)])[0]
        relu2 = lambda acc: (acc, jnp.square(jnp.maximum(acc, 0.0)))
        if l + 1 < depth:
            (u, act), ready_a = _matmul("mlp_up", h2, w["w_up"], "nn", outs=[F32, BF16], epilogue=relu2,
                                        comm=_Comm("gather_fwd", [], lands=got[1]))
        else:
            u, act = _matmul("mlp_up", h2, w["w_up"], "nn", outs=[F32, BF16], epilogue=relu2)
        y2, x_out = _matmul("mlp_down", act, w["w_down"], "nn", epilogue=resid, rows=[x_mid], vecs=[g2], outs=[F32, F32])
        saved.append(dict(x=xs, h1=h1, p=p, o_sb=o_sb, tot_sb=tot_sb, y_sb=y_sb, qn=qn, kvn=kvn, qf=qf, kf=kf, vf=vf, o_mla=o_mla,
                          lse=lse, y_mla=y_mla, merged=merged, y1=y1, x_mid=x_mid, h2=h2, u=u, act=act, y2=y2))
        xs = x_out

    def final_fn(xv, tv, y2, g, gate):
        r, xh = _norm_parts(xv)
        diff = xh * g - tv
        dy = diff * (1.0 / d)
        dxh = dy * g
        dx = r * (dxh - xh * jnp.mean(dxh * xh, axis=-1, keepdims=True))
        return dx, dx * gate, diff * diff, dy * xh, dx * y2

    dx, dy2, sq, dg_final, dgate2 = _rowwise(
        "loss_head", final_fn, [xs, loss_target[0], saved[-1]["y2"]], [g_final.reshape(1, d), mods[-1][5]],
        [(d, F32), (d, BF16)], reds=[d, d, d])
    loss = lax.psum(0.5 * jnp.sum(sq) / d, ("x", "y", "c"))

    def norm2_bwd_fn(dh, xv, dres, y1, g, sc, gate):
        dx_mid, dsh, dsc, dg = _rmsmod_bwd(dh, xv, dres, g, sc)
        return dx_mid, dx_mid * gate, dsh, dsc, dg, dx_mid * y1

    def norm1_bwd_fn(dh, xv, dres, y2, g, sc, gate):
        dxv, dsh, dsc, dg = _rmsmod_bwd(dh, xv, dres, g, sc)
        return dxv, dxv * gate, dsh, dsc, dg, dxv * y2

    def chunk(gfull, wref, by_rows):
        rows, cols = wref.shape[1], wref.shape[2]
        if by_rows:
            return gfull.reshape(NDEV, rows, cols).astype(BF16)
        return jnp.transpose(gfull.reshape(rows, NDEV, cols), (1, 0, 2)).astype(BF16)

    dmods, small = [None] * depth, [None] * depth
    late, lands = None, [None] * len(big)
    for l in reversed(range(depth)):
        w, sv = weights[l], saved[l]
        sh1, sc1, g1, sh2, sc2, g2 = mods[l]
        gr = {}
        du = _matmul("mlp_down_dx", dy2, w["w_down"], "nt", outs=[BF16], rows=[sv["u"]],
                     epilogue=lambda acc, uv: (acc * 2.0 * jnp.maximum(uv, 0.0),))
        gr["w_down"] = _matmul("mlp_down_dw", sv["act"], dy2, "tn", BF16)
        dh2 = _matmul("mlp_up_dx", du, w["w_up"], "nt")
        gr["w_up"] = _matmul("mlp_up_dw", sv["h2"], du, "tn", BF16, owner_cols=w_up.shape[2])
        dx_mid, dy1, dsh2, dsc2, dg_mlp, dgate1 = _rowwise(
            "norm2_bwd", norm2_bwd_fn, [dh2, sv["x_mid"], dx, sv["y1"]], [g_mlp_norm[l:l + 1], sc2, g1],
            [(d, F32), (d, BF16)], reds=[d, d, d, d])
        dy_sb, dy_mla, dgate_sb, dgate_mla = _matmul(
            "mix_out_dx", dy1, w["w_mix"], "nt", outs=[BF16] * 4, epilogue=merge_bwd,
            rows=gates_of(sv["p"]) + [sv["y_sb"], sv["y_mla"]])
        gr["w_mix"] = _matmul("mix_out_dw", sv["merged"], dy1, "tn", BF16)
        do_sb = _matmul("sb_out_dx", dy_sb, w["w_sb"], "nt")
        gr["w_sb"] = _matmul("sb_out_dw", sv["o_sb"], dy_sb, "tn", BF16)
        do_mla = _matmul("mla_out_dx", dy_mla, w["w_mla"], "nt")
        gr["w_mla"] = _matmul("mla_out_dw", sv["o_mla"], dy_mla, "tn", BF16)
        ready = {3: gr["w_sb"], 4: gr["w_mla"], 5: gr["w_mix"], 7: gr["w_down"]}
        ready = {i: chunk(g, big[i], row_sharded[i]) for i, g in ready.items()}
        ready[6] = gr["w_up"]
        ids_a = [6] + ([0] if late is not None else [])
        comm_a = _Comm("scatter", [ready[6]] + (late or [])[:1], [lands[i] for i in ids_a], [l, l + 1], depth)
        comm_b = _Comm("scatter", [ready[7]], [lands[7]], [l], depth)
        (dq_sb, dk_sb, dv_sb), got_a = _sb_bwd("sb_bwd", sv["p"], sv["tot_sb"], do_sb, nhp_sb, tq, tk, comm_a)
        (dqf, dkf, dvf), got_b = _mla_bwd("mla_bwd", sv["qf"], sv["kf"], sv["vf"], sv["o_mla"], sv["lse"], do_mla,
                                          tq, tk, comm_b)
        for i, t in zip(ids_a + [7], list(got_a) + list(got_b)):
            lands[i] = t
        dq2, dkvs, drope = _rowwise("rope_bwd", rope_bwd, [dqf, dkf, dvf, rope_c, rope_s], [],
                                    [(2 * nh * LANES, BF16), (nh * LANES + mlaw, BF16), (2 * LANES, BF16)])
        dqn = _matmul("q_up_dx", dq2, w["w_q"], "nt")
        gr["w_q"] = _matmul("q_up_dw", sv["qn"], dq2, "tn")
        dkvn = _matmul("kv_up_dx", dkvs, w["w_kv"], "nt")
        gr["w_kv"] = _matmul("kv_up_dw", sv["kvn"], dkvs, "tn")
        dqlat, dg_q = _rowwise("norm_q_bwd", _rms_bwd_plain, [dqn, (sv["p"], qr, o_qlat // qr, 0)],
                               [g_q_lat[l:l + 1]], [(qr, BF16)], reds=[qr])
        dkvlat, dg_kv = _rowwise("norm_kv_bwd", _rms_bwd_plain, [dkvn, (sv["p"], kvr, o_kvlat // kvr, 0)],
                                 [g_kv_lat[l:l + 1]], [(kvr, BF16)], reds=[kvr])
        zb = lambda n: jnp.zeros((seq, n), BF16)
        dp = _cat([dq_sb.astype(BF16), dk_sb.astype(BF16), dv_sb.astype(BF16), zb(o_qlat - 3 * sbw), dqlat,
                   zb(o_kvlat - o_qlat - qr), dkvlat, zb(o_rope - o_kvlat - kvr), drope,
                   zb(o_gate - o_rope - 2 * LANES), dgate_sb, dgate_mla])
        ids_x = [5] + ([1, 2] if late is not None else [])
        dh1, got = _matmul("in_proj_dx", dp, w["w_in"], "nt",
                           comm=_Comm("scatter", [ready[5]] + (late or [])[1:], [lands[i] for i in ids_x],
                                      [l, l + 1, l + 1], depth))
        for i, t in zip(ids_x, got):
            lands[i] = t
        gr["w_in"], got = _matmul("in_proj_dw", sv["h1"], dp, "tn", BF16,
                                  comm=_Comm("scatter", [ready[3], ready[4]], [lands[3], lands[4]], [l, l], depth))
        lands[3], lands[4] = got
        dmods[l] = [None, None, dgate1, dsh2, dsc2, dgate2]
        if l > 0:
            dx, dy2, dsh1, dsc1, dg_mix, dgate2 = _rowwise(
                "norm1_bwd", norm1_bwd_fn, [dh1, sv["x"], dx_mid, saved[l - 1]["y2"]],
                [g_mix_norm[l:l + 1], sc1, mods[l - 1][5]], [(d, F32), (d, BF16)], reds=[d, d, d, d])
        else:
            dx, dsh1, dsc1, dg_mix = _rowwise("norm1_bwd", _rmsmod_bwd, [dh1, sv["x"], dx_mid],
                                              [g_mix_norm[l:l + 1], sc1], [(d, F32)], reds=[d, d, d])
        dmods[l] = _cat([dsh1, dsc1] + dmods[l][2:])
        small[l] = (dg_mix, dg_q, dg_kv, dg_mlp)
        g_in, g_q, g_kv = fold(gr)[:3]
        late = [g_in, chunk(g_q, big[1], False), chunk(g_kv, big[2], False)]

    small_parts = [jnp.concatenate(dmods, axis=0)]
    small_parts += [jnp.concatenate([small[l][i] for l in range(depth)], axis=0) for i in range(4)]
    small_parts.append(dg_final)
    small_all = _exchange("ag_small", _Comm("gather_all", small_parts))

    dmod_mine = lax.dynamic_slice_in_dim(small_all[0], me * ada_n, ada_n, axis=2)
    c_act_t = jnp.transpose(c_act)

    def outer_fn(ct, dm):
        acc = ct[:, 0:1] * dm[0:1, :]
        for b in range(1, NDEV):
            acc = acc + ct[:, b:b + 1] * dm[b:b + 1, :]
        return (acc,)

    g_w_ada = jnp.stack([_rowwise("ada_dw", outer_fn, [c_act_t], [dmod_mine[:, l, :]], [(ada_n, F32)])[0]
                         for l in range(depth)])

    hosts = ["w_ada", "w_up", "w_down"]
    if lands[0] is not None:
        step = _roundup(d // len(hosts), 16)
        bounds = [min(d, i * step) for i in range(len(hosts) + 1)]
        riders = [dict(srcs=[late[0][:, a:b]], lands=[0], row0=[a]) for a, b in zip(bounds[:-1], bounds[1:])]
        riders[-1] = dict(srcs=riders[-1]["srcs"] + late[1:], lands=[0, 1, 2], row0=riders[-1]["row0"] + [None, None])
    else:
        riders, hosts = [], []
        lands[:3] = _exchange("a2a_last", _Comm("scatter", late, lands[:3], [0] * 3, depth))

    def rider_of(name):
        if name not in hosts:
            return None, None
        r = riders[hosts.index(name)]
        return _Comm("scatter", r["srcs"], [lands[i] for i in r["lands"]], [0] * len(r["srcs"]), depth, r["row0"]), r

    moments = dict(
        w_ada=(w_ada, m_w_ada, v_w_ada), b_ada=(b_ada, m_b_ada, v_b_ada),
        g_mix_norm=(g_mix_norm, m_g_mix_norm, v_g_mix_norm), w_in=(w_in, m_w_in, v_w_in),
        g_q_lat=(g_q_lat, m_g_q_lat, v_g_q_lat), w_q_up=(w_q_up, m_w_q_up, v_w_q_up),
        g_kv_lat=(g_kv_lat, m_g_kv_lat, v_g_kv_lat), w_kv_up=(w_kv_up, m_w_kv_up, v_w_kv_up),
        w_sb_out=(w_sb_out, m_w_sb_out, v_w_sb_out), w_mla_out=(w_mla_out, m_w_mla_out, v_w_mla_out),
        w_mix_out=(w_mix_out, m_w_mix_out, v_w_mix_out), g_mlp_norm=(g_mlp_norm, m_g_mlp_norm, v_g_mlp_norm),
        w_up=(w_up, m_w_up, v_w_up), w_down=(w_down, m_w_down, v_w_down),
        g_final=(g_final.reshape(1, d), m_g_final.reshape(1, d), v_g_final.reshape(1, d)))
    small_lands = dict(b_ada=small_all[0], g_mix_norm=small_all[1], g_q_lat=small_all[2], g_kv_lat=small_all[3],
                       g_mlp_norm=small_all[4], g_final=small_all[5])
    big_index = dict(w_in=0, w_q_up=1, w_kv_up=2, w_sb_out=3, w_mla_out=4, w_mix_out=5, w_up=6, w_down=7)
    results = {}
    for name in hosts + [n for n in moments if n not in hosts]:
        wt, mt, vt = moments[name]
        comm, rider = rider_of(name)
        if name == "w_ada":
            res = _adamw("adamw_" + name, wt, g_w_ada, mt, vt, comm)
        else:
            land = small_lands[name] if name in small_lands else lands[big_index[name]]
            res = _sum_adamw("adamw_" + name, land, wt, mt, vt, comm)
        if comm is not None:
            res, got = res
            for i, t in zip(rider["lands"], got):
                lands[i] = t
        res = ([g_w_ada] if name == "w_ada" else []) + list(res)
        results[name] = [t.reshape(d) for t in res] if name == "g_final" else res
    gs, deltas, new_ms, new_vs = ([results[n][k] for n in moments] for k in range(4))

    return (loss, dx[None], *gs, *deltas, *new_ms, *new_vs)
```

```python
import functools

import jax
import jax.numpy as jnp
from jax import lax
from jax.experimental import pallas as pl
from jax.experimental.pallas import tpu as pltpu

F32 = jnp.float32
BF16 = jnp.bfloat16
NDEV = 8
LANES = 128
HEAD_DIM = 64
ROPE_DIM = 32
MLA_QK_DIM = HEAD_DIM + ROPE_DIM
ROPE_THETA = 10000.0
NORM_EPS = 1e-6
ADAM_LR = 0.001
ADAM_B1 = 0.9
ADAM_B2 = 0.999
ADAM_EPS = 1e-08
ADAM_WD = 0.01
ADAM_STEP = 10
VMEM_LIMIT = 48 * 1024 * 1024


def _pcall(body, **kw):
    return pl.pallas_call(body, **kw)


def _tile(n, pref):
    for t in (512, 384, 256, 128, 64, 32, 16, 8):
        if t <= pref and n % t == 0:
            return t
    return n


def _roundup(n, m):
    return (n + m - 1) // m * m


_CP = pltpu.CompilerParams(vmem_limit_bytes=VMEM_LIMIT)


def _rowwise(name, fn, rows, vecs, outs, reds=(), tb=256, comm=None):
    rows = [r if isinstance(r, tuple) else (r, r.shape[1], 0, 0) for r in rows]
    nrows = None
    for arr, width, col, roff in rows:
        if roff == 0 and nrows is None:
            nrows = arr.shape[0]
    first_off = [r for r in rows if r[3] != 0]
    if first_off:
        nrows = min(nrows, first_off[0][3])
    tb = _tile(nrows, tb)
    nblk = nrows // tb
    n_in = len(rows) + len(vecs)
    n_out = len(outs)

    def body(*refs):
        vals = [r[...] for r in refs[:n_in]]
        res = fn(*vals)
        if not isinstance(res, (tuple, list)):
            res = (res,)
        for ref, val in zip(refs[n_in:n_in + n_out], res[:n_out]):
            ref[...] = val.astype(ref.dtype)
        for ref, val in zip(refs[n_in + n_out:], res[n_out:]):
            @pl.when(pl.program_id(0) == 0)
            def _(ref=ref):
                ref[...] = jnp.zeros_like(ref)
            ref[...] += jnp.sum(val.astype(F32), axis=0, keepdims=True)

    in_specs = []
    for arr, width, col, roff in rows:
        in_specs.append(pl.BlockSpec((tb, width), functools.partial(
            lambda i, col, rb: (rb + i, col), col=col, rb=roff // tb)))
    for v in vecs:
        in_specs.append(pl.BlockSpec(v.shape, lambda i, nd=v.ndim: (0,) * nd))
    out_specs = [pl.BlockSpec((tb, w), lambda i: (i, 0)) for w, _ in outs]
    out_specs += [pl.BlockSpec((1, w), lambda i: (0, 0)) for w in reds]
    out_shape = [jax.ShapeDtypeStruct((nrows, w), dt) for w, dt in outs]
    out_shape += [jax.ShapeDtypeStruct((1, w), F32) for w in reds]
    res, got = _hosted(name, body, (nblk,), [r[0] for r in rows] + list(vecs), in_specs, out_shape, out_specs, comm)
    return res if comm is None else (res, got)


_DIMS = {"nn": (((1,), (0,)), ((), ())), "nt": (((1,), (1,)), ((), ())), "tn": (((0,), (0,)), ((), ()))}


def _matmul(name, a, b, mode, out_dtype=F32, epilogue=None, rows=(), vecs=(), outs=None, comm=None, owner_cols=None):
    if mode == "nn":
        (m, k), n = a.shape, b.shape[1]
    elif mode == "nt":
        (m, k), n = a.shape, b.shape[0]
    else:
        (k, m), n = a.shape, b.shape[1]
    tm = 1024 if m % 1024 == 0 else _tile(m, 512)
    tn = owner_cols or next((t for t in (1536, 1024) if n % t == 0 and n > t), _tile(n, 512))
    dims = _DIMS[mode]
    outs = [out_dtype] if outs is None else outs
    n_extra = len(rows) + len(vecs)
    rows = [r if isinstance(r, tuple) else (r, 0) for r in rows]

    def body(a_ref, b_ref, *refs):
        acc = lax.dot_general(a_ref[...].astype(BF16), b_ref[...].astype(BF16), dims, preferred_element_type=F32)
        res = (acc,) if epilogue is None else epilogue(acc, *[r[...] for r in refs[:n_extra]])
        for o_ref, val in zip(refs[n_extra:], res):
            o_ref[...] = val.astype(o_ref.dtype)

    a_spec = pl.BlockSpec((k, tm), lambda j, i: (0, i)) if mode == "tn" else pl.BlockSpec((tm, k), lambda j, i: (i, 0))
    b_spec = pl.BlockSpec((tn, k), lambda j, i: (j, 0)) if mode == "nt" else pl.BlockSpec((k, tn), lambda j, i: (0, j))
    blk = pl.BlockSpec((tm, tn), lambda j, i: (i, j))
    assert all(off % tn == 0 for _, off in rows), (name, tn)
    row_specs = [pl.BlockSpec((tm, tn), functools.partial(lambda j, i, first: (i, first + j), first=off // tn))
                 for _, off in rows]
    out_blk, out_dims = blk, (m, n)
    if owner_cols:
        out_blk, out_dims = pl.BlockSpec((None, tm, tn), lambda j, i: (j, i, 0)), (n // tn, m, tn)
    res, got = _hosted(name, body, (n // tn, m // tm), [a, b, *[r for r, _ in rows], *vecs],
                       [a_spec, b_spec] + row_specs + [pl.BlockSpec((1, tn), lambda j, i: (0, j))] * len(vecs),
                       [jax.ShapeDtypeStruct(out_dims, dt) for dt in outs], [out_blk] * len(outs), comm)
    res = res[0] if len(outs) == 1 else res
    return res if comm is None else (res, got)


class _Comm:
    KS = {"gather_all": (1, 2, 3, 4, 5, 6, 7), "gather_own": (1, 2, 4, 6), "gather_fwd": (2, 4, 6),
          "scatter": (1, 2, 3, 4, 5, 6, 7)}

    def __init__(self, kind, srcs, lands=None, layers=None, depth=None, row0=None):
        self.kind, self.srcs, self.layers, self.row0 = kind, list(srcs), layers, row0
        self.n = len(lands) if kind == "gather_fwd" else len(srcs)
        self.lands = list(lands) if lands is not None else [None] * self.n
        self.out_shapes = []
        for i, land in enumerate(self.lands):
            if land is not None:
                self.out_shapes.append(jax.ShapeDtypeStruct(land.shape, land.dtype))
            elif kind == "scatter":
                self.out_shapes.append(jax.ShapeDtypeStruct((NDEV, depth) + srcs[i].shape[1:], srcs[i].dtype))
            else:
                self.out_shapes.append(jax.ShapeDtypeStruct((NDEV,) + srcs[i].shape, srcs[i].dtype))
        self.operands = self.srcs + [t for t in self.lands if t is not None]
        self.scratch = [pltpu.SemaphoreType.DMA((NDEV - 1, self.n)), pltpu.SemaphoreType.DMA((NDEV - 1, self.n)),
                        pltpu.SemaphoreType.DMA((self.n,))]

    def aliases(self, first_in, first_out):
        given = [i for i, t in enumerate(self.lands) if t is not None]
        return {first_in + len(self.srcs) + pos: first_out + i for pos, i in enumerate(given)}

    def copies(self, in_refs, out_refs, send_sems, recv_sems, local_sems):
        x, y, c = lax.axis_index("x"), lax.axis_index("y"), lax.axis_index("c")
        me = 4 * x + 2 * y + c
        def landing(i):
            if self.row0 is None or self.row0[i] is None:
                return out_refs[i].at[me, self.layers[i]]
            return out_refs[i].at[me, self.layers[i], pl.ds(self.row0[i], self.srcs[i].shape[1])]

        cps = []
        if self.kind != "gather_fwd":
            for i in range(self.n):
                src = in_refs[i].at[me] if self.kind == "scatter" else in_refs[i]
                dst = landing(i) if self.kind == "scatter" else out_refs[i].at[me]
                cps.append(pltpu.make_async_copy(src, dst, local_sems.at[i]))
        for k in self.KS[self.kind]:
            px = 1 - x if k & 4 else x
            py = 1 - y if k & 2 else y
            pc = 1 - c if k & 1 else c
            peer = 4 * px + 2 * py + pc
            for i in range(self.n):
                if self.kind == "gather_fwd":
                    src, dst, to = in_refs[i].at[peer], out_refs[i].at[peer], (x, y, 1 - c)
                elif self.kind == "scatter":
                    src, dst, to = in_refs[i].at[peer], landing(i), (px, py, pc)
                else:
                    src, dst, to = in_refs[i], out_refs[i].at[me], (px, py, pc)
                cps.append(pltpu.make_async_remote_copy(
                    src_ref=src, dst_ref=dst, send_sem=send_sems.at[k - 1, i], recv_sem=recv_sems.at[k - 1, i],
                    device_id=to, device_id_type=pl.DeviceIdType.MESH))
        return cps


class _CommGroup:
    def __init__(self, comms):
        self.comms = comms
        self.n = sum(cm.n for cm in comms)
        self.operands = [t for cm in comms for t in cm.operands]
        self.out_shapes = [t for cm in comms for t in cm.out_shapes]
        self.scratch = [t for cm in comms for t in cm.scratch]

    def aliases(self, first_in, first_out):
        out = {}
        for cm in self.comms:
            out.update(cm.aliases(first_in, first_out))
            first_in, first_out = first_in + len(cm.operands), first_out + cm.n
        return out

    def copies(self, in_refs, out_refs, *sems):
        cps, i, o = [], 0, 0
        for j, cm in enumerate(self.comms):
            cps += cm.copies(in_refs[i:i + len(cm.operands)], out_refs[o:o + cm.n], *sems[3 * j:3 * j + 3])
            i, o = i + len(cm.operands), o + cm.n
        return cps

    def split(self, outs):
        res, o = [], 0
        for cm in self.comms:
            res.append(list(outs[o:o + cm.n]))
            o += cm.n
        return res


_ANY = pl.BlockSpec(memory_space=pl.ANY)


def _exchange(name, comm):
    nci = len(comm.operands)

    def body(*refs):
        cps = comm.copies(refs[:nci], refs[nci:nci + comm.n], *refs[nci + comm.n:])
        for cp in cps:
            cp.start()
        for cp in cps:
            cp.wait()

    return _pcall(body, name=name, in_specs=[_ANY] * nci, out_specs=[_ANY] * comm.n, out_shape=comm.out_shapes,
                  scratch_shapes=comm.scratch, input_output_aliases=comm.aliases(0, 0))(*comm.operands)


def _hosted(name, body, grid, arrays, in_specs, out_shapes, out_specs, comm):
    if comm is None:
        return _pcall(body, name=name, grid=grid, in_specs=in_specs, out_specs=out_specs, out_shape=out_shapes,
                      compiler_params=_CP)(*arrays), []
    ni, no, nci = len(arrays), len(out_shapes), len(comm.operands)

    def full(*refs):
        ins, cin = refs[:ni], refs[ni:ni + nci]
        outs = refs[ni + nci:ni + nci + no]
        cout = refs[ni + nci + no:ni + nci + no + comm.n]
        sems = refs[ni + nci + no + comm.n:]
        first = functools.reduce(jnp.logical_and, [pl.program_id(a) == 0 for a in range(len(grid))])
        last = functools.reduce(jnp.logical_and, [pl.program_id(a) == grid[a] - 1 for a in range(len(grid))])

        @pl.when(first)
        def _():
            for cp in comm.copies(cin, cout, *sems):
                cp.start()

        body(*ins, *outs)

        @pl.when(last)
        def _():
            for cp in comm.copies(cin, cout, *sems):
                cp.wait()

    res = _pcall(full, name=name, grid=grid, in_specs=list(in_specs) + [_ANY] * nci,
                 out_specs=list(out_specs) + [_ANY] * comm.n, out_shape=list(out_shapes) + comm.out_shapes,
                 scratch_shapes=comm.scratch, input_output_aliases=comm.aliases(ni, no),
                 compiler_params=_CP)(*arrays, *comm.operands)
    return res[:no], res[no:]


def _dot_nt(a, b):
    return lax.dot_general(a, b, _DIMS["nt"], preferred_element_type=F32)


def _dot_tn(a, b):
    return lax.dot_general(a, b, _DIMS["tn"], preferred_element_type=F32)


def _dot_nn(a, b):
    return jnp.dot(a, b, preferred_element_type=F32)


def _tri(tk, rel):
    j = lax.broadcasted_iota(jnp.int32, (tk, tk), 0)
    s = lax.broadcasted_iota(jnp.int32, (tk, tk), 1)
    return {"after": j > s, "upto": j <= s, "before": j < s}[rel].astype(BF16)


def _pairs_per_step(nhp):
    return 2 if nhp % 2 == 0 else 1


def _pair(a, pr):
    return a[:, pr * LANES:(pr + 1) * LANES]


def _head_masks():
    lane = lax.broadcasted_iota(jnp.int32, (1, LANES), 1)
    return [(lane // HEAD_DIM) == h for h in range(2)]


ROW_CHUNK = 32


def _by_rows(fn, n_out, *arrays):
    rows = arrays[0].shape[0]
    step = min(ROW_CHUNK, rows)
    outs = [[] for _ in range(n_out)]
    for r in range(0, rows, step):
        for o, val in zip(outs, fn(r, *[a[r:r + step] for a in arrays])):
            o.append(val)
    return [jnp.concatenate(o, axis=0) for o in outs]


def _causal(r0, k0, rows, tk, strict):
    row = lax.broadcasted_iota(jnp.int32, (rows, tk), 0) + r0
    col = lax.broadcasted_iota(jnp.int32, (rows, tk), 1) + k0
    return col < row if strict else col <= row


def _wide(stat, width):
    return stat if width == LANES else jnp.concatenate([stat] * (width // LANES), axis=1)


def _row_sum(v):
    return jnp.broadcast_to(jnp.sum(v, axis=1, keepdims=True), (v.shape[0], LANES))


def _split_bf16(v):
    hi = v.astype(BF16)
    return hi, (v - hi.astype(F32)).astype(BF16)


def _sb_logs(z, scale, mask):
    z = z * scale
    e = jnp.exp(-jnp.abs(z))
    log_sig = jnp.minimum(z, 0.0) - jnp.log(1.0 + e)
    log_fail = log_sig - z
    return z, log_sig, (log_fail if mask is None else jnp.where(mask, log_fail, 0.0))


def _two_loops(n_full, nkb, near_first, step, carry):
    if near_first:
        carry = lax.fori_loop(0, nkb - n_full, lambda j, c: step(nkb - 1 - j, True, c), carry)
        return lax.fori_loop(0, n_full, lambda j, c: step(n_full - 1 - j, False, c), carry)
    carry = lax.fori_loop(0, n_full, lambda j, c: step(j, False, c), carry)
    return lax.fori_loop(n_full, nkb, lambda j, c: step(j, True, c), carry)


def _sb_fwd(name, p, nhp, tq, tk, comm=None):
    s = p.shape[0]
    scale = HEAD_DIM ** -0.5
    nq = s // tq
    pp = _pairs_per_step(nhp)
    wide = pp * LANES

    def body(q_ref, k_ref, v_ref, o_ref, tot_ref):
        qi = pl.program_id(1)
        masks = _head_masks()
        after = _tri(tk, "after")
        nkb = ((qi + 1) * tq + tk - 1) // tk
        q = q_ref[...]
        qhs = [jnp.where(hm, _pair(q, pr), 0.0).astype(BF16) for pr in range(pp) for hm in masks]

        def step(kb, masked, carry):
            ks = pl.multiple_of(kb * tk, tk)
            ks_all = k_ref[pl.ds(ks, tk), :].astype(BF16)
            vs_all = v_ref[pl.ds(ks, tk), :].astype(BF16)
            mask_of = lambda r, n: _causal(qi * tq + r, ks, n, tk, True) if masked else None
            heads = range(len(qhs))

            def logs(r, zc):
                _, log_sig, log_fail = _sb_logs(zc, scale, mask_of(r, zc.shape[0]))
                return (log_sig,) + _split_bf16(log_fail) + (_row_sum(log_fail),)

            def weights(r, lsc, runc, laterc):
                w = jnp.exp(lsc + runc + _wide(laterc, tk))
                return ((jnp.where(mask_of(r, w.shape[0]), w, 0.0) if masked else w).astype(BF16),)

            zs = [_dot_nt(qhs[i], _pair(ks_all, i // 2)) for i in heads]
            first = [_by_rows(logs, 4, zs[i]) for i in heads]
            runs = [_dot_nn(first[i][1], after) + _dot_nn(first[i][2], after) for i in heads]
            ws = [_by_rows(weights, 1, first[i][0], runs[i], carry[i][0])[0] for i in heads]
            pvs = [_dot_nn(ws[i], _pair(vs_all, i // 2)) for i in heads]
            return tuple((carry[i][0] + first[i][3], carry[i][1] + pvs[i]) for i in heads)

        init = (jnp.zeros((tq, LANES), F32), jnp.zeros((tq, LANES), F32))
        res = _two_loops((qi * tq) // tk, nkb, True, step, (init,) * (2 * pp))
        for pr in range(pp):
            (tot0, acc0), (tot1, acc1) = res[2 * pr], res[2 * pr + 1]
            o_ref[:, pr * LANES:(pr + 1) * LANES] = jnp.where(masks[0], acc0, acc1)
            tot_ref[:, pr * LANES:(pr + 1) * LANES] = jnp.where(masks[0], tot0, tot1)

    ng = nhp // pp
    blk = pl.BlockSpec((tq, wide), lambda h, i: (i, h))
    shape = jax.ShapeDtypeStruct((s, nhp * LANES), F32)
    return _hosted(name, body, (ng, nq), [p, p, p],
                   [blk, pl.BlockSpec((s, wide), lambda h, i: (0, ng + h)),
                    pl.BlockSpec((s, wide), lambda h, i: (0, 2 * ng + h))], [shape, shape], [blk, blk], comm)


def _sb_bwd(name, p, tot, do, nhp, tq, tk, comm=None):
    s = p.shape[0]
    scale = HEAD_DIM ** -0.5
    nq = s // tq
    pp = _pairs_per_step(nhp)
    wide = pp * LANES

    def body(q_ref, k_ref, v_ref, tot_ref, do_ref, dq_ref, dk_ref, dv_ref):
        qi = pl.program_id(1)

        @pl.when(qi == 0)
        def _():
            dk_ref[...] = jnp.zeros_like(dk_ref)
            dv_ref[...] = jnp.zeros_like(dv_ref)

        masks = _head_masks()
        upto, before = _tri(tk, "upto"), _tri(tk, "before")
        nkb = ((qi + 1) * tq + tk - 1) // tk
        q = q_ref[...]
        qbs = q.astype(BF16)
        dout = do_ref[...]
        doutbs = dout.astype(BF16)
        qhs = [jnp.where(hm, _pair(q, pr), 0.0).astype(BF16) for pr in range(pp) for hm in masks]
        dohs = [jnp.where(hm, _pair(dout, pr), 0.0).astype(BF16) for pr in range(pp) for hm in masks]
        tot = tot_ref[...]
        totals = [jnp.broadcast_to(tot[:, h * HEAD_DIM:h * HEAD_DIM + 1], (tq, LANES)) for h in range(2 * pp)]

        def step(kb, masked, carry):
            ks = pl.multiple_of(kb * tk, tk)
            ks_all = k_ref[pl.ds(ks, tk), :].astype(BF16)
            vs_all = v_ref[pl.ds(ks, tk), :].astype(BF16)
            mask_of = lambda r, n: _causal(qi * tq + r, ks, n, tk, True) if masked else None
            heads = range(len(qhs))

            def logs(r, zc):
                _, log_sig, log_fail = _sb_logs(zc, scale, mask_of(r, zc.shape[0]))
                return (log_sig,) + _split_bf16(log_fail) + (_row_sum(log_fail),)

            def weights(r, lsc, runc, basec, dwc):
                w = jnp.exp(lsc + (_wide(basec, tk) - runc))
                if masked:
                    w = jnp.where(mask_of(r, w.shape[0]), w, 0.0)
                g = w * dwc
                return (w.astype(BF16), g) + _split_bf16(g) + (_row_sum(g),)

            def dscore(r, gc, lsc, zc, grc, gbc):
                dz = gc * jnp.exp(lsc - zc * scale) - jnp.exp(lsc) * (_wide(gbc, tk) + grc)
                if masked:
                    dz = jnp.where(mask_of(r, dz.shape[0]), dz, 0.0)
                return ((dz * scale).astype(BF16),)

            zs = [_dot_nt(qhs[i], _pair(ks_all, i // 2)) for i in heads]
            dws = [_dot_nt(dohs[i], _pair(vs_all, i // 2)) for i in heads]
            first = [_by_rows(logs, 4, zs[i]) for i in heads]
            runs = [_dot_nn(first[i][1], upto) + _dot_nn(first[i][2], upto) for i in heads]
            second = [_by_rows(weights, 5, first[i][0], runs[i], totals[i] - carry[i][0], dws[i])
                      for i in heads]
            g_runs = [_dot_nn(second[i][2], before) + _dot_nn(second[i][3], before) for i in heads]
            dzs = [_by_rows(dscore, 1, second[i][1], first[i][0], zs[i], g_runs[i], carry[i][1])[0] for i in heads]
            dks = [_dot_tn(dzs[i], _pair(qbs, i // 2)) for i in heads]
            dvs = [_dot_tn(second[i][0], _pair(doutbs, i // 2)) for i in heads]
            dqs = [_dot_nn(dzs[i], _pair(ks_all, i // 2)) for i in heads]
            for pr in range(pp):
                cols = slice(pr * LANES, (pr + 1) * LANES)
                dk_ref[pl.ds(ks, tk), cols] += jnp.where(masks[0], dks[2 * pr], dks[2 * pr + 1])
                dv_ref[pl.ds(ks, tk), cols] += jnp.where(masks[0], dvs[2 * pr], dvs[2 * pr + 1])
            return tuple((carry[i][0] + first[i][3], carry[i][1] + second[i][4], carry[i][2] + dqs[i]) for i in heads)

        zero = jnp.zeros((tq, LANES), F32)
        res = _two_loops((qi * tq) // tk, nkb, False, step, ((zero, zero, zero),) * (2 * pp))
        for pr in range(pp):
            dq_ref[:, pr * LANES:(pr + 1) * LANES] = jnp.where(masks[0], res[2 * pr][2], res[2 * pr + 1][2])

    ng = nhp // pp
    blk = pl.BlockSpec((tq, wide), lambda h, i: (i, h))
    full = pl.BlockSpec((s, wide), lambda h, i: (0, h))
    shape = jax.ShapeDtypeStruct((s, nhp * LANES), F32)
    return _hosted(name, body, (ng, nq), [p, p, p, tot, do],
                   [blk, pl.BlockSpec((s, wide), lambda h, i: (0, ng + h)),
                    pl.BlockSpec((s, wide), lambda h, i: (0, 2 * ng + h)), blk, blk],
                   [shape, shape, shape], [blk, full, full], comm)


def _mla_fwd(name, q, k, v, tq, tk, comm=None):
    s = q.shape[0]
    nhp = v.shape[1] // LANES
    scale = MLA_QK_DIM ** -0.5
    nq = s // tq
    pp = _pairs_per_step(nhp)

    def body(q_ref, k_ref, v_ref, o_ref, lse_ref):
        qi = pl.program_id(1)
        masks = _head_masks()
        nkb = ((qi + 1) * tq + tk - 1) // tk
        qhs = [q_ref[:, h * LANES:(h + 1) * LANES] for h in range(2 * pp)]

        def step(kb, masked, carry):
            ks = pl.multiple_of(kb * tk, tk)
            heads = range(len(qhs))

            def soft(r, zc, mc, lc):
                zc = zc * scale
                if masked:
                    zc = jnp.where(_causal(qi * tq + r, ks, zc.shape[0], tk, False), zc, -1e30)
                m_new = jnp.maximum(mc, jnp.max(zc, axis=1, keepdims=True))
                a = jnp.exp(mc - m_new)
                pr = jnp.exp(zc - _wide(m_new, tk))
                return pr.astype(BF16), m_new, a * lc + _row_sum(pr), a

            zs = [_dot_nt(qhs[h], k_ref[pl.ds(ks, tk), h * LANES:(h + 1) * LANES]) for h in heads]
            first = [_by_rows(soft, 4, zs[h], carry[h][0], carry[h][1]) for h in heads]
            pvs = [_dot_nn(first[h][0], v_ref[pl.ds(ks, tk), (h // 2) * LANES:(h // 2 + 1) * LANES]) for h in heads]
            accs = [_by_rows(lambda r, ac, aa, pc: (aa * ac + pc,), 1, carry[h][2], first[h][3], pvs[h])[0]
                    for h in heads]
            return tuple((first[h][1], first[h][2], accs[h]) for h in heads)

        init = (jnp.full((tq, LANES), -1e30, F32), jnp.zeros((tq, LANES), F32), jnp.zeros((tq, LANES), F32))
        res = _two_loops((qi * tq) // tk, nkb, False, step, (init,) * (2 * pp))
        for pr in range(pp):
            (m0, l0, acc0), (m1, l1, acc1) = res[2 * pr], res[2 * pr + 1]
            o_ref[:, pr * LANES:(pr + 1) * LANES] = jnp.where(masks[0], acc0 / l0, acc1 / l1)
            lse_ref[:, pr * LANES:(pr + 1) * LANES] = jnp.where(masks[0], m0 + jnp.log(l0), m1 + jnp.log(l1))

    shape = jax.ShapeDtypeStruct((s, nhp * LANES), F32)
    blk = pl.BlockSpec((tq, pp * LANES), lambda h, i: (i, h))
    return _hosted(name, body, (nhp // pp, nq), [q, k, v],
                   [pl.BlockSpec((tq, 2 * pp * LANES), lambda h, i: (i, h)),
                    pl.BlockSpec((s, 2 * pp * LANES), lambda h, i: (0, h)),
                    pl.BlockSpec((s, pp * LANES), lambda h, i: (0, h))], [shape, shape], [blk, blk], comm)


def _mla_bwd(name, q, k, v, o, lse, do, tq, tk, comm=None):
    s = q.shape[0]
    nhp = v.shape[1] // LANES
    scale = MLA_QK_DIM ** -0.5
    nq = s // tq
    pp = _pairs_per_step(nhp)

    def body(q_ref, k_ref, v_ref, o_ref, lse_ref, do_ref, dq_ref, dk_ref, dv_ref):
        qi = pl.program_id(1)

        @pl.when(qi == 0)
        def _():
            dk_ref[...] = jnp.zeros_like(dk_ref)
            dv_ref[...] = jnp.zeros_like(dv_ref)

        masks = _head_masks()
        nkb = ((qi + 1) * tq + tk - 1) // tk
        dout = do_ref[...]
        doutbs = dout.astype(BF16)
        prod = dout * o_ref[...]
        qhs = [q_ref[:, h * LANES:(h + 1) * LANES] for h in range(2 * pp)]
        dohs = [jnp.where(hm, _pair(dout, pr), 0.0).astype(BF16) for pr in range(pp) for hm in masks]
        totals = [_row_sum(jnp.where(hm, _pair(prod, pr), 0.0)) for pr in range(pp) for hm in masks]
        lse = lse_ref[...]
        lses = [jnp.broadcast_to(lse[:, h * HEAD_DIM:h * HEAD_DIM + 1], (tq, LANES)) for h in range(2 * pp)]

        def step(kb, masked, carry):
            ks = pl.multiple_of(kb * tk, tk)
            heads = range(len(qhs))

            def probs(r, zc, dpc, lsec, totc):
                pr = jnp.exp(zc * scale - _wide(lsec, tk))
                if masked:
                    pr = jnp.where(_causal(qi * tq + r, ks, pr.shape[0], tk, False), pr, 0.0)
                return pr.astype(BF16), (pr * (dpc - _wide(totc, tk)) * scale).astype(BF16)

            khs = [k_ref[pl.ds(ks, tk), h * LANES:(h + 1) * LANES] for h in heads]
            vvs = [v_ref[pl.ds(ks, tk), (h // 2) * LANES:(h // 2 + 1) * LANES] for h in heads]
            zs = [_dot_nt(qhs[h], khs[h]) for h in heads]
            dps = [_dot_nt(dohs[h], vvs[h]) for h in heads]
            both = [_by_rows(probs, 2, zs[h], dps[h], lses[h], totals[h]) for h in heads]
            dks = [_dot_tn(both[h][1], qhs[h]) for h in heads]
            dvs = [_dot_tn(both[h][0], _pair(doutbs, h // 2)) for h in heads]
            dqs = [_dot_nn(both[h][1], khs[h]) for h in heads]
            for h in heads:
                dk_ref[pl.ds(ks, tk), h * LANES:(h + 1) * LANES] += dks[h]
            for pr in range(pp):
                dv_ref[pl.ds(ks, tk), pr * LANES:(pr + 1) * LANES] += jnp.where(masks[0], dvs[2 * pr], dvs[2 * pr + 1])
            return tuple(carry[h] + dqs[h] for h in heads)

        zero = jnp.zeros((tq, LANES), F32)
        dqs = _two_loops((qi * tq) // tk, nkb, False, step, (zero,) * (2 * pp))
        for h in range(2 * pp):
            dq_ref[:, h * LANES:(h + 1) * LANES] = dqs[h]

    blk = pl.BlockSpec((tq, pp * LANES), lambda h, i: (i, h))
    blk2 = pl.BlockSpec((tq, 2 * pp * LANES), lambda h, i: (i, h))
    full = pl.BlockSpec((s, pp * LANES), lambda h, i: (0, h))
    full2 = pl.BlockSpec((s, 2 * pp * LANES), lambda h, i: (0, h))
    return _hosted(name, body, (nhp // pp, nq), [q, k, v, o, lse, do], [blk2, full2, full, blk, blk, blk],
                   [jax.ShapeDtypeStruct(q.shape, F32), jax.ShapeDtypeStruct(k.shape, F32),
                    jax.ShapeDtypeStruct(v.shape, F32)], [blk2, full2, full], comm)


def _norm_parts(x):
    r = lax.rsqrt(jnp.mean(x * x, axis=-1, keepdims=True) + NORM_EPS)
    return r, x * r


def _rmsmod_fwd(x, g, sc, sh):
    _, xh = _norm_parts(x)
    return ((xh * g) * (1.0 + sc) + sh,)


def _rmsmod_bwd(dh, x, dres, g, sc):
    r, xh = _norm_parts(x)
    dy = dh * (1.0 + sc)
    dxh = dy * g
    dx = r * (dxh - xh * jnp.mean(dxh * xh, axis=-1, keepdims=True)) + dres
    return dx, dh, dh * (xh * g), dy * xh


def _rms_bwd_plain(dh, x, g):
    r, xh = _norm_parts(x)
    dxh = dh * g
    return r * (dxh - xh * jnp.mean(dxh * xh, axis=-1, keepdims=True)), dh * xh


def _cat(parts):
    return jnp.concatenate(parts, axis=1)


def _swap_halves(a):
    half = a.shape[-1] // 2
    return jnp.concatenate([a[..., half:], a[..., :half]], axis=-1)


def _adamw_fn(w, g, m, v):
    m = ADAM_B1 * m + (1.0 - ADAM_B1) * g
    v = ADAM_B2 * v + (1.0 - ADAM_B2) * jnp.square(g)
    m_hat = m / (1.0 - ADAM_B1 ** ADAM_STEP)
    v_hat = v / (1.0 - ADAM_B2 ** ADAM_STEP)
    delta = -ADAM_LR * (m_hat / (jnp.sqrt(v_hat) + ADAM_EPS) + ADAM_WD * w)
    return delta, m, v


def _adamw(name, w, g, m, v, comm=None):
    shape = w.shape
    width = shape[-1]
    flat = [t.reshape(-1, width) for t in (w, g, m, v)]
    res = _rowwise(name, _adamw_fn, flat, [], [(width, F32)] * 3, comm=comm)
    res, got = res if comm is not None else (res, None)
    res = [t.reshape(shape) for t in res]
    return res if comm is None else (res, got)


def _sum_adamw(name, land, w, m, v, comm=None):
    shape = w.shape
    width = shape[-1]
    rows = w.size // width

    def fn(*blocks):
        g = blocks[0].astype(F32)
        for b in blocks[1:NDEV]:
            g = g + b.astype(F32)
        return (g,) + _adamw_fn(blocks[NDEV], g, blocks[NDEV + 1], blocks[NDEV + 2])

    def fn_whole(wb, mb, vb, lb):
        return fn(*[lb[i] for i in range(NDEV)], wb, mb, vb)

    flat = [t.reshape(rows, width) for t in (w, m, v)]
    if rows % 16 == 0:
        views = [(land.reshape(NDEV * rows, width), width, 0, i * rows) for i in range(NDEV)]
        res = _rowwise(name, fn, views + flat, [], [(width, F32)] * 4, comm=comm)
    else:
        res = _rowwise(name, fn_whole, flat, [land.reshape(NDEV, rows, width)], [(width, F32)] * 4, comm=comm)
    res, got = res if comm is not None else (res, None)
    res = [t.reshape(shape) for t in res]
    return res if comm is None else (res, got)


def kernel(x, c, positions, w_ada, b_ada, g_mix_norm, w_in, g_q_lat, w_q_up, g_kv_lat, w_kv_up, w_sb_out, w_mla_out, w_mix_out, g_mlp_norm, w_up, w_down, g_final, loss_target, m_w_ada, m_b_ada, m_g_mix_norm, m_w_in, m_g_q_lat, m_w_q_up, m_g_kv_lat, m_w_kv_up, m_w_sb_out, m_w_mla_out, m_w_mix_out, m_g_mlp_norm, m_w_up, m_w_down, m_g_final, v_w_ada, v_b_ada, v_g_mix_norm, v_w_in, v_g_q_lat, v_w_q_up, v_g_kv_lat, v_w_kv_up, v_w_sb_out, v_w_mla_out, v_w_mix_out, v_g_mlp_norm, v_w_up, v_w_down, v_g_final):
    seq, d = x.shape[1], x.shape[2]
    depth = w_ada.shape[0]
    qr, kvr = g_q_lat.shape[1], g_kv_lat.shape[1]
    sbw, mlaw = w_sb_out.shape[1], w_mla_out.shape[1]
    nh = mlaw // HEAD_DIM
    nhp_sb = sbw // LANES
    dff = w_up.shape[2] * NDEV
    ada_n = w_ada.shape[2]
    gb = min(512, d)
    tq, tk = min(256, seq), min(256, seq)
    me = 4 * lax.axis_index("x") + 2 * lax.axis_index("y") + lax.axis_index("c")

    o_qlat = _roundup(3 * sbw, qr)
    o_kvlat = _roundup(o_qlat + qr, kvr)
    o_rope = _roundup(o_kvlat + kvr, 2 * LANES)
    o_gate = _roundup(o_rope + 2 * LANES, gb)
    wp = o_gate + 2 * d

    c_all = _exchange("ag_c", _Comm("gather_all", [c.reshape(d // LANES, LANES)]))[0].reshape(NDEV, d)
    c_act = _rowwise("silu_c", lambda t: (t * (1.0 / (1.0 + jnp.exp(-t))),), [c_all], [], [(d, F32)])[0]
    parts = jnp.stack([_matmul("ada_fwd", c_act, w_ada[l], "nn") for l in range(depth)])
    parts_all = _exchange("ag_mod", _Comm("gather_all", [parts]))[0]
    mine = jnp.transpose(lax.dynamic_index_in_dim(parts_all, me, axis=2, keepdims=False), (1, 0, 2))
    mod = _rowwise("mod_bias", lambda a, b: (a + b,), [mine.reshape(depth, NDEV * ada_n), b_ada], [],
                   [(6 * d, F32)])[0]
    mods = [[mod[l:l + 1, i * d:(i + 1) * d] for i in range(6)] for l in range(depth)]

    big = [w_in, w_q_up, w_kv_up, w_sb_out, w_mla_out, w_mix_out, w_up, w_down]
    row_sharded = [False, False, False, False, False, True, False, True]
    shards = [[w[l].astype(BF16) for w in big] for l in range(depth)]

    ids_a, ids_b = [0, 1, 2, 3, 4, 5], [6, 7]
    pick = lambda l, ids: [shards[l][i] for i in ids]

    def unpack(gathered, ids):
        out = []
        for g, i in zip(gathered, ids):
            _, rows, cols = g.shape
            if i == 0:
                out.append(g)
            elif row_sharded[i]:
                out.append(g.reshape(NDEV * rows, cols))
            else:
                out.append(jnp.transpose(g, (1, 0, 2)).reshape(rows, NDEV * cols))
        return out

    n_in = w_in.shape[2]
    r0 = 3 * sbw + qr + kvr
    g0 = r0 + ROPE_DIM
    runs = [(0, 3 * sbw, 0), (3 * sbw, 3 * sbw + qr, o_qlat), (3 * sbw + qr, r0, o_kvlat), (g0, g0 + 2 * d, o_gate)]

    def shard_cols(g, a, b):
        return [g[j][:, max(a, n_in * j) - n_in * j:min(b, n_in * (j + 1)) - n_in * j]
                for j in range(a // n_in, (b - 1) // n_in + 1)]

    def derive(full):
        wi, wq, wkv, wsb, wmla, wmix = full
        dt = wi.dtype
        z = lambda r, n: jnp.zeros((r, n), dt)
        kr = _cat(shard_cols(wi, r0, g0))
        pieces, at = [], 0
        for a, b, start in runs[:3]:
            pieces += [z(d, start - at)] + shard_cols(wi, a, b)
            at = start + b - a
        pieces += [z(d, o_rope - at), z(d, HEAD_DIM), kr, z(d, LANES - MLA_QK_DIM),
                   z(d, HEAD_DIM), _swap_halves(kr), z(d, LANES - MLA_QK_DIM),
                   z(d, o_gate - o_rope - 2 * LANES)] + shard_cols(wi, g0, g0 + 2 * d)
        w_in_pad = _cat([t for t in pieces if t.shape[1]])
        wq3 = wq.reshape(qr, nh, MLA_QK_DIM)
        z3 = lambda n: jnp.zeros((qr, nh, n), dt)
        rope_w = wq3[:, :, HEAD_DIM:]
        wq_a = jnp.concatenate([wq3[:, :, :HEAD_DIM], rope_w, z3(LANES - MLA_QK_DIM)], axis=2).reshape(qr, nh * LANES)
        wq_b = jnp.concatenate([z3(HEAD_DIM), _swap_halves(rope_w), z3(LANES - MLA_QK_DIM)], axis=2).reshape(qr, nh * LANES)
        wkv3 = wkv.reshape(kvr, nh, 2 * HEAD_DIM)
        wk = jnp.concatenate([wkv3[:, :, :HEAD_DIM], jnp.zeros((kvr, nh, HEAD_DIM), dt)], axis=2).reshape(kvr, nh * LANES)
        wv = wkv3[:, :, HEAD_DIM:].reshape(kvr, nh * HEAD_DIM)
        return dict(w_in=w_in_pad, w_q=_cat([wq_a, wq_b]), w_kv=_cat([wk, wv]), w_sb=wsb, w_mla=wmla, w_mix=wmix)

    def fold(gr):
        gi, gq, gkv = gr["w_in"], gr["w_q"], gr["w_kv"]
        ra = gi[:, o_rope + HEAD_DIM:o_rope + MLA_QK_DIM]
        rb = gi[:, o_rope + LANES + HEAD_DIM:o_rope + LANES + MLA_QK_DIM]
        rope = (ra.astype(F32) + _swap_halves(rb).astype(F32)).astype(gi.dtype)

        def cols(a, b):
            out = []
            for s0, s1, start in runs[:3] + [(r0, g0, None)] + runs[3:]:
                lo, hi = max(a, s0), min(b, s1)
                if lo < hi:
                    out.append(rope[:, lo - r0:hi - r0] if start is None else gi[:, start + lo - s0:start + hi - s0])
            return out

        g_in = jnp.stack([_cat(cols(n_in * j, n_in * (j + 1))) for j in range(NDEV)]).astype(BF16)
        ga = gq[:, :nh * LANES].reshape(qr, nh, LANES)
        gb_ = gq[:, nh * LANES:].reshape(qr, nh, LANES)
        g_q = jnp.concatenate([ga[:, :, :HEAD_DIM], ga[:, :, HEAD_DIM:MLA_QK_DIM]
                               + _swap_halves(gb_[:, :, HEAD_DIM:MLA_QK_DIM])], axis=2).reshape(qr, nh * MLA_QK_DIM)
        gk = gkv[:, :nh * LANES].reshape(kvr, nh, LANES)[:, :, :HEAD_DIM]
        gv = gkv[:, nh * LANES:].reshape(kvr, nh, HEAD_DIM)
        g_kv = jnp.concatenate([gk, gv], axis=2).reshape(kvr, nh * 2 * HEAD_DIM)
        return [g_in, g_q, g_kv, gr["w_sb"], gr["w_mla"], gr["w_mix"], gr["w_up"], gr["w_down"]]

    part_a = _exchange("ag_w0_own", _Comm("gather_own", pick(0, ids_a)))
    ready_a = _exchange("ag_w0_fwd", _Comm("gather_fwd", [], lands=part_a))
    part_b, weights = None, []

    inv_freq = 1.0 / (ROPE_THETA ** (jnp.arange(0, ROPE_DIM, 2, dtype=F32) / ROPE_DIM))
    ang = positions[0].astype(F32)[:, None] * inv_freq
    cos, sin = jnp.cos(ang), jnp.sin(ang)
    tail = jnp.zeros((seq, LANES - MLA_QK_DIM), F32)
    rope_c = _cat([jnp.ones((seq, HEAD_DIM), F32), cos, cos, tail])
    rope_s = _cat([jnp.zeros((seq, HEAD_DIM), F32), -sin, sin, tail])
    zero_vec = lambda n: jnp.zeros((1, n), F32)

    def rope_fwd(q2, kvs, pd, tc, ts):
        c8, s8 = _cat([tc] * nh), _cat([ts] * nh)
        qf = q2[:, :nh * LANES] * c8 + q2[:, nh * LANES:] * s8
        kpe = pd[:, :LANES] * tc + pd[:, LANES:] * ts
        return qf, kvs[:, :nh * LANES] + _cat([kpe] * nh), kvs[:, nh * LANES:]

    def rope_bwd(dq, dk, dv, tc, ts):
        c8, s8 = _cat([tc] * nh), _cat([ts] * nh)
        dks = dk[:, :LANES]
        for h in range(1, nh):
            dks = dks + dk[:, h * LANES:(h + 1) * LANES]
        return _cat([dq * c8, dq * s8]), _cat([dk, dv]), _cat([dks * tc, dks * ts])

    def merge_fwd(gs, gm, osb, omla):
        return osb / (1.0 + jnp.exp(-gs)) + omla / (1.0 + jnp.exp(-gm))

    def merge_bwd(dm, gs, gm, osb, omla):
        ss, sm = 1.0 / (1.0 + jnp.exp(-gs)), 1.0 / (1.0 + jnp.exp(-gm))
        return ss * dm, sm * dm, dm * osb * ss * (1.0 - ss), dm * omla * sm * (1.0 - sm)

    def gates_of(p):
        return [(p, o_gate), (p, o_gate + d)]

    xs = x[0]
    saved = []
    for l in range(depth):
        w = derive(unpack(ready_a, ids_a))
        sh1, sc1, g1, sh2, sc2, g2 = mods[l]
        h1 = _rowwise("norm1", _rmsmod_fwd, [xs], [g_mix_norm[l:l + 1], sc1, sh1], [(d, BF16)])[0]
        p = _matmul("in_proj", h1, w["w_in"], "nn")
        (o_sb, tot_sb), part_b = _sb_fwd("sb_fwd", p, nhp_sb, tq, tk, _Comm("gather_own", pick(l, ids_b)))
        y_sb = _matmul("sb_out", o_sb, w["w_sb"], "nn")
        qn = _rowwise("norm_q", _rmsmod_fwd, [(p, qr, o_qlat // qr, 0)],
                      [g_q_lat[l:l + 1], zero_vec(qr), zero_vec(qr)], [(qr, BF16)])[0]
        kvn = _rowwise("norm_kv", _rmsmod_fwd, [(p, kvr, o_kvlat // kvr, 0)],
                       [g_kv_lat[l:l + 1], zero_vec(kvr), zero_vec(kvr)], [(kvr, BF16)])[0]
        q2 = _matmul("q_up", qn, w["w_q"], "nn")
        kvs = _matmul("kv_up", kvn, w["w_kv"], "nn")
        qf, kf, vf = _rowwise("rope_fwd", rope_fwd, [q2, kvs, (p, 2 * LANES, o_rope // (2 * LANES), 0), rope_c, rope_s],
                              [], [(nh * LANES, BF16), (nh * LANES, BF16), (mlaw, BF16)])
        comms = [_Comm("gather_fwd", [], lands=part_b)]
        if l + 1 < depth:
            comms.append(_Comm("gather_own", pick(l + 1, ids_a)))
        group = _CommGroup(comms)
        (o_mla, lse), got = _mla_fwd("mla_fwd", qf, kf, vf, tq, tk, group)
        got = group.split(got)
        w["w_up"], w["w_down"] = unpack(got[0], ids_b)
        weights.append(w)
        y_mla, merged = _matmul("mla_out", o_mla, w["w_mla"], "nn", outs=[F32, BF16], rows=gates_of(p) + [y_sb],
                                epilogue=lambda acc, gs, gm, osb: (acc, merge_fwd(gs, gm, osb, acc)))
        resid = lambda acc, xv, g: (acc, xv + g * acc)
        y1, x_mid = _matmul("mix_out", merged, w["w_mix"], "nn", epilogue=resid, rows=[xs], vecs=[g1], outs=[F32, F32])
        h2 = _rowwise("norm2", _rmsmod_fwd, [x_mid], [g_mlp_norm[l:l + 1], sc2, sh2], [(d, BF16)])[0]
        relu2 = lambda acc: (acc, jnp.square(jnp.maximum(acc, 0.0)))
        if l + 1 < depth:
            (u, act), ready_a = _matmul("mlp_up", h2, w["w_up"], "nn", outs=[F32, BF16], epilogue=relu2,
                                        comm=_Comm("gather_fwd", [], lands=got[1]))
        else:
            u, act = _matmul("mlp_up", h2, w["w_up"], "nn", outs=[F32, BF16], epilogue=relu2)
        y2, x_out = _matmul("mlp_down", act, w["w_down"], "nn", epilogue=resid, rows=[x_mid], vecs=[g2], outs=[F32, F32])
        saved.append(dict(x=xs, h1=h1, p=p, o_sb=o_sb, tot_sb=tot_sb, y_sb=y_sb, qn=qn, kvn=kvn, qf=qf, kf=kf, vf=vf, o_mla=o_mla,
                          lse=lse, y_mla=y_mla, merged=merged, y1=y1, x_mid=x_mid, h2=h2, u=u, act=act, y2=y2))
        xs = x_out

    def final_fn(xv, tv, y2, g, gate):
        r, xh = _norm_parts(xv)
        diff = xh * g - tv
        dy = diff * (1.0 / d)
        dxh = dy * g
        dx = r * (dxh - xh * jnp.mean(dxh * xh, axis=-1, keepdims=True))
        return dx, dx * gate, diff * diff, dy * xh, dx * y2

    dx, dy2, sq, dg_final, dgate2 = _rowwise(
        "loss_head", final_fn, [xs, loss_target[0], saved[-1]["y2"]], [g_final.reshape(1, d), mods[-1][5]],
        [(d, F32), (d, BF16)], reds=[d, d, d])
    loss = lax.psum(0.5 * jnp.sum(sq) / d, ("x", "y", "c"))

    def norm2_bwd_fn(dh, xv, dres, y1, g, sc, gate):
        dx_mid, dsh, dsc, dg = _rmsmod_bwd(dh, xv, dres, g, sc)
        return dx_mid, dx_mid * gate, dsh, dsc, dg, dx_mid * y1

    def norm1_bwd_fn(dh, xv, dres, y2, g, sc, gate):
        dxv, dsh, dsc, dg = _rmsmod_bwd(dh, xv, dres, g, sc)
        return dxv, dxv * gate, dsh, dsc, dg, dxv * y2

    def chunk(gfull, wref, by_rows):
        rows, cols = wref.shape[1], wref.shape[2]
        if by_rows:
            return gfull.reshape(NDEV, rows, cols).astype(BF16)
        return jnp.transpose(gfull.reshape(rows, NDEV, cols), (1, 0, 2)).astype(BF16)

    dmods, small = [None] * depth, [None] * depth
    late, lands = None, [None] * len(big)
    for l in reversed(range(depth)):
        w, sv = weights[l], saved[l]
        sh1, sc1, g1, sh2, sc2, g2 = mods[l]
        gr = {}
        du = _matmul("mlp_down_dx", dy2, w["w_down"], "nt", outs=[BF16], rows=[sv["u"]],
                     epilogue=lambda acc, uv: (acc * 2.0 * jnp.maximum(uv, 0.0),))
        gr["w_down"] = _matmul("mlp_down_dw", sv["act"], dy2, "tn", BF16)
        dh2 = _matmul("mlp_up_dx", du, w["w_up"], "nt")
        gr["w_up"] = _matmul("mlp_up_dw", sv["h2"], du, "tn", BF16, owner_cols=w_up.shape[2])
        dx_mid, dy1, dsh2, dsc2, dg_mlp, dgate1 = _rowwise(
            "norm2_bwd", norm2_bwd_fn, [dh2, sv["x_mid"], dx, sv["y1"]], [g_mlp_norm[l:l + 1], sc2, g1],
            [(d, F32), (d, BF16)], reds=[d, d, d, d])
        dy_sb, dy_mla, dgate_sb, dgate_mla = _matmul(
            "mix_out_dx", dy1, w["w_mix"], "nt", outs=[BF16] * 4, epilogue=merge_bwd,
            rows=gates_of(sv["p"]) + [sv["y_sb"], sv["y_mla"]])
        gr["w_mix"] = _matmul("mix_out_dw", sv["merged"], dy1, "tn", BF16)
        do_sb = _matmul("sb_out_dx", dy_sb, w["w_sb"], "nt")
        gr["w_sb"] = _matmul("sb_out_dw", sv["o_sb"], dy_sb, "tn", BF16)
        do_mla = _matmul("mla_out_dx", dy_mla, w["w_mla"], "nt")
        gr["w_mla"] = _matmul("mla_out_dw", sv["o_mla"], dy_mla, "tn", BF16)
        ready = {3: gr["w_sb"], 4: gr["w_mla"], 5: gr["w_mix"], 7: gr["w_down"]}
        ready = {i: chunk(g, big[i], row_sharded[i]) for i, g in ready.items()}
        ready[6] = gr["w_up"]
        ids_a = [6] + ([0, 1, 2] if late is not None else [])
        comm_a = _Comm("scatter", [ready[6]] + (late or []), [lands[i] for i in ids_a], [l] + [l + 1] * 3, depth)
        comm_b = _Comm("scatter", [ready[7]], [lands[7]], [l], depth)
        (dq_sb, dk_sb, dv_sb), got_a = _sb_bwd("sb_bwd", sv["p"], sv["tot_sb"], do_sb, nhp_sb, tq, tk, comm_a)
        (dqf, dkf, dvf), got_b = _mla_bwd("mla_bwd", sv["qf"], sv["kf"], sv["vf"], sv["o_mla"], sv["lse"], do_mla,
                                          tq, tk, comm_b)
        for i, t in zip(ids_a + [7], list(got_a) + list(got_b)):
            lands[i] = t
        dq2, dkvs, drope = _rowwise("rope_bwd", rope_bwd, [dqf, dkf, dvf, rope_c, rope_s], [],
                                    [(2 * nh * LANES, BF16), (nh * LANES + mlaw, BF16), (2 * LANES, BF16)])
        dqn = _matmul("q_up_dx", dq2, w["w_q"], "nt")
        gr["w_q"] = _matmul("q_up_dw", sv["qn"], dq2, "tn")
        dkvn = _matmul("kv_up_dx", dkvs, w["w_kv"], "nt")
        gr["w_kv"] = _matmul("kv_up_dw", sv["kvn"], dkvs, "tn")
        dqlat, dg_q = _rowwise("norm_q_bwd", _rms_bwd_plain, [dqn, (sv["p"], qr, o_qlat // qr, 0)],
                               [g_q_lat[l:l + 1]], [(qr, BF16)], reds=[qr])
        dkvlat, dg_kv = _rowwise("norm_kv_bwd", _rms_bwd_plain, [dkvn, (sv["p"], kvr, o_kvlat // kvr, 0)],
                                 [g_kv_lat[l:l + 1]], [(kvr, BF16)], reds=[kvr])
        zb = lambda n: jnp.zeros((seq, n), BF16)
        dp = _cat([dq_sb.astype(BF16), dk_sb.astype(BF16), dv_sb.astype(BF16), zb(o_qlat - 3 * sbw), dqlat,
                   zb(o_kvlat - o_qlat - qr), dkvlat, zb(o_rope - o_kvlat - kvr), drope,
                   zb(o_gate - o_rope - 2 * LANES), dgate_sb, dgate_mla])
        dh1, got = _matmul("in_proj_dx", dp, w["w_in"], "nt", comm=_Comm("scatter", [ready[5]], [lands[5]], [l], depth))
        lands[5] = got[0]
        gr["w_in"], got = _matmul("in_proj_dw", sv["h1"], dp, "tn", BF16,
                                  comm=_Comm("scatter", [ready[3], ready[4]], [lands[3], lands[4]], [l, l], depth))
        lands[3], lands[4] = got
        dmods[l] = [None, None, dgate1, dsh2, dsc2, dgate2]
        if l > 0:
            dx, dy2, dsh1, dsc1, dg_mix, dgate2 = _rowwise(
                "norm1_bwd", norm1_bwd_fn, [dh1, sv["x"], dx_mid, saved[l - 1]["y2"]],
                [g_mix_norm[l:l + 1], sc1, mods[l - 1][5]], [(d, F32), (d, BF16)], reds=[d, d, d, d])
        else:
            dx, dsh1, dsc1, dg_mix = _rowwise("norm1_bwd", _rmsmod_bwd, [dh1, sv["x"], dx_mid],
                                              [g_mix_norm[l:l + 1], sc1], [(d, F32)], reds=[d, d, d])
        dmods[l] = _cat([dsh1, dsc1] + dmods[l][2:])
        small[l] = (dg_mix, dg_q, dg_kv, dg_mlp)
        g_in, g_q, g_kv = fold(gr)[:3]
        late = [g_in, chunk(g_q, big[1], False), chunk(g_kv, big[2], False)]

    small_parts = [jnp.concatenate(dmods, axis=0)]
    small_parts += [jnp.concatenate([small[l][i] for l in range(depth)], axis=0) for i in range(4)]
    small_parts.append(dg_final)
    small_all = _exchange("ag_small", _Comm("gather_all", small_parts))

    dmod_mine = lax.dynamic_slice_in_dim(small_all[0], me * ada_n, ada_n, axis=2)
    c_act_t = jnp.transpose(c_act)

    def outer_fn(ct, dm):
        acc = ct[:, 0:1] * dm[0:1, :]
        for b in range(1, NDEV):
            acc = acc + ct[:, b:b + 1] * dm[b:b + 1, :]
        return (acc,)

    g_w_ada = jnp.stack([_rowwise("ada_dw", outer_fn, [c_act_t], [dmod_mine[:, l, :]], [(ada_n, F32)])[0]
                         for l in range(depth)])

    hosts = ["w_ada", "w_up", "w_down"]
    if lands[0] is not None:
        step = _roundup(d // len(hosts), 16)
        bounds = [min(d, i * step) for i in range(len(hosts) + 1)]
        riders = [dict(srcs=[late[0][:, a:b]], lands=[0], row0=[a]) for a, b in zip(bounds[:-1], bounds[1:])]
        riders[-1] = dict(srcs=riders[-1]["srcs"] + late[1:], lands=[0, 1, 2], row0=riders[-1]["row0"] + [None, None])
    else:
        riders, hosts = [], []
        lands[:3] = _exchange("a2a_last", _Comm("scatter", late, lands[:3], [0] * 3, depth))

    def rider_of(name):
        if name not in hosts:
            return None, None
        r = riders[hosts.index(name)]
        return _Comm("scatter", r["srcs"], [lands[i] for i in r["lands"]], [0] * len(r["srcs"]), depth, r["row0"]), r

    moments = dict(
        w_ada=(w_ada, m_w_ada, v_w_ada), b_ada=(b_ada, m_b_ada, v_b_ada),
        g_mix_norm=(g_mix_norm, m_g_mix_norm, v_g_mix_norm), w_in=(w_in, m_w_in, v_w_in),
        g_q_lat=(g_q_lat, m_g_q_lat, v_g_q_lat), w_q_up=(w_q_up, m_w_q_up, v_w_q_up),
        g_kv_lat=(g_kv_lat, m_g_kv_lat, v_g_kv_lat), w_kv_up=(w_kv_up, m_w_kv_up, v_w_kv_up),
        w_sb_out=(w_sb_out, m_w_sb_out, v_w_sb_out), w_mla_out=(w_mla_out, m_w_mla_out, v_w_mla_out),
        w_mix_out=(w_mix_out, m_w_mix_out, v_w_mix_out), g_mlp_norm=(g_mlp_norm, m_g_mlp_norm, v_g_mlp_norm),
        w_up=(w_up, m_w_up, v_w_up), w_down=(w_down, m_w_down, v_w_down),
        g_final=(g_final.reshape(1, d), m_g_final.reshape(1, d), v_g_final.reshape(1, d)))
    small_lands = dict(b_ada=small_all[0], g_mix_norm=small_all[1], g_q_lat=small_all[2], g_kv_lat=small_all[3],
                       g_mlp_norm=small_all[4], g_final=small_all[5])
    big_index = dict(w_in=0, w_q_up=1, w_kv_up=2, w_sb_out=3, w_mla_out=4, w_mix_out=5, w_up=6, w_down=7)
    results = {}
    for name in hosts + [n for n in moments if n not in hosts]:
        wt, mt, vt = moments[name]
        comm, rider = rider_of(name)
        if name == "w_ada":
            res = _adamw("adamw_" + name, wt, g_w_ada, mt, vt, comm)
        else:
            land = small_lands[name] if name in small_lands else lands[big_index[name]]
            res = _sum_adamw("adamw_" + name, land, wt, mt, vt, comm)
        if comm is not None:
            res, got = res
            for i, t in zip(rider["lands"], got):
                lands[i] = t
        res = ([g_w_ada] if name == "w_ada" else []) + list(res)
        results[name] = [t.reshape(d) for t in res] if name == "g_final" else res
    gs, deltas, new_ms, new_vs = ([results[n][k] for n in moments] for k in range(4))

    return (loss, dx[None], *gs, *deltas, *new_ms, *new_vs)
```

```python
import functools

import jax
import jax.numpy as jnp
from jax import lax
from jax.experimental import pallas as pl
from jax.experimental.pallas import tpu as pltpu

F32 = jnp.float32
BF16 = jnp.bfloat16
NDEV = 8
LANES = 128
HEAD_DIM = 64
ROPE_DIM = 32
MLA_QK_DIM = HEAD_DIM + ROPE_DIM
ROPE_THETA = 10000.0
NORM_EPS = 1e-6
ADAM_LR = 0.001
ADAM_B1 = 0.9
ADAM_B2 = 0.999
ADAM_EPS = 1e-08
ADAM_WD = 0.01
ADAM_STEP = 10
VMEM_LIMIT = 48 * 1024 * 1024


def _pcall(body, **kw):
    return pl.pallas_call(body, **kw)


def _tile(n, pref):
    for t in (512, 384, 256, 128, 64, 32, 16, 8):
        if t <= pref and n % t == 0:
            return t
    return n


def _roundup(n, m):
    return (n + m - 1) // m * m


_CP = pltpu.CompilerParams(vmem_limit_bytes=VMEM_LIMIT)


def _rowwise(name, fn, rows, vecs, outs, reds=(), tb=512, comm=None):
    rows = [r if isinstance(r, tuple) else (r, r.shape[1], 0, 0) for r in rows]
    nrows = None
    for arr, width, col, roff in rows:
        if roff == 0 and nrows is None:
            nrows = arr.shape[0]
    first_off = [r for r in rows if r[3] != 0]
    if first_off:
        nrows = min(nrows, first_off[0][3])
    tb = _tile(nrows, tb)
    nblk = nrows // tb
    n_in = len(rows) + len(vecs)
    n_out = len(outs)

    def body(*refs):
        vals = [r[...] for r in refs[:n_in]]
        res = fn(*vals)
        if not isinstance(res, (tuple, list)):
            res = (res,)
        for ref, val in zip(refs[n_in:n_in + n_out], res[:n_out]):
            ref[...] = val.astype(ref.dtype)
        for ref, val in zip(refs[n_in + n_out:], res[n_out:]):
            @pl.when(pl.program_id(0) == 0)
            def _(ref=ref):
                ref[...] = jnp.zeros_like(ref)
            ref[...] += jnp.sum(val.astype(F32), axis=0, keepdims=True)

    in_specs = []
    for arr, width, col, roff in rows:
        in_specs.append(pl.BlockSpec((tb, width), functools.partial(
            lambda i, col, rb: (rb + i, col), col=col, rb=roff // tb)))
    for v in vecs:
        in_specs.append(pl.BlockSpec(v.shape, lambda i, nd=v.ndim: (0,) * nd))
    out_specs = [pl.BlockSpec((tb, w), lambda i: (i, 0)) for w, _ in outs]
    out_specs += [pl.BlockSpec((1, w), lambda i: (0, 0)) for w in reds]
    out_shape = [jax.ShapeDtypeStruct((nrows, w), dt) for w, dt in outs]
    out_shape += [jax.ShapeDtypeStruct((1, w), F32) for w in reds]
    res, got = _hosted(name, body, (nblk,), [r[0] for r in rows] + list(vecs), in_specs, out_shape, out_specs, comm)
    return res if comm is None else (res, got)


_DIMS = {"nn": (((1,), (0,)), ((), ())), "nt": (((1,), (1,)), ((), ())), "tn": (((0,), (0,)), ((), ()))}


def _matmul(name, a, b, mode, out_dtype=F32, epilogue=None, rows=(), vecs=(), outs=None, comm=None, owner_cols=None):
    if mode == "nn":
        (m, k), n = a.shape, b.shape[1]
    elif mode == "nt":
        (m, k), n = a.shape, b.shape[0]
    else:
        (k, m), n = a.shape, b.shape[1]
    tm = 1024 if m % 1024 == 0 else _tile(m, 512)
    tn = owner_cols or next((t for t in (1536, 1024) if n % t == 0 and n > t), _tile(n, 512))
    dims = _DIMS[mode]
    outs = [out_dtype] if outs is None else outs
    n_extra = len(rows) + len(vecs)
    rows = [r if isinstance(r, tuple) else (r, 0) for r in rows]

    def body(a_ref, b_ref, *refs):
        acc = lax.dot_general(a_ref[...].astype(BF16), b_ref[...].astype(BF16), dims, preferred_element_type=F32)
        res = (acc,) if epilogue is None else epilogue(acc, *[r[...] for r in refs[:n_extra]])
        for o_ref, val in zip(refs[n_extra:], res):
            o_ref[...] = val.astype(o_ref.dtype)

    a_spec = pl.BlockSpec((k, tm), lambda j, i: (0, i)) if mode == "tn" else pl.BlockSpec((tm, k), lambda j, i: (i, 0))
    b_spec = pl.BlockSpec((tn, k), lambda j, i: (j, 0)) if mode == "nt" else pl.BlockSpec((k, tn), lambda j, i: (0, j))
    blk = pl.BlockSpec((tm, tn), lambda j, i: (i, j))
    assert all(off % tn == 0 for _, off in rows), (name, tn)
    row_specs = [pl.BlockSpec((tm, tn), functools.partial(lambda j, i, first: (i, first + j), first=off // tn))
                 for _, off in rows]
    out_blk, out_dims = blk, (m, n)
    if owner_cols:
        out_blk, out_dims = pl.BlockSpec((None, tm, tn), lambda j, i: (j, i, 0)), (n // tn, m, tn)
    res, got = _hosted(name, body, (n // tn, m // tm), [a, b, *[r for r, _ in rows], *vecs],
                       [a_spec, b_spec] + row_specs + [pl.BlockSpec((1, tn), lambda j, i: (0, j))] * len(vecs),
                       [jax.ShapeDtypeStruct(out_dims, dt) for dt in outs], [out_blk] * len(outs), comm)
    res = res[0] if len(outs) == 1 else res
    return res if comm is None else (res, got)


class _Comm:
    KS = {"gather_all": (1, 2, 3, 4, 5, 6, 7), "gather_own": (1, 2, 4, 6), "gather_fwd": (2, 4, 6),
          "scatter": (1, 2, 3, 4, 5, 6, 7)}

    def __init__(self, kind, srcs, lands=None, layers=None, depth=None, row0=None):
        self.kind, self.srcs, self.layers, self.row0 = kind, list(srcs), layers, row0
        self.n = len(lands) if kind == "gather_fwd" else len(srcs)
        self.lands = list(lands) if lands is not None else [None] * self.n
        self.out_shapes = []
        for i, land in enumerate(self.lands):
            if land is not None:
                self.out_shapes.append(jax.ShapeDtypeStruct(land.shape, land.dtype))
            elif kind == "scatter":
                self.out_shapes.append(jax.ShapeDtypeStruct((NDEV, depth) + srcs[i].shape[1:], srcs[i].dtype))
            else:
                self.out_shapes.append(jax.ShapeDtypeStruct((NDEV,) + srcs[i].shape, srcs[i].dtype))
        self.operands = self.srcs + [t for t in self.lands if t is not None]
        self.scratch = [pltpu.SemaphoreType.DMA((NDEV - 1, self.n)), pltpu.SemaphoreType.DMA((NDEV - 1, self.n)),
                        pltpu.SemaphoreType.DMA((self.n,))]

    def aliases(self, first_in, first_out):
        given = [i for i, t in enumerate(self.lands) if t is not None]
        return {first_in + len(self.srcs) + pos: first_out + i for pos, i in enumerate(given)}

    def copies(self, in_refs, out_refs, send_sems, recv_sems, local_sems):
        x, y, c = lax.axis_index("x"), lax.axis_index("y"), lax.axis_index("c")
        me = 4 * x + 2 * y + c
        def landing(i):
            if self.row0 is None or self.row0[i] is None:
                return out_refs[i].at[me, self.layers[i]]
            return out_refs[i].at[me, self.layers[i], pl.ds(self.row0[i], self.srcs[i].shape[1])]

        cps = []
        if self.kind != "gather_fwd":
            for i in range(self.n):
                src = in_refs[i].at[me] if self.kind == "scatter" else in_refs[i]
                dst = landing(i) if self.kind == "scatter" else out_refs[i].at[me]
                cps.append(pltpu.make_async_copy(src, dst, local_sems.at[i]))
        for k in self.KS[self.kind]:
            px = 1 - x if k & 4 else x
            py = 1 - y if k & 2 else y
            pc = 1 - c if k & 1 else c
            peer = 4 * px + 2 * py + pc
            for i in range(self.n):
                if self.kind == "gather_fwd":
                    src, dst, to = in_refs[i].at[peer], out_refs[i].at[peer], (x, y, 1 - c)
                elif self.kind == "scatter":
                    src, dst, to = in_refs[i].at[peer], landing(i), (px, py, pc)
                else:
                    src, dst, to = in_refs[i], out_refs[i].at[me], (px, py, pc)
                cps.append(pltpu.make_async_remote_copy(
                    src_ref=src, dst_ref=dst, send_sem=send_sems.at[k - 1, i], recv_sem=recv_sems.at[k - 1, i],
                    device_id=to, device_id_type=pl.DeviceIdType.MESH))
        return cps


class _CommGroup:
    def __init__(self, comms):
        self.comms = comms
        self.n = sum(cm.n for cm in comms)
        self.operands = [t for cm in comms for t in cm.operands]
        self.out_shapes = [t for cm in comms for t in cm.out_shapes]
        self.scratch = [t for cm in comms for t in cm.scratch]

    def aliases(self, first_in, first_out):
        out = {}
        for cm in self.comms:
            out.update(cm.aliases(first_in, first_out))
            first_in, first_out = first_in + len(cm.operands), first_out + cm.n
        return out

    def copies(self, in_refs, out_refs, *sems):
        cps, i, o = [], 0, 0
        for j, cm in enumerate(self.comms):
            cps += cm.copies(in_refs[i:i + len(cm.operands)], out_refs[o:o + cm.n], *sems[3 * j:3 * j + 3])
            i, o = i + len(cm.operands), o + cm.n
        return cps

    def split(self, outs):
        res, o = [], 0
        for cm in self.comms:
            res.append(list(outs[o:o + cm.n]))
            o += cm.n
        return res


_ANY = pl.BlockSpec(memory_space=pl.ANY)


def _exchange(name, comm):
    nci = len(comm.operands)

    def body(*refs):
        cps = comm.copies(refs[:nci], refs[nci:nci + comm.n], *refs[nci + comm.n:])
        for cp in cps:
            cp.start()
        for cp in cps:
            cp.wait()

    return _pcall(body, name=name, in_specs=[_ANY] * nci, out_specs=[_ANY] * comm.n, out_shape=comm.out_shapes,
                  scratch_shapes=comm.scratch, input_output_aliases=comm.aliases(0, 0))(*comm.operands)


def _hosted(name, body, grid, arrays, in_specs, out_shapes, out_specs, comm):
    if comm is None:
        return _pcall(body, name=name, grid=grid, in_specs=in_specs, out_specs=out_specs, out_shape=out_shapes,
                      compiler_params=_CP)(*arrays), []
    ni, no, nci = len(arrays), len(out_shapes), len(comm.operands)

    def full(*refs):
        ins, cin = refs[:ni], refs[ni:ni + nci]
        outs = refs[ni + nci:ni + nci + no]
        cout = refs[ni + nci + no:ni + nci + no + comm.n]
        sems = refs[ni + nci + no + comm.n:]
        first = functools.reduce(jnp.logical_and, [pl.program_id(a) == 0 for a in range(len(grid))])
        last = functools.reduce(jnp.logical_and, [pl.program_id(a) == grid[a] - 1 for a in range(len(grid))])

        @pl.when(first)
        def _():
            for cp in comm.copies(cin, cout, *sems):
                cp.start()

        body(*ins, *outs)

        @pl.when(last)
        def _():
            for cp in comm.copies(cin, cout, *sems):
                cp.wait()

    res = _pcall(full, name=name, grid=grid, in_specs=list(in_specs) + [_ANY] * nci,
                 out_specs=list(out_specs) + [_ANY] * comm.n, out_shape=list(out_shapes) + comm.out_shapes,
                 scratch_shapes=comm.scratch, input_output_aliases=comm.aliases(ni, no),
                 compiler_params=_CP)(*arrays, *comm.operands)
    return res[:no], res[no:]


def _dot_nt(a, b):
    return lax.dot_general(a, b, _DIMS["nt"], preferred_element_type=F32)


def _dot_tn(a, b):
    return lax.dot_general(a, b, _DIMS["tn"], preferred_element_type=F32)


def _dot_nn(a, b):
    return jnp.dot(a, b, preferred_element_type=F32)


def _tri(tk, rel):
    j = lax.broadcasted_iota(jnp.int32, (tk, tk), 0)
    s = lax.broadcasted_iota(jnp.int32, (tk, tk), 1)
    return {"after": j > s, "upto": j <= s, "before": j < s}[rel].astype(BF16)


def _pairs_per_step(nhp):
    return 2 if nhp % 2 == 0 else 1


def _pair(a, pr):
    return a[:, pr * LANES:(pr + 1) * LANES]


def _head_masks():
    lane = lax.broadcasted_iota(jnp.int32, (1, LANES), 1)
    return [(lane // HEAD_DIM) == h for h in range(2)]


ROW_CHUNK = 32


def _by_rows(fn, n_out, *arrays):
    rows = arrays[0].shape[0]
    step = min(ROW_CHUNK, rows)
    outs = [[] for _ in range(n_out)]
    for r in range(0, rows, step):
        for o, val in zip(outs, fn(r, *[a[r:r + step] for a in arrays])):
            o.append(val)
    return [jnp.concatenate(o, axis=0) for o in outs]


def _causal(r0, k0, rows, tk, strict):
    row = lax.broadcasted_iota(jnp.int32, (rows, tk), 0) + r0
    col = lax.broadcasted_iota(jnp.int32, (rows, tk), 1) + k0
    return col < row if strict else col <= row


def _wide(stat, width):
    return stat if width == LANES else jnp.concatenate([stat] * (width // LANES), axis=1)


def _row_sum(v):
    return jnp.broadcast_to(jnp.sum(v, axis=1, keepdims=True), (v.shape[0], LANES))


def _split_bf16(v):
    hi = v.astype(BF16)
    return hi, (v - hi.astype(F32)).astype(BF16)


def _sb_logs(z, scale, mask):
    z = z * scale
    e = jnp.exp(-jnp.abs(z))
    log_sig = jnp.minimum(z, 0.0) - jnp.log(1.0 + e)
    log_fail = log_sig - z
    return z, log_sig, (log_fail if mask is None else jnp.where(mask, log_fail, 0.0))


def _two_loops(n_full, nkb, near_first, step, carry):
    if near_first:
        carry = lax.fori_loop(0, nkb - n_full, lambda j, c: step(nkb - 1 - j, True, c), carry)
        return lax.fori_loop(0, n_full, lambda j, c: step(n_full - 1 - j, False, c), carry)
    carry = lax.fori_loop(0, n_full, lambda j, c: step(j, False, c), carry)
    return lax.fori_loop(n_full, nkb, lambda j, c: step(j, True, c), carry)


def _sb_fwd(name, p, nhp, tq, tk, comm=None):
    s = p.shape[0]
    scale = HEAD_DIM ** -0.5
    nq = s // tq
    pp = _pairs_per_step(nhp)
    wide = pp * LANES

    def body(q_ref, k_ref, v_ref, o_ref, tot_ref):
        qi = pl.program_id(1)
        masks = _head_masks()
        after = _tri(tk, "after")
        nkb = ((qi + 1) * tq + tk - 1) // tk
        q = q_ref[...]
        qhs = [jnp.where(hm, _pair(q, pr), 0.0).astype(BF16) for pr in range(pp) for hm in masks]

        def step(kb, masked, carry):
            ks = pl.multiple_of(kb * tk, tk)
            ks_all = k_ref[pl.ds(ks, tk), :].astype(BF16)
            vs_all = v_ref[pl.ds(ks, tk), :].astype(BF16)
            mask_of = lambda r, n: _causal(qi * tq + r, ks, n, tk, True) if masked else None
            heads = range(len(qhs))

            def logs(r, zc):
                _, log_sig, log_fail = _sb_logs(zc, scale, mask_of(r, zc.shape[0]))
                return (log_sig,) + _split_bf16(log_fail) + (_row_sum(log_fail),)

            def weights(r, lsc, runc, laterc):
                w = jnp.exp(lsc + runc + _wide(laterc, tk))
                return ((jnp.where(mask_of(r, w.shape[0]), w, 0.0) if masked else w).astype(BF16),)

            zs = [_dot_nt(qhs[i], _pair(ks_all, i // 2)) for i in heads]
            first = [_by_rows(logs, 4, zs[i]) for i in heads]
            runs = [_dot_nn(first[i][1], after) + _dot_nn(first[i][2], after) for i in heads]
            ws = [_by_rows(weights, 1, first[i][0], runs[i], carry[i][0])[0] for i in heads]
            pvs = [_dot_nn(ws[i], _pair(vs_all, i // 2)) for i in heads]
            return tuple((carry[i][0] + first[i][3], carry[i][1] + pvs[i]) for i in heads)

        init = (jnp.zeros((tq, LANES), F32), jnp.zeros((tq, LANES), F32))
        res = _two_loops((qi * tq) // tk, nkb, True, step, (init,) * (2 * pp))
        for pr in range(pp):
            (tot0, acc0), (tot1, acc1) = res[2 * pr], res[2 * pr + 1]
            o_ref[:, pr * LANES:(pr + 1) * LANES] = jnp.where(masks[0], acc0, acc1)
            tot_ref[:, pr * LANES:(pr + 1) * LANES] = jnp.where(masks[0], tot0, tot1)

    ng = nhp // pp
    blk = pl.BlockSpec((tq, wide), lambda h, i: (i, h))
    shape = jax.ShapeDtypeStruct((s, nhp * LANES), F32)
    return _hosted(name, body, (ng, nq), [p, p, p],
                   [blk, pl.BlockSpec((s, wide), lambda h, i: (0, ng + h)),
                    pl.BlockSpec((s, wide), lambda h, i: (0, 2 * ng + h))], [shape, shape], [blk, blk], comm)


def _sb_bwd(name, p, tot, do, nhp, tq, tk, comm=None):
    s = p.shape[0]
    scale = HEAD_DIM ** -0.5
    nq = s // tq
    pp = _pairs_per_step(nhp)
    wide = pp * LANES

    def body(q_ref, k_ref, v_ref, tot_ref, do_ref, dq_ref, dk_ref, dv_ref):
        qi = pl.program_id(1)

        @pl.when(qi == 0)
        def _():
            dk_ref[...] = jnp.zeros_like(dk_ref)
            dv_ref[...] = jnp.zeros_like(dv_ref)

        masks = _head_masks()
        upto, before = _tri(tk, "upto"), _tri(tk, "before")
        nkb = ((qi + 1) * tq + tk - 1) // tk
        q = q_ref[...]
        qbs = q.astype(BF16)
        dout = do_ref[...]
        doutbs = dout.astype(BF16)
        qhs = [jnp.where(hm, _pair(q, pr), 0.0).astype(BF16) for pr in range(pp) for hm in masks]
        dohs = [jnp.where(hm, _pair(dout, pr), 0.0).astype(BF16) for pr in range(pp) for hm in masks]
        tot = tot_ref[...]
        totals = [jnp.broadcast_to(tot[:, h * HEAD_DIM:h * HEAD_DIM + 1], (tq, LANES)) for h in range(2 * pp)]

        def step(kb, masked, carry):
            ks = pl.multiple_of(kb * tk, tk)
            ks_all = k_ref[pl.ds(ks, tk), :].astype(BF16)
            vs_all = v_ref[pl.ds(ks, tk), :].astype(BF16)
            mask_of = lambda r, n: _causal(qi * tq + r, ks, n, tk, True) if masked else None
            heads = range(len(qhs))

            def logs(r, zc):
                _, log_sig, log_fail = _sb_logs(zc, scale, mask_of(r, zc.shape[0]))
                return (log_sig,) + _split_bf16(log_fail) + (_row_sum(log_fail),)

            def weights(r, lsc, runc, basec, dwc):
                w = jnp.exp(lsc + (_wide(basec, tk) - runc))
                if masked:
                    w = jnp.where(mask_of(r, w.shape[0]), w, 0.0)
                g = w * dwc
                return (w.astype(BF16), g) + _split_bf16(g) + (_row_sum(g),)

            def dscore(r, gc, lsc, zc, grc, gbc):
                dz = gc * jnp.exp(lsc - zc * scale) - jnp.exp(lsc) * (_wide(gbc, tk) + grc)
                if masked:
                    dz = jnp.where(mask_of(r, dz.shape[0]), dz, 0.0)
                return ((dz * scale).astype(BF16),)

            zs = [_dot_nt(qhs[i], _pair(ks_all, i // 2)) for i in heads]
            dws = [_dot_nt(dohs[i], _pair(vs_all, i // 2)) for i in heads]
            first = [_by_rows(logs, 4, zs[i]) for i in heads]
            runs = [_dot_nn(first[i][1], upto) + _dot_nn(first[i][2], upto) for i in heads]
            second = [_by_rows(weights, 5, first[i][0], runs[i], totals[i] - carry[i][0], dws[i])
                      for i in heads]
            g_runs = [_dot_nn(second[i][2], before) + _dot_nn(second[i][3], before) for i in heads]
            dzs = [_by_rows(dscore, 1, second[i][1], first[i][0], zs[i], g_runs[i], carry[i][1])[0] for i in heads]
            dks = [_dot_tn(dzs[i], _pair(qbs, i // 2)) for i in heads]
            dvs = [_dot_tn(second[i][0], _pair(doutbs, i // 2)) for i in heads]
            dqs = [_dot_nn(dzs[i], _pair(ks_all, i // 2)) for i in heads]
            for pr in range(pp):
                cols = slice(pr * LANES, (pr + 1) * LANES)
                dk_ref[pl.ds(ks, tk), cols] += jnp.where(masks[0], dks[2 * pr], dks[2 * pr + 1])
                dv_ref[pl.ds(ks, tk), cols] += jnp.where(masks[0], dvs[2 * pr], dvs[2 * pr + 1])
            return tuple((carry[i][0] + first[i][3], carry[i][1] + second[i][4], carry[i][2] + dqs[i]) for i in heads)

        zero = jnp.zeros((tq, LANES), F32)
        res = _two_loops((qi * tq) // tk, nkb, False, step, ((zero, zero, zero),) * (2 * pp))
        for pr in range(pp):
            dq_ref[:, pr * LANES:(pr + 1) * LANES] = jnp.where(masks[0], res[2 * pr][2], res[2 * pr + 1][2])

    ng = nhp // pp
    blk = pl.BlockSpec((tq, wide), lambda h, i: (i, h))
    full = pl.BlockSpec((s, wide), lambda h, i: (0, h))
    shape = jax.ShapeDtypeStruct((s, nhp * LANES), F32)
    return _hosted(name, body, (ng, nq), [p, p, p, tot, do],
                   [blk, pl.BlockSpec((s, wide), lambda h, i: (0, ng + h)),
                    pl.BlockSpec((s, wide), lambda h, i: (0, 2 * ng + h)), blk, blk],
                   [shape, shape, shape], [blk, full, full], comm)


def _mla_fwd(name, q, k, v, tq, tk, comm=None):
    s = q.shape[0]
    nhp = v.shape[1] // LANES
    scale = MLA_QK_DIM ** -0.5
    nq = s // tq
    pp = _pairs_per_step(nhp)

    def body(q_ref, k_ref, v_ref, o_ref, lse_ref):
        qi = pl.program_id(1)
        masks = _head_masks()
        nkb = ((qi + 1) * tq + tk - 1) // tk
        qhs = [q_ref[:, h * LANES:(h + 1) * LANES] for h in range(2 * pp)]

        def step(kb, masked, carry):
            ks = pl.multiple_of(kb * tk, tk)
            heads = range(len(qhs))

            def soft(r, zc, mc, lc):
                zc = zc * scale
                if masked:
                    zc = jnp.where(_causal(qi * tq + r, ks, zc.shape[0], tk, False), zc, -1e30)
                m_new = jnp.maximum(mc, jnp.max(zc, axis=1, keepdims=True))
                a = jnp.exp(mc - m_new)
                pr = jnp.exp(zc - _wide(m_new, tk))
                return pr.astype(BF16), m_new, a * lc + _row_sum(pr), a

            zs = [_dot_nt(qhs[h], k_ref[pl.ds(ks, tk), h * LANES:(h + 1) * LANES]) for h in heads]
            first = [_by_rows(soft, 4, zs[h], carry[h][0], carry[h][1]) for h in heads]
            pvs = [_dot_nn(first[h][0], v_ref[pl.ds(ks, tk), (h // 2) * LANES:(h // 2 + 1) * LANES]) for h in heads]
            accs = [_by_rows(lambda r, ac, aa, pc: (aa * ac + pc,), 1, carry[h][2], first[h][3], pvs[h])[0]
                    for h in heads]
            return tuple((first[h][1], first[h][2], accs[h]) for h in heads)

        init = (jnp.full((tq, LANES), -1e30, F32), jnp.zeros((tq, LANES), F32), jnp.zeros((tq, LANES), F32))
        res = _two_loops((qi * tq) // tk, nkb, False, step, (init,) * (2 * pp))
        for pr in range(pp):
            (m0, l0, acc0), (m1, l1, acc1) = res[2 * pr], res[2 * pr + 1]
            o_ref[:, pr * LANES:(pr + 1) * LANES] = jnp.where(masks[0], acc0 / l0, acc1 / l1)
            lse_ref[:, pr * LANES:(pr + 1) * LANES] = jnp.where(masks[0], m0 + jnp.log(l0), m1 + jnp.log(l1))

    shape = jax.ShapeDtypeStruct((s, nhp * LANES), F32)
    blk = pl.BlockSpec((tq, pp * LANES), lambda h, i: (i, h))
    return _hosted(name, body, (nhp // pp, nq), [q, k, v],
                   [pl.BlockSpec((tq, 2 * pp * LANES), lambda h, i: (i, h)),
                    pl.BlockSpec((s, 2 * pp * LANES), lambda h, i: (0, h)),
                    pl.BlockSpec((s, pp * LANES), lambda h, i: (0, h))], [shape, shape], [blk, blk], comm)


def _mla_bwd(name, q, k, v, o, lse, do, tq, tk, comm=None):
    s = q.shape[0]
    nhp = v.shape[1] // LANES
    scale = MLA_QK_DIM ** -0.5
    nq = s // tq
    pp = _pairs_per_step(nhp)

    def body(q_ref, k_ref, v_ref, o_ref, lse_ref, do_ref, dq_ref, dk_ref, dv_ref):
        qi = pl.program_id(1)

        @pl.when(qi == 0)
        def _():
            dk_ref[...] = jnp.zeros_like(dk_ref)
            dv_ref[...] = jnp.zeros_like(dv_ref)

        masks = _head_masks()
        nkb = ((qi + 1) * tq + tk - 1) // tk
        dout = do_ref[...]
        doutbs = dout.astype(BF16)
        prod = dout * o_ref[...]
        qhs = [q_ref[:, h * LANES:(h + 1) * LANES] for h in range(2 * pp)]
        dohs = [jnp.where(hm, _pair(dout, pr), 0.0).astype(BF16) for pr in range(pp) for hm in masks]
        totals = [_row_sum(jnp.where(hm, _pair(prod, pr), 0.0)) for pr in range(pp) for hm in masks]
        lse = lse_ref[...]
        lses = [jnp.broadcast_to(lse[:, h * HEAD_DIM:h * HEAD_DIM + 1], (tq, LANES)) for h in range(2 * pp)]

        def step(kb, masked, carry):
            ks = pl.multiple_of(kb * tk, tk)
            heads = range(len(qhs))

            def probs(r, zc, dpc, lsec, totc):
                pr = jnp.exp(zc * scale - _wide(lsec, tk))
                if masked:
                    pr = jnp.where(_causal(qi * tq + r, ks, pr.shape[0], tk, False), pr, 0.0)
                return pr.astype(BF16), (pr * (dpc - _wide(totc, tk)) * scale).astype(BF16)

            khs = [k_ref[pl.ds(ks, tk), h * LANES:(h + 1) * LANES] for h in heads]
            vvs = [v_ref[pl.ds(ks, tk), (h // 2) * LANES:(h // 2 + 1) * LANES] for h in heads]
            zs = [_dot_nt(qhs[h], khs[h]) for h in heads]
            dps = [_dot_nt(dohs[h], vvs[h]) for h in heads]
            both = [_by_rows(probs, 2, zs[h], dps[h], lses[h], totals[h]) for h in heads]
            dks = [_dot_tn(both[h][1], qhs[h]) for h in heads]
            dvs = [_dot_tn(both[h][0], _pair(doutbs, h // 2)) for h in heads]
            dqs = [_dot_nn(both[h][1], khs[h]) for h in heads]
            for h in heads:
                dk_ref[pl.ds(ks, tk), h * LANES:(h + 1) * LANES] += dks[h]
            for pr in range(pp):
                dv_ref[pl.ds(ks, tk), pr * LANES:(pr + 1) * LANES] += jnp.where(masks[0], dvs[2 * pr], dvs[2 * pr + 1])
            return tuple(carry[h] + dqs[h] for h in heads)

        zero = jnp.zeros((tq, LANES), F32)
        dqs = _two_loops((qi * tq) // tk, nkb, False, step, (zero,) * (2 * pp))
        for h in range(2 * pp):
            dq_ref[:, h * LANES:(h + 1) * LANES] = dqs[h]

    blk = pl.BlockSpec((tq, pp * LANES), lambda h, i: (i, h))
    blk2 = pl.BlockSpec((tq, 2 * pp * LANES), lambda h, i: (i, h))
    full = pl.BlockSpec((s, pp * LANES), lambda h, i: (0, h))
    full2 = pl.BlockSpec((s, 2 * pp * LANES), lambda h, i: (0, h))
    return _hosted(name, body, (nhp // pp, nq), [q, k, v, o, lse, do], [blk2, full2, full, blk, blk, blk],
                   [jax.ShapeDtypeStruct(q.shape, F32), jax.ShapeDtypeStruct(k.shape, F32),
                    jax.ShapeDtypeStruct(v.shape, F32)], [blk2, full2, full], comm)


def _norm_parts(x):
    r = lax.rsqrt(jnp.mean(x * x, axis=-1, keepdims=True) + NORM_EPS)
    return r, x * r


def _rmsmod_fwd(x, g, sc, sh):
    _, xh = _norm_parts(x)
    return ((xh * g) * (1.0 + sc) + sh,)


def _rmsmod_bwd(dh, x, dres, g, sc):
    r, xh = _norm_parts(x)
    dy = dh * (1.0 + sc)
    dxh = dy * g
    dx = r * (dxh - xh * jnp.mean(dxh * xh, axis=-1, keepdims=True)) + dres
    return dx, dh, dh * (xh * g), dy * xh


def _rms_bwd_plain(dh, x, g):
    r, xh = _norm_parts(x)
    dxh = dh * g
    return r * (dxh - xh * jnp.mean(dxh * xh, axis=-1, keepdims=True)), dh * xh


def _cat(parts):
    return jnp.concatenate(parts, axis=1)


def _swap_halves(a):
    half = a.shape[-1] // 2
    return jnp.concatenate([a[..., half:], a[..., :half]], axis=-1)


def _adamw_fn(w, g, m, v):
    m = ADAM_B1 * m + (1.0 - ADAM_B1) * g
    v = ADAM_B2 * v + (1.0 - ADAM_B2) * jnp.square(g)
    m_hat = m / (1.0 - ADAM_B1 ** ADAM_STEP)
    v_hat = v / (1.0 - ADAM_B2 ** ADAM_STEP)
    delta = -ADAM_LR * (m_hat / (jnp.sqrt(v_hat) + ADAM_EPS) + ADAM_WD * w)
    return delta, m, v


def _adamw(name, w, g, m, v, comm=None):
    shape = w.shape
    width = shape[-1]
    flat = [t.reshape(-1, width) for t in (w, g, m, v)]
    res = _rowwise(name, _adamw_fn, flat, [], [(width, F32)] * 3, comm=comm)
    res, got = res if comm is not None else (res, None)
    res = [t.reshape(shape) for t in res]
    return res if comm is None else (res, got)


def _sum_adamw(name, land, w, m, v, comm=None):
    shape = w.shape
    width = shape[-1]
    rows = w.size // width

    def fn(*blocks):
        g = blocks[0].astype(F32)
        for b in blocks[1:NDEV]:
            g = g + b.astype(F32)
        return (g,) + _adamw_fn(blocks[NDEV], g, blocks[NDEV + 1], blocks[NDEV + 2])

    def fn_whole(wb, mb, vb, lb):
        return fn(*[lb[i] for i in range(NDEV)], wb, mb, vb)

    flat = [t.reshape(rows, width) for t in (w, m, v)]
    if rows % 16 == 0:
        views = [(land.reshape(NDEV * rows, width), width, 0, i * rows) for i in range(NDEV)]
        res = _rowwise(name, fn, views + flat, [], [(width, F32)] * 4, comm=comm)
    else:
        res = _rowwise(name, fn_whole, flat, [land.reshape(NDEV, rows, width)], [(width, F32)] * 4, comm=comm)
    res, got = res if comm is not None else (res, None)
    res = [t.reshape(shape) for t in res]
    return res if comm is None else (res, got)


def kernel(x, c, positions, w_ada, b_ada, g_mix_norm, w_in, g_q_lat, w_q_up, g_kv_lat, w_kv_up, w_sb_out, w_mla_out, w_mix_out, g_mlp_norm, w_up, w_down, g_final, loss_target, m_w_ada, m_b_ada, m_g_mix_norm, m_w_in, m_g_q_lat, m_w_q_up, m_g_kv_lat, m_w_kv_up, m_w_sb_out, m_w_mla_out, m_w_mix_out, m_g_mlp_norm, m_w_up, m_w_down, m_g_final, v_w_ada, v_b_ada, v_g_mix_norm, v_w_in, v_g_q_lat, v_w_q_up, v_g_kv_lat, v_w_kv_up, v_w_sb_out, v_w_mla_out, v_w_mix_out, v_g_mlp_norm, v_w_up, v_w_down, v_g_final):
    seq, d = x.shape[1], x.shape[2]
    depth = w_ada.shape[0]
    qr, kvr = g_q_lat.shape[1], g_kv_lat.shape[1]
    sbw, mlaw = w_sb_out.shape[1], w_mla_out.shape[1]
    nh = mlaw // HEAD_DIM
    nhp_sb = sbw // LANES
    dff = w_up.shape[2] * NDEV
    ada_n = w_ada.shape[2]
    gb = min(512, d)
    tq, tk = min(256, seq), min(256, seq)
    me = 4 * lax.axis_index("x") + 2 * lax.axis_index("y") + lax.axis_index("c")

    o_qlat = _roundup(3 * sbw, qr)
    o_kvlat = _roundup(o_qlat + qr, kvr)
    o_rope = _roundup(o_kvlat + kvr, 2 * LANES)
    o_gate = _roundup(o_rope + 2 * LANES, gb)
    wp = o_gate + 2 * d

    c_all = _exchange("ag_c", _Comm("gather_all", [c.reshape(d // LANES, LANES)]))[0].reshape(NDEV, d)
    c_act = _rowwise("silu_c", lambda t: (t * (1.0 / (1.0 + jnp.exp(-t))),), [c_all], [], [(d, F32)])[0]
    parts = jnp.stack([_matmul("ada_fwd", c_act, w_ada[l], "nn") for l in range(depth)])
    parts_all = _exchange("ag_mod", _Comm("gather_all", [parts]))[0]
    mine = jnp.transpose(lax.dynamic_index_in_dim(parts_all, me, axis=2, keepdims=False), (1, 0, 2))
    mod = _rowwise("mod_bias", lambda a, b: (a + b,), [mine.reshape(depth, NDEV * ada_n), b_ada], [],
                   [(6 * d, F32)])[0]
    mods = [[mod[l:l + 1, i * d:(i + 1) * d] for i in range(6)] for l in range(depth)]

    big = [w_in, w_q_up, w_kv_up, w_sb_out, w_mla_out, w_mix_out, w_up, w_down]
    row_sharded = [False, False, False, False, False, True, False, True]
    shards = [[w[l].astype(BF16) for w in big] for l in range(depth)]

    ids_a, ids_b = [0, 1, 2, 3, 4, 5], [6, 7]
    pick = lambda l, ids: [shards[l][i] for i in ids]

    def unpack(gathered, ids):
        out = []
        for g, i in zip(gathered, ids):
            _, rows, cols = g.shape
            if i == 0:
                out.append(g)
            elif row_sharded[i]:
                out.append(g.reshape(NDEV * rows, cols))
            else:
                out.append(jnp.transpose(g, (1, 0, 2)).reshape(rows, NDEV * cols))
        return out

    n_in = w_in.shape[2]
    r0 = 3 * sbw + qr + kvr
    g0 = r0 + ROPE_DIM
    runs = [(0, 3 * sbw, 0), (3 * sbw, 3 * sbw + qr, o_qlat), (3 * sbw + qr, r0, o_kvlat), (g0, g0 + 2 * d, o_gate)]

    def shard_cols(g, a, b):
        return [g[j][:, max(a, n_in * j) - n_in * j:min(b, n_in * (j + 1)) - n_in * j]
                for j in range(a // n_in, (b - 1) // n_in + 1)]

    def derive(full):
        wi, wq, wkv, wsb, wmla, wmix = full
        dt = wi.dtype
        z = lambda r, n: jnp.zeros((r, n), dt)
        kr = _cat(shard_cols(wi, r0, g0))
        pieces, at = [], 0
        for a, b, start in runs[:3]:
            pieces += [z(d, start - at)] + shard_cols(wi, a, b)
            at = start + b - a
        pieces += [z(d, o_rope - at), z(d, HEAD_DIM), kr, z(d, LANES - MLA_QK_DIM),
                   z(d, HEAD_DIM), _swap_halves(kr), z(d, LANES - MLA_QK_DIM),
                   z(d, o_gate - o_rope - 2 * LANES)] + shard_cols(wi, g0, g0 + 2 * d)
        w_in_pad = _cat([t for t in pieces if t.shape[1]])
        wq3 = wq.reshape(qr, nh, MLA_QK_DIM)
        z3 = lambda n: jnp.zeros((qr, nh, n), dt)
        rope_w = wq3[:, :, HEAD_DIM:]
        wq_a = jnp.concatenate([wq3[:, :, :HEAD_DIM], rope_w, z3(LANES - MLA_QK_DIM)], axis=2).reshape(qr, nh * LANES)
        wq_b = jnp.concatenate([z3(HEAD_DIM), _swap_halves(rope_w), z3(LANES - MLA_QK_DIM)], axis=2).reshape(qr, nh * LANES)
        wkv3 = wkv.reshape(kvr, nh, 2 * HEAD_DIM)
        wk = jnp.concatenate([wkv3[:, :, :HEAD_DIM], jnp.zeros((kvr, nh, HEAD_DIM), dt)], axis=2).reshape(kvr, nh * LANES)
        wv = wkv3[:, :, HEAD_DIM:].reshape(kvr, nh * HEAD_DIM)
        return dict(w_in=w_in_pad, w_q=_cat([wq_a, wq_b]), w_kv=_cat([wk, wv]), w_sb=wsb, w_mla=wmla, w_mix=wmix)

    def fold(gr):
        gi, gq, gkv = gr["w_in"], gr["w_q"], gr["w_kv"]
        ra = gi[:, o_rope + HEAD_DIM:o_rope + MLA_QK_DIM]
        rb = gi[:, o_rope + LANES + HEAD_DIM:o_rope + LANES + MLA_QK_DIM]
        rope = (ra.astype(F32) + _swap_halves(rb).astype(F32)).astype(gi.dtype)

        def cols(a, b):
            out = []
            for s0, s1, start in runs[:3] + [(r0, g0, None)] + runs[3:]:
                lo, hi = max(a, s0), min(b, s1)
                if lo < hi:
                    out.append(rope[:, lo - r0:hi - r0] if start is None else gi[:, start + lo - s0:start + hi - s0])
            return out

        g_in = jnp.stack([_cat(cols(n_in * j, n_in * (j + 1))) for j in range(NDEV)]).astype(BF16)
        ga = gq[:, :nh * LANES].reshape(qr, nh, LANES)
        gb_ = gq[:, nh * LANES:].reshape(qr, nh, LANES)
        g_q = jnp.concatenate([ga[:, :, :HEAD_DIM], ga[:, :, HEAD_DIM:MLA_QK_DIM]
                               + _swap_halves(gb_[:, :, HEAD_DIM:MLA_QK_DIM])], axis=2).reshape(qr, nh * MLA_QK_DIM)
        gk = gkv[:, :nh * LANES].reshape(kvr, nh, LANES)[:, :, :HEAD_DIM]
        gv = gkv[:, nh * LANES:].reshape(kvr, nh, HEAD_DIM)
        g_kv = jnp.concatenate([gk, gv], axis=2).reshape(kvr, nh * 2 * HEAD_DIM)
        return [g_in, g_q, g_kv, gr["w_sb"], gr["w_mla"], gr["w_mix"], gr["w_up"], gr["w_down"]]

    part_a = _exchange("ag_w0_own", _Comm("gather_own", pick(0, ids_a)))
    ready_a = _exchange("ag_w0_fwd", _Comm("gather_fwd", [], lands=part_a))
    part_b, weights = None, []

    inv_freq = 1.0 / (ROPE_THETA ** (jnp.arange(0, ROPE_DIM, 2, dtype=F32) / ROPE_DIM))
    ang = positions[0].astype(F32)[:, None] * inv_freq
    cos, sin = jnp.cos(ang), jnp.sin(ang)
    tail = jnp.zeros((seq, LANES - MLA_QK_DIM), F32)
    rope_c = _cat([jnp.ones((seq, HEAD_DIM), F32), cos, cos, tail])
    rope_s = _cat([jnp.zeros((seq, HEAD_DIM), F32), -sin, sin, tail])
    zero_vec = lambda n: jnp.zeros((1, n), F32)

    def rope_fwd(q2, kvs, pd, tc, ts):
        c8, s8 = _cat([tc] * nh), _cat([ts] * nh)
        qf = q2[:, :nh * LANES] * c8 + q2[:, nh * LANES:] * s8
        kpe = pd[:, :LANES] * tc + pd[:, LANES:] * ts
        return qf, kvs[:, :nh * LANES] + _cat([kpe] * nh), kvs[:, nh * LANES:]

    def rope_bwd(dq, dk, dv, tc, ts):
        c8, s8 = _cat([tc] * nh), _cat([ts] * nh)
        dks = dk[:, :LANES]
        for h in range(1, nh):
            dks = dks + dk[:, h * LANES:(h + 1) * LANES]
        return _cat([dq * c8, dq * s8]), _cat([dk, dv]), _cat([dks * tc, dks * ts])

    def merge_fwd(gs, gm, osb, omla):
        return osb / (1.0 + jnp.exp(-gs)) + omla / (1.0 + jnp.exp(-gm))

    def merge_bwd(dm, gs, gm, osb, omla):
        ss, sm = 1.0 / (1.0 + jnp.exp(-gs)), 1.0 / (1.0 + jnp.exp(-gm))
        return ss * dm, sm * dm, dm * osb * ss * (1.0 - ss), dm * omla * sm * (1.0 - sm)

    def gates_of(p):
        return [(p, o_gate), (p, o_gate + d)]

    xs = x[0]
    saved = []
    for l in range(depth):
        w = derive(unpack(ready_a, ids_a))
        sh1, sc1, g1, sh2, sc2, g2 = mods[l]
        h1 = _rowwise("norm1", _rmsmod_fwd, [xs], [g_mix_norm[l:l + 1], sc1, sh1], [(d, BF16)])[0]
        p = _matmul("in_proj", h1, w["w_in"], "nn")
        (o_sb, tot_sb), part_b = _sb_fwd("sb_fwd", p, nhp_sb, tq, tk, _Comm("gather_own", pick(l, ids_b)))
        y_sb = _matmul("sb_out", o_sb, w["w_sb"], "nn")
        qn = _rowwise("norm_q", _rmsmod_fwd, [(p, qr, o_qlat // qr, 0)],
                      [g_q_lat[l:l + 1], zero_vec(qr), zero_vec(qr)], [(qr, BF16)])[0]
        kvn = _rowwise("norm_kv", _rmsmod_fwd, [(p, kvr, o_kvlat // kvr, 0)],
                       [g_kv_lat[l:l + 1], zero_vec(kvr), zero_vec(kvr)], [(kvr, BF16)])[0]
        q2 = _matmul("q_up", qn, w["w_q"], "nn")
        kvs = _matmul("kv_up", kvn, w["w_kv"], "nn")
        qf, kf, vf = _rowwise("rope_fwd", rope_fwd, [q2, kvs, (p, 2 * LANES, o_rope // (2 * LANES), 0), rope_c, rope_s],
                              [], [(nh * LANES, BF16), (nh * LANES, BF16), (mlaw, BF16)])
        comms = [_Comm("gather_fwd", [], lands=part_b)]
        if l + 1 < depth:
            comms.append(_Comm("gather_own", pick(l + 1, ids_a)))
        group = _CommGroup(comms)
        (o_mla, lse), got = _mla_fwd("mla_fwd", qf, kf, vf, tq, tk, group)
        got = group.split(got)
        w["w_up"], w["w_down"] = unpack(got[0], ids_b)
        weights.append(w)
        y_mla, merged = _matmul("mla_out", o_mla, w["w_mla"], "nn", outs=[F32, BF16], rows=gates_of(p) + [y_sb],
                                epilogue=lambda acc, gs, gm, osb: (acc, merge_fwd(gs, gm, osb, acc)))
        resid = lambda acc, xv, g: (acc, xv + g * acc)
        y1, x_mid = _matmul("mix_out", merged, w["w_mix"], "nn", epilogue=resid, rows=[xs], vecs=[g1], outs=[F32, F32])
        h2 = _rowwise("norm2", _rmsmod_fwd, [x_mid], [g_mlp_norm[l:l + 1], sc2, sh2], [(d, BF16)])[0]
        relu2 = lambda acc: (acc, jnp.square(jnp.maximum(acc, 0.0)))
        if l + 1 < depth:
            (u, act), ready_a = _matmul("mlp_up", h2, w["w_up"], "nn", outs=[F32, BF16], epilogue=relu2,
                                        comm=_Comm("gather_fwd", [], lands=got[1]))
        else:
            u, act = _matmul("mlp_up", h2, w["w_up"], "nn", outs=[F32, BF16], epilogue=relu2)
        y2, x_out = _matmul("mlp_down", act, w["w_down"], "nn", epilogue=resid, rows=[x_mid], vecs=[g2], outs=[F32, F32])
        saved.append(dict(x=xs, h1=h1, p=p, o_sb=o_sb, tot_sb=tot_sb, y_sb=y_sb, qn=qn, kvn=kvn, qf=qf, kf=kf, vf=vf, o_mla=o_mla,
                          lse=lse, y_mla=y_mla, merged=merged, y1=y1, x_mid=x_mid, h2=h2, u=u, act=act, y2=y2))
        xs = x_out

    def final_fn(xv, tv, y2, g, gate):
        r, xh = _norm_parts(xv)
        diff = xh * g - tv
        dy = diff * (1.0 / d)
        dxh = dy * g
        dx = r * (dxh - xh * jnp.mean(dxh * xh, axis=-1, keepdims=True))
        return dx, dx * gate, diff * diff, dy * xh, dx * y2

    dx, dy2, sq, dg_final, dgate2 = _rowwise(
        "loss_head", final_fn, [xs, loss_target[0], saved[-1]["y2"]], [g_final.reshape(1, d), mods[-1][5]],
        [(d, F32), (d, BF16)], reds=[d, d, d])
    loss = lax.psum(0.5 * jnp.sum(sq) / d, ("x", "y", "c"))

    def norm2_bwd_fn(dh, xv, dres, y1, g, sc, gate):
        dx_mid, dsh, dsc, dg = _rmsmod_bwd(dh, xv, dres, g, sc)
        return dx_mid, dx_mid * gate, dsh, dsc, dg, dx_mid * y1

    def norm1_bwd_fn(dh, xv, dres, y2, g, sc, gate):
        dxv, dsh, dsc, dg = _rmsmod_bwd(dh, xv, dres, g, sc)
        return dxv, dxv * gate, dsh, dsc, dg, dxv * y2

    def chunk(gfull, wref, by_rows):
        rows, cols = wref.shape[1], wref.shape[2]
        if by_rows:
            return gfull.reshape(NDEV, rows, cols).astype(BF16)
        return jnp.transpose(gfull.reshape(rows, NDEV, cols), (1, 0, 2)).astype(BF16)

    dmods, small = [None] * depth, [None] * depth
    late, lands = None, [None] * len(big)
    for l in reversed(range(depth)):
        w, sv = weights[l], saved[l]
        sh1, sc1, g1, sh2, sc2, g2 = mods[l]
        gr = {}
        du = _matmul("mlp_down_dx", dy2, w["w_down"], "nt", outs=[BF16], rows=[sv["u"]],
                     epilogue=lambda acc, uv: (acc * 2.0 * jnp.maximum(uv, 0.0),))
        gr["w_down"] = _matmul("mlp_down_dw", sv["act"], dy2, "tn", BF16)
        dh2 = _matmul("mlp_up_dx", du, w["w_up"], "nt")
        gr["w_up"] = _matmul("mlp_up_dw", sv["h2"], du, "tn", BF16, owner_cols=w_up.shape[2])
        dx_mid, dy1, dsh2, dsc2, dg_mlp, dgate1 = _rowwise(
            "norm2_bwd", norm2_bwd_fn, [dh2, sv["x_mid"], dx, sv["y1"]], [g_mlp_norm[l:l + 1], sc2, g1],
            [(d, F32), (d, BF16)], reds=[d, d, d, d])
        dy_sb, dy_mla, dgate_sb, dgate_mla = _matmul(
            "mix_out_dx", dy1, w["w_mix"], "nt", outs=[BF16] * 4, epilogue=merge_bwd,
            rows=gates_of(sv["p"]) + [sv["y_sb"], sv["y_mla"]])
        gr["w_mix"] = _matmul("mix_out_dw", sv["merged"], dy1, "tn", BF16)
        do_sb = _matmul("sb_out_dx", dy_sb, w["w_sb"], "nt")
        gr["w_sb"] = _matmul("sb_out_dw", sv["o_sb"], dy_sb, "tn", BF16)
        do_mla = _matmul("mla_out_dx", dy_mla, w["w_mla"], "nt")
        gr["w_mla"] = _matmul("mla_out_dw", sv["o_mla"], dy_mla, "tn", BF16)
        ready = {3: gr["w_sb"], 4: gr["w_mla"], 5: gr["w_mix"], 7: gr["w_down"]}
        ready = {i: chunk(g, big[i], row_sharded[i]) for i, g in ready.items()}
        ready[6] = gr["w_up"]
        ids_a = [6] + ([0, 1, 2] if late is not None else [])
        comm_a = _Comm("scatter", [ready[6]] + (late or []), [lands[i] for i in ids_a], [l] + [l + 1] * 3, depth)
        comm_b = _Comm("scatter", [ready[7]], [lands[7]], [l], depth)
        (dq_sb, dk_sb, dv_sb), got_a = _sb_bwd("sb_bwd", sv["p"], sv["tot_sb"], do_sb, nhp_sb, tq, tk, comm_a)
        (dqf, dkf, dvf), got_b = _mla_bwd("mla_bwd", sv["qf"], sv["kf"], sv["vf"], sv["o_mla"], sv["lse"], do_mla,
                                          tq, tk, comm_b)
        for i, t in zip(ids_a + [7], list(got_a) + list(got_b)):
            lands[i] = t
        dq2, dkvs, drope = _rowwise("rope_bwd", rope_bwd, [dqf, dkf, dvf, rope_c, rope_s], [],
                                    [(2 * nh * LANES, BF16), (nh * LANES + mlaw, BF16), (2 * LANES, BF16)])
        dqn = _matmul("q_up_dx", dq2, w["w_q"], "nt")
        gr["w_q"] = _matmul("q_up_dw", sv["qn"], dq2, "tn")
        dkvn = _matmul("kv_up_dx", dkvs, w["w_kv"], "nt")
        gr["w_kv"] = _matmul("kv_up_dw", sv["kvn"], dkvs, "tn")
        dqlat, dg_q = _rowwise("norm_q_bwd", _rms_bwd_plain, [dqn, (sv["p"], qr, o_qlat // qr, 0)],
                               [g_q_lat[l:l + 1]], [(qr, BF16)], reds=[qr])
        dkvlat, dg_kv = _rowwise("norm_kv_bwd", _rms_bwd_plain, [dkvn, (sv["p"], kvr, o_kvlat // kvr, 0)],
                                 [g_kv_lat[l:l + 1]], [(kvr, BF16)], reds=[kvr])
        zb = lambda n: jnp.zeros((seq, n), BF16)
        dp = _cat([dq_sb.astype(BF16), dk_sb.astype(BF16), dv_sb.astype(BF16), zb(o_qlat - 3 * sbw), dqlat,
                   zb(o_kvlat - o_qlat - qr), dkvlat, zb(o_rope - o_kvlat - kvr), drope,
                   zb(o_gate - o_rope - 2 * LANES), dgate_sb, dgate_mla])
        dh1, got = _matmul("in_proj_dx", dp, w["w_in"], "nt", comm=_Comm("scatter", [ready[5]], [lands[5]], [l], depth))
        lands[5] = got[0]
        gr["w_in"], got = _matmul("in_proj_dw", sv["h1"], dp, "tn", BF16,
                                  comm=_Comm("scatter", [ready[3], ready[4]], [lands[3], lands[4]], [l, l], depth))
        lands[3], lands[4] = got
        dmods[l] = [None, None, dgate1, dsh2, dsc2, dgate2]
        if l > 0:
            dx, dy2, dsh1, dsc1, dg_mix, dgate2 = _rowwise(
                "norm1_bwd", norm1_bwd_fn, [dh1, sv["x"], dx_mid, saved[l - 1]["y2"]],
                [g_mix_norm[l:l + 1], sc1, mods[l - 1][5]], [(d, F32), (d, BF16)], reds=[d, d, d, d])
        else:
            dx, dsh1, dsc1, dg_mix = _rowwise("norm1_bwd", _rmsmod_bwd, [dh1, sv["x"], dx_mid],
                                              [g_mix_norm[l:l + 1], sc1], [(d, F32)], reds=[d, d, d])
        dmods[l] = _cat([dsh1, dsc1] + dmods[l][2:])
        small[l] = (dg_mix, dg_q, dg_kv, dg_mlp)
        g_in, g_q, g_kv = fold(gr)[:3]
        late = [g_in, chunk(g_q, big[1], False), chunk(g_kv, big[2], False)]

    small_parts = [jnp.concatenate(dmods, axis=0)]
    small_parts += [jnp.concatenate([small[l][i] for l in range(depth)], axis=0) for i in range(4)]
    small_parts.append(dg_final)
    small_all = _exchange("ag_small", _Comm("gather_all", small_parts))

    dmod_mine = lax.dynamic_slice_in_dim(small_all[0], me * ada_n, ada_n, axis=2)
    c_act_t = jnp.transpose(c_act)

    def outer_fn(ct, dm):
        acc = ct[:, 0:1] * dm[0:1, :]
        for b in range(1, NDEV):
            acc = acc + ct[:, b:b + 1] * dm[b:b + 1, :]
        return (acc,)

    g_w_ada = jnp.stack([_rowwise("ada_dw", outer_fn, [c_act_t], [dmod_mine[:, l, :]], [(ada_n, F32)])[0]
                         for l in range(depth)])

    hosts = ["w_ada", "w_up", "w_down"]
    if lands[0] is not None:
        step = _roundup(d // len(hosts), 16)
        bounds = [min(d, i * step) for i in range(len(hosts) + 1)]
        riders = [dict(srcs=[late[0][:, a:b]], lands=[0], row0=[a]) for a, b in zip(bounds[:-1], bounds[1:])]
        riders[-1] = dict(srcs=riders[-1]["srcs"] + late[1:], lands=[0, 1, 2], row0=riders[-1]["row0"] + [None, None])
    else:
        riders, hosts = [], []
        lands[:3] = _exchange("a2a_last", _Comm("scatter", late, lands[:3], [0] * 3, depth))

    def rider_of(name):
        if name not in hosts:
            return None, None
        r = riders[hosts.index(name)]
        return _Comm("scatter", r["srcs"], [lands[i] for i in r["lands"]], [0] * len(r["srcs"]), depth, r["row0"]), r

    moments = dict(
        w_ada=(w_ada, m_w_ada, v_w_ada), b_ada=(b_ada, m_b_ada, v_b_ada),
        g_mix_norm=(g_mix_norm, m_g_mix_norm, v_g_mix_norm), w_in=(w_in, m_w_in, v_w_in),
        g_q_lat=(g_q_lat, m_g_q_lat, v_g_q_lat), w_q_up=(w_q_up, m_w_q_up, v_w_q_up),
        g_kv_lat=(g_kv_lat, m_g_kv_lat, v_g_kv_lat), w_kv_up=(w_kv_up, m_w_kv_up, v_w_kv_up),
        w_sb_out=(w_sb_out, m_w_sb_out, v_w_sb_out), w_mla_out=(w_mla_out, m_w_mla_out, v_w_mla_out),
        w_mix_out=(w_mix_out, m_w_mix_out, v_w_mix_out), g_mlp_norm=(g_mlp_norm, m_g_mlp_norm, v_g_mlp_norm),
        w_up=(w_up, m_w_up, v_w_up), w_down=(w_down, m_w_down, v_w_down),
        g_final=(g_final.reshape(1, d), m_g_final.reshape(1, d), v_g_final.reshape(1, d)))
    small_lands = dict(b_ada=small_all[0], g_mix_norm=small_all[1], g_q_lat=small_all[2], g_kv_lat=small_all[3],
                       g_mlp_norm=small_all[4], g_final=small_all[5])
    big_index = dict(w_in=0, w_q_up=1, w_kv_up=2, w_sb_out=3, w_mla_out=4, w_mix_out=5, w_up=6, w_down=7)
    results = {}
    for name in hosts + [n for n in moments if n not in hosts]:
        wt, mt, vt = moments[name]
        comm, rider = rider_of(name)
        if name == "w_ada":
            res = _adamw("adamw_" + name, wt, g_w_ada, mt, vt, comm)
        else:
            land = small_lands[name] if name in small_lands else lands[big_index[name]]
            res = _sum_adamw("adamw_" + name, land, wt, mt, vt, comm)
        if comm is not None:
            res, got = res
            for i, t in zip(rider["lands"], got):
                lands[i] = t
        res = ([g_w_ada] if name == "w_ada" else []) + list(res)
        results[name] = [t.reshape(d) for t in res] if name == "g_final" else res
    gs, deltas, new_ms, new_vs = ([results[n][k] for n in moments] for k in range(4))

    return (loss, dx[None], *gs, *deltas, *new_ms, *new_vs)
```

```python
import functools

import jax
import jax.numpy as jnp
from jax import lax
from jax.experimental import pallas as pl
from jax.experimental.pallas import tpu as pltpu

F32 = jnp.float32
BF16 = jnp.bfloat16
NDEV = 8
LANES = 128
HEAD_DIM = 64
ROPE_DIM = 32
MLA_QK_DIM = HEAD_DIM + ROPE_DIM
ROPE_THETA = 10000.0
NORM_EPS = 1e-6
ADAM_LR = 0.001
ADAM_B1 = 0.9
ADAM_B2 = 0.999
ADAM_EPS = 1e-08
ADAM_WD = 0.01
ADAM_STEP = 10
VMEM_LIMIT = 48 * 1024 * 1024


def _pcall(body, **kw):
    return pl.pallas_call(body, **kw)


def _tile(n, pref):
    for t in (512, 384, 256, 128, 64, 32, 16, 8):
        if t <= pref and n % t == 0:
            return t
    return n


def _roundup(n, m):
    return (n + m - 1) // m * m


_CP = pltpu.CompilerParams(vmem_limit_bytes=VMEM_LIMIT)


def _rowwise(name, fn, rows, vecs, outs, reds=(), tb=512, comm=None):
    rows = [r if isinstance(r, tuple) else (r, r.shape[1], 0, 0) for r in rows]
    nrows = None
    for arr, width, col, roff in rows:
        if roff == 0 and nrows is None:
            nrows = arr.shape[0]
    first_off = [r for r in rows if r[3] != 0]
    if first_off:
        nrows = min(nrows, first_off[0][3])
    tb = _tile(nrows, tb)
    nblk = nrows // tb
    n_in = len(rows) + len(vecs)
    n_out = len(outs)

    def body(*refs):
        vals = [r[...] for r in refs[:n_in]]
        res = fn(*vals)
        if not isinstance(res, (tuple, list)):
            res = (res,)
        for ref, val in zip(refs[n_in:n_in + n_out], res[:n_out]):
            ref[...] = val.astype(ref.dtype)
        for ref, val in zip(refs[n_in + n_out:], res[n_out:]):
            @pl.when(pl.program_id(0) == 0)
            def _(ref=ref):
                ref[...] = jnp.zeros_like(ref)
            ref[...] += jnp.sum(val.astype(F32), axis=0, keepdims=True)

    in_specs = []
    for arr, width, col, roff in rows:
        in_specs.append(pl.BlockSpec((tb, width), functools.partial(
            lambda i, col, rb: (rb + i, col), col=col, rb=roff // tb)))
    for v in vecs:
        in_specs.append(pl.BlockSpec(v.shape, lambda i, nd=v.ndim: (0,) * nd))
    out_specs = [pl.BlockSpec((tb, w), lambda i: (i, 0)) for w, _ in outs]
    out_specs += [pl.BlockSpec((1, w), lambda i: (0, 0)) for w in reds]
    out_shape = [jax.ShapeDtypeStruct((nrows, w), dt) for w, dt in outs]
    out_shape += [jax.ShapeDtypeStruct((1, w), F32) for w in reds]
    res, got = _hosted(name, body, (nblk,), [r[0] for r in rows] + list(vecs), in_specs, out_shape, out_specs, comm)
    return res if comm is None else (res, got)


_DIMS = {"nn": (((1,), (0,)), ((), ())), "nt": (((1,), (1,)), ((), ())), "tn": (((0,), (0,)), ((), ()))}


def _matmul(name, a, b, mode, out_dtype=F32, epilogue=None, rows=(), vecs=(), outs=None, comm=None, owner_cols=None):
    if mode == "nn":
        (m, k), n = a.shape, b.shape[1]
    elif mode == "nt":
        (m, k), n = a.shape, b.shape[0]
    else:
        (k, m), n = a.shape, b.shape[1]
    tm = 1024 if m % 1024 == 0 else _tile(m, 512)
    tn = owner_cols or next((t for t in (1536, 1024) if n % t == 0 and n > t), _tile(n, 512))
    dims = _DIMS[mode]
    outs = [out_dtype] if outs is None else outs
    n_extra = len(rows) + len(vecs)
    rows = [r if isinstance(r, tuple) else (r, 0) for r in rows]

    def body(a_ref, b_ref, *refs):
        acc = lax.dot_general(a_ref[...].astype(BF16), b_ref[...].astype(BF16), dims, preferred_element_type=F32)
        res = (acc,) if epilogue is None else epilogue(acc, *[r[...] for r in refs[:n_extra]])
        for o_ref, val in zip(refs[n_extra:], res):
            o_ref[...] = val.astype(o_ref.dtype)

    a_spec = pl.BlockSpec((k, tm), lambda j, i: (0, i)) if mode == "tn" else pl.BlockSpec((tm, k), lambda j, i: (i, 0))
    b_spec = pl.BlockSpec((tn, k), lambda j, i: (j, 0)) if mode == "nt" else pl.BlockSpec((k, tn), lambda j, i: (0, j))
    blk = pl.BlockSpec((tm, tn), lambda j, i: (i, j))
    assert all(off % tn == 0 for _, off in rows), (name, tn)
    row_specs = [pl.BlockSpec((tm, tn), functools.partial(lambda j, i, first: (i, first + j), first=off // tn))
                 for _, off in rows]
    out_blk, out_dims = blk, (m, n)
    if owner_cols:
        out_blk, out_dims = pl.BlockSpec((None, tm, tn), lambda j, i: (j, i, 0)), (n // tn, m, tn)
    res, got = _hosted(name, body, (n // tn, m // tm), [a, b, *[r for r, _ in rows], *vecs],
                       [a_spec, b_spec] + row_specs + [pl.BlockSpec((1, tn), lambda j, i: (0, j))] * len(vecs),
                       [jax.ShapeDtypeStruct(out_dims, dt) for dt in outs], [out_blk] * len(outs), comm)
    res = res[0] if len(outs) == 1 else res
    return res if comm is None else (res, got)


class _Comm:
    KS = {"gather_all": (1, 2, 3, 4, 5, 6, 7), "gather_own": (1, 2, 4, 6), "gather_fwd": (2, 4, 6),
          "scatter": (1, 2, 3, 4, 5, 6, 7)}

    def __init__(self, kind, srcs, lands=None, layers=None, depth=None, row0=None):
        self.kind, self.srcs, self.layers, self.row0 = kind, list(srcs), layers, row0
        self.n = len(lands) if kind == "gather_fwd" else len(srcs)
        self.lands = list(lands) if lands is not None else [None] * self.n
        self.out_shapes = []
        for i, land in enumerate(self.lands):
            if land is not None:
                self.out_shapes.append(jax.ShapeDtypeStruct(land.shape, land.dtype))
            elif kind == "scatter":
                self.out_shapes.append(jax.ShapeDtypeStruct((NDEV, depth) + srcs[i].shape[1:], srcs[i].dtype))
            else:
                self.out_shapes.append(jax.ShapeDtypeStruct((NDEV,) + srcs[i].shape, srcs[i].dtype))
        self.operands = self.srcs + [t for t in self.lands if t is not None]
        self.scratch = [pltpu.SemaphoreType.DMA((NDEV - 1, self.n)), pltpu.SemaphoreType.DMA((NDEV - 1, self.n)),
                        pltpu.SemaphoreType.DMA((self.n,))]

    def aliases(self, first_in, first_out):
        given = [i for i, t in enumerate(self.lands) if t is not None]
        return {first_in + len(self.srcs) + pos: first_out + i for pos, i in enumerate(given)}

    def copies(self, in_refs, out_refs, send_sems, recv_sems, local_sems):
        x, y, c = lax.axis_index("x"), lax.axis_index("y"), lax.axis_index("c")
        me = 4 * x + 2 * y + c
        def landing(i):
            if self.row0 is None or self.row0[i] is None:
                return out_refs[i].at[me, self.layers[i]]
            return out_refs[i].at[me, self.layers[i], pl.ds(self.row0[i], self.srcs[i].shape[1])]

        cps = []
        if self.kind != "gather_fwd":
            for i in range(self.n):
                src = in_refs[i].at[me] if self.kind == "scatter" else in_refs[i]
                dst = landing(i) if self.kind == "scatter" else out_refs[i].at[me]
                cps.append(pltpu.make_async_copy(src, dst, local_sems.at[i]))
        for k in self.KS[self.kind]:
            px = 1 - x if k & 4 else x
            py = 1 - y if k & 2 else y
            pc = 1 - c if k & 1 else c
            peer = 4 * px + 2 * py + pc
            for i in range(self.n):
                if self.kind == "gather_fwd":
                    src, dst, to = in_refs[i].at[peer], out_refs[i].at[peer], (x, y, 1 - c)
                elif self.kind == "scatter":
                    src, dst, to = in_refs[i].at[peer], landing(i), (px, py, pc)
                else:
                    src, dst, to = in_refs[i], out_refs[i].at[me], (px, py, pc)
                cps.append(pltpu.make_async_remote_copy(
                    src_ref=src, dst_ref=dst, send_sem=send_sems.at[k - 1, i], recv_sem=recv_sems.at[k - 1, i],
                    device_id=to, device_id_type=pl.DeviceIdType.MESH))
        return cps


class _CommGroup:
    def __init__(self, comms):
        self.comms = comms
        self.n = sum(cm.n for cm in comms)
        self.operands = [t for cm in comms for t in cm.operands]
        self.out_shapes = [t for cm in comms for t in cm.out_shapes]
        self.scratch = [t for cm in comms for t in cm.scratch]

    def aliases(self, first_in, first_out):
        out = {}
        for cm in self.comms:
            out.update(cm.aliases(first_in, first_out))
            first_in, first_out = first_in + len(cm.operands), first_out + cm.n
        return out

    def copies(self, in_refs, out_refs, *sems):
        cps, i, o = [], 0, 0
        for j, cm in enumerate(self.comms):
            cps += cm.copies(in_refs[i:i + len(cm.operands)], out_refs[o:o + cm.n], *sems[3 * j:3 * j + 3])
            i, o = i + len(cm.operands), o + cm.n
        return cps

    def split(self, outs):
        res, o = [], 0
        for cm in self.comms:
            res.append(list(outs[o:o + cm.n]))
            o += cm.n
        return res


_ANY = pl.BlockSpec(memory_space=pl.ANY)


def _exchange(name, comm):
    nci = len(comm.operands)

    def body(*refs):
        cps = comm.copies(refs[:nci], refs[nci:nci + comm.n], *refs[nci + comm.n:])
        for cp in cps:
            cp.start()
        for cp in cps:
            cp.wait()

    return _pcall(body, name=name, in_specs=[_ANY] * nci, out_specs=[_ANY] * comm.n, out_shape=comm.out_shapes,
                  scratch_shapes=comm.scratch, input_output_aliases=comm.aliases(0, 0))(*comm.operands)


def _hosted(name, body, grid, arrays, in_specs, out_shapes, out_specs, comm):
    if comm is None:
        return _pcall(body, name=name, grid=grid, in_specs=in_specs, out_specs=out_specs, out_shape=out_shapes,
                      compiler_params=_CP)(*arrays), []
    ni, no, nci = len(arrays), len(out_shapes), len(comm.operands)

    def full(*refs):
        ins, cin = refs[:ni], refs[ni:ni + nci]
        outs = refs[ni + nci:ni + nci + no]
        cout = refs[ni + nci + no:ni + nci + no + comm.n]
        sems = refs[ni + nci + no + comm.n:]
        first = functools.reduce(jnp.logical_and, [pl.program_id(a) == 0 for a in range(len(grid))])
        last = functools.reduce(jnp.logical_and, [pl.program_id(a) == grid[a] - 1 for a in range(len(grid))])

        @pl.when(first)
        def _():
            for cp in comm.copies(cin, cout, *sems):
                cp.start()

        body(*ins, *outs)

        @pl.when(last)
        def _():
            for cp in comm.copies(cin, cout, *sems):
                cp.wait()

    res = _pcall(full, name=name, grid=grid, in_specs=list(in_specs) + [_ANY] * nci,
                 out_specs=list(out_specs) + [_ANY] * comm.n, out_shape=list(out_shapes) + comm.out_shapes,
                 scratch_shapes=comm.scratch, input_output_aliases=comm.aliases(ni, no),
                 compiler_params=_CP)(*arrays, *comm.operands)
    return res[:no], res[no:]


def _dot_nt(a, b):
    return lax.dot_general(a, b, _DIMS["nt"], preferred_element_type=F32)


def _dot_tn(a, b):
    return lax.dot_general(a, b, _DIMS["tn"], preferred_element_type=F32)


def _dot_nn(a, b):
    return jnp.dot(a, b, preferred_element_type=F32)


def _tri(tk, rel):
    j = lax.broadcasted_iota(jnp.int32, (tk, tk), 0)
    s = lax.broadcasted_iota(jnp.int32, (tk, tk), 1)
    return {"after": j > s, "upto": j <= s, "before": j < s}[rel].astype(BF16)


def _pairs_per_step(nhp):
    return 2 if nhp % 2 == 0 else 1


def _pair(a, pr):
    return a[:, pr * LANES:(pr + 1) * LANES]


def _head_masks():
    lane = lax.broadcasted_iota(jnp.int32, (1, LANES), 1)
    return [(lane // HEAD_DIM) == h for h in range(2)]


ROW_CHUNK = 32


def _by_rows(fn, n_out, *arrays):
    rows = arrays[0].shape[0]
    step = min(ROW_CHUNK, rows)
    outs = [[] for _ in range(n_out)]
    for r in range(0, rows, step):
        for o, val in zip(outs, fn(r, *[a[r:r + step] for a in arrays])):
            o.append(val)
    return [jnp.concatenate(o, axis=0) for o in outs]


def _causal(r0, k0, rows, tk, strict):
    row = lax.broadcasted_iota(jnp.int32, (rows, tk), 0) + r0
    col = lax.broadcasted_iota(jnp.int32, (rows, tk), 1) + k0
    return col < row if strict else col <= row


def _wide(stat, width):
    return stat if width == LANES else jnp.concatenate([stat] * (width // LANES), axis=1)


def _row_sum(v):
    return jnp.broadcast_to(jnp.sum(v, axis=1, keepdims=True), (v.shape[0], LANES))


def _split_bf16(v):
    hi = v.astype(BF16)
    return hi, (v - hi.astype(F32)).astype(BF16)


def _sb_logs(z, scale, mask):
    z = z * scale
    e = jnp.exp(-jnp.abs(z))
    log_sig = jnp.minimum(z, 0.0) - jnp.log(1.0 + e)
    log_fail = log_sig - z
    return z, log_sig, (log_fail if mask is None else jnp.where(mask, log_fail, 0.0))


def _two_loops(n_full, nkb, near_first, step, carry):
    if near_first:
        carry = lax.fori_loop(0, nkb - n_full, lambda j, c: step(nkb - 1 - j, True, c), carry)
        return lax.fori_loop(0, n_full, lambda j, c: step(n_full - 1 - j, False, c), carry)
    carry = lax.fori_loop(0, n_full, lambda j, c: step(j, False, c), carry)
    return lax.fori_loop(n_full, nkb, lambda j, c: step(j, True, c), carry)


def _sb_fwd(name, p, nhp, tq, tk, comm=None):
    s = p.shape[0]
    scale = HEAD_DIM ** -0.5
    nq = s // tq
    pp = _pairs_per_step(nhp)
    wide = pp * LANES

    def body(q_ref, k_ref, v_ref, o_ref, tot_ref):
        qi = pl.program_id(1)
        masks = _head_masks()
        after = _tri(tk, "after")
        nkb = ((qi + 1) * tq + tk - 1) // tk
        q = q_ref[...]
        qhs = [jnp.where(hm, _pair(q, pr), 0.0).astype(BF16) for pr in range(pp) for hm in masks]

        def step(kb, masked, carry):
            ks = pl.multiple_of(kb * tk, tk)
            ks_all = k_ref[pl.ds(ks, tk), :].astype(BF16)
            vs_all = v_ref[pl.ds(ks, tk), :].astype(BF16)
            mask_of = lambda r, n: _causal(qi * tq + r, ks, n, tk, True) if masked else None
            heads = range(len(qhs))

            def logs(r, zc):
                _, log_sig, log_fail = _sb_logs(zc, scale, mask_of(r, zc.shape[0]))
                return (log_sig,) + _split_bf16(log_fail) + (_row_sum(log_fail),)

            def weights(r, lsc, runc, laterc):
                w = jnp.exp(lsc + runc + _wide(laterc, tk))
                return ((jnp.where(mask_of(r, w.shape[0]), w, 0.0) if masked else w).astype(BF16),)

            zs = [_dot_nt(qhs[i], _pair(ks_all, i // 2)) for i in heads]
            first = [_by_rows(logs, 4, zs[i]) for i in heads]
            runs = [_dot_nn(first[i][1], after) + _dot_nn(first[i][2], after) for i in heads]
            ws = [_by_rows(weights, 1, first[i][0], runs[i], carry[i][0])[0] for i in heads]
            pvs = [_dot_nn(ws[i], _pair(vs_all, i // 2)) for i in heads]
            return tuple((carry[i][0] + first[i][3], carry[i][1] + pvs[i]) for i in heads)

        init = (jnp.zeros((tq, LANES), F32), jnp.zeros((tq, LANES), F32))
        res = _two_loops((qi * tq) // tk, nkb, True, step, (init,) * (2 * pp))
        for pr in range(pp):
            (tot0, acc0), (tot1, acc1) = res[2 * pr], res[2 * pr + 1]
            o_ref[:, pr * LANES:(pr + 1) * LANES] = jnp.where(masks[0], acc0, acc1)
            tot_ref[:, pr * LANES:(pr + 1) * LANES] = jnp.where(masks[0], tot0, tot1)

    ng = nhp // pp
    blk = pl.BlockSpec((tq, wide), lambda h, i: (i, h))
    shape = jax.ShapeDtypeStruct((s, nhp * LANES), F32)
    return _hosted(name, body, (ng, nq), [p, p, p],
                   [blk, pl.BlockSpec((s, wide), lambda h, i: (0, ng + h)),
                    pl.BlockSpec((s, wide), lambda h, i: (0, 2 * ng + h))], [shape, shape], [blk, blk], comm)


def _sb_bwd(name, p, tot, do, nhp, tq, tk, comm=None):
    s = p.shape[0]
    scale = HEAD_DIM ** -0.5
    nq = s // tq
    pp = _pairs_per_step(nhp)
    wide = pp * LANES

    def body(q_ref, k_ref, v_ref, tot_ref, do_ref, dq_ref, dk_ref, dv_ref):
        qi = pl.program_id(1)

        @pl.when(qi == 0)
        def _():
            dk_ref[...] = jnp.zeros_like(dk_ref)
            dv_ref[...] = jnp.zeros_like(dv_ref)

        masks = _head_masks()
        upto, before = _tri(tk, "upto"), _tri(tk, "before")
        nkb = ((qi + 1) * tq + tk - 1) // tk
        q = q_ref[...]
        qbs = q.astype(BF16)
        dout = do_ref[...]
        doutbs = dout.astype(BF16)
        qhs = [jnp.where(hm, _pair(q, pr), 0.0).astype(BF16) for pr in range(pp) for hm in masks]
        dohs = [jnp.where(hm, _pair(dout, pr), 0.0).astype(BF16) for pr in range(pp) for hm in masks]
        tot = tot_ref[...]
        totals = [jnp.broadcast_to(tot[:, h * HEAD_DIM:h * HEAD_DIM + 1], (tq, LANES)) for h in range(2 * pp)]

        def step(kb, masked, carry):
            ks = pl.multiple_of(kb * tk, tk)
            ks_all = k_ref[pl.ds(ks, tk), :].astype(BF16)
            vs_all = v_ref[pl.ds(ks, tk), :].astype(BF16)
            mask_of = lambda r, n: _causal(qi * tq + r, ks, n, tk, True) if masked else None
            heads = range(len(qhs))

            def logs(r, zc):
                _, log_sig, log_fail = _sb_logs(zc, scale, mask_of(r, zc.shape[0]))
                return (log_sig,) + _split_bf16(log_fail) + (_row_sum(log_fail),)

            def weights(r, lsc, runc, basec, dwc):
                w = jnp.exp(lsc + (_wide(basec, tk) - runc))
                if masked:
                    w = jnp.where(mask_of(r, w.shape[0]), w, 0.0)
                g = w * dwc
                return (w.astype(BF16), g) + _split_bf16(g) + (_row_sum(g),)

            def dscore(r, gc, lsc, zc, grc, gbc):
                dz = gc * jnp.exp(lsc - zc * scale) - jnp.exp(lsc) * (_wide(gbc, tk) + grc)
                if masked:
                    dz = jnp.where(mask_of(r, dz.shape[0]), dz, 0.0)
                return ((dz * scale).astype(BF16),)

            zs = [_dot_nt(qhs[i], _pair(ks_all, i // 2)) for i in heads]
            dws = [_dot_nt(dohs[i], _pair(vs_all, i // 2)) for i in heads]
            first = [_by_rows(logs, 4, zs[i]) for i in heads]
            runs = [_dot_nn(first[i][1], upto) + _dot_nn(first[i][2], upto) for i in heads]
            second = [_by_rows(weights, 5, first[i][0], runs[i], totals[i] - carry[i][0], dws[i])
                      for i in heads]
            g_runs = [_dot_nn(second[i][2], before) + _dot_nn(second[i][3], before) for i in heads]
            dzs = [_by_rows(dscore, 1, second[i][1], first[i][0], zs[i], g_runs[i], carry[i][1])[0] for i in heads]
            dks = [_dot_tn(dzs[i], _pair(qbs, i // 2)) for i in heads]
            dvs = [_dot_tn(second[i][0], _pair(doutbs, i // 2)) for i in heads]
            dqs = [_dot_nn(dzs[i], _pair(ks_all, i // 2)) for i in heads]
            for pr in range(pp):
                cols = slice(pr * LANES, (pr + 1) * LANES)
                dk_ref[pl.ds(ks, tk), cols] += jnp.where(masks[0], dks[2 * pr], dks[2 * pr + 1])
                dv_ref[pl.ds(ks, tk), cols] += jnp.where(masks[0], dvs[2 * pr], dvs[2 * pr + 1])
            return tuple((carry[i][0] + first[i][3], carry[i][1] + second[i][4], carry[i][2] + dqs[i]) for i in heads)

        zero = jnp.zeros((tq, LANES), F32)
        res = _two_loops((qi * tq) // tk, nkb, False, step, ((zero, zero, zero),) * (2 * pp))
        for pr in range(pp):
            dq_ref[:, pr * LANES:(pr + 1) * LANES] = jnp.where(masks[0], res[2 * pr][2], res[2 * pr + 1][2])

    ng = nhp // pp
    blk = pl.BlockSpec((tq, wide), lambda h, i: (i, h))
    full = pl.BlockSpec((s, wide), lambda h, i: (0, h))
    shape = jax.ShapeDtypeStruct((s, nhp * LANES), F32)
    return _hosted(name, body, (ng, nq), [p, p, p, tot, do],
                   [blk, pl.BlockSpec((s, wide), lambda h, i: (0, ng + h)),
                    pl.BlockSpec((s, wide), lambda h, i: (0, 2 * ng + h)), blk, blk],
                   [shape, shape, shape], [blk, full, full], comm)


def _mla_fwd(name, q, k, v, tq, tk, comm=None):
    s = q.shape[0]
    nhp = v.shape[1] // LANES
    scale = MLA_QK_DIM ** -0.5
    nq = s // tq
    pp = _pairs_per_step(nhp)

    def body(q_ref, k_ref, v_ref, o_ref, lse_ref):
        qi = pl.program_id(1)
        masks = _head_masks()
        nkb = ((qi + 1) * tq + tk - 1) // tk
        qhs = [q_ref[:, h * LANES:(h + 1) * LANES] for h in range(2 * pp)]

        def step(kb, masked, carry):
            ks = pl.multiple_of(kb * tk, tk)
            heads = range(len(qhs))

            def soft(r, zc, mc, lc):
                zc = zc * scale
                if masked:
                    zc = jnp.where(_causal(qi * tq + r, ks, zc.shape[0], tk, False), zc, -1e30)
                m_new = jnp.maximum(mc, jnp.max(zc, axis=1, keepdims=True))
                a = jnp.exp(mc - m_new)
                pr = jnp.exp(zc - _wide(m_new, tk))
                return pr.astype(BF16), m_new, a * lc + _row_sum(pr), a

            zs = [_dot_nt(qhs[h], k_ref[pl.ds(ks, tk), h * LANES:(h + 1) * LANES]) for h in heads]
            first = [_by_rows(soft, 4, zs[h], carry[h][0], carry[h][1]) for h in heads]
            pvs = [_dot_nn(first[h][0], v_ref[pl.ds(ks, tk), (h // 2) * LANES:(h // 2 + 1) * LANES]) for h in heads]
            accs = [_by_rows(lambda r, ac, aa, pc: (aa * ac + pc,), 1, carry[h][2], first[h][3], pvs[h])[0]
                    for h in heads]
            return tuple((first[h][1], first[h][2], accs[h]) for h in heads)

        init = (jnp.full((tq, LANES), -1e30, F32), jnp.zeros((tq, LANES), F32), jnp.zeros((tq, LANES), F32))
        res = _two_loops((qi * tq) // tk, nkb, False, step, (init,) * (2 * pp))
        for pr in range(pp):
            (m0, l0, acc0), (m1, l1, acc1) = res[2 * pr], res[2 * pr + 1]
            o_ref[:, pr * LANES:(pr + 1) * LANES] = jnp.where(masks[0], acc0 / l0, acc1 / l1)
            lse_ref[:, pr * LANES:(pr + 1) * LANES] = jnp.where(masks[0], m0 + jnp.log(l0), m1 + jnp.log(l1))

    shape = jax.ShapeDtypeStruct((s, nhp * LANES), F32)
    blk = pl.BlockSpec((tq, pp * LANES), lambda h, i: (i, h))
    return _hosted(name, body, (nhp // pp, nq), [q, k, v],
                   [pl.BlockSpec((tq, 2 * pp * LANES), lambda h, i: (i, h)),
                    pl.BlockSpec((s, 2 * pp * LANES), lambda h, i: (0, h)),
                    pl.BlockSpec((s, pp * LANES), lambda h, i: (0, h))], [shape, shape], [blk, blk], comm)


def _mla_bwd(name, q, k, v, o, lse, do, tq, tk, comm=None):
    s = q.shape[0]
    nhp = v.shape[1] // LANES
    scale = MLA_QK_DIM ** -0.5
    nq = s // tq
    pp = _pairs_per_step(nhp)

    def body(q_ref, k_ref, v_ref, o_ref, lse_ref, do_ref, dq_ref, dk_ref, dv_ref):
        qi = pl.program_id(1)

        @pl.when(qi == 0)
        def _():
            dk_ref[...] = jnp.zeros_like(dk_ref)
            dv_ref[...] = jnp.zeros_like(dv_ref)

        masks = _head_masks()
        nkb = ((qi + 1) * tq + tk - 1) // tk
        dout = do_ref[...]
        doutbs = dout.astype(BF16)
        prod = dout * o_ref[...]
        qhs = [q_ref[:, h * LANES:(h + 1) * LANES] for h in range(2 * pp)]
        dohs = [jnp.where(hm, _pair(dout, pr), 0.0).astype(BF16) for pr in range(pp) for hm in masks]
        totals = [_row_sum(jnp.where(hm, _pair(prod, pr), 0.0)) for pr in range(pp) for hm in masks]
        lse = lse_ref[...]
        lses = [jnp.broadcast_to(lse[:, h * HEAD_DIM:h * HEAD_DIM + 1], (tq, LANES)) for h in range(2 * pp)]

        def step(kb, masked, carry):
            ks = pl.multiple_of(kb * tk, tk)
            heads = range(len(qhs))

            def probs(r, zc, dpc, lsec, totc):
                pr = jnp.exp(zc * scale - _wide(lsec, tk))
                if masked:
                    pr = jnp.where(_causal(qi * tq + r, ks, pr.shape[0], tk, False), pr, 0.0)
                return pr.astype(BF16), (pr * (dpc - _wide(totc, tk)) * scale).astype(BF16)

            khs = [k_ref[pl.ds(ks, tk), h * LANES:(h + 1) * LANES] for h in heads]
            vvs = [v_ref[pl.ds(ks, tk), (h // 2) * LANES:(h // 2 + 1) * LANES] for h in heads]
            zs = [_dot_nt(qhs[h], khs[h]) for h in heads]
            dps = [_dot_nt(dohs[h], vvs[h]) for h in heads]
            both = [_by_rows(probs, 2, zs[h], dps[h], lses[h], totals[h]) for h in heads]
            dks = [_dot_tn(both[h][1], qhs[h]) for h in heads]
            dvs = [_dot_tn(both[h][0], _pair(doutbs, h // 2)) for h in heads]
            dqs = [_dot_nn(both[h][1], khs[h]) for h in heads]
            for h in heads:
                dk_ref[pl.ds(ks, tk), h * LANES:(h + 1) * LANES] += dks[h]
            for pr in range(pp):
                dv_ref[pl.ds(ks, tk), pr * LANES:(pr + 1) * LANES] += jnp.where(masks[0], dvs[2 * pr], dvs[2 * pr + 1])
            return tuple(carry[h] + dqs[h] for h in heads)

        zero = jnp.zeros((tq, LANES), F32)
        dqs = _two_loops((qi * tq) // tk, nkb, False, step, (zero,) * (2 * pp))
        for h in range(2 * pp):
            dq_ref[:, h * LANES:(h + 1) * LANES] = dqs[h]

    blk = pl.BlockSpec((tq, pp * LANES), lambda h, i: (i, h))
    blk2 = pl.BlockSpec((tq, 2 * pp * LANES), lambda h, i: (i, h))
    full = pl.BlockSpec((s, pp * LANES), lambda h, i: (0, h))
    full2 = pl.BlockSpec((s, 2 * pp * LANES), lambda h, i: (0, h))
    return _hosted(name, body, (nhp // pp, nq), [q, k, v, o, lse, do], [blk2, full2, full, blk, blk, blk],
                   [jax.ShapeDtypeStruct(q.shape, F32), jax.ShapeDtypeStruct(k.shape, F32),
                    jax.ShapeDtypeStruct(v.shape, F32)], [blk2, full2, full], comm)


def _norm_parts(x):
    r = lax.rsqrt(jnp.mean(x * x, axis=-1, keepdims=True) + NORM_EPS)
    return r, x * r


def _rmsmod_fwd(x, g, sc, sh):
    _, xh = _norm_parts(x)
    return ((xh * g) * (1.0 + sc) + sh,)


def _rmsmod_bwd(dh, x, dres, g, sc):
    r, xh = _norm_parts(x)
    dy = dh * (1.0 + sc)
    dxh = dy * g
    dx = r * (dxh - xh * jnp.mean(dxh * xh, axis=-1, keepdims=True)) + dres
    return dx, dh, dh * (xh * g), dy * xh


def _rms_bwd_plain(dh, x, g):
    r, xh = _norm_parts(x)
    dxh = dh * g
    return r * (dxh - xh * jnp.mean(dxh * xh, axis=-1, keepdims=True)), dh * xh


def _cat(parts):
    return jnp.concatenate(parts, axis=1)


def _swap_halves(a):
    half = a.shape[-1] // 2
    return jnp.concatenate([a[..., half:], a[..., :half]], axis=-1)


def _adamw_fn(w, g, m, v):
    m = ADAM_B1 * m + (1.0 - ADAM_B1) * g
    v = ADAM_B2 * v + (1.0 - ADAM_B2) * jnp.square(g)
    m_hat = m / (1.0 - ADAM_B1 ** ADAM_STEP)
    v_hat = v / (1.0 - ADAM_B2 ** ADAM_STEP)
    delta = -ADAM_LR * (m_hat / (jnp.sqrt(v_hat) + ADAM_EPS) + ADAM_WD * w)
    return delta, m, v


def _adamw(name, w, g, m, v, comm=None):
    shape = w.shape
    width = shape[-1]
    flat = [t.reshape(-1, width) for t in (w, g, m, v)]
    res = _rowwise(name, _adamw_fn, flat, [], [(width, F32)] * 3, comm=comm)
    res, got = res if comm is not None else (res, None)
    res = [t.reshape(shape) for t in res]
    return res if comm is None else (res, got)


def _sum_adamw(name, land, w, m, v, comm=None):
    shape = w.shape
    width = shape[-1]
    rows = w.size // width

    def fn(*blocks):
        g = blocks[0].astype(F32)
        for b in blocks[1:NDEV]:
            g = g + b.astype(F32)
        return (g,) + _adamw_fn(blocks[NDEV], g, blocks[NDEV + 1], blocks[NDEV + 2])

    def fn_whole(wb, mb, vb, lb):
        return fn(*[lb[i] for i in range(NDEV)], wb, mb, vb)

    flat = [t.reshape(rows, width) for t in (w, m, v)]
    if rows % 16 == 0:
        views = [(land.reshape(NDEV * rows, width), width, 0, i * rows) for i in range(NDEV)]
        res = _rowwise(name, fn, views + flat, [], [(width, F32)] * 4, comm=comm)
    else:
        res = _rowwise(name, fn_whole, flat, [land.reshape(NDEV, rows, width)], [(width, F32)] * 4, comm=comm)
    res, got = res if comm is not None else (res, None)
    res = [t.reshape(shape) for t in res]
    return res if comm is None else (res, got)


def kernel(x, c, positions, w_ada, b_ada, g_mix_norm, w_in, g_q_lat, w_q_up, g_kv_lat, w_kv_up, w_sb_out, w_mla_out, w_mix_out, g_mlp_norm, w_up, w_down, g_final, loss_target, m_w_ada, m_b_ada, m_g_mix_norm, m_w_in, m_g_q_lat, m_w_q_up, m_g_kv_lat, m_w_kv_up, m_w_sb_out, m_w_mla_out, m_w_mix_out, m_g_mlp_norm, m_w_up, m_w_down, m_g_final, v_w_ada, v_b_ada, v_g_mix_norm, v_w_in, v_g_q_lat, v_w_q_up, v_g_kv_lat, v_w_kv_up, v_w_sb_out, v_w_mla_out, v_w_mix_out, v_g_mlp_norm, v_w_up, v_w_down, v_g_final):
    seq, d = x.shape[1], x.shape[2]
    depth = w_ada.shape[0]
    qr, kvr = g_q_lat.shape[1], g_kv_lat.shape[1]
    sbw, mlaw = w_sb_out.shape[1], w_mla_out.shape[1]
    nh = mlaw // HEAD_DIM
    nhp_sb = sbw // LANES
    dff = w_up.shape[2] * NDEV
    ada_n = w_ada.shape[2]
    gb = min(512, d)
    tq, tk = min(256, seq), min(256, seq)
    me = 4 * lax.axis_index("x") + 2 * lax.axis_index("y") + lax.axis_index("c")

    o_qlat = _roundup(3 * sbw, qr)
    o_kvlat = _roundup(o_qlat + qr, kvr)
    o_rope = _roundup(o_kvlat + kvr, 2 * LANES)
    o_gate = _roundup(o_rope + 2 * LANES, gb)
    wp = o_gate + 2 * d

    c_all = _exchange("ag_c", _Comm("gather_all", [c.reshape(d // LANES, LANES)]))[0].reshape(NDEV, d)
    c_act = _rowwise("silu_c", lambda t: (t * (1.0 / (1.0 + jnp.exp(-t))),), [c_all], [], [(d, F32)])[0]
    parts = jnp.stack([_matmul("ada_fwd", c_act, w_ada[l], "nn") for l in range(depth)])
    parts_all = _exchange("ag_mod", _Comm("gather_all", [parts]))[0]
    mine = jnp.transpose(lax.dynamic_index_in_dim(parts_all, me, axis=2, keepdims=False), (1, 0, 2))
    mod = _rowwise("mod_bias", lambda a, b: (a + b,), [mine.reshape(depth, NDEV * ada_n), b_ada], [],
                   [(6 * d, F32)])[0]
    mods = [[mod[l:l + 1, i * d:(i + 1) * d] for i in range(6)] for l in range(depth)]

    big = [w_in, w_q_up, w_kv_up, w_sb_out, w_mla_out, w_mix_out, w_up, w_down]
    row_sharded = [False, False, False, False, False, True, False, True]
    shards = [[w[l].astype(BF16) for w in big] for l in range(depth)]

    ids_a, ids_b = [0, 1, 2, 3, 4, 5], [6, 7]
    pick = lambda l, ids: [shards[l][i] for i in ids]

    def unpack(gathered, ids):
        out = []
        for g, i in zip(gathered, ids):
            _, rows, cols = g.shape
            if i == 0:
                out.append(g)
            elif row_sharded[i]:
                out.append(g.reshape(NDEV * rows, cols))
            else:
                out.append(jnp.transpose(g, (1, 0, 2)).reshape(rows, NDEV * cols))
        return out

    n_in = w_in.shape[2]
    r0 = 3 * sbw + qr + kvr
    g0 = r0 + ROPE_DIM
    runs = [(0, 3 * sbw, 0), (3 * sbw, 3 * sbw + qr, o_qlat), (3 * sbw + qr, r0, o_kvlat), (g0, g0 + 2 * d, o_gate)]

    def shard_cols(g, a, b):
        return [g[j][:, max(a, n_in * j) - n_in * j:min(b, n_in * (j + 1)) - n_in * j]
                for j in range(a // n_in, (b - 1) // n_in + 1)]

    def derive(full):
        wi, wq, wkv, wsb, wmla, wmix = full
        dt = wi.dtype
        z = lambda r, n: jnp.zeros((r, n), dt)
        kr = _cat(shard_cols(wi, r0, g0))
        pieces, at = [], 0
        for a, b, start in runs[:3]:
            pieces += [z(d, start - at)] + shard_cols(wi, a, b)
            at = start + b - a
        pieces += [z(d, o_rope - at), z(d, HEAD_DIM), kr, z(d, LANES - MLA_QK_DIM),
                   z(d, HEAD_DIM), _swap_halves(kr), z(d, LANES - MLA_QK_DIM),
                   z(d, o_gate - o_rope - 2 * LANES)] + shard_cols(wi, g0, g0 + 2 * d)
        w_in_pad = _cat([t for t in pieces if t.shape[1]])
        wq3 = wq.reshape(qr, nh, MLA_QK_DIM)
        z3 = lambda n: jnp.zeros((qr, nh, n), dt)
        rope_w = wq3[:, :, HEAD_DIM:]
        wq_a = jnp.concatenate([wq3[:, :, :HEAD_DIM], rope_w, z3(LANES - MLA_QK_DIM)], axis=2).reshape(qr, nh * LANES)
        wq_b = jnp.concatenate([z3(HEAD_DIM), _swap_halves(rope_w), z3(LANES - MLA_QK_DIM)], axis=2).reshape(qr, nh * LANES)
        wkv3 = wkv.reshape(kvr, nh, 2 * HEAD_DIM)
        wk = jnp.concatenate([wkv3[:, :, :HEAD_DIM], jnp.zeros((kvr, nh, HEAD_DIM), dt)], axis=2).reshape(kvr, nh * LANES)
        wv = wkv3[:, :, HEAD_DIM:].reshape(kvr, nh * HEAD_DIM)
        return dict(w_in=w_in_pad, w_q=_cat([wq_a, wq_b]), w_kv=_cat([wk, wv]), w_sb=wsb, w_mla=wmla, w_mix=wmix)

    def fold(gr):
        gi, gq, gkv = gr["w_in"], gr["w_q"], gr["w_kv"]
        ra = gi[:, o_rope + HEAD_DIM:o_rope + MLA_QK_DIM]
        rb = gi[:, o_rope + LANES + HEAD_DIM:o_rope + LANES + MLA_QK_DIM]
        rope = (ra.astype(F32) + _swap_halves(rb).astype(F32)).astype(gi.dtype)

        def cols(a, b):
            out = []
            for s0, s1, start in runs[:3] + [(r0, g0, None)] + runs[3:]:
                lo, hi = max(a, s0), min(b, s1)
                if lo < hi:
                    out.append(rope[:, lo - r0:hi - r0] if start is None else gi[:, start + lo - s0:start + hi - s0])
            return out

        g_in = jnp.stack([_cat(cols(n_in * j, n_in * (j + 1))) for j in range(NDEV)]).astype(BF16)
        ga = gq[:, :nh * LANES].reshape(qr, nh, LANES)
        gb_ = gq[:, nh * LANES:].reshape(qr, nh, LANES)
        g_q = jnp.concatenate([ga[:, :, :HEAD_DIM], ga[:, :, HEAD_DIM:MLA_QK_DIM]
                               + _swap_halves(gb_[:, :, HEAD_DIM:MLA_QK_DIM])], axis=2).reshape(qr, nh * MLA_QK_DIM)
        gk = gkv[:, :nh * LANES].reshape(kvr, nh, LANES)[:, :, :HEAD_DIM]
        gv = gkv[:, nh * LANES:].reshape(kvr, nh, HEAD_DIM)
        g_kv = jnp.concatenate([gk, gv], axis=2).reshape(kvr, nh * 2 * HEAD_DIM)
        return [g_in, g_q, g_kv, gr["w_sb"], gr["w_mla"], gr["w_mix"], gr["w_up"], gr["w_down"]]

    part_a = _exchange("ag_w0_own", _Comm("gather_own", pick(0, ids_a)))
    ready_a = _exchange("ag_w0_fwd", _Comm("gather_fwd", [], lands=part_a))
    part_b, weights = None, []

    inv_freq = 1.0 / (ROPE_THETA ** (jnp.arange(0, ROPE_DIM, 2, dtype=F32) / ROPE_DIM))
    ang = positions[0].astype(F32)[:, None] * inv_freq
    cos, sin = jnp.cos(ang), jnp.sin(ang)
    tail = jnp.zeros((seq, LANES - MLA_QK_DIM), F32)
    rope_c = _cat([jnp.ones((seq, HEAD_DIM), F32), cos, cos, tail])
    rope_s = _cat([jnp.zeros((seq, HEAD_DIM), F32), -sin, sin, tail])
    zero_vec = lambda n: jnp.zeros((1, n), F32)

    def rope_fwd(q2, kvs, pd, tc, ts):
        c8, s8 = _cat([tc] * nh), _cat([ts] * nh)
        qf = q2[:, :nh * LANES] * c8 + q2[:, nh * LANES:] * s8
        kpe = pd[:, :LANES] * tc + pd[:, LANES:] * ts
        return qf, kvs[:, :nh * LANES] + _cat([kpe] * nh), kvs[:, nh * LANES:]

    def rope_bwd(dq, dk, dv, tc, ts):
        c8, s8 = _cat([tc] * nh), _cat([ts] * nh)
        dks = dk[:, :LANES]
        for h in range(1, nh):
            dks = dks + dk[:, h * LANES:(h + 1) * LANES]
        return _cat([dq * c8, dq * s8]), _cat([dk, dv]), _cat([dks * tc, dks * ts])

    def merge_fwd(gs, gm, osb, omla):
        return osb / (1.0 + jnp.exp(-gs)) + omla / (1.0 + jnp.exp(-gm))

    def merge_bwd(dm, gs, gm, osb, omla):
        ss, sm = 1.0 / (1.0 + jnp.exp(-gs)), 1.0 / (1.0 + jnp.exp(-gm))
        return ss * dm, sm * dm, dm * osb * ss * (1.0 - ss), dm * omla * sm * (1.0 - sm)

    def gates_of(p):
        return [(p, o_gate), (p, o_gate + d)]

    xs = x[0]
    saved = []
    for l in range(depth):
        w = derive(unpack(ready_a, ids_a))
        sh1, sc1, g1, sh2, sc2, g2 = mods[l]
        h1 = _rowwise("norm1", _rmsmod_fwd, [xs], [g_mix_norm[l:l + 1], sc1, sh1], [(d, BF16)])[0]
        p = _matmul("in_proj", h1, w["w_in"], "nn")
        qkv = p[:, :3 * sbw].astype(BF16)
        (o_sb, tot_sb), part_b = _sb_fwd("sb_fwd", qkv, nhp_sb, tq, tk, _Comm("gather_own", pick(l, ids_b)))
        y_sb = _matmul("sb_out", o_sb, w["w_sb"], "nn")
        qn = _rowwise("norm_q", _rmsmod_fwd, [(p, qr, o_qlat // qr, 0)],
                      [g_q_lat[l:l + 1], zero_vec(qr), zero_vec(qr)], [(qr, BF16)])[0]
        kvn = _rowwise("norm_kv", _rmsmod_fwd, [(p, kvr, o_kvlat // kvr, 0)],
                       [g_kv_lat[l:l + 1], zero_vec(kvr), zero_vec(kvr)], [(kvr, BF16)])[0]
        q2 = _matmul("q_up", qn, w["w_q"], "nn")
        kvs = _matmul("kv_up", kvn, w["w_kv"], "nn")
        qf, kf, vf = _rowwise("rope_fwd", rope_fwd, [q2, kvs, (p, 2 * LANES, o_rope // (2 * LANES), 0), rope_c, rope_s],
                              [], [(nh * LANES, BF16), (nh * LANES, BF16), (mlaw, BF16)])
        comms = [_Comm("gather_fwd", [], lands=part_b)]
        if l + 1 < depth:
            comms.append(_Comm("gather_own", pick(l + 1, ids_a)))
        group = _CommGroup(comms)
        (o_mla, lse), got = _mla_fwd("mla_fwd", qf, kf, vf, tq, tk, group)
        got = group.split(got)
        w["w_up"], w["w_down"] = unpack(got[0], ids_b)
        weights.append(w)
        y_mla, merged = _matmul("mla_out", o_mla, w["w_mla"], "nn", outs=[F32, BF16], rows=gates_of(p) + [y_sb],
                                epilogue=lambda acc, gs, gm, osb: (acc, merge_fwd(gs, gm, osb, acc)))
        resid = lambda acc, xv, g: (acc, xv + g * acc)
        y1, x_mid = _matmul("mix_out", merged, w["w_mix"], "nn", epilogue=resid, rows=[xs], vecs=[g1], outs=[F32, F32])
        h2 = _rowwise("norm2", _rmsmod_fwd, [x_mid], [g_mlp_norm[l:l + 1], sc2, sh2], [(d, BF16)])[0]
        relu2 = lambda acc: (acc, jnp.square(jnp.maximum(acc, 0.0)))
        if l + 1 < depth:
            (u, act), ready_a = _matmul("mlp_up", h2, w["w_up"], "nn", outs=[F32, BF16], epilogue=relu2,
                                        comm=_Comm("gather_fwd", [], lands=got[1]))
        else:
            u, act = _matmul("mlp_up", h2, w["w_up"], "nn", outs=[F32, BF16], epilogue=relu2)
        y2, x_out = _matmul("mlp_down", act, w["w_down"], "nn", epilogue=resid, rows=[x_mid], vecs=[g2], outs=[F32, F32])
        saved.append(dict(x=xs, h1=h1, p=p, qkv=qkv, o_sb=o_sb, tot_sb=tot_sb, y_sb=y_sb, qn=qn, kvn=kvn, qf=qf, kf=kf, vf=vf, o_mla=o_mla,
                          lse=lse, y_mla=y_mla, merged=merged, y1=y1, x_mid=x_mid, h2=h2, u=u, act=act, y2=y2))
        xs = x_out

    def final_fn(xv, tv, y2, g, gate):
        r, xh = _norm_parts(xv)
        diff = xh * g - tv
        dy = diff * (1.0 / d)
        dxh = dy * g
        dx = r * (dxh - xh * jnp.mean(dxh * xh, axis=-1, keepdims=True))
        return dx, dx * gate, diff * diff, dy * xh, dx * y2

    dx, dy2, sq, dg_final, dgate2 = _rowwise(
        "loss_head", final_fn, [xs, loss_target[0], saved[-1]["y2"]], [g_final.reshape(1, d), mods[-1][5]],
        [(d, F32), (d, BF16)], reds=[d, d, d])
    loss = lax.psum(0.5 * jnp.sum(sq) / d, ("x", "y", "c"))

    def norm2_bwd_fn(dh, xv, dres, y1, g, sc, gate):
        dx_mid, dsh, dsc, dg = _rmsmod_bwd(dh, xv, dres, g, sc)
        return dx_mid, dx_mid * gate, dsh, dsc, dg, dx_mid * y1

    def norm1_bwd_fn(dh, xv, dres, y2, g, sc, gate):
        dxv, dsh, dsc, dg = _rmsmod_bwd(dh, xv, dres, g, sc)
        return dxv, dxv * gate, dsh, dsc, dg, dxv * y2

    def chunk(gfull, wref, by_rows):
        rows, cols = wref.shape[1], wref.shape[2]
        if by_rows:
            return gfull.reshape(NDEV, rows, cols).astype(BF16)
        return jnp.transpose(gfull.reshape(rows, NDEV, cols), (1, 0, 2)).astype(BF16)

    dmods, small = [None] * depth, [None] * depth
    late, lands = None, [None] * len(big)
    for l in reversed(range(depth)):
        w, sv = weights[l], saved[l]
        sh1, sc1, g1, sh2, sc2, g2 = mods[l]
        gr = {}
        du = _matmul("mlp_down_dx", dy2, w["w_down"], "nt", outs=[BF16], rows=[sv["u"]],
                     epilogue=lambda acc, uv: (acc * 2.0 * jnp.maximum(uv, 0.0),))
        gr["w_down"] = _matmul("mlp_down_dw", sv["act"], dy2, "tn", BF16)
        dh2 = _matmul("mlp_up_dx", du, w["w_up"], "nt")
        gr["w_up"] = _matmul("mlp_up_dw", sv["h2"], du, "tn", BF16, owner_cols=w_up.shape[2])
        dx_mid, dy1, dsh2, dsc2, dg_mlp, dgate1 = _rowwise(
            "norm2_bwd", norm2_bwd_fn, [dh2, sv["x_mid"], dx, sv["y1"]], [g_mlp_norm[l:l + 1], sc2, g1],
            [(d, F32), (d, BF16)], reds=[d, d, d, d])
        dy_sb, dy_mla, dgate_sb, dgate_mla = _matmul(
            "mix_out_dx", dy1, w["w_mix"], "nt", outs=[BF16] * 4, epilogue=merge_bwd,
            rows=gates_of(sv["p"]) + [sv["y_sb"], sv["y_mla"]])
        gr["w_mix"] = _matmul("mix_out_dw", sv["merged"], dy1, "tn", BF16)
        do_sb = _matmul("sb_out_dx", dy_sb, w["w_sb"], "nt")
        gr["w_sb"] = _matmul("sb_out_dw", sv["o_sb"], dy_sb, "tn", BF16)
        do_mla = _matmul("mla_out_dx", dy_mla, w["w_mla"], "nt")
        gr["w_mla"] = _matmul("mla_out_dw", sv["o_mla"], dy_mla, "tn", BF16)
        ready = {3: gr["w_sb"], 4: gr["w_mla"], 5: gr["w_mix"], 7: gr["w_down"]}
        ready = {i: chunk(g, big[i], row_sharded[i]) for i, g in ready.items()}
        ready[6] = gr["w_up"]
        ids_a = [6] + ([0, 1, 2] if late is not None else [])
        comm_a = _Comm("scatter", [ready[6]] + (late or []), [lands[i] for i in ids_a], [l] + [l + 1] * 3, depth)
        comm_b = _Comm("scatter", [ready[7]], [lands[7]], [l], depth)
        (dq_sb, dk_sb, dv_sb), got_a = _sb_bwd("sb_bwd", sv["qkv"], sv["tot_sb"], do_sb, nhp_sb, tq, tk, comm_a)
        (dqf, dkf, dvf), got_b = _mla_bwd("mla_bwd", sv["qf"], sv["kf"], sv["vf"], sv["o_mla"], sv["lse"], do_mla,
                                          tq, tk, comm_b)
        for i, t in zip(ids_a + [7], list(got_a) + list(got_b)):
            lands[i] = t
        dq2, dkvs, drope = _rowwise("rope_bwd", rope_bwd, [dqf, dkf, dvf, rope_c, rope_s], [],
                                    [(2 * nh * LANES, BF16), (nh * LANES + mlaw, BF16), (2 * LANES, BF16)])
        dqn = _matmul("q_up_dx", dq2, w["w_q"], "nt")
        gr["w_q"] = _matmul("q_up_dw", sv["qn"], dq2, "tn")
        dkvn = _matmul("kv_up_dx", dkvs, w["w_kv"], "nt")
        gr["w_kv"] = _matmul("kv_up_dw", sv["kvn"], dkvs, "tn")
        dqlat, dg_q = _rowwise("norm_q_bwd", _rms_bwd_plain, [dqn, (sv["p"], qr, o_qlat // qr, 0)],
                               [g_q_lat[l:l + 1]], [(qr, BF16)], reds=[qr])
        dkvlat, dg_kv = _rowwise("norm_kv_bwd", _rms_bwd_plain, [dkvn, (sv["p"], kvr, o_kvlat // kvr, 0)],
                                 [g_kv_lat[l:l + 1]], [(kvr, BF16)], reds=[kvr])
        zb = lambda n: jnp.zeros((seq, n), BF16)
        dp = _cat([dq_sb.astype(BF16), dk_sb.astype(BF16), dv_sb.astype(BF16), zb(o_qlat - 3 * sbw), dqlat,
                   zb(o_kvlat - o_qlat - qr), dkvlat, zb(o_rope - o_kvlat - kvr), drope,
                   zb(o_gate - o_rope - 2 * LANES), dgate_sb, dgate_mla])
        dh1, got = _matmul("in_proj_dx", dp, w["w_in"], "nt", comm=_Comm("scatter", [ready[5]], [lands[5]], [l], depth))
        lands[5] = got[0]
        gr["w_in"], got = _matmul("in_proj_dw", sv["h1"], dp, "tn", BF16,
                                  comm=_Comm("scatter", [ready[3], ready[4]], [lands[3], lands[4]], [l, l], depth))
        lands[3], lands[4] = got
        dmods[l] = [None, None, dgate1, dsh2, dsc2, dgate2]
        if l > 0:
            dx, dy2, dsh1, dsc1, dg_mix, dgate2 = _rowwise(
                "norm1_bwd", norm1_bwd_fn, [dh1, sv["x"], dx_mid, saved[l - 1]["y2"]],
                [g_mix_norm[l:l + 1], sc1, mods[l - 1][5]], [(d, F32), (d, BF16)], reds=[d, d, d, d])
        else:
            dx, dsh1, dsc1, dg_mix = _rowwise("norm1_bwd", _rmsmod_bwd, [dh1, sv["x"], dx_mid],
                                              [g_mix_norm[l:l + 1], sc1], [(d, F32)], reds=[d, d, d])
        dmods[l] = _cat([dsh1, dsc1] + dmods[l][2:])
        small[l] = (dg_mix, dg_q, dg_kv, dg_mlp)
        g_in, g_q, g_kv = fold(gr)[:3]
        late = [g_in, chunk(g_q, big[1], False), chunk(g_kv, big[2], False)]

    small_parts = [jnp.concatenate(dmods, axis=0)]
    small_parts += [jnp.concatenate([small[l][i] for l in range(depth)], axis=0) for i in range(4)]
    small_parts.append(dg_final)
    small_all = _exchange("ag_small", _Comm("gather_all", small_parts))

    dmod_mine = lax.dynamic_slice_in_dim(small_all[0], me * ada_n, ada_n, axis=2)
    c_act_t = jnp.transpose(c_act)

    def outer_fn(ct, dm):
        acc = ct[:, 0:1] * dm[0:1, :]
        for b in range(1, NDEV):
            acc = acc + ct[:, b:b + 1] * dm[b:b + 1, :]
        return (acc,)

    g_w_ada = jnp.stack([_rowwise("ada_dw", outer_fn, [c_act_t], [dmod_mine[:, l, :]], [(ada_n, F32)])[0]
                         for l in range(depth)])

    hosts = ["w_ada", "w_up", "w_down"]
    if lands[0] is not None:
        step = _roundup(d // len(hosts), 16)
        bounds = [min(d, i * step) for i in range(len(hosts) + 1)]
        riders = [dict(srcs=[late[0][:, a:b]], lands=[0], row0=[a]) for a, b in zip(bounds[:-1], bounds[1:])]
        riders[-1] = dict(srcs=riders[-1]["srcs"] + late[1:], lands=[0, 1, 2], row0=riders[-1]["row0"] + [None, None])
    else:
        riders, hosts = [], []
        lands[:3] = _exchange("a2a_last", _Comm("scatter", late, lands[:3], [0] * 3, depth))

    def rider_of(name):
        if name not in hosts:
            return None, None
        r = riders[hosts.index(name)]
        return _Comm("scatter", r["srcs"], [lands[i] for i in r["lands"]], [0] * len(r["srcs"]), depth, r["row0"]), r

    moments = dict(
        w_ada=(w_ada, m_w_ada, v_w_ada), b_ada=(b_ada, m_b_ada, v_b_ada),
        g_mix_norm=(g_mix_norm, m_g_mix_norm, v_g_mix_norm), w_in=(w_in, m_w_in, v_w_in),
        g_q_lat=(g_q_lat, m_g_q_lat, v_g_q_lat), w_q_up=(w_q_up, m_w_q_up, v_w_q_up),
        g_kv_lat=(g_kv_lat, m_g_kv_lat, v_g_kv_lat), w_kv_up=(w_kv_up, m_w_kv_up, v_w_kv_up),
        w_sb_out=(w_sb_out, m_w_sb_out, v_w_sb_out), w_mla_out=(w_mla_out, m_w_mla_out, v_w_mla_out),
        w_mix_out=(w_mix_out, m_w_mix_out, v_w_mix_out), g_mlp_norm=(g_mlp_norm, m_g_mlp_norm, v_g_mlp_norm),
        w_up=(w_up, m_w_up, v_w_up), w_down=(w_down, m_w_down, v_w_down),
        g_final=(g_final.reshape(1, d), m_g_final.reshape(1, d), v_g_final.reshape(1, d)))
    small_lands = dict(b_ada=small_all[0], g_mix_norm=small_all[1], g_q_lat=small_all[2], g_kv_lat=small_all[3],
                       g_mlp_norm=small_all[4], g_final=small_all[5])
    big_index = dict(w_in=0, w_q_up=1, w_kv_up=2, w_sb_out=3, w_mla_out=4, w_mix_out=5, w_up=6, w_down=7)
    results = {}
    for name in hosts + [n for n in moments if n not in hosts]:
        wt, mt, vt = moments[name]
        comm, rider = rider_of(name)
        if name == "w_ada":
            res = _adamw("adamw_" + name, wt, g_w_ada, mt, vt, comm)
        else:
            land = small_lands[name] if name in small_lands else lands[big_index[name]]
            res = _sum_adamw("adamw_" + name, land, wt, mt, vt, comm)
        if comm is not None:
            res, got = res
            for i, t in zip(rider["lands"], got):
                lands[i] = t
        res = ([g_w_ada] if name == "w_ada" else []) + list(res)
        results[name] = [t.reshape(d) for t in res] if name == "g_final" else res
    gs, deltas, new_ms, new_vs = ([results[n][k] for n in moments] for k in range(4))

    return (loss, dx[None], *gs, *deltas, *new_ms, *new_vs)
```
